```python
import jax, jax.numpy as jnp
from jax import lax
import numpy as np

D_MODEL = 1024
BATCH = 2
SEQ = 8192
DEPTH = 2

GRID_W = 64
CTX_LEN = 256
N_EVEN = (DEPTH + 1) // 2
N_ODD = DEPTH // 2
EPS = 1e-6

RWKV_HEAD = 64
RWKV_HEADS = (3 * D_MODEL // 4) // RWKV_HEAD
RWKV_W = RWKV_HEADS * RWKV_HEAD
LORA_W = 64
LORA_A = 64
FOUR_GROUPS = 4
FOUR_W = D_MODEL - RWKV_W
FOUR_GROUP_W = FOUR_W // FOUR_GROUPS
MIX_W = RWKV_W + FOUR_W
N_SHIFT = 3 * RWKV_W + 2 * LORA_W + 2 * LORA_A
EVEN_IN = N_SHIFT + RWKV_W + 2 * FOUR_W
RWKV_GN_EPS = 64e-5

RET_HEADS = 8
RET_HEAD = D_MODEL // RET_HEADS
RET_W = RET_HEADS * RET_HEAD
ODD_IN = 4 * RET_W
RET_CHUNK = 128
ROPE_BASE = 10000.0

kernel_name = "hybrid_rwkv7_fnet_retention_dit"


def rmsnorm(x, g):
    xf = x.astype(jnp.float32)
    y = xf * lax.rsqrt(jnp.mean(xf * xf, axis=-1, keepdims=True) + EPS)
    return (y * g.astype(jnp.float32)).astype(x.dtype)


def qshift(u, rows):
    b, l, ch = u.shape
    g = u.reshape(b, rows, GRID_W, ch // 4, 4)
    left = jnp.pad(g[:, :, :-1, :, 0], ((0, 0), (0, 0), (1, 0), (0, 0)))
    right = jnp.pad(g[:, :, 1:, :, 1], ((0, 0), (0, 0), (0, 1), (0, 0)))
    up = jnp.pad(g[:, :-1, :, :, 2], ((0, 0), (1, 0), (0, 0), (0, 0)))
    down = jnp.pad(g[:, 1:, :, :, 3], ((0, 0), (0, 1), (0, 0), (0, 0)))
    return jnp.stack([left, right, up, down], axis=-1).reshape(b, l, ch)


def bishift(u):
    b, l, ch = u.shape
    g = u.reshape(b, l, ch // 2, 2)
    prev = jnp.pad(g[:, :-1, :, 0], ((0, 0), (1, 0), (0, 0)))
    nxt = jnp.pad(g[:, 1:, :, 1], ((0, 0), (0, 1), (0, 0)))
    return jnp.stack([prev, nxt], axis=-1).reshape(b, l, ch)


def rwkv_dir_terms(r, k0, v, wl, al, w0, w2, a0, a2, k_k, k_a, r_k):
    b, l, _ = r.shape
    hs = lambda t: t.reshape(b, l, RWKV_HEADS, RWKV_HEAD)
    w_log = -jax.nn.softplus(-(w0 + jnp.tanh(wl) @ w2)) - 0.5
    decay = jnp.exp(-jnp.exp(w_log))
    a = jax.nn.sigmoid(a0 + al @ a2)
    kk = hs(k0 * k_k)
    kk = kk / jnp.maximum(jnp.sqrt(jnp.sum(kk * kk, axis=-1, keepdims=True)), 1e-12)
    k = hs(k0 * (1.0 + (a - 1.0) * k_a))
    rh, vh = hs(r), hs(v)
    bonus = jnp.sum(rh * k * r_k, axis=-1, keepdims=True) * vh
    return rh, hs(decay), k, vh, kk, kk * hs(a), bonus


def rwkv7_scan(r, w, k, v, kk, bb, s0, reverse):
    def step(s, inp):
        r_t, w_t, k_t, v_t, kk_t, b_t = inp
        sa = jnp.einsum('bhij,bhj->bhi', s, kk_t)
        s = s * w_t[:, :, None, :] - sa[..., None] * b_t[:, :, None, :] + v_t[..., None] * k_t[:, :, None, :]
        return s, jnp.einsum('bhij,bhj->bhi', s, r_t)
    xs = tuple(jnp.moveaxis(t, 1, 0) for t in (r, w, k, v, kk, bb))
    s, ys = lax.scan(step, s0, xs, reverse=reverse)
    return s, jnp.moveaxis(ys, 0, 1)


def fourier_mix(u):
    b, l, _ = u.shape
    g = u.astype(jnp.float32).reshape(b, l, FOUR_GROUPS, FOUR_GROUP_W)
    f = jnp.fft.fft2(g, axes=(1, 3), norm='ortho').real
    return f.reshape(b, l, FOUR_W).astype(u.dtype)


def even_mixer(hc, hl, rows, w_in, mu, w0, w2, a0, a2, k_k, k_a, r_k, lnx_g, lnx_b, w_out, need_ctx):
    cuts = np.cumsum([RWKV_W, RWKV_W, RWKV_W, LORA_W, LORA_W, LORA_A]).tolist()

    def proj(h, shift_fn):
        u = h @ w_in
        s, gate_r, four, gate_f = jnp.split(u, [N_SHIFT, N_SHIFT + RWKV_W, N_SHIFT + RWKV_W + FOUR_W], axis=-1)
        s = (s + (shift_fn(s) - s) * mu).astype(jnp.float32)
        r, k0, v, wl_f, wl_b, al_f, al_b = jnp.split(s, cuts, axis=-1)
        return r, k0, v, (wl_f, wl_b), (al_f, al_b), gate_r, four, gate_f

    pc = proj(hc, bishift)
    pl = proj(hl, lambda t: qshift(t, rows))
    bsz = hl.shape[0]
    zeros = jnp.zeros((bsz, RWKV_HEADS, RWKV_HEAD, RWKV_HEAD), jnp.float32)
    oc, ol = 0.0, 0.0
    for d in range(2):
        tc = rwkv_dir_terms(pc[0], pc[1], pc[2], pc[3][d], pc[4][d], w0[d], w2[d], a0[d], a2[d], k_k, k_a, r_k)
        tl = rwkv_dir_terms(pl[0], pl[1], pl[2], pl[3][d], pl[4][d], w0[d], w2[d], a0[d], a2[d], k_k, k_a, r_k)
        s_ctx, yc = rwkv7_scan(*tc[:6], zeros, d == 1)
        _, yl = rwkv7_scan(*tl[:6], s_ctx, d == 1)
        oc = oc + yc + tc[6]
        ol = ol + yl + tl[6]

    def finish(o, gate_r, four, gate_f):
        b, l = o.shape[:2]
        mean = jnp.mean(o, axis=-1, keepdims=True)
        var = jnp.mean(jnp.square(o - mean), axis=-1, keepdims=True)
        o = ((o - mean) * lax.rsqrt(var + RWKV_GN_EPS)).reshape(b, l, RWKV_W) * lnx_g + lnx_b
        yr = o.astype(gate_r.dtype) * jax.nn.silu(gate_r)
        yf = fourier_mix(four) * jax.nn.silu(gate_f)
        return jnp.concatenate([yr, yf], axis=-1) @ w_out

    out_l = finish(ol, pl[5], pl[6], pl[7])
    out_c = finish(oc, pc[5], pc[6], pc[7]) if need_ctx else None
    return out_c, out_l


def axial_rope(t):
    l = t.shape[1]
    pos = jnp.arange(l)
    rc = jnp.stack([pos // GRID_W, pos % GRID_W], axis=-1).astype(jnp.float32)
    half = RET_HEAD // 2
    inv = ROPE_BASE ** (-jnp.arange(0, half, 2, dtype=jnp.float32) / half)
    ang = (rc[:, :, None] * inv).reshape(l, half)[:, None, :]
    cos, sin = jnp.cos(ang), jnp.sin(ang)
    t1, t2 = t[..., :half], t[..., half:]
    return jnp.concatenate([t1 * cos - t2 * sin, t1 * sin + t2 * cos], axis=-1)


def retention_scan(q, k, v, log_gamma, r0, strict):
    bsz, l, h, dk = q.shape
    dv = v.shape[-1]
    n_chunks = l // RET_CHUNK
    pos = jnp.arange(RET_CHUNK, dtype=jnp.float32)
    diff = pos[:, None] - pos[None, :]
    mask = diff > 0 if strict else diff >= 0
    dmat = jnp.where(mask[None], jnp.exp(jnp.where(mask, diff, 0.0)[None] * log_gamma[:, None, None]), 0.0)
    xi = jnp.exp((pos[:, None] + 1.0) * log_gamma[None, :])
    zeta = jnp.exp((RET_CHUNK - 1.0 - pos[:, None]) * log_gamma[None, :])
    g_chunk = jnp.exp(RET_CHUNK * log_gamma)

    def chunk(r, inp):
        qc, kc, vc = inp
        sc = jnp.einsum('bihd,bjhd->bhij', qc, kc) * dmat
        inner = jnp.einsum('bhij,bjhe->bihe', sc, vc)
        cross = jnp.einsum('bihd,bhde->bihe', qc, r) * xi[None, :, :, None]
        r = r * g_chunk[None, :, None, None] + jnp.einsum('bjhd,bjhe->bhde', kc * zeta[None, :, :, None], vc)
        return r, inner + cross

    to_chunks = lambda t: jnp.moveaxis(t.reshape(bsz, n_chunks, RET_CHUNK, h, t.shape[-1]), 1, 0)
    r, o = lax.scan(chunk, r0, (to_chunks(q), to_chunks(k), to_chunks(v)))
    return r, jnp.moveaxis(o, 0, 1).reshape(bsz, l, h, dv)


def odd_mixer(hc, hl, w_in, decay_logit, w_out, need_ctx):
    bsz = hl.shape[0]

    def proj(h, rope):
        u = h @ w_in
        q, k, v, g = jnp.split(u, 4, axis=-1)
        l = h.shape[1]
        hs = lambda t: t.astype(jnp.float32).reshape(bsz, l, RET_HEADS, RET_HEAD)
        q, k, v = hs(q), hs(k) * (RET_HEAD ** -0.5), hs(v)
        if rope:
            q, k = axial_rope(q), axial_rope(k)
        return q, k, v, g

    qc, kc, vc, gc = proj(hc, False)
    ql, kl, vl, gl = proj(hl, True)
    lg = jax.nn.log_sigmoid(decay_logit.astype(jnp.float32))
    zeros = jnp.zeros((bsz, RET_HEADS, RET_HEAD, RET_HEAD), jnp.float32)
    flip = lambda t: jnp.flip(t, axis=1)
    rc_f, oc_f = retention_scan(qc, kc, vc, lg[0], zeros, False)
    _, ol_f = retention_scan(ql, kl, vl, lg[0], rc_f, False)
    rc_b, oc_b = retention_scan(flip(qc), flip(kc), flip(vc), lg[1], zeros, True)
    _, ol_b = retention_scan(flip(ql), flip(kl), flip(vl), lg[1], rc_b, True)

    def finish(of, ob, g):
        o = of + flip(ob)
        o = o * lax.rsqrt(jnp.mean(o * o, axis=-1, keepdims=True) + EPS)
        o = o.reshape(o.shape[0], o.shape[1], RET_W).astype(g.dtype)
        return (o * jax.nn.silu(g)) @ w_out

    out_l = finish(ol_f, ol_b, gl)
    out_c = finish(oc_f, oc_b, gc) if need_ctx else None
    return out_c, out_l


def setup_inputs(seed: int = 0) -> dict:
    key = jax.random.key(seed)
    ks = jax.random.split(key, 24)
    f32 = jnp.float32
    nrm = lambda k, shape, s: jax.random.normal(k, shape, f32) * s
    D = D_MODEL
    gam = 1.0 - 2.0 ** (-5.0 - jnp.arange(RET_HEADS, dtype=f32))
    return {
        'x': nrm(ks[0], (BATCH, SEQ, D), 1.0),
        'c': nrm(ks[1], (BATCH, D), 1.0),
        'ctx': nrm(ks[2], (BATCH, CTX_LEN, D), 1.0),
        'c_ctx': nrm(ks[3], (D,), 1.0),
        'ada_w': nrm(ks[4], (DEPTH, D, 3 * D), D ** -0.5),
        'ada_b': nrm(ks[5], (DEPTH, 3 * D), 0.02),
        'norm_pre': 1.0 + nrm(ks[6], (DEPTH, D), 0.05),
        'norm_post': 1.0 + nrm(ks[7], (DEPTH, D), 0.05),
        'ev_w_in': nrm(ks[8], (N_EVEN, D, EVEN_IN), D ** -0.5),
        'ev_mu': jax.random.uniform(ks[9], (N_EVEN, N_SHIFT), f32),
        'ev_w0': jax.random.uniform(ks[10], (N_EVEN, 2, RWKV_W), f32, -4.0, 0.5),
        'ev_w2': nrm(ks[11], (N_EVEN, 2, LORA_W, RWKV_W), 0.1 * LORA_W ** -0.5),
        'ev_a0': nrm(ks[12], (N_EVEN, 2, RWKV_W), 0.5),
        'ev_a2': nrm(ks[13], (N_EVEN, 2, LORA_A, RWKV_W), 0.1 * LORA_A ** -0.5),
        'ev_k_k': 0.85 + nrm(ks[14], (N_EVEN, RWKV_W), 0.05),
        'ev_k_a': 1.0 + nrm(ks[15], (N_EVEN, RWKV_W), 0.05),
        'ev_r_k': nrm(ks[16], (N_EVEN, RWKV_HEADS, RWKV_HEAD), 0.1),
        'ev_lnx_g': 1.0 + nrm(ks[17], (N_EVEN, RWKV_W), 0.05),
        'ev_lnx_b': nrm(ks[18], (N_EVEN, RWKV_W), 0.02),
        'ev_w_out': nrm(ks[19], (N_EVEN, MIX_W, D), MIX_W ** -0.5),
        'od_w_in': nrm(ks[20], (N_ODD, D, ODD_IN), D ** -0.5),
        'od_decay_logit': (jnp.log(gam) - jnp.log1p(-gam)) + nrm(ks[21], (N_ODD, 2, RET_HEADS), 0.1),
        'od_w_out': nrm(ks[22], (N_ODD, RET_W, D), RET_W ** -0.5),
    }


def reference(x, c, ctx, c_ctx, ada_w, ada_b, norm_pre, norm_post, ev_w_in, ev_mu, ev_w0, ev_w2, ev_a0, ev_a2,
              ev_k_k, ev_k_a, ev_r_k, ev_lnx_g, ev_lnx_b, ev_w_out, od_w_in, od_decay_logit, od_w_out):
    rows = x.shape[1] // GRID_W
    xl, xc = x, ctx
    sc, scc = jax.nn.silu(c), jax.nn.silu(c_ctx)
    for layer in range(DEPTH):
        need_ctx = layer < DEPTH - 1
        shift_l, scale_l, gate_l = jnp.split(sc @ ada_w[layer] + ada_b[layer], 3, axis=-1)
        shift_c, scale_c, gate_c = jnp.split(scc @ ada_w[layer] + ada_b[layer], 3, axis=-1)
        hl = rmsnorm(xl, norm_pre[layer]) * (1.0 + scale_l[:, None, :]) + shift_l[:, None, :]
        hc = rmsnorm(xc, norm_pre[layer]) * (1.0 + scale_c) + shift_c
        if layer % 2 == 0:
            i = layer // 2
            oc, ol = even_mixer(hc, hl, rows, ev_w_in[i], ev_mu[i], ev_w0[i], ev_w2[i], ev_a0[i], ev_a2[i],
                                ev_k_k[i], ev_k_a[i], ev_r_k[i], ev_lnx_g[i], ev_lnx_b[i], ev_w_out[i], need_ctx)
        else:
            i = layer // 2
            oc, ol = odd_mixer(hc, hl, od_w_in[i], od_decay_logit[i], od_w_out[i], need_ctx)
        xl = xl + gate_l[:, None, :] * rmsnorm(ol, norm_post[layer])
        if need_ctx:
            xc = xc + gate_c * rmsnorm(oc, norm_post[layer])
    return xl
```

```python
import functools

import numpy as np
import jax
import jax.numpy as jnp
from jax import lax
from jax.experimental import pallas as pl
from jax.experimental.pallas import tpu as pltpu

F32 = jnp.float32
BF16 = jnp.bfloat16
HIGHEST = lax.Precision.HIGHEST

D_MODEL = 1024
BATCH = 2
SEQ = 8192
GRID_W = 64
GRID_ROWS = SEQ // GRID_W
CTX_LEN = 256
TOK = SEQ + CTX_LEN
EPS = 1e-6

RWKV_HEAD = 64
RWKV_HEADS = 12
RWKV_W = RWKV_HEADS * RWKV_HEAD
LORA = 64
FOUR_GROUPS = 4
FOUR_W = 256
FOUR_GROUP_W = 64
N_SHIFT = 3 * RWKV_W + 4 * LORA
EVEN_IN = N_SHIFT + RWKV_W + 2 * FOUR_W
TAIL_W = EVEN_IN - N_SHIFT
RWKV_GN_EPS = 64e-5

RET_HEADS = 8
RET_HEAD = 128
RET_W = 1024
ODD_IN = 4 * RET_W
ROPE_BASE = 10000.0

TILE = 256
N_TILES = TOK // TILE
CTX_TILE = N_TILES - 1
RW_CHUNK = 64
RW_NCHUNK = TOK // RW_CHUNK
RW_CTX_CHUNKS = CTX_LEN // RW_CHUNK
RW_LAT_CHUNKS = SEQ // RW_CHUNK
PAIR_W = 2 * RWKV_HEAD
N_PAIRS = RWKV_W // PAIR_W
RET_CHUNK = 256
RET_NCHUNK = TOK // RET_CHUNK
VMEM_LIMIT = 56 * 1024 * 1024


def _dot(a, b, precision=HIGHEST):
    return jnp.dot(a, b, precision=precision, preferred_element_type=F32)


def _dot_nt(a, b, precision=HIGHEST):
    return lax.dot_general(a, b, (((1,), (1,)), ((), ())), precision=precision, preferred_element_type=F32)


def _dot_tn(a, b, precision=HIGHEST):
    return lax.dot_general(a, b, (((0,), (0,)), ((), ())), precision=precision, preferred_element_type=F32)


def _silu(x):
    return x * (1.0 / (1.0 + jnp.exp(-x)))


def _sigmoid(x):
    return 1.0 / (1.0 + jnp.exp(-x))


def _params(*sem):
    return pltpu.CompilerParams(dimension_semantics=sem, vmem_limit_bytes=VMEM_LIMIT)


def _ada_kernel(c_ref, w_ref, b_ref, o_ref):
    o_ref[0] = _dot(_silu(c_ref[...]), w_ref[0]) + b_ref[0]


def _ada_mod(cvec, ada_w, ada_b):
    depth = ada_w.shape[0]
    nt = 3 * D_MODEL // 1024
    return pl.pallas_call(
        _ada_kernel,
        out_shape=jax.ShapeDtypeStruct((depth, 8, 3 * D_MODEL), F32),
        grid=(depth, nt),
        in_specs=[
            pl.BlockSpec((8, D_MODEL), lambda l, n: (0, 0)),
            pl.BlockSpec((1, D_MODEL, 1024), lambda l, n: (l, 0, n)),
            pl.BlockSpec((1, 1, 1024), lambda l, n: (l, 0, n)),
        ],
        out_specs=pl.BlockSpec((1, 8, 1024), lambda l, n: (l, 0, n)),
        compiler_params=_params("arbitrary", "arbitrary"),
        name="ada_mod",
    )(cvec, ada_w, ada_b.reshape(depth, 1, 3 * D_MODEL))


def _mod_row(b, j):
    return jnp.where(j == CTX_TILE, BATCH, b)


def _norm_proj_kernel(x_ref, g_ref, mod_ref, w_ref, o_ref):
    x = x_ref[0]
    y = x * lax.rsqrt(jnp.mean(x * x, axis=-1, keepdims=True) + EPS) * g_ref[...]
    shift = mod_ref[0, :, 0:D_MODEL]
    scale = mod_ref[0, :, D_MODEL:2 * D_MODEL]
    h = y * (1.0 + scale) + shift
    o_ref[0] = jnp.dot(h.astype(BF16), w_ref[...], preferred_element_type=F32)


def _norm_proj(xcomb, g, mod, w_bf16):
    n_out = w_bf16.shape[1]
    return pl.pallas_call(
        _norm_proj_kernel,
        out_shape=jax.ShapeDtypeStruct((BATCH, TOK, n_out), F32),
        grid=(BATCH, N_TILES),
        in_specs=[
            pl.BlockSpec((1, TILE, D_MODEL), lambda b, j: (b, j, 0)),
            pl.BlockSpec((1, D_MODEL), lambda b, j: (0, 0)),
            pl.BlockSpec((1, 1, 3 * D_MODEL), lambda b, j: (_mod_row(b, j), 0, 0)),
            pl.BlockSpec((D_MODEL, n_out), lambda b, j: (0, 0)),
        ],
        out_specs=pl.BlockSpec((1, TILE, n_out), lambda b, j: (b, j, 0)),
        compiler_params=_params("arbitrary", "arbitrary"),
        name="norm_proj",
    )(xcomb, g.reshape(1, D_MODEL), mod.reshape(8, 1, 3 * D_MODEL), w_bf16)


def _even_terms_kernel(u_ref, up_ref, dn_ref, mu_ref, w0_ref, w2_ref, a0_ref, a2_ref, kk_ref_, ka_ref, rk_ref,
                       ones_ref, r_out, v_out, kk_out, k_out, b_out, lw_out, bonus_out):
    j = pl.program_id(1)
    is_ctx = j == CTX_TILE
    x = u_ref[0]
    t = lax.broadcasted_iota(jnp.int32, (TILE, N_SHIFT), 0)
    lane = lax.broadcasted_iota(jnp.int32, (TILE, N_SHIFT), 1)
    left = pltpu.roll(x, 1, 0)
    right = pltpu.roll(x, TILE - 1, 0)
    edge = jnp.where(is_ctx, TILE - 1, GRID_W - 1)
    left = jnp.where((t & edge) == 0, 0.0, left)
    right = jnp.where((t & edge) == edge, 0.0, right)
    up = jnp.concatenate([up_ref[0], x[:TILE - GRID_W]], axis=0)
    down = jnp.concatenate([x[GRID_W:], dn_ref[0]], axis=0)
    up = jnp.where(t < jnp.where(j == 0, GRID_W, 0), 0.0, up)
    down = jnp.where(t >= jnp.where(j == CTX_TILE - 1, TILE - GRID_W, TILE), 0.0, down)
    sel = jnp.where(is_ctx, lane & 1, lane & 3)
    shifted = jnp.where(sel == 0, left, jnp.where(sel == 1, right, jnp.where(sel == 2, up, down)))
    s = x + (shifted - x) * mu_ref[...]

    r = s[:, 0:RWKV_W]
    k0 = s[:, RWKV_W:2 * RWKV_W]
    v = s[:, 2 * RWKV_W:3 * RWKV_W]
    ones_bd = ones_ref[...]
    kk = k0 * kk_ref_[...]
    norm = jnp.sqrt(_dot(kk * kk, ones_bd))
    kk = kk / jnp.maximum(norm, 1e-12)
    r_out[0] = r
    v_out[0] = v
    kk_out[0] = kk
    bonus = jnp.zeros_like(v)
    for d in range(2):
        wl = s[:, 3 * RWKV_W + d * LORA:3 * RWKV_W + (d + 1) * LORA]
        al = s[:, 3 * RWKV_W + (2 + d) * LORA:3 * RWKV_W + (3 + d) * LORA]
        z = w0_ref[d:d + 1, :] + _dot(jnp.tanh(wl), w2_ref[d])
        lw_out[d, 0] = -np.float32(np.exp(-0.5)) * _sigmoid(z)
        a = _sigmoid(a0_ref[d:d + 1, :] + _dot(al, a2_ref[d]))
        k = k0 * (1.0 + (a - 1.0) * ka_ref[...])
        k_out[d, 0] = k
        b_out[d, 0] = kk * a
        bonus = bonus + _dot(r * k * rk_ref[...], ones_bd) * v
    bonus_out[0] = bonus


def _even_terms(u, mu, w0, w2, a0, a2, k_k, k_a, r_k, ones_bd):
    tok = jax.ShapeDtypeStruct((BATCH, TOK, RWKV_W), F32)
    tok2 = jax.ShapeDtypeStruct((2, BATCH, TOK, RWKV_W), F32)
    per = TILE // GRID_W
    last64 = TOK // GRID_W - 1
    full = lambda shape: pl.BlockSpec(shape, lambda b, j: (0,) * len(shape))
    tspec = pl.BlockSpec((1, TILE, RWKV_W), lambda b, j: (b, j, 0))
    t2spec = pl.BlockSpec((2, 1, TILE, RWKV_W), lambda b, j: (0, b, j, 0))
    return pl.pallas_call(
        _even_terms_kernel,
        out_shape=(tok, tok, tok, tok2, tok2, tok2, tok),
        grid=(BATCH, N_TILES),
        in_specs=[
            pl.BlockSpec((1, TILE, N_SHIFT), lambda b, j: (b, j, 0)),
            pl.BlockSpec((1, GRID_W, N_SHIFT), lambda b, j: (b, jnp.maximum(j * per - 1, 0), 0)),
            pl.BlockSpec((1, GRID_W, N_SHIFT), lambda b, j: (b, jnp.minimum(j * per + per, last64), 0)),
            full((1, N_SHIFT)), full((2, RWKV_W)), full((2, LORA, RWKV_W)), full((2, RWKV_W)),
            full((2, LORA, RWKV_W)), full((1, RWKV_W)), full((1, RWKV_W)), full((1, RWKV_W)),
            full((RWKV_W, RWKV_W)),
        ],
        out_specs=(tspec, tspec, tspec, t2spec, t2spec, t2spec, tspec),
        compiler_params=_params("arbitrary", "arbitrary"),
        name="even_terms",
    )(u, u, u, mu.reshape(1, N_SHIFT), w0, w2, a0, a2, k_k.reshape(1, RWKV_W), k_a.reshape(1, RWKV_W),
      r_k.reshape(1, RWKV_W), ones_bd)


def _rw_chunk_index(d, s):
    fwd = jnp.where(s < RW_CTX_CHUNKS, RW_LAT_CHUNKS + s, s - RW_CTX_CHUNKS)
    return jnp.where(d == 0, fwd, RW_NCHUNK - 1 - s)


def _rwkv_scan_kernel(r_ref, v_ref, kk_ref, k_ref, b_ref, lw_ref, y_ref, h_ref):
    d = pl.program_id(0)
    s = pl.program_id(3)
    n = 2 * RW_CHUNK

    @pl.when(s == 0)
    def _():
        h_ref[...] = jnp.zeros_like(h_ref)

    rev = d == 1
    ri = lax.broadcasted_iota(jnp.int32, (n, n), 0)
    ci = lax.broadcasted_iota(jnp.int32, (n, n), 1)
    pr = jnp.where(rev, RW_CHUNK - 1 - (ri & (RW_CHUNK - 1)), ri & (RW_CHUNK - 1))
    pc = jnp.where(rev, RW_CHUNK - 1 - (ci & (RW_CHUNK - 1)), ci & (RW_CHUNK - 1))
    same = (ri >> 6) == (ci >> 6)
    strict = same & (pc < pr)
    incl = same & (pc <= pr)
    eye = ri == ci

    lw = lw_ref[0, 0]
    cum = jnp.where(incl[:RW_CHUNK, :RW_CHUNK], 1.0, 0.0)
    lc = _dot(cum, lw)
    lc_end = jnp.sum(lw, axis=0, keepdims=True)
    g_in = jnp.exp(lc)
    g_inv = jnp.exp(-lc)
    g_end = jnp.exp(lc_end - lc)
    kkc = kk_ref[0]
    at = -kkc * jnp.exp(lc - lw)
    rt = r_ref[0] * g_in
    kc = k_ref[0, 0]
    bc = b_ref[0, 0]
    head0 = lax.broadcasted_iota(jnp.int32, (RW_CHUNK, PAIR_W), 1) < RWKV_HEAD

    def stack(x):
        return jnp.concatenate([jnp.where(head0, x, 0.0), jnp.where(head0, 0.0, x)], axis=0)

    at_s, rt_s = stack(at), stack(rt)
    kh_s, bh_s = stack(kc * g_inv), stack(bc * g_inv)
    ke_s, be_s = stack(kc * g_end), stack(bc * g_end)
    v_s = stack(v_ref[0])

    ar = jnp.concatenate([at_s, rt_s], axis=0)
    kb = jnp.concatenate([kh_s, bh_s], axis=0)
    quad = _dot_nt(ar, kb)
    a_k = jnp.where(strict, quad[:n, :n], 0.0)
    a_b = jnp.where(strict, quad[:n, n:], 0.0)
    b_k = jnp.where(incl, quad[n:, :n], 0.0)
    b_b = jnp.where(incl, quad[n:, n:], 0.0)

    blk8 = (ri >> 3) == (ci >> 3)
    n1 = jnp.where(blk8, a_b, 0.0)
    n2 = _dot(n1, n1)
    n4 = _dot(n2, n2)
    ident = jnp.where(eye, 1.0, 0.0)
    tinv = _dot(_dot(ident + n1, ident + n2), ident + n4)
    for sh in (3, 4, 5):
        inner = (ri >> sh) == (ci >> sh)
        outer = (ri >> (sh + 1)) == (ci >> (sh + 1))
        off = jnp.where(outer & jnp.logical_not(inner), a_b, 0.0)
        tinv = tinv + _dot(_dot(tinv, off), tinv)

    h0 = h_ref[...]
    arh = _dot(ar, h0)
    u = _dot(tinv, arh[:n] + _dot(a_k, v_s))
    y_s = arh[n:] + _dot(b_k, v_s) + _dot(b_b, u)
    y_ref[0, 0] = y_s[:RW_CHUNK] + y_s[RW_CHUNK:]
    g_diag = jnp.where(eye, jnp.broadcast_to(jnp.exp(lc_end), (n, n)), 0.0)
    lhs = jnp.concatenate([g_diag, ke_s, be_s], axis=0)
    rhs = jnp.concatenate([h0, v_s, u], axis=0)
    h_ref[...] = _dot_tn(lhs, rhs)


def _rwkv_scan(r, v, kk, k2, b2, lw2):
    tok_spec = pl.BlockSpec((1, RW_CHUNK, PAIR_W), lambda d, b, p, s: (b, _rw_chunk_index(d, s), p))
    dir_spec = pl.BlockSpec((1, 1, RW_CHUNK, PAIR_W), lambda d, b, p, s: (d, b, _rw_chunk_index(d, s), p))
    return pl.pallas_call(
        _rwkv_scan_kernel,
        out_shape=jax.ShapeDtypeStruct((2, BATCH, TOK, RWKV_W), F32),
        grid=(2, BATCH, N_PAIRS, RW_NCHUNK),
        in_specs=[tok_spec, tok_spec, tok_spec, dir_spec, dir_spec, dir_spec],
        out_specs=dir_spec,
        scratch_shapes=[pltpu.VMEM((PAIR_W, PAIR_W), F32)],
        compiler_params=_params("arbitrary", "arbitrary", "arbitrary", "arbitrary"),
        name="rwkv_scan",
    )(r, v, kk, k2, b2, lw2)


def _dft_mats(n):
    i = np.arange(n)
    ang = 2.0 * np.pi * ((i[:, None] * i[None, :]) % n) / n
    return np.cos(ang).astype(np.float32), np.sin(ang).astype(np.float32)


def _chan_dft_mat():
    c, s = _dft_mats(FOUR_GROUP_W)
    eye = np.eye(FOUR_GROUPS, dtype=np.float32)
    return np.concatenate([np.kron(eye, c), np.kron(eye, s)], axis=1)


def _four_ctx_kernel(u_ref, cs_ref, c_ref, s_ref, o_ref):
    x = u_ref[0][:, RWKV_W:RWKV_W + FOUR_W]
    pq = _dot(x, cs_ref[...])
    scale = np.float32(1.0 / np.sqrt(CTX_LEN * FOUR_GROUP_W))
    o_ref[0] = (_dot(c_ref[...], pq[:, :FOUR_W]) - _dot(s_ref[...], pq[:, FOUR_W:])) * scale


def _four_ctx(u):
    c, s = _dft_mats(CTX_LEN)
    full = lambda shape: pl.BlockSpec(shape, lambda b: (0,) * len(shape))
    return pl.pallas_call(
        _four_ctx_kernel,
        out_shape=jax.ShapeDtypeStruct((BATCH, CTX_LEN, FOUR_W), F32),
        grid=(BATCH,),
        in_specs=[pl.BlockSpec((1, CTX_LEN, TAIL_W), lambda b: (b, CTX_TILE, N_SHIFT // TAIL_W)),
                  full((FOUR_W, 2 * FOUR_W)), full((CTX_LEN, CTX_LEN)), full((CTX_LEN, CTX_LEN))],
        out_specs=pl.BlockSpec((1, CTX_LEN, FOUR_W), lambda b: (b, 0, 0)),
        compiler_params=_params("arbitrary"),
        name="fourier_ctx",
    )(u, jnp.asarray(_chan_dft_mat()), jnp.asarray(c), jnp.asarray(s))


L2_PER_STEP = 8


def _four_lat_a_kernel(u_ref, cs_ref, c_ref, s_ref, tc_ref, ts_ref, zr_ref, zi_ref):
    cs = cs_ref[...]
    cm = c_ref[...]
    sm = s_ref[...]
    for i in range(L2_PER_STEP):
        x = u_ref[0, :, i, RWKV_W:RWKV_W + FOUR_W]
        pq = _dot(x, cs)
        p, q = pq[:, :FOUR_W], pq[:, FOUR_W:]
        zr = _dot(cm, p) - _dot(sm, q)
        zi = -_dot(cm, q) - _dot(sm, p)
        tc, ts = tc_ref[i], ts_ref[i]
        zr_ref[0, i] = zr * tc + zi * ts
        zi_ref[0, i] = zi * tc - zr * ts


def _four_lat_b_kernel(zr_ref, zi_ref, c_ref, s_ref, o_ref):
    scale = np.float32(1.0 / np.sqrt(SEQ * FOUR_GROUP_W))
    o_ref[0] = (_dot(c_ref[...], zr_ref[0]) + _dot(s_ref[...], zi_ref[0])) * scale


def _four_lat(u):
    c1, s1 = _dft_mats(GRID_ROWS)
    c2, s2 = _dft_mats(GRID_W)
    k1 = np.arange(GRID_ROWS)[None, :, None]
    l2 = np.arange(GRID_W)[:, None, None]
    ang = 2.0 * np.pi * (k1 * l2) / SEQ
    tc = np.broadcast_to(np.cos(ang), (GRID_W, GRID_ROWS, FOUR_W)).astype(np.float32)
    ts = np.broadcast_to(np.sin(ang), (GRID_W, GRID_ROWS, FOUR_W)).astype(np.float32)
    u4 = u.reshape(BATCH, TOK // GRID_W, GRID_W, EVEN_IN)
    full = lambda shape: pl.BlockSpec(shape, lambda b, g: (0,) * len(shape))
    zshape = jax.ShapeDtypeStruct((BATCH, GRID_W, GRID_ROWS, FOUR_W), F32)
    zspec = pl.BlockSpec((1, L2_PER_STEP, GRID_ROWS, FOUR_W), lambda b, g: (b, g, 0, 0))
    tspec = pl.BlockSpec((L2_PER_STEP, GRID_ROWS, FOUR_W), lambda b, g: (g, 0, 0))
    zr, zi = pl.pallas_call(
        _four_lat_a_kernel,
        out_shape=(zshape, zshape),
        grid=(BATCH, GRID_W // L2_PER_STEP),
        in_specs=[pl.BlockSpec((1, GRID_ROWS, L2_PER_STEP, TAIL_W), lambda b, g: (b, 0, g, N_SHIFT // TAIL_W)),
                  full((FOUR_W, 2 * FOUR_W)), full((GRID_ROWS, GRID_ROWS)), full((GRID_ROWS, GRID_ROWS)),
                  tspec, tspec],
        out_specs=(zspec, zspec),
        compiler_params=_params("arbitrary", "arbitrary"),
        name="fourier_lat_a",
    )(u4, jnp.asarray(_chan_dft_mat()), jnp.asarray(c1), jnp.asarray(s1), jnp.asarray(tc), jnp.asarray(ts))
    ncol = GRID_ROWS * FOUR_W
    cb = 4096
    full2 = lambda shape: pl.BlockSpec(shape, lambda b, g: (0,) * len(shape))
    zb = pl.BlockSpec((1, GRID_W, cb), lambda b, g: (b, 0, g))
    out = pl.pallas_call(
        _four_lat_b_kernel,
        out_shape=jax.ShapeDtypeStruct((BATCH, GRID_W, ncol), F32),
        grid=(BATCH, ncol // cb),
        in_specs=[zb, zb, full2((GRID_W, GRID_W)), full2((GRID_W, GRID_W))],
        out_specs=zb,
        compiler_params=_params("arbitrary", "arbitrary"),
        name="fourier_lat_b",
    )(zr.reshape(BATCH, GRID_W, ncol), zi.reshape(BATCH, GRID_W, ncol), jnp.asarray(c2), jnp.asarray(s2))
    return out.reshape(BATCH, SEQ, FOUR_W)


def _post_residual(x, out, npost, gate):
    y = out * lax.rsqrt(jnp.mean(out * out, axis=-1, keepdims=True) + EPS) * npost
    return x + gate * y


def _even_finish_kernel(y_ref, bonus_ref, tail_ref, four_ref, x_ref, ones_ref, g_ref, bta_ref, w_ref, np_ref,
                        mod_ref, o_ref):
    o = y_ref[0, 0] + y_ref[1, 0] + bonus_ref[0]
    ones_bd = ones_ref[...]
    inv = np.float32(1.0 / RWKV_HEAD)
    mean = _dot(o, ones_bd) * inv
    cen = o - mean
    var = _dot(cen * cen, ones_bd) * inv
    on = cen * lax.rsqrt(var + RWKV_GN_EPS) * g_ref[...] + bta_ref[...]
    tail = tail_ref[0]
    yr = on * _silu(tail[:, :RWKV_W])
    yf = four_ref[0] * _silu(tail[:, RWKV_W + FOUR_W:])
    z = jnp.concatenate([yr, yf], axis=-1).astype(BF16)
    out = jnp.dot(z, w_ref[...], preferred_element_type=F32)
    o_ref[0] = _post_residual(x_ref[0], out, np_ref[...], mod_ref[0, :, 2 * D_MODEL:])


def _even_finish(y2, bonus, u, four, xcomb, ones_bd, lnx_g, lnx_b, w_out_bf16, npost, mod):
    full = lambda shape: pl.BlockSpec(shape, lambda b, j: (0,) * len(shape))
    return pl.pallas_call(
        _even_finish_kernel,
        out_shape=jax.ShapeDtypeStruct((BATCH, TOK, D_MODEL), F32),
        grid=(BATCH, N_TILES),
        in_specs=[
            pl.BlockSpec((2, 1, TILE, RWKV_W), lambda b, j: (0, b, j, 0)),
            pl.BlockSpec((1, TILE, RWKV_W), lambda b, j: (b, j, 0)),
            pl.BlockSpec((1, TILE, TAIL_W), lambda b, j: (b, j, N_SHIFT // TAIL_W)),
            pl.BlockSpec((1, TILE, FOUR_W), lambda b, j: (b, j, 0)),
            pl.BlockSpec((1, TILE, D_MODEL), lambda b, j: (b, j, 0)),
            full((RWKV_W, RWKV_W)), full((1, RWKV_W)), full((1, RWKV_W)), full((D_MODEL, D_MODEL)),
            full((1, D_MODEL)),
            pl.BlockSpec((1, 1, 3 * D_MODEL), lambda b, j: (_mod_row(b, j), 0, 0)),
        ],
        out_specs=pl.BlockSpec((1, TILE, D_MODEL), lambda b, j: (b, j, 0)),
        compiler_params=_params("arbitrary", "arbitrary"),
        name="even_finish",
    )(y2, bonus, u, four, xcomb, ones_bd, lnx_g.reshape(1, RWKV_W), lnx_b.reshape(1, RWKV_W), w_out_bf16,
      npost.reshape(1, D_MODEL), mod.reshape(8, 1, 3 * D_MODEL))


def _ret_chunk_index(d, s):
    lat = jnp.where(d == 0, s - 1, RET_NCHUNK - 1 - s)
    return jnp.where(s == 0, RET_NCHUNK - 1, lat)


def _ret_scan_kernel(q_ref, k_ref, v_ref, cos_ref, sin_ref, dl_ref, o_ref, st_ref):
    d = pl.program_id(0)
    s = pl.program_id(3)
    c = RET_CHUNK

    @pl.when(s == 0)
    def _():
        st_ref[...] = jnp.zeros_like(st_ref)

    rev = d == 1
    x = dl_ref[0, 0]
    lg = jnp.minimum(x, 0.0) - jnp.log(1.0 + jnp.exp(-jnp.abs(x)))
    cosf, sinf = cos_ref[...], sin_ref[...]

    def rope(t):
        return t * cosf + pltpu.roll(t, RET_HEAD // 2, 1) * sinf

    q = rope(q_ref[0])
    k = rope(k_ref[0]) * np.float32(RET_HEAD ** -0.5)
    v = v_ref[0]
    ri = lax.broadcasted_iota(jnp.int32, (c, c), 0)
    ci = lax.broadcasted_iota(jnp.int32, (c, c), 1)
    diff = jnp.where(rev, ci - ri, ri - ci)
    mask = diff >= jnp.where(rev, 1, 0)
    lg_cc = jnp.broadcast_to(lg[:, :1], (c, c))
    dmat = jnp.where(mask, jnp.exp(jnp.where(mask, diff, 0).astype(F32) * lg_cc), 0.0)
    sc = _dot_nt(q, k) * dmat
    ti = lax.broadcasted_iota(jnp.int32, (c, RET_HEAD), 0)
    p = jnp.where(rev, c - 1 - ti, ti).astype(F32)
    xi = jnp.exp((p + 1.0) * lg)
    zeta = jnp.exp((c - 1.0 - p) * lg)
    st = st_ref[...]
    o_ref[0, 0] = _dot(sc, v) + _dot(q, st) * xi
    st_ref[...] = st * jnp.exp(np.float32(c) * lg) + _dot_tn(k * zeta, v)


def _ret_scan(u2, cosf, sinf, dl):
    def tok_spec(off):
        return pl.BlockSpec((1, RET_CHUNK, RET_HEAD), lambda d, b, h, s: (b, _ret_chunk_index(d, s), off + h))
    rope_spec = pl.BlockSpec((RET_CHUNK, RET_HEAD), lambda d, b, h, s: (_ret_chunk_index(d, s), 0))
    return pl.pallas_call(
        _ret_scan_kernel,
        out_shape=jax.ShapeDtypeStruct((2, BATCH, TOK, RET_W), F32),
        grid=(2, BATCH, RET_HEADS, RET_NCHUNK),
        in_specs=[tok_spec(0), tok_spec(RET_HEADS), tok_spec(2 * RET_HEADS), rope_spec, rope_spec,
                  pl.BlockSpec((1, 1, 1, RET_HEAD), lambda d, b, h, s: (d, h, 0, 0))],
        out_specs=pl.BlockSpec((1, 1, RET_CHUNK, RET_HEAD), lambda d, b, h, s: (d, b, _ret_chunk_index(d, s), h)),
        scratch_shapes=[pltpu.VMEM((RET_HEAD, RET_HEAD), F32)],
        compiler_params=_params("arbitrary", "arbitrary", "arbitrary", "arbitrary"),
        name="ret_scan",
    )(u2, u2, u2, cosf, sinf, dl)


def _rope_tables():
    pos = np.arange(SEQ)
    half = RET_HEAD // 2
    inv = ROPE_BASE ** (-np.arange(0, half, 2, dtype=np.float32) / half)
    ang = np.concatenate([(pos // GRID_W)[:, None] * inv[None, :], (pos % GRID_W)[:, None] * inv[None, :]], axis=1)
    cos = np.concatenate([np.cos(ang), np.ones((CTX_LEN, half))], axis=0).astype(np.float32)
    sin = np.concatenate([np.sin(ang), np.zeros((CTX_LEN, half))], axis=0).astype(np.float32)
    return np.concatenate([cos, cos], axis=1), np.concatenate([-sin, sin], axis=1)


def _odd_finish_kernel(o_ref, g_ref, x_ref, w_ref, np_ref, mod_ref, out_ref):
    o = o_ref[0, 0] + o_ref[1, 0]
    parts = []
    for h in range(RET_HEADS):
        oh = o[:, h * RET_HEAD:(h + 1) * RET_HEAD]
        parts.append(oh * lax.rsqrt(jnp.mean(oh * oh, axis=-1, keepdims=True) + EPS))
    z = (jnp.concatenate(parts, axis=-1) * _silu(g_ref[0])).astype(BF16)
    out = jnp.dot(z, w_ref[...], preferred_element_type=F32)
    out_ref[0] = _post_residual(x_ref[0], out, np_ref[...], mod_ref[0, :, 2 * D_MODEL:])


def _odd_finish(o2, u2, xcomb, w_out_bf16, npost, mod):
    full = lambda shape: pl.BlockSpec(shape, lambda b, j: (0,) * len(shape))
    return pl.pallas_call(
        _odd_finish_kernel,
        out_shape=jax.ShapeDtypeStruct((BATCH, SEQ, D_MODEL), F32),
        grid=(BATCH, SEQ // TILE),
        in_specs=[
            pl.BlockSpec((2, 1, TILE, RET_W), lambda b, j: (0, b, j, 0)),
            pl.BlockSpec((1, TILE, RET_W), lambda b, j: (b, j, 3)),
            pl.BlockSpec((1, TILE, D_MODEL), lambda b, j: (b, j, 0)),
            full((RET_W, D_MODEL)), full((1, D_MODEL)),
            pl.BlockSpec((1, 1, 3 * D_MODEL), lambda b, j: (b, 0, 0)),
        ],
        out_specs=pl.BlockSpec((1, TILE, D_MODEL), lambda b, j: (b, j, 0)),
        compiler_params=_params("arbitrary", "arbitrary"),
        name="odd_finish",
    )(o2, u2, xcomb, w_out_bf16, npost.reshape(1, D_MODEL), mod.reshape(8, 1, 3 * D_MODEL))


def _head_ones():
    return np.kron(np.eye(RWKV_HEADS, dtype=np.float32), np.ones((RWKV_HEAD, RWKV_HEAD), np.float32))


def kernel(x, c, ctx, c_ctx, ada_w, ada_b, norm_pre, norm_post, ev_w_in, ev_mu, ev_w0, ev_w2, ev_a0, ev_a2, ev_k_k, ev_k_a, ev_r_k, ev_lnx_g, ev_lnx_b, ev_w_out, od_w_in, od_decay_logit, od_w_out):
    cvec = jnp.concatenate([c, c_ctx[None, :], jnp.zeros((8 - BATCH - 1, D_MODEL), F32)], axis=0)
    mod = _ada_mod(cvec, ada_w, ada_b)
    xcomb = jnp.concatenate([x, ctx], axis=1)
    ones_bd = jnp.asarray(_head_ones())

    u = _norm_proj(xcomb, norm_pre[0], mod[0], ev_w_in[0].astype(BF16))
    r, v, kk, k2, b2, lw2, bonus = _even_terms(u, ev_mu[0], ev_w0[0], ev_w2[0], ev_a0[0], ev_a2[0], ev_k_k[0],
                                                ev_k_a[0], ev_r_k[0].reshape(RWKV_W), ones_bd)
    y2 = _rwkv_scan(r, v, kk, k2, b2, lw2)
    four = jnp.concatenate([_four_lat(u), _four_ctx(u)], axis=1)
    xcomb = _even_finish(y2, bonus, u, four, xcomb, ones_bd, ev_lnx_g[0], ev_lnx_b[0], ev_w_out[0].astype(BF16),
                         norm_post[0], mod[0])

    u2 = _norm_proj(xcomb, norm_pre[1], mod[1], od_w_in[0].astype(BF16))
    cosf, sinf = _rope_tables()
    dl = jnp.broadcast_to(od_decay_logit[0][:, :, None, None], (2, RET_HEADS, 1, RET_HEAD))
    o2 = _ret_scan(u2, jnp.asarray(cosf), jnp.asarray(sinf), dl)
    return _odd_finish(o2, u2, xcomb, od_w_out[0].astype(BF16), norm_post[1], mod[1])
```

```python
import functools

import numpy as np
import jax
import jax.numpy as jnp
from jax import lax
from jax.experimental import pallas as pl
from jax.experimental.pallas import tpu as pltpu

F32 = jnp.float32
BF16 = jnp.bfloat16
HIGHEST = lax.Precision.HIGHEST

D_MODEL = 1024
BATCH = 2
SEQ = 8192
GRID_W = 64
GRID_ROWS = SEQ // GRID_W
CTX_LEN = 256
TOK = SEQ + CTX_LEN
EPS = 1e-6

RWKV_HEAD = 64
RWKV_HEADS = 12
RWKV_W = RWKV_HEADS * RWKV_HEAD
LORA = 64
FOUR_GROUPS = 4
FOUR_W = 256
FOUR_GROUP_W = 64
N_SHIFT = 3 * RWKV_W + 4 * LORA
EVEN_IN = N_SHIFT + RWKV_W + 2 * FOUR_W
TAIL_W = EVEN_IN - N_SHIFT
RWKV_GN_EPS = 64e-5

RET_HEADS = 8
RET_HEAD = 128
RET_W = 1024
ODD_IN = 4 * RET_W
ROPE_BASE = 10000.0

TILE = 256
N_TILES = TOK // TILE
CTX_TILE = N_TILES - 1
RW_CHUNK = 64
RW_NCHUNK = TOK // RW_CHUNK
RW_CTX_CHUNKS = CTX_LEN // RW_CHUNK
RW_LAT_CHUNKS = SEQ // RW_CHUNK
PAIR_W = 2 * RWKV_HEAD
N_PAIRS = RWKV_W // PAIR_W
RW_PAIRS_PER_STEP = 6
RET_CHUNK = 256
RET_NCHUNK = TOK // RET_CHUNK
VMEM_LIMIT = 56 * 1024 * 1024


def _dot(a, b, precision=HIGHEST):
    return jnp.dot(a, b, precision=precision, preferred_element_type=F32)


def _dot_nt(a, b, precision=HIGHEST):
    return lax.dot_general(a, b, (((1,), (1,)), ((), ())), precision=precision, preferred_element_type=F32)


def _dot_tn(a, b, precision=HIGHEST):
    return lax.dot_general(a, b, (((0,), (0,)), ((), ())), precision=precision, preferred_element_type=F32)


def _split_bf16(x):
    hi = x.astype(BF16)
    return hi, (x - hi.astype(F32)).astype(BF16)


def _dot3(a, b):
    ah, al = _split_bf16(a)
    bh, bl = _split_bf16(b)
    return _dot(ah, bh, None) + (_dot(ah, bl, None) + _dot(al, bh, None))


def _head_sum(x, ones_bd):
    hi, lo = _split_bf16(x)
    return _dot(hi, ones_bd, None) + _dot(lo, ones_bd, None)


def _silu(x):
    return x * (1.0 / (1.0 + jnp.exp(-x)))


def _sigmoid(x):
    return 1.0 / (1.0 + jnp.exp(-x))


def _params(*sem):
    return pltpu.CompilerParams(dimension_semantics=sem, vmem_limit_bytes=VMEM_LIMIT)


def _ada_kernel(c_ref, w_ref, b_ref, o_ref):
    o_ref[0] = _dot(_silu(c_ref[...]), w_ref[0]) + b_ref[0]


def _ada_mod(cvec, ada_w, ada_b):
    depth = ada_w.shape[0]
    nt = 3 * D_MODEL // 1024
    return pl.pallas_call(
        _ada_kernel,
        out_shape=jax.ShapeDtypeStruct((depth, 8, 3 * D_MODEL), F32),
        grid=(depth, nt),
        in_specs=[
            pl.BlockSpec((8, D_MODEL), lambda l, n: (0, 0)),
            pl.BlockSpec((1, D_MODEL, 1024), lambda l, n: (l, 0, n)),
            pl.BlockSpec((1, 1, 1024), lambda l, n: (l, 0, n)),
        ],
        out_specs=pl.BlockSpec((1, 8, 1024), lambda l, n: (l, 0, n)),
        compiler_params=_params("arbitrary", "arbitrary"),
        name="ada_mod",
    )(cvec, ada_w, ada_b.reshape(depth, 1, 3 * D_MODEL))


def _mod_row(b, j):
    return jnp.where(j == CTX_TILE, BATCH, b)


def _norm_proj_kernel(x_ref, g_ref, mod_ref, w_ref, o_ref):
    x = x_ref[0]
    y = x * lax.rsqrt(jnp.mean(x * x, axis=-1, keepdims=True) + EPS) * g_ref[...]
    shift = mod_ref[0, :, 0:D_MODEL]
    scale = mod_ref[0, :, D_MODEL:2 * D_MODEL]
    h = y * (1.0 + scale) + shift
    o_ref[0] = jnp.dot(h.astype(BF16), w_ref[...], preferred_element_type=F32)


def _norm_proj(xcomb, g, mod, w_bf16):
    n_out = w_bf16.shape[1]
    return pl.pallas_call(
        _norm_proj_kernel,
        out_shape=jax.ShapeDtypeStruct((BATCH, TOK, n_out), F32),
        grid=(BATCH, N_TILES),
        in_specs=[
            pl.BlockSpec((1, TILE, D_MODEL), lambda b, j: (b, j, 0)),
            pl.BlockSpec((1, D_MODEL), lambda b, j: (0, 0)),
            pl.BlockSpec((1, 1, 3 * D_MODEL), lambda b, j: (_mod_row(b, j), 0, 0)),
            pl.BlockSpec((D_MODEL, n_out), lambda b, j: (0, 0)),
        ],
        out_specs=pl.BlockSpec((1, TILE, n_out), lambda b, j: (b, j, 0)),
        compiler_params=_params("arbitrary", "arbitrary"),
        name="norm_proj",
    )(xcomb, g.reshape(1, D_MODEL), mod.reshape(8, 1, 3 * D_MODEL), w_bf16)


def _even_terms_kernel(u_ref, up_ref, dn_ref, mu_ref, w0_ref, w2_ref, a0_ref, a2_ref, kk_ref_, ka_ref, rk_ref,
                       ones_ref, r_out, v_out, kk_out, k_out, b_out, lw_out, bonus_out):
    j = pl.program_id(1)
    is_ctx = j == CTX_TILE
    x = u_ref[0]
    t = lax.broadcasted_iota(jnp.int32, (TILE, N_SHIFT), 0)
    lane = lax.broadcasted_iota(jnp.int32, (TILE, N_SHIFT), 1)
    left = pltpu.roll(x, 1, 0)
    right = pltpu.roll(x, TILE - 1, 0)
    edge = jnp.where(is_ctx, TILE - 1, GRID_W - 1)
    left = jnp.where((t & edge) == 0, 0.0, left)
    right = jnp.where((t & edge) == edge, 0.0, right)
    up = jnp.concatenate([up_ref[0], x[:TILE - GRID_W]], axis=0)
    down = jnp.concatenate([x[GRID_W:], dn_ref[0]], axis=0)
    up = jnp.where(t < jnp.where(j == 0, GRID_W, 0), 0.0, up)
    down = jnp.where(t >= jnp.where(j == CTX_TILE - 1, TILE - GRID_W, TILE), 0.0, down)
    sel = jnp.where(is_ctx, lane & 1, lane & 3)
    shifted = jnp.where(sel == 0, left, jnp.where(sel == 1, right, jnp.where(sel == 2, up, down)))
    s = x + (shifted - x) * mu_ref[...]

    r = s[:, 0:RWKV_W]
    k0 = s[:, RWKV_W:2 * RWKV_W]
    v = s[:, 2 * RWKV_W:3 * RWKV_W]
    ones_bd = ones_ref[...]
    kk = k0 * kk_ref_[...]
    norm = jnp.sqrt(_head_sum(kk * kk, ones_bd))
    kk = kk / jnp.maximum(norm, 1e-12)
    r_out[0] = r
    v_out[0] = v
    kk_out[0] = kk
    bonus = jnp.zeros_like(v)
    for d in range(2):
        wl = s[:, 3 * RWKV_W + d * LORA:3 * RWKV_W + (d + 1) * LORA]
        al = s[:, 3 * RWKV_W + (2 + d) * LORA:3 * RWKV_W + (3 + d) * LORA]
        z = w0_ref[d:d + 1, :] + _dot(jnp.tanh(wl).astype(BF16), w2_ref[d].astype(BF16), None)
        lw_out[d, 0] = -np.float32(np.exp(-0.5)) * _sigmoid(z)
        a = _sigmoid(a0_ref[d:d + 1, :] + _dot(al.astype(BF16), a2_ref[d].astype(BF16), None))
        k = k0 * (1.0 + (a - 1.0) * ka_ref[...])
        k_out[d, 0] = k
        b_out[d, 0] = kk * a
        bonus = bonus + _head_sum(r * k * rk_ref[...], ones_bd) * v
    bonus_out[0] = bonus


def _even_terms(u, mu, w0, w2, a0, a2, k_k, k_a, r_k, ones_bd):
    tok = jax.ShapeDtypeStruct((BATCH, TOK, RWKV_W), F32)
    tok2 = jax.ShapeDtypeStruct((2, BATCH, TOK, RWKV_W), F32)
    per = TILE // GRID_W
    last64 = TOK // GRID_W - 1
    full = lambda shape: pl.BlockSpec(shape, lambda b, j: (0,) * len(shape))
    tspec = pl.BlockSpec((1, TILE, RWKV_W), lambda b, j: (b, j, 0))
    t2spec = pl.BlockSpec((2, 1, TILE, RWKV_W), lambda b, j: (0, b, j, 0))
    return pl.pallas_call(
        _even_terms_kernel,
        out_shape=(tok, tok, tok, tok2, tok2, tok2, tok),
        grid=(BATCH, N_TILES),
        in_specs=[
            pl.BlockSpec((1, TILE, N_SHIFT), lambda b, j: (b, j, 0)),
            pl.BlockSpec((1, GRID_W, N_SHIFT), lambda b, j: (b, jnp.maximum(j * per - 1, 0), 0)),
            pl.BlockSpec((1, GRID_W, N_SHIFT), lambda b, j: (b, jnp.minimum(j * per + per, last64), 0)),
            full((1, N_SHIFT)), full((2, RWKV_W)), full((2, LORA, RWKV_W)), full((2, RWKV_W)),
            full((2, LORA, RWKV_W)), full((1, RWKV_W)), full((1, RWKV_W)), full((1, RWKV_W)),
            full((RWKV_W, RWKV_W)),
        ],
        out_specs=(tspec, tspec, tspec, t2spec, t2spec, t2spec, tspec),
        compiler_params=_params("arbitrary", "arbitrary"),
        name="even_terms",
    )(u, u, u, mu.reshape(1, N_SHIFT), w0, w2, a0, a2, k_k.reshape(1, RWKV_W), k_a.reshape(1, RWKV_W),
      r_k.reshape(1, RWKV_W), ones_bd)


def _rw_chunk_index(d, s):
    fwd = jnp.where(s < RW_CTX_CHUNKS, RW_LAT_CHUNKS + s, s - RW_CTX_CHUNKS)
    return jnp.where(d == 0, fwd, RW_NCHUNK - 1 - s)


def _rwkv_scan_kernel(r_ref, v_ref, kk_ref, k_ref, b_ref, lw_ref, y_ref, h_ref):
    d = pl.program_id(0)
    s = pl.program_id(3)
    n = 2 * RW_CHUNK

    @pl.when(s == 0)
    def _():
        h_ref[...] = jnp.zeros_like(h_ref)

    rev = d == 1
    ri = lax.broadcasted_iota(jnp.int32, (n, n), 0)
    ci = lax.broadcasted_iota(jnp.int32, (n, n), 1)
    pr = jnp.where(rev, RW_CHUNK - 1 - (ri & (RW_CHUNK - 1)), ri & (RW_CHUNK - 1))
    pc = jnp.where(rev, RW_CHUNK - 1 - (ci & (RW_CHUNK - 1)), ci & (RW_CHUNK - 1))
    same = (ri >> 6) == (ci >> 6)
    strict = same & (pc < pr)
    incl = same & (pc <= pr)
    eye = ri == ci
    ident = jnp.where(eye, 1.0, 0.0)
    blk8 = (ri >> 3) == (ci >> 3)
    merges = []
    for sh in (3, 4, 5):
        inner = (ri >> sh) == (ci >> sh)
        outer = (ri >> (sh + 1)) == (ci >> (sh + 1))
        merges.append(outer & jnp.logical_not(inner))
    cum = jnp.where(incl[:RW_CHUNK, :RW_CHUNK], 1.0, 0.0)
    head0 = lax.broadcasted_iota(jnp.int32, (RW_CHUNK, PAIR_W), 1) < RWKV_HEAD

    def stack(x):
        xb = x.astype(BF16)
        zero = jnp.zeros_like(xb)
        return jnp.concatenate([jnp.where(head0, xb, zero), jnp.where(head0, zero, xb)], axis=0)

    pairs = range(RW_PAIRS_PER_STEP)
    col = [slice(p * PAIR_W, (p + 1) * PAIR_W) for p in pairs]
    lw_all = lw_ref[0, 0]
    lc_all = _dot(cum, lw_all)
    ar, kb, ke, v_s, g_col = [], [], [], [], []
    for p in pairs:
        lw, lc = lw_all[:, col[p]], lc_all[:, col[p]]
        lc_end = jnp.sum(lw, axis=0, keepdims=True)
        g_inv = jnp.exp(-lc)
        g_end = jnp.exp(lc_end - lc)
        kc, bc = k_ref[0, 0, :, col[p]], b_ref[0, 0, :, col[p]]
        at_s = stack(-kk_ref[0, :, col[p]] * jnp.exp(lc - lw))
        rt_s = stack(r_ref[0, :, col[p]] * jnp.exp(lc))
        ar.append(jnp.concatenate([at_s, rt_s], axis=0))
        kb.append(jnp.concatenate([stack(kc * g_inv), stack(bc * g_inv)], axis=0))
        ke.append(jnp.concatenate([stack(kc * g_end), stack(bc * g_end)], axis=0))
        v_s.append(stack(v_ref[0, :, col[p]]))
        g_col.append(jnp.sum(jnp.where(eye, jnp.broadcast_to(jnp.exp(lc_end), (n, n)), 0.0), axis=1, keepdims=True))

    quad = [_dot_nt(ar[p], kb[p], None) for p in pairs]
    a_b = [jnp.where(strict, quad[p][:n, n:], 0.0) for p in pairs]
    akbk = [jnp.concatenate([jnp.where(strict, quad[p][:n, :n], 0.0), jnp.where(incl, quad[p][n:, :n], 0.0)],
                            axis=0).astype(BF16) for p in pairs]
    b_b = [jnp.where(incl, quad[p][n:, n:], 0.0).astype(BF16) for p in pairs]
    h0 = [h_ref[p] for p in pairs]
    arh = [_dot(ar[p], h0[p].astype(BF16), None) for p in pairs]
    abv = [_dot(akbk[p], v_s[p], None) for p in pairs]

    n1 = [jnp.where(blk8, a_b[p], 0.0) for p in pairs]
    n2 = [_dot3(n1[p], n1[p]) for p in pairs]
    n4 = [_dot3(n2[p], n2[p]) for p in pairs]
    t12 = [_dot3(ident + n1[p], ident + n2[p]) for p in pairs]
    tinv = [_dot3(t12[p], ident + n4[p]) for p in pairs]
    for m in merges:
        off = [jnp.where(m, a_b[p], 0.0) for p in pairs]
        half = [_dot3(tinv[p], off[p]) for p in pairs]
        tinv = [tinv[p] + _dot3(half[p], tinv[p]) for p in pairs]

    ub = [_dot(tinv[p].astype(BF16), (arh[p][:n] + abv[p][:n]).astype(BF16), None).astype(BF16) for p in pairs]
    bbu = [_dot(b_b[p], ub[p], None) for p in pairs]
    upd = [_dot_tn(ke[p], jnp.concatenate([v_s[p], ub[p]], axis=0), None) for p in pairs]
    for p in pairs:
        y_s = arh[p][n:] + abv[p][n:] + bbu[p]
        y_ref[0, 0, :, col[p]] = y_s[:RW_CHUNK] + y_s[RW_CHUNK:]
        h_ref[p] = h0[p] * g_col[p] + upd[p]


def _rwkv_scan(r, v, kk, k2, b2, lw2):
    w = PAIR_W * RW_PAIRS_PER_STEP
    tok_spec = pl.BlockSpec((1, RW_CHUNK, w), lambda d, b, p, s: (b, _rw_chunk_index(d, s), p))
    dir_spec = pl.BlockSpec((1, 1, RW_CHUNK, w), lambda d, b, p, s: (d, b, _rw_chunk_index(d, s), p))
    return pl.pallas_call(
        _rwkv_scan_kernel,
        out_shape=jax.ShapeDtypeStruct((2, BATCH, TOK, RWKV_W), F32),
        grid=(2, BATCH, N_PAIRS // RW_PAIRS_PER_STEP, RW_NCHUNK),
        in_specs=[tok_spec, tok_spec, tok_spec, dir_spec, dir_spec, dir_spec],
        out_specs=dir_spec,
        scratch_shapes=[pltpu.VMEM((RW_PAIRS_PER_STEP, PAIR_W, PAIR_W), F32)],
        compiler_params=_params("arbitrary", "arbitrary", "arbitrary", "arbitrary"),
        name="rwkv_scan",
    )(r, v, kk, k2, b2, lw2)


def _dft_mats(n):
    i = np.arange(n)
    ang = 2.0 * np.pi * ((i[:, None] * i[None, :]) % n) / n
    return np.cos(ang).astype(np.float32), np.sin(ang).astype(np.float32)


def _chan_dft_mat():
    c, s = _dft_mats(FOUR_GROUP_W)
    eye = np.eye(FOUR_GROUPS, dtype=np.float32)
    return np.concatenate([np.kron(eye, c), np.kron(eye, s)], axis=1)


def _four_ctx_kernel(u_ref, cs_ref, c_ref, s_ref, o_ref):
    x = u_ref[0][:, RWKV_W:RWKV_W + FOUR_W]
    pq = _dot(x, cs_ref[...])
    scale = np.float32(1.0 / np.sqrt(CTX_LEN * FOUR_GROUP_W))
    o_ref[0] = (_dot(c_ref[...], pq[:, :FOUR_W]) - _dot(s_ref[...], pq[:, FOUR_W:])) * scale


def _four_ctx(u):
    c, s = _dft_mats(CTX_LEN)
    full = lambda shape: pl.BlockSpec(shape, lambda b: (0,) * len(shape))
    return pl.pallas_call(
        _four_ctx_kernel,
        out_shape=jax.ShapeDtypeStruct((BATCH, CTX_LEN, FOUR_W), F32),
        grid=(BATCH,),
        in_specs=[pl.BlockSpec((1, CTX_LEN, TAIL_W), lambda b: (b, CTX_TILE, N_SHIFT // TAIL_W)),
                  full((FOUR_W, 2 * FOUR_W)), full((CTX_LEN, CTX_LEN)), full((CTX_LEN, CTX_LEN))],
        out_specs=pl.BlockSpec((1, CTX_LEN, FOUR_W), lambda b: (b, 0, 0)),
        compiler_params=_params("arbitrary"),
        name="fourier_ctx",
    )(u, jnp.asarray(_chan_dft_mat()), jnp.asarray(c), jnp.asarray(s))


L2_PER_STEP = 8


def _four_lat_a_kernel(u_ref, cs_ref, c_ref, s_ref, tc_ref, ts_ref, zr_ref, zi_ref):
    cs = cs_ref[...]
    cm = c_ref[...]
    sm = s_ref[...]
    for i in range(L2_PER_STEP):
        x = u_ref[0, :, i, RWKV_W:RWKV_W + FOUR_W]
        pq = _dot(x, cs)
        p, q = pq[:, :FOUR_W], pq[:, FOUR_W:]
        zr = _dot(cm, p) - _dot(sm, q)
        zi = -_dot(cm, q) - _dot(sm, p)
        tc, ts = tc_ref[i], ts_ref[i]
        zr_ref[0, i] = zr * tc + zi * ts
        zi_ref[0, i] = zi * tc - zr * ts


def _four_lat_b_kernel(zr_ref, zi_ref, c_ref, s_ref, o_ref):
    scale = np.float32(1.0 / np.sqrt(SEQ * FOUR_GROUP_W))
    o_ref[0] = (_dot(c_ref[...], zr_ref[0]) + _dot(s_ref[...], zi_ref[0])) * scale


def _four_lat(u):
    c1, s1 = _dft_mats(GRID_ROWS)
    c2, s2 = _dft_mats(GRID_W)
    k1 = np.arange(GRID_ROWS)[None, :, None]
    l2 = np.arange(GRID_W)[:, None, None]
    ang = 2.0 * np.pi * (k1 * l2) / SEQ
    tc = np.broadcast_to(np.cos(ang), (GRID_W, GRID_ROWS, FOUR_W)).astype(np.float32)
    ts = np.broadcast_to(np.sin(ang), (GRID_W, GRID_ROWS, FOUR_W)).astype(np.float32)
    u4 = u.reshape(BATCH, TOK // GRID_W, GRID_W, EVEN_IN)
    full = lambda shape: pl.BlockSpec(shape, lambda b, g: (0,) * len(shape))
    zshape = jax.ShapeDtypeStruct((BATCH, GRID_W, GRID_ROWS, FOUR_W), F32)
    zspec = pl.BlockSpec((1, L2_PER_STEP, GRID_ROWS, FOUR_W), lambda b, g: (b, g, 0, 0))
    tspec = pl.BlockSpec((L2_PER_STEP, GRID_ROWS, FOUR_W), lambda b, g: (g, 0, 0))
    zr, zi = pl.pallas_call(
        _four_lat_a_kernel,
        out_shape=(zshape, zshape),
        grid=(BATCH, GRID_W // L2_PER_STEP),
        in_specs=[pl.BlockSpec((1, GRID_ROWS, L2_PER_STEP, TAIL_W), lambda b, g: (b, 0, g, N_SHIFT // TAIL_W)),
                  full((FOUR_W, 2 * FOUR_W)), full((GRID_ROWS, GRID_ROWS)), full((GRID_ROWS, GRID_ROWS)),
                  tspec, tspec],
        out_specs=(zspec, zspec),
        compiler_params=_params("arbitrary", "arbitrary"),
        name="fourier_lat_a",
    )(u4, jnp.asarray(_chan_dft_mat()), jnp.asarray(c1), jnp.asarray(s1), jnp.asarray(tc), jnp.asarray(ts))
    ncol = GRID_ROWS * FOUR_W
    cb = 4096
    full2 = lambda shape: pl.BlockSpec(shape, lambda b, g: (0,) * len(shape))
    zb = pl.BlockSpec((1, GRID_W, cb), lambda b, g: (b, 0, g))
    out = pl.pallas_call(
        _four_lat_b_kernel,
        out_shape=jax.ShapeDtypeStruct((BATCH, GRID_W, ncol), F32),
        grid=(BATCH, ncol // cb),
        in_specs=[zb, zb, full2((GRID_W, GRID_W)), full2((GRID_W, GRID_W))],
        out_specs=zb,
        compiler_params=_params("arbitrary", "arbitrary"),
        name="fourier_lat_b",
    )(zr.reshape(BATCH, GRID_W, ncol), zi.reshape(BATCH, GRID_W, ncol), jnp.asarray(c2), jnp.asarray(s2))
    return out.reshape(BATCH, SEQ, FOUR_W)


def _post_residual(x, out, npost, gate):
    y = out * lax.rsqrt(jnp.mean(out * out, axis=-1, keepdims=True) + EPS) * npost
    return x + gate * y


def _even_finish_kernel(y_ref, bonus_ref, tail_ref, four_ref, x_ref, ones_ref, g_ref, bta_ref, w_ref, np_ref,
                        mod_ref, o_ref):
    o = y_ref[0, 0] + y_ref[1, 0] + bonus_ref[0]
    ones_bd = ones_ref[...]
    inv = np.float32(1.0 / RWKV_HEAD)
    mean = _head_sum(o, ones_bd) * inv
    cen = o - mean
    var = _head_sum(cen * cen, ones_bd) * inv
    on = cen * lax.rsqrt(var + RWKV_GN_EPS) * g_ref[...] + bta_ref[...]
    tail = tail_ref[0]
    yr = on * _silu(tail[:, :RWKV_W])
    yf = four_ref[0] * _silu(tail[:, RWKV_W + FOUR_W:])
    z = jnp.concatenate([yr, yf], axis=-1).astype(BF16)
    out = jnp.dot(z, w_ref[...], preferred_element_type=F32)
    o_ref[0] = _post_residual(x_ref[0], out, np_ref[...], mod_ref[0, :, 2 * D_MODEL:])


def _even_finish(y2, bonus, u, four, xcomb, ones_bd, lnx_g, lnx_b, w_out_bf16, npost, mod):
    full = lambda shape: pl.BlockSpec(shape, lambda b, j: (0,) * len(shape))
    return pl.pallas_call(
        _even_finish_kernel,
        out_shape=jax.ShapeDtypeStruct((BATCH, TOK, D_MODEL), F32),
        grid=(BATCH, N_TILES),
        in_specs=[
            pl.BlockSpec((2, 1, TILE, RWKV_W), lambda b, j: (0, b, j, 0)),
            pl.BlockSpec((1, TILE, RWKV_W), lambda b, j: (b, j, 0)),
            pl.BlockSpec((1, TILE, TAIL_W), lambda b, j: (b, j, N_SHIFT // TAIL_W)),
            pl.BlockSpec((1, TILE, FOUR_W), lambda b, j: (b, j, 0)),
            pl.BlockSpec((1, TILE, D_MODEL), lambda b, j: (b, j, 0)),
            full((RWKV_W, RWKV_W)), full((1, RWKV_W)), full((1, RWKV_W)), full((D_MODEL, D_MODEL)),
            full((1, D_MODEL)),
            pl.BlockSpec((1, 1, 3 * D_MODEL), lambda b, j: (_mod_row(b, j), 0, 0)),
        ],
        out_specs=pl.BlockSpec((1, TILE, D_MODEL), lambda b, j: (b, j, 0)),
        compiler_params=_params("arbitrary", "arbitrary"),
        name="even_finish",
    )(y2, bonus, u, four, xcomb, ones_bd, lnx_g.reshape(1, RWKV_W), lnx_b.reshape(1, RWKV_W), w_out_bf16,
      npost.reshape(1, D_MODEL), mod.reshape(8, 1, 3 * D_MODEL))


def _ret_chunk_index(d, s):
    lat = jnp.where(d == 0, s - 1, RET_NCHUNK - 1 - s)
    return jnp.where(s == 0, RET_NCHUNK - 1, lat)


def _ret_scan_kernel(q_ref, k_ref, v_ref, cos_ref, sin_ref, dl_ref, o_ref, st_ref):
    d = pl.program_id(0)
    s = pl.program_id(2)
    c = RET_CHUNK

    @pl.when(s == 0)
    def _():
        st_ref[...] = jnp.zeros_like(st_ref)

    rev = d == 1
    cosf, sinf = cos_ref[...], sin_ref[...]
    ri = lax.broadcasted_iota(jnp.int32, (c, c), 0)
    ci = lax.broadcasted_iota(jnp.int32, (c, c), 1)
    diff = jnp.where(rev, ci - ri, ri - ci)
    mask = diff >= jnp.where(rev, 1, 0)
    dist = jnp.where(mask, diff, 0).astype(F32)
    ti = lax.broadcasted_iota(jnp.int32, (c, RET_HEAD), 0)
    pos = jnp.where(rev, c - 1 - ti, ti).astype(F32)

    def rope(t):
        return t * cosf + pltpu.roll(t, RET_HEAD // 2, 1) * sinf

    heads = range(RET_HEADS)
    col = [slice(h * RET_HEAD, (h + 1) * RET_HEAD) for h in heads]
    lg, q, k, v = [], [], [], []
    for h in heads:
        x = dl_ref[0, h]
        lg.append(jnp.minimum(x, 0.0) - jnp.log(1.0 + jnp.exp(-jnp.abs(x))))
        q.append(rope(q_ref[0, :, col[h]]).astype(BF16))
        k.append(rope(k_ref[0, :, col[h]]) * np.float32(RET_HEAD ** -0.5))
        v.append(v_ref[0, :, col[h]].astype(BF16))
    qk = [_dot_nt(q[h], k[h].astype(BF16), None) for h in heads]
    st = [st_ref[h] for h in heads]
    cross = [_dot(q[h], st[h].astype(BF16), None) for h in heads]
    sc = [(qk[h] * jnp.where(mask, jnp.exp(dist * jnp.broadcast_to(lg[h][:, :1], (c, c))), 0.0)).astype(BF16)
          for h in heads]
    inner = [_dot(sc[h], v[h], None) for h in heads]
    upd = [_dot_tn((k[h] * jnp.exp((c - 1.0 - pos) * lg[h])).astype(BF16), v[h], None) for h in heads]
    for h in heads:
        o_ref[0, 0, :, col[h]] = inner[h] + cross[h] * jnp.exp((pos + 1.0) * lg[h])
        st_ref[h] = st[h] * jnp.exp(np.float32(c) * lg[h]) + upd[h]


def _ret_scan(u2, cosf, sinf, dl):
    def tok_spec(blk):
        return pl.BlockSpec((1, RET_CHUNK, RET_W), lambda d, b, s: (b, _ret_chunk_index(d, s), blk))
    rope_spec = pl.BlockSpec((RET_CHUNK, RET_HEAD), lambda d, b, s: (_ret_chunk_index(d, s), 0))
    return pl.pallas_call(
        _ret_scan_kernel,
        out_shape=jax.ShapeDtypeStruct((2, BATCH, TOK, RET_W), F32),
        grid=(2, BATCH, RET_NCHUNK),
        in_specs=[tok_spec(0), tok_spec(1), tok_spec(2), rope_spec, rope_spec,
                  pl.BlockSpec((1, RET_HEADS, 1, RET_HEAD), lambda d, b, s: (d, 0, 0, 0))],
        out_specs=pl.BlockSpec((1, 1, RET_CHUNK, RET_W), lambda d, b, s: (d, b, _ret_chunk_index(d, s), 0)),
        scratch_shapes=[pltpu.VMEM((RET_HEADS, RET_HEAD, RET_HEAD), F32)],
        compiler_params=_params("arbitrary", "arbitrary", "arbitrary"),
        name="ret_scan",
    )(u2, u2, u2, cosf, sinf, dl)


def _rope_tables():
    pos = np.arange(SEQ)
    half = RET_HEAD // 2
    inv = ROPE_BASE ** (-np.arange(0, half, 2, dtype=np.float32) / half)
    ang = np.concatenate([(pos // GRID_W)[:, None] * inv[None, :], (pos % GRID_W)[:, None] * inv[None, :]], axis=1)
    cos = np.concatenate([np.cos(ang), np.ones((CTX_LEN, half))], axis=0).astype(np.float32)
    sin = np.concatenate([np.sin(ang), np.zeros((CTX_LEN, half))], axis=0).astype(np.float32)
    return np.concatenate([cos, cos], axis=1), np.concatenate([-sin, sin], axis=1)


def _odd_finish_kernel(o_ref, g_ref, x_ref, w_ref, np_ref, mod_ref, out_ref):
    o = o_ref[0, 0] + o_ref[1, 0]
    parts = []
    for h in range(RET_HEADS):
        oh = o[:, h * RET_HEAD:(h + 1) * RET_HEAD]
        parts.append(oh * lax.rsqrt(jnp.mean(oh * oh, axis=-1, keepdims=True) + EPS))
    z = (jnp.concatenate(parts, axis=-1) * _silu(g_ref[0])).astype(BF16)
    out = jnp.dot(z, w_ref[...], preferred_element_type=F32)
    out_ref[0] = _post_residual(x_ref[0], out, np_ref[...], mod_ref[0, :, 2 * D_MODEL:])


def _odd_finish(o2, u2, xcomb, w_out_bf16, npost, mod):
    full = lambda shape: pl.BlockSpec(shape, lambda b, j: (0,) * len(shape))
    return pl.pallas_call(
        _odd_finish_kernel,
        out_shape=jax.ShapeDtypeStruct((BATCH, SEQ, D_MODEL), F32),
        grid=(BATCH, SEQ // TILE),
        in_specs=[
            pl.BlockSpec((2, 1, TILE, RET_W), lambda b, j: (0, b, j, 0)),
            pl.BlockSpec((1, TILE, RET_W), lambda b, j: (b, j, 3)),
            pl.BlockSpec((1, TILE, D_MODEL), lambda b, j: (b, j, 0)),
            full((RET_W, D_MODEL)), full((1, D_MODEL)),
            pl.BlockSpec((1, 1, 3 * D_MODEL), lambda b, j: (b, 0, 0)),
        ],
        out_specs=pl.BlockSpec((1, TILE, D_MODEL), lambda b, j: (b, j, 0)),
        compiler_params=_params("arbitrary", "arbitrary"),
        name="odd_finish",
    )(o2, u2, xcomb, w_out_bf16, npost.reshape(1, D_MODEL), mod.reshape(8, 1, 3 * D_MODEL))


def _head_ones():
    return np.kron(np.eye(RWKV_HEADS, dtype=np.float32), np.ones((RWKV_HEAD, RWKV_HEAD), np.float32))


def kernel(x, c, ctx, c_ctx, ada_w, ada_b, norm_pre, norm_post, ev_w_in, ev_mu, ev_w0, ev_w2, ev_a0, ev_a2, ev_k_k, ev_k_a, ev_r_k, ev_lnx_g, ev_lnx_b, ev_w_out, od_w_in, od_decay_logit, od_w_out):
    cvec = jnp.concatenate([c, c_ctx[None, :], jnp.zeros((8 - BATCH - 1, D_MODEL), F32)], axis=0)
    mod = _ada_mod(cvec, ada_w, ada_b)
    xcomb = jnp.concatenate([x, ctx], axis=1)
    ones_bd = jnp.asarray(_head_ones(), dtype=BF16)

    u = _norm_proj(xcomb, norm_pre[0], mod[0], ev_w_in[0].astype(BF16))
    r, v, kk, k2, b2, lw2, bonus = _even_terms(u, ev_mu[0], ev_w0[0], ev_w2[0], ev_a0[0], ev_a2[0], ev_k_k[0],
                                                ev_k_a[0], ev_r_k[0].reshape(RWKV_W), ones_bd)
    y2 = _rwkv_scan(r, v, kk, k2, b2, lw2)
    four = jnp.concatenate([_four_lat(u), _four_ctx(u)], axis=1)
    xcomb = _even_finish(y2, bonus, u, four, xcomb, ones_bd, ev_lnx_g[0], ev_lnx_b[0], ev_w_out[0].astype(BF16),
                         norm_post[0], mod[0])

    u2 = _norm_proj(xcomb, norm_pre[1], mod[1], od_w_in[0].astype(BF16))
    cosf, sinf = _rope_tables()
    dl = jnp.broadcast_to(od_decay_logit[0][:, :, None, None], (2, RET_HEADS, 1, RET_HEAD))
    o2 = _ret_scan(u2, jnp.asarray(cosf), jnp.asarray(sinf), dl)
    return _odd_finish(o2, u2, xcomb, od_w_out[0].astype(BF16), norm_post[1], mod[1])
```

```python
import functools

import numpy as np
import jax
import jax.numpy as jnp
from jax import lax
from jax.experimental import pallas as pl
from jax.experimental.pallas import tpu as pltpu

F32 = jnp.float32
BF16 = jnp.bfloat16
HIGHEST = lax.Precision.HIGHEST

D_MODEL = 1024
BATCH = 2
SEQ = 8192
GRID_W = 64
GRID_ROWS = SEQ // GRID_W
CTX_LEN = 256
TOK = SEQ + CTX_LEN
EPS = 1e-6

RWKV_HEAD = 64
RWKV_HEADS = 12
RWKV_W = RWKV_HEADS * RWKV_HEAD
LORA = 64
FOUR_GROUPS = 4
FOUR_W = 256
FOUR_GROUP_W = 64
N_SHIFT = 3 * RWKV_W + 4 * LORA
EVEN_IN = N_SHIFT + RWKV_W + 2 * FOUR_W
TAIL_W = EVEN_IN - N_SHIFT
RWKV_GN_EPS = 64e-5

RET_HEADS = 8
RET_HEAD = 128
RET_W = 1024
ODD_IN = 4 * RET_W
ROPE_BASE = 10000.0

TILE = 256
N_TILES = TOK // TILE
CTX_TILE = N_TILES - 1
RW_CHUNK = 64
RW_NCHUNK = TOK // RW_CHUNK
RW_CTX_CHUNKS = CTX_LEN // RW_CHUNK
RW_LAT_CHUNKS = SEQ // RW_CHUNK
PAIR_W = 2 * RWKV_HEAD
N_PAIRS = RWKV_W // PAIR_W
RET_CHUNK = 256
RET_NCHUNK = TOK // RET_CHUNK
VMEM_LIMIT = 56 * 1024 * 1024


def _dot(a, b, precision=HIGHEST):
    return jnp.dot(a, b, precision=precision, preferred_element_type=F32)


def _dot_nt(a, b, precision=HIGHEST):
    return lax.dot_general(a, b, (((1,), (1,)), ((), ())), precision=precision, preferred_element_type=F32)


def _dot_tn(a, b, precision=HIGHEST):
    return lax.dot_general(a, b, (((0,), (0,)), ((), ())), precision=precision, preferred_element_type=F32)


def _split_bf16(x):
    hi = x.astype(BF16)
    return hi, (x - hi.astype(F32)).astype(BF16)


def _dot3(a, b):
    ah, al = _split_bf16(a)
    bh, bl = _split_bf16(b)
    return _dot(jnp.concatenate([ah, al], axis=1), jnp.concatenate([bh, bh], axis=0), None) + _dot(ah, bl, None)


def _head_sum(x, ones_bd):
    hi, lo = _split_bf16(x)
    out = []
    for c in range(x.shape[1] // PAIR_W):
        cols = slice(c * PAIR_W, (c + 1) * PAIR_W)
        out.append(_dot(jnp.concatenate([hi[:, cols], lo[:, cols]], axis=1), ones_bd, None))
    return jnp.concatenate(out, axis=1)


def _silu(x):
    return x * (1.0 / (1.0 + jnp.exp(-x)))


def _sigmoid(x):
    return 1.0 / (1.0 + jnp.exp(-x))


def _params(*sem):
    return pltpu.CompilerParams(dimension_semantics=sem, vmem_limit_bytes=VMEM_LIMIT)


def _ada_kernel(c_ref, w_ref, b_ref, o_ref):
    o_ref[0] = _dot(_silu(c_ref[...]), w_ref[0]) + b_ref[0]


def _ada_mod(cvec, ada_w, ada_b):
    depth = ada_w.shape[0]
    nt = 3 * D_MODEL // 1024
    return pl.pallas_call(
        _ada_kernel,
        out_shape=jax.ShapeDtypeStruct((depth, 8, 3 * D_MODEL), F32),
        grid=(depth, nt),
        in_specs=[
            pl.BlockSpec((8, D_MODEL), lambda l, n: (0, 0)),
            pl.BlockSpec((1, D_MODEL, 1024), lambda l, n: (l, 0, n)),
            pl.BlockSpec((1, 1, 1024), lambda l, n: (l, 0, n)),
        ],
        out_specs=pl.BlockSpec((1, 8, 1024), lambda l, n: (l, 0, n)),
        compiler_params=_params("arbitrary", "arbitrary"),
        name="ada_mod",
    )(cvec, ada_w, ada_b.reshape(depth, 1, 3 * D_MODEL))


def _mod_row(b, j):
    return jnp.where(j == CTX_TILE, BATCH, b)


def _norm_proj_kernel(x_ref, g_ref, mod_ref, w_ref, o_ref):
    x = x_ref[0]
    y = x * lax.rsqrt(jnp.mean(x * x, axis=-1, keepdims=True) + EPS) * g_ref[...]
    shift = mod_ref[0, :, 0:D_MODEL]
    scale = mod_ref[0, :, D_MODEL:2 * D_MODEL]
    h = y * (1.0 + scale) + shift
    o_ref[0] = jnp.dot(h.astype(BF16), w_ref[...], preferred_element_type=F32)


def _norm_proj(xcomb, g, mod, w_bf16):
    n_out = w_bf16.shape[1]
    return pl.pallas_call(
        _norm_proj_kernel,
        out_shape=jax.ShapeDtypeStruct((BATCH, TOK, n_out), F32),
        grid=(BATCH, N_TILES),
        in_specs=[
            pl.BlockSpec((1, TILE, D_MODEL), lambda b, j: (b, j, 0)),
            pl.BlockSpec((1, D_MODEL), lambda b, j: (0, 0)),
            pl.BlockSpec((1, 1, 3 * D_MODEL), lambda b, j: (_mod_row(b, j), 0, 0)),
            pl.BlockSpec((D_MODEL, n_out), lambda b, j: (0, 0)),
        ],
        out_specs=pl.BlockSpec((1, TILE, n_out), lambda b, j: (b, j, 0)),
        compiler_params=_params("arbitrary", "arbitrary"),
        name="norm_proj",
    )(xcomb, g.reshape(1, D_MODEL), mod.reshape(8, 1, 3 * D_MODEL), w_bf16)


def _even_terms_kernel(u_ref, up_ref, dn_ref, mu_ref, w0_ref, w2_ref, a0_ref, a2_ref, kk_ref_, ka_ref, rk_ref,
                       ones_ref, r_out, v_out, kk_out, k_out, b_out, lw_out, bonus_out):
    j = pl.program_id(1)
    is_ctx = j == CTX_TILE
    x = u_ref[0]
    t = lax.broadcasted_iota(jnp.int32, (TILE, N_SHIFT), 0)
    lane = lax.broadcasted_iota(jnp.int32, (TILE, N_SHIFT), 1)
    left = pltpu.roll(x, 1, 0)
    right = pltpu.roll(x, TILE - 1, 0)
    edge = jnp.where(is_ctx, TILE - 1, GRID_W - 1)
    left = jnp.where((t & edge) == 0, 0.0, left)
    right = jnp.where((t & edge) == edge, 0.0, right)
    up = jnp.concatenate([up_ref[0], x[:TILE - GRID_W]], axis=0)
    down = jnp.concatenate([x[GRID_W:], dn_ref[0]], axis=0)
    up = jnp.where(t < jnp.where(j == 0, GRID_W, 0), 0.0, up)
    down = jnp.where(t >= jnp.where(j == CTX_TILE - 1, TILE - GRID_W, TILE), 0.0, down)
    sel = jnp.where(is_ctx, lane & 1, lane & 3)
    shifted = jnp.where(sel == 0, left, jnp.where(sel == 1, right, jnp.where(sel == 2, up, down)))
    s = x + (shifted - x) * mu_ref[...]

    r = s[:, 0:RWKV_W]
    k0 = s[:, RWKV_W:2 * RWKV_W]
    v = s[:, 2 * RWKV_W:3 * RWKV_W]
    ones_bd = ones_ref[...]
    kk = k0 * kk_ref_[...]
    norm = jnp.sqrt(_head_sum(kk * kk, ones_bd))
    kk = kk / jnp.maximum(norm, 1e-12)
    r_out[0] = r
    v_out[0] = v
    kk_out[0] = kk
    bonus = jnp.zeros_like(v)
    for d in range(2):
        wl = s[:, 3 * RWKV_W + d * LORA:3 * RWKV_W + (d + 1) * LORA]
        al = s[:, 3 * RWKV_W + (2 + d) * LORA:3 * RWKV_W + (3 + d) * LORA]
        z = w0_ref[d:d + 1, :] + _dot(jnp.tanh(wl).astype(BF16), w2_ref[d].astype(BF16), None)
        lw_out[d, 0] = -np.float32(np.exp(-0.5)) * _sigmoid(z)
        a = _sigmoid(a0_ref[d:d + 1, :] + _dot(al.astype(BF16), a2_ref[d].astype(BF16), None))
        k = k0 * (1.0 + (a - 1.0) * ka_ref[...])
        k_out[d, 0] = k
        b_out[d, 0] = kk * a
        bonus = bonus + _head_sum(r * k * rk_ref[...], ones_bd) * v
    bonus_out[0] = bonus


def _even_terms(u, mu, w0, w2, a0, a2, k_k, k_a, r_k, ones_bd):
    tok = jax.ShapeDtypeStruct((BATCH, TOK, RWKV_W), F32)
    tok2 = jax.ShapeDtypeStruct((2, BATCH, TOK, RWKV_W), F32)
    per = TILE // GRID_W
    last64 = TOK // GRID_W - 1
    full = lambda shape: pl.BlockSpec(shape, lambda b, j: (0,) * len(shape))
    tspec = pl.BlockSpec((1, TILE, RWKV_W), lambda b, j: (b, j, 0))
    t2spec = pl.BlockSpec((2, 1, TILE, RWKV_W), lambda b, j: (0, b, j, 0))
    return pl.pallas_call(
        _even_terms_kernel,
        out_shape=(tok, tok, tok, tok2, tok2, tok2, tok),
        grid=(BATCH, N_TILES),
        in_specs=[
            pl.BlockSpec((1, TILE, N_SHIFT), lambda b, j: (b, j, 0)),
            pl.BlockSpec((1, GRID_W, N_SHIFT), lambda b, j: (b, jnp.maximum(j * per - 1, 0), 0)),
            pl.BlockSpec((1, GRID_W, N_SHIFT), lambda b, j: (b, jnp.minimum(j * per + per, last64), 0)),
            full((1, N_SHIFT)), full((2, RWKV_W)), full((2, LORA, RWKV_W)), full((2, RWKV_W)),
            full((2, LORA, RWKV_W)), full((1, RWKV_W)), full((1, RWKV_W)), full((1, RWKV_W)),
            full((2 * PAIR_W, PAIR_W)),
        ],
        out_specs=(tspec, tspec, tspec, t2spec, t2spec, t2spec, tspec),
        compiler_params=_params("arbitrary", "arbitrary"),
        name="even_terms",
    )(u, u, u, mu.reshape(1, N_SHIFT), w0, w2, a0, a2, k_k.reshape(1, RWKV_W), k_a.reshape(1, RWKV_W),
      r_k.reshape(1, RWKV_W), ones_bd)


def _rw_chunk_index(d, s):
    fwd = jnp.where(s < RW_CTX_CHUNKS, RW_LAT_CHUNKS + s, s - RW_CTX_CHUNKS)
    return jnp.where(d == 0, fwd, RW_NCHUNK - 1 - s)


def _rwkv_scan_kernel(r_ref, v_ref, kk_ref, k_ref, b_ref, lw_ref, y_ref, h_ref):
    d = pl.program_id(0)
    s = pl.program_id(1)
    n = PAIR_W

    @pl.when(s == 0)
    def _():
        h_ref[...] = jnp.zeros_like(h_ref)

    c = RW_CHUNK
    rev = d == 1
    ti = lax.broadcasted_iota(jnp.int32, (c, n), 0)
    li = lax.broadcasted_iota(jnp.int32, (c, n), 1)
    si = li & (c - 1)
    pt = jnp.where(rev, c - 1 - ti, ti)
    ps = jnp.where(rev, c - 1 - si, si)
    strict = ps < pt
    incl = ps <= pt
    ident = jnp.where(si == ti, 1.0, 0.0)
    blk8 = (ti >> 3) == (si >> 3)
    merges = [((ti >> (sh + 1)) == (si >> (sh + 1))) & ((ti >> sh) != (si >> sh)) for sh in (3, 4, 5)]
    head0 = li < RWKV_HEAD
    cumb = jnp.where(incl[:, :c], 1.0, 0.0).astype(BF16)
    ri = lax.broadcasted_iota(jnp.int32, (n, n), 0)
    ci = lax.broadcasted_iota(jnp.int32, (n, n), 1)
    eye = ri == ci
    same_head = (ri >> 6) == (ci >> 6)

    def stack(xb):
        zero = jnp.zeros_like(xb)
        return jnp.concatenate([jnp.where(head0, xb, zero), jnp.where(head0, zero, xb)], axis=0)

    def prod3(x, y):
        xh, xl = _split_bf16(x)
        yh, yl = _split_bf16(y)
        hi = _dot(jnp.concatenate([xh, xl], axis=0), stack(yh), None)
        return hi[:c] + hi[c:] + _dot(xh, stack(yl), None)

    pairs = range(BATCH * N_PAIRS)
    bat = [p // N_PAIRS for p in pairs]
    col = [slice((p % N_PAIRS) * PAIR_W, (p % N_PAIRS + 1) * PAIR_W) for p in pairs]
    lw_all, lc_all = [], []
    for bb in range(BATCH):
        lw_b = lw_ref[0, bb]
        lw_hi = lw_b.astype(BF16)
        lw_mid, lw_lo = _split_bf16(lw_b - lw_hi.astype(F32))
        lw_all.append(lw_b)
        lc_all.append(_dot(jnp.concatenate([cumb, cumb, cumb], axis=1),
                           jnp.concatenate([lw_hi, lw_mid, lw_lo], axis=0), None))
    ar, kb, ke, vb, g_col = [], [], [], [], []
    for p in pairs:
        lw, lc = lw_all[bat[p]][:, col[p]], lc_all[bat[p]][:, col[p]]
        lc_end = jnp.sum(lw, axis=0, keepdims=True)
        g_inv = jnp.exp(-lc)
        g_end = jnp.exp(lc_end - lc)
        kc, bc = k_ref[0, bat[p], :, col[p]], b_ref[0, bat[p], :, col[p]]
        at = (-kk_ref[bat[p], :, col[p]] * jnp.exp(lc - lw)).astype(BF16)
        rt = (r_ref[bat[p], :, col[p]] * jnp.exp(lc)).astype(BF16)
        ar.append(jnp.concatenate([at, rt], axis=0))
        kb.append(jnp.concatenate([stack((kc * g_inv).astype(BF16)), stack((bc * g_inv).astype(BF16))], axis=0))
        ke.append(jnp.concatenate([(kc * g_end).astype(BF16), (bc * g_end).astype(BF16)], axis=0))
        vb.append(v_ref[bat[p], :, col[p]].astype(BF16))
        g_col.append(jnp.sum(jnp.where(eye, jnp.broadcast_to(jnp.exp(lc_end), (n, n)), 0.0), axis=1, keepdims=True))

    quad = [_dot_nt(ar[p], kb[p], None) for p in pairs]
    a_b = [jnp.where(strict, quad[p][:c, n:], 0.0) for p in pairs]
    akbk = [jnp.concatenate([jnp.where(strict, quad[p][:c, :n], 0.0), jnp.where(incl, quad[p][c:, :n], 0.0)],
                            axis=0).astype(BF16) for p in pairs]
    b_b = [jnp.where(incl, quad[p][c:, n:], 0.0).astype(BF16) for p in pairs]
    h0 = [h_ref[p] for p in pairs]
    arh = [_dot(ar[p], h0[p].astype(BF16), None) for p in pairs]
    abv = [_dot(akbk[p], stack(vb[p]), None) for p in pairs]

    n1 = [jnp.where(blk8, a_b[p], 0.0) for p in pairs]
    n2 = [prod3(n1[p], n1[p]) for p in pairs]
    n4 = [prod3(n2[p], n2[p]) for p in pairs]
    t12 = [prod3(ident + n1[p], ident + n2[p]) for p in pairs]
    tinv = [prod3(t12[p], ident + n4[p]) for p in pairs]
    for m in merges:
        off = [jnp.where(m, a_b[p], 0.0) for p in pairs]
        half = [prod3(tinv[p], off[p]) for p in pairs]
        tinv = [tinv[p] + prod3(half[p], tinv[p]) for p in pairs]

    ub = [_dot(tinv[p].astype(BF16), stack((arh[p][:c] + abv[p][:c]).astype(BF16)), None).astype(BF16) for p in pairs]
    bbu = [_dot(b_b[p], stack(ub[p]), None) for p in pairs]
    upd = [_dot_tn(ke[p], jnp.concatenate([vb[p], ub[p]], axis=0), None) for p in pairs]
    for p in pairs:
        y_ref[0, bat[p], :, col[p]] = arh[p][c:] + abv[p][c:] + bbu[p]
        h_ref[p] = h0[p] * g_col[p] + jnp.where(same_head, upd[p], 0.0)


def _rwkv_scan(r, v, kk, k2, b2, lw2):
    tok_spec = pl.BlockSpec((BATCH, RW_CHUNK, RWKV_W), lambda d, s: (0, _rw_chunk_index(d, s), 0))
    dir_spec = pl.BlockSpec((1, BATCH, RW_CHUNK, RWKV_W), lambda d, s: (d, 0, _rw_chunk_index(d, s), 0))
    return pl.pallas_call(
        _rwkv_scan_kernel,
        out_shape=jax.ShapeDtypeStruct((2, BATCH, TOK, RWKV_W), F32),
        grid=(2, RW_NCHUNK),
        in_specs=[tok_spec, tok_spec, tok_spec, dir_spec, dir_spec, dir_spec],
        out_specs=dir_spec,
        scratch_shapes=[pltpu.VMEM((BATCH * N_PAIRS, PAIR_W, PAIR_W), F32)],
        compiler_params=_params("arbitrary", "arbitrary"),
        name="rwkv_scan",
    )(r, v, kk, k2, b2, lw2)


def _dft_mats(n):
    i = np.arange(n)
    ang = 2.0 * np.pi * ((i[:, None] * i[None, :]) % n) / n
    return np.cos(ang).astype(np.float32), np.sin(ang).astype(np.float32)


def _chan_dft_mat():
    c, s = _dft_mats(FOUR_GROUP_W)
    eye = np.eye(FOUR_GROUPS, dtype=np.float32)
    return np.concatenate([np.kron(eye, c), np.kron(eye, s)], axis=1)


def _four_ctx_kernel(u_ref, cs_ref, c_ref, s_ref, o_ref):
    x = u_ref[0][:, RWKV_W:RWKV_W + FOUR_W]
    pq = _dot(x, cs_ref[...])
    scale = np.float32(1.0 / np.sqrt(CTX_LEN * FOUR_GROUP_W))
    o_ref[0] = (_dot(c_ref[...], pq[:, :FOUR_W]) - _dot(s_ref[...], pq[:, FOUR_W:])) * scale


def _four_ctx(u):
    c, s = _dft_mats(CTX_LEN)
    full = lambda shape: pl.BlockSpec(shape, lambda b: (0,) * len(shape))
    return pl.pallas_call(
        _four_ctx_kernel,
        out_shape=jax.ShapeDtypeStruct((BATCH, CTX_LEN, FOUR_W), F32),
        grid=(BATCH,),
        in_specs=[pl.BlockSpec((1, CTX_LEN, TAIL_W), lambda b: (b, CTX_TILE, N_SHIFT // TAIL_W)),
                  full((FOUR_W, 2 * FOUR_W)), full((CTX_LEN, CTX_LEN)), full((CTX_LEN, CTX_LEN))],
        out_specs=pl.BlockSpec((1, CTX_LEN, FOUR_W), lambda b: (b, 0, 0)),
        compiler_params=_params("arbitrary"),
        name="fourier_ctx",
    )(u, jnp.asarray(_chan_dft_mat()), jnp.asarray(c), jnp.asarray(s))


L2_PER_STEP = 8


def _four_lat_a_kernel(u_ref, cs_ref, c_ref, s_ref, tc_ref, ts_ref, zr_ref, zi_ref):
    cs = cs_ref[...]
    cm = c_ref[...]
    sm = s_ref[...]
    for i in range(L2_PER_STEP):
        x = u_ref[0, :, i, RWKV_W:RWKV_W + FOUR_W]
        pq = _dot(x, cs)
        p, q = pq[:, :FOUR_W], pq[:, FOUR_W:]
        zr = _dot(cm, p) - _dot(sm, q)
        zi = -_dot(cm, q) - _dot(sm, p)
        tc, ts = tc_ref[i], ts_ref[i]
        zr_ref[0, i] = zr * tc + zi * ts
        zi_ref[0, i] = zi * tc - zr * ts


def _four_lat_b_kernel(zr_ref, zi_ref, c_ref, s_ref, o_ref):
    scale = np.float32(1.0 / np.sqrt(SEQ * FOUR_GROUP_W))
    o_ref[0] = (_dot(c_ref[...], zr_ref[0]) + _dot(s_ref[...], zi_ref[0])) * scale


def _four_lat(u):
    c1, s1 = _dft_mats(GRID_ROWS)
    c2, s2 = _dft_mats(GRID_W)
    k1 = np.arange(GRID_ROWS)[None, :, None]
    l2 = np.arange(GRID_W)[:, None, None]
    ang = 2.0 * np.pi * (k1 * l2) / SEQ
    tc = np.broadcast_to(np.cos(ang), (GRID_W, GRID_ROWS, FOUR_W)).astype(np.float32)
    ts = np.broadcast_to(np.sin(ang), (GRID_W, GRID_ROWS, FOUR_W)).astype(np.float32)
    u4 = u.reshape(BATCH, TOK // GRID_W, GRID_W, EVEN_IN)
    full = lambda shape: pl.BlockSpec(shape, lambda b, g: (0,) * len(shape))
    zshape = jax.ShapeDtypeStruct((BATCH, GRID_W, GRID_ROWS, FOUR_W), F32)
    zspec = pl.BlockSpec((1, L2_PER_STEP, GRID_ROWS, FOUR_W), lambda b, g: (b, g, 0, 0))
    tspec = pl.BlockSpec((L2_PER_STEP, GRID_ROWS, FOUR_W), lambda b, g: (g, 0, 0))
    zr, zi = pl.pallas_call(
        _four_lat_a_kernel,
        out_shape=(zshape, zshape),
        grid=(BATCH, GRID_W // L2_PER_STEP),
        in_specs=[pl.BlockSpec((1, GRID_ROWS, L2_PER_STEP, TAIL_W), lambda b, g: (b, 0, g, N_SHIFT // TAIL_W)),
                  full((FOUR_W, 2 * FOUR_W)), full((GRID_ROWS, GRID_ROWS)), full((GRID_ROWS, GRID_ROWS)),
                  tspec, tspec],
        out_specs=(zspec, zspec),
        compiler_params=_params("arbitrary", "arbitrary"),
        name="fourier_lat_a",
    )(u4, jnp.asarray(_chan_dft_mat()), jnp.asarray(c1), jnp.asarray(s1), jnp.asarray(tc), jnp.asarray(ts))
    ncol = GRID_ROWS * FOUR_W
    cb = 4096
    full2 = lambda shape: pl.BlockSpec(shape, lambda b, g: (0,) * len(shape))
    zb = pl.BlockSpec((1, GRID_W, cb), lambda b, g: (b, 0, g))
    out = pl.pallas_call(
        _four_lat_b_kernel,
        out_shape=jax.ShapeDtypeStruct((BATCH, GRID_W, ncol), F32),
        grid=(BATCH, ncol // cb),
        in_specs=[zb, zb, full2((GRID_W, GRID_W)), full2((GRID_W, GRID_W))],
        out_specs=zb,
        compiler_params=_params("arbitrary", "arbitrary"),
        name="fourier_lat_b",
    )(zr.reshape(BATCH, GRID_W, ncol), zi.reshape(BATCH, GRID_W, ncol), jnp.asarray(c2), jnp.asarray(s2))
    return out.reshape(BATCH, SEQ, FOUR_W)


def _post_residual(x, out, npost, gate):
    y = out * lax.rsqrt(jnp.mean(out * out, axis=-1, keepdims=True) + EPS) * npost
    return x + gate * y


def _even_finish_kernel(y_ref, bonus_ref, tail_ref, four_ref, x_ref, ones_ref, g_ref, bta_ref, w_ref, np_ref,
                        mod_ref, o_ref):
    o = y_ref[0, 0] + y_ref[1, 0] + bonus_ref[0]
    ones_bd = ones_ref[...]
    inv = np.float32(1.0 / RWKV_HEAD)
    mean = _head_sum(o, ones_bd) * inv
    cen = o - mean
    var = _head_sum(cen * cen, ones_bd) * inv
    on = cen * lax.rsqrt(var + RWKV_GN_EPS) * g_ref[...] + bta_ref[...]
    tail = tail_ref[0]
    yr = on * _silu(tail[:, :RWKV_W])
    yf = four_ref[0] * _silu(tail[:, RWKV_W + FOUR_W:])
    z = jnp.concatenate([yr, yf], axis=-1).astype(BF16)
    out = jnp.dot(z, w_ref[...], preferred_element_type=F32)
    o_ref[0] = _post_residual(x_ref[0], out, np_ref[...], mod_ref[0, :, 2 * D_MODEL:])


def _even_finish(y2, bonus, u, four, xcomb, ones_bd, lnx_g, lnx_b, w_out_bf16, npost, mod):
    full = lambda shape: pl.BlockSpec(shape, lambda b, j: (0,) * len(shape))
    return pl.pallas_call(
        _even_finish_kernel,
        out_shape=jax.ShapeDtypeStruct((BATCH, TOK, D_MODEL), F32),
        grid=(BATCH, N_TILES),
        in_specs=[
            pl.BlockSpec((2, 1, TILE, RWKV_W), lambda b, j: (0, b, j, 0)),
            pl.BlockSpec((1, TILE, RWKV_W), lambda b, j: (b, j, 0)),
            pl.BlockSpec((1, TILE, TAIL_W), lambda b, j: (b, j, N_SHIFT // TAIL_W)),
            pl.BlockSpec((1, TILE, FOUR_W), lambda b, j: (b, j, 0)),
            pl.BlockSpec((1, TILE, D_MODEL), lambda b, j: (b, j, 0)),
            full((2 * PAIR_W, PAIR_W)), full((1, RWKV_W)), full((1, RWKV_W)), full((D_MODEL, D_MODEL)),
            full((1, D_MODEL)),
            pl.BlockSpec((1, 1, 3 * D_MODEL), lambda b, j: (_mod_row(b, j), 0, 0)),
        ],
        out_specs=pl.BlockSpec((1, TILE, D_MODEL), lambda b, j: (b, j, 0)),
        compiler_params=_params("arbitrary", "arbitrary"),
        name="even_finish",
    )(y2, bonus, u, four, xcomb, ones_bd, lnx_g.reshape(1, RWKV_W), lnx_b.reshape(1, RWKV_W), w_out_bf16,
      npost.reshape(1, D_MODEL), mod.reshape(8, 1, 3 * D_MODEL))


def _ret_chunk_index(d, s):
    lat = jnp.where(d == 0, s - 1, RET_NCHUNK - 1 - s)
    return jnp.where(s == 0, RET_NCHUNK - 1, lat)


def _ret_scan_kernel(q_ref, k_ref, v_ref, cos_ref, sin_ref, dl_ref, o_ref, st_ref):
    d = pl.program_id(0)
    s = pl.program_id(2)
    c = RET_CHUNK

    @pl.when(s == 0)
    def _():
        st_ref[...] = jnp.zeros_like(st_ref)

    rev = d == 1
    cosf, sinf = cos_ref[...], sin_ref[...]
    ri = lax.broadcasted_iota(jnp.int32, (c, c), 0)
    ci = lax.broadcasted_iota(jnp.int32, (c, c), 1)
    diff = jnp.where(rev, ci - ri, ri - ci)
    mask = diff >= jnp.where(rev, 1, 0)
    dist = jnp.where(mask, diff, 0).astype(F32)
    ti = lax.broadcasted_iota(jnp.int32, (c, RET_HEAD), 0)
    pos = jnp.where(rev, c - 1 - ti, ti).astype(F32)

    def rope(t):
        return t * cosf + pltpu.roll(t, RET_HEAD // 2, 1) * sinf

    heads = range(RET_HEADS)
    col = [slice(h * RET_HEAD, (h + 1) * RET_HEAD) for h in heads]
    lg, q, k, v = [], [], [], []
    for h in heads:
        x = dl_ref[0, h]
        lg.append(jnp.minimum(x, 0.0) - jnp.log(1.0 + jnp.exp(-jnp.abs(x))))
        q.append(rope(q_ref[0, :, col[h]]).astype(BF16))
        k.append(rope(k_ref[0, :, col[h]]) * np.float32(RET_HEAD ** -0.5))
        v.append(v_ref[0, :, col[h]].astype(BF16))
    qk = [_dot_nt(q[h], k[h].astype(BF16), None) for h in heads]
    st = [st_ref[h] for h in heads]
    cross = [_dot(q[h], st[h].astype(BF16), None) for h in heads]
    sc = [(qk[h] * jnp.where(mask, jnp.exp(dist * jnp.broadcast_to(lg[h][:, :1], (c, c))), 0.0)).astype(BF16)
          for h in heads]
    inner = [_dot(sc[h], v[h], None) for h in heads]
    upd = [_dot_tn((k[h] * jnp.exp((c - 1.0 - pos) * lg[h])).astype(BF16), v[h], None) for h in heads]
    for h in heads:
        o_ref[0, 0, :, col[h]] = inner[h] + cross[h] * jnp.exp((pos + 1.0) * lg[h])
        st_ref[h] = st[h] * jnp.exp(np.float32(c) * lg[h]) + upd[h]


def _ret_scan(u2, cosf, sinf, dl):
    def tok_spec(blk):
        return pl.BlockSpec((1, RET_CHUNK, RET_W), lambda d, b, s: (b, _ret_chunk_index(d, s), blk))
    rope_spec = pl.BlockSpec((RET_CHUNK, RET_HEAD), lambda d, b, s: (_ret_chunk_index(d, s), 0))
    return pl.pallas_call(
        _ret_scan_kernel,
        out_shape=jax.ShapeDtypeStruct((2, BATCH, TOK, RET_W), F32),
        grid=(2, BATCH, RET_NCHUNK),
        in_specs=[tok_spec(0), tok_spec(1), tok_spec(2), rope_spec, rope_spec,
                  pl.BlockSpec((1, RET_HEADS, 1, RET_HEAD), lambda d, b, s: (d, 0, 0, 0))],
        out_specs=pl.BlockSpec((1, 1, RET_CHUNK, RET_W), lambda d, b, s: (d, b, _ret_chunk_index(d, s), 0)),
        scratch_shapes=[pltpu.VMEM((RET_HEADS, RET_HEAD, RET_HEAD), F32)],
        compiler_params=_params("arbitrary", "arbitrary", "arbitrary"),
        name="ret_scan",
    )(u2, u2, u2, cosf, sinf, dl)


def _rope_tables():
    pos = np.arange(SEQ)
    half = RET_HEAD // 2
    inv = ROPE_BASE ** (-np.arange(0, half, 2, dtype=np.float32) / half)
    ang = np.concatenate([(pos // GRID_W)[:, None] * inv[None, :], (pos % GRID_W)[:, None] * inv[None, :]], axis=1)
    cos = np.concatenate([np.cos(ang), np.ones((CTX_LEN, half))], axis=0).astype(np.float32)
    sin = np.concatenate([np.sin(ang), np.zeros((CTX_LEN, half))], axis=0).astype(np.float32)
    return np.concatenate([cos, cos], axis=1), np.concatenate([-sin, sin], axis=1)


def _odd_finish_kernel(o_ref, g_ref, x_ref, w_ref, np_ref, mod_ref, out_ref):
    o = o_ref[0, 0] + o_ref[1, 0]
    parts = []
    for h in range(RET_HEADS):
        oh = o[:, h * RET_HEAD:(h + 1) * RET_HEAD]
        parts.append(oh * lax.rsqrt(jnp.mean(oh * oh, axis=-1, keepdims=True) + EPS))
    z = (jnp.concatenate(parts, axis=-1) * _silu(g_ref[0])).astype(BF16)
    out = jnp.dot(z, w_ref[...], preferred_element_type=F32)
    out_ref[0] = _post_residual(x_ref[0], out, np_ref[...], mod_ref[0, :, 2 * D_MODEL:])


def _odd_finish(o2, u2, xcomb, w_out_bf16, npost, mod):
    full = lambda shape: pl.BlockSpec(shape, lambda b, j: (0,) * len(shape))
    return pl.pallas_call(
        _odd_finish_kernel,
        out_shape=jax.ShapeDtypeStruct((BATCH, SEQ, D_MODEL), F32),
        grid=(BATCH, SEQ // TILE),
        in_specs=[
            pl.BlockSpec((2, 1, TILE, RET_W), lambda b, j: (0, b, j, 0)),
            pl.BlockSpec((1, TILE, RET_W), lambda b, j: (b, j, 3)),
            pl.BlockSpec((1, TILE, D_MODEL), lambda b, j: (b, j, 0)),
            full((RET_W, D_MODEL)), full((1, D_MODEL)),
            pl.BlockSpec((1, 1, 3 * D_MODEL), lambda b, j: (b, 0, 0)),
        ],
        out_specs=pl.BlockSpec((1, TILE, D_MODEL), lambda b, j: (b, j, 0)),
        compiler_params=_params("arbitrary", "arbitrary"),
        name="odd_finish",
    )(o2, u2, xcomb, w_out_bf16, npost.reshape(1, D_MODEL), mod.reshape(8, 1, 3 * D_MODEL))


def _head_ones():
    two_heads = np.kron(np.eye(2, dtype=np.float32), np.ones((RWKV_HEAD, RWKV_HEAD), np.float32))
    return np.concatenate([two_heads, two_heads], axis=0)


def kernel(x, c, ctx, c_ctx, ada_w, ada_b, norm_pre, norm_post, ev_w_in, ev_mu, ev_w0, ev_w2, ev_a0, ev_a2, ev_k_k, ev_k_a, ev_r_k, ev_lnx_g, ev_lnx_b, ev_w_out, od_w_in, od_decay_logit, od_w_out):
    cvec = jnp.concatenate([c, c_ctx[None, :], jnp.zeros((8 - BATCH - 1, D_MODEL), F32)], axis=0)
    mod = _ada_mod(cvec, ada_w, ada_b)
    xcomb = jnp.concatenate([x, ctx], axis=1)
    ones_bd = jnp.asarray(_head_ones(), dtype=BF16)

    u = _norm_proj(xcomb, norm_pre[0], mod[0], ev_w_in[0].astype(BF16))
    r, v, kk, k2, b2, lw2, bonus = _even_terms(u, ev_mu[0], ev_w0[0], ev_w2[0], ev_a0[0], ev_a2[0], ev_k_k[0],
                                                ev_k_a[0], ev_r_k[0].reshape(RWKV_W), ones_bd)
    y2 = _rwkv_scan(r, v, kk, k2, b2, lw2)
    four = jnp.concatenate([_four_lat(u), _four_ctx(u)], axis=1)
    xcomb = _even_finish(y2, bonus, u, four, xcomb, ones_bd, ev_lnx_g[0], ev_lnx_b[0], ev_w_out[0].astype(BF16),
                         norm_post[0], mod[0])

    u2 = _norm_proj(xcomb, norm_pre[1], mod[1], od_w_in[0].astype(BF16))
    cosf, sinf = _rope_tables()
    dl = jnp.broadcast_to(od_decay_logit[0][:, :, None, None], (2, RET_HEADS, 1, RET_HEAD))
    o2 = _ret_scan(u2, jnp.asarray(cosf), jnp.asarray(sinf), dl)
    return _odd_finish(o2, u2, xcomb, od_w_out[0].astype(BF16), norm_post[1], mod[1])
```

```python
import functools

import numpy as np
import jax
import jax.numpy as jnp
from jax import lax
from jax.experimental import pallas as pl
from jax.experimental.pallas import tpu as pltpu

F32 = jnp.float32
BF16 = jnp.bfloat16
HIGHEST = lax.Precision.HIGHEST

D_MODEL = 1024
BATCH = 2
SEQ = 8192
GRID_W = 64
GRID_ROWS = SEQ // GRID_W
CTX_LEN = 256
TOK = SEQ + CTX_LEN
EPS = 1e-6

RWKV_HEAD = 64
RWKV_HEADS = 12
RWKV_W = RWKV_HEADS * RWKV_HEAD
LORA = 64
FOUR_GROUPS = 4
FOUR_W = 256
FOUR_GROUP_W = 64
N_SHIFT = 3 * RWKV_W + 4 * LORA
EVEN_IN = N_SHIFT + RWKV_W + 2 * FOUR_W
TAIL_W = EVEN_IN - N_SHIFT
RWKV_GN_EPS = 64e-5

RET_HEADS = 8
RET_HEAD = 128
RET_W = 1024
ODD_IN = 4 * RET_W
ROPE_BASE = 10000.0

TILE = 256
N_TILES = TOK // TILE
CTX_TILE = N_TILES - 1
RW_CHUNK = 64
RW_NCHUNK = TOK // RW_CHUNK
RW_CTX_CHUNKS = CTX_LEN // RW_CHUNK
RW_LAT_CHUNKS = SEQ // RW_CHUNK
PAIR_W = 2 * RWKV_HEAD
N_PAIRS = RWKV_W // PAIR_W
RET_CHUNK = 256
RET_NCHUNK = TOK // RET_CHUNK
VMEM_LIMIT = 56 * 1024 * 1024


def _dot(a, b, precision=HIGHEST):
    return jnp.dot(a, b, precision=precision, preferred_element_type=F32)


def _dot_nt(a, b, precision=HIGHEST):
    return lax.dot_general(a, b, (((1,), (1,)), ((), ())), precision=precision, preferred_element_type=F32)


def _dot_tn(a, b, precision=HIGHEST):
    return lax.dot_general(a, b, (((0,), (0,)), ((), ())), precision=precision, preferred_element_type=F32)


def _split_bf16(x):
    hi = x.astype(BF16)
    return hi, (x - hi.astype(F32)).astype(BF16)


def _dot3(a, b):
    ah, al = _split_bf16(a)
    bh, bl = _split_bf16(b)
    return _dot(jnp.concatenate([ah, al], axis=1), jnp.concatenate([bh, bh], axis=0), None) + _dot(ah, bl, None)


def _head_sum(x, ones_bd):
    hi, lo = _split_bf16(x)
    out = []
    for c in range(x.shape[1] // PAIR_W):
        cols = slice(c * PAIR_W, (c + 1) * PAIR_W)
        out.append(_dot(jnp.concatenate([hi[:, cols], lo[:, cols]], axis=1), ones_bd, None))
    return jnp.concatenate(out, axis=1)


def _silu(x):
    return x * (1.0 / (1.0 + jnp.exp(-x)))


def _sigmoid(x):
    return 1.0 / (1.0 + jnp.exp(-x))


def _params(*sem):
    return pltpu.CompilerParams(dimension_semantics=sem, vmem_limit_bytes=VMEM_LIMIT)


def _ada_kernel(c_ref, w_ref, b_ref, o_ref):
    o_ref[0] = _dot(_silu(c_ref[...]), w_ref[0]) + b_ref[0]


def _ada_mod(cvec, ada_w, ada_b):
    depth = ada_w.shape[0]
    nt = 3 * D_MODEL // 1024
    return pl.pallas_call(
        _ada_kernel,
        out_shape=jax.ShapeDtypeStruct((depth, 8, 3 * D_MODEL), F32),
        grid=(depth, nt),
        in_specs=[
            pl.BlockSpec((8, D_MODEL), lambda l, n: (0, 0)),
            pl.BlockSpec((1, D_MODEL, 1024), lambda l, n: (l, 0, n)),
            pl.BlockSpec((1, 1, 1024), lambda l, n: (l, 0, n)),
        ],
        out_specs=pl.BlockSpec((1, 8, 1024), lambda l, n: (l, 0, n)),
        compiler_params=_params("arbitrary", "arbitrary"),
        name="ada_mod",
    )(cvec, ada_w, ada_b.reshape(depth, 1, 3 * D_MODEL))


def _mod_row(b, j):
    return jnp.where(j == CTX_TILE, BATCH, b)


def _norm_proj_kernel(x_ref, g_ref, mod_ref, w_ref, *o_refs):
    x = x_ref[0]
    y = x * lax.rsqrt(jnp.mean(x * x, axis=-1, keepdims=True) + EPS) * g_ref[...]
    shift = mod_ref[0, :, 0:D_MODEL]
    scale = mod_ref[0, :, D_MODEL:2 * D_MODEL]
    h = (y * (1.0 + scale) + shift).astype(BF16)
    start = 0
    for o_ref in o_refs:
        width = o_ref.shape[-1]
        o_ref[0] = jnp.dot(h, w_ref[:, start:start + width], preferred_element_type=F32).astype(o_ref.dtype)
        start += width


def _norm_proj(xcomb, g, mod, w_bf16, out_cols):
    n_out = w_bf16.shape[1]
    assert sum(w for w, _ in out_cols) == n_out
    return pl.pallas_call(
        _norm_proj_kernel,
        out_shape=tuple(jax.ShapeDtypeStruct((BATCH, TOK, w), dt) for w, dt in out_cols),
        grid=(BATCH, N_TILES),
        in_specs=[
            pl.BlockSpec((1, TILE, D_MODEL), lambda b, j: (b, j, 0)),
            pl.BlockSpec((1, D_MODEL), lambda b, j: (0, 0)),
            pl.BlockSpec((1, 1, 3 * D_MODEL), lambda b, j: (_mod_row(b, j), 0, 0)),
            pl.BlockSpec((D_MODEL, n_out), lambda b, j: (0, 0)),
        ],
        out_specs=tuple(pl.BlockSpec((1, TILE, w), lambda b, j: (b, j, 0)) for w, _ in out_cols),
        compiler_params=_params("arbitrary", "arbitrary"),
        name="norm_proj",
    )(xcomb, g.reshape(1, D_MODEL), mod.reshape(8, 1, 3 * D_MODEL), w_bf16)


def _even_terms_kernel(u_ref, up_ref, dn_ref, mu_ref, w0_ref, w2_ref, a0_ref, a2_ref, kk_ref_, ka_ref, rk_ref,
                       ones_ref, r_out, v_out, kk_out, k_out, b_out, lw_out, bonus_out):
    j = pl.program_id(1)
    is_ctx = j == CTX_TILE
    x = u_ref[0].astype(F32)
    t = lax.broadcasted_iota(jnp.int32, (TILE, N_SHIFT), 0)
    lane = lax.broadcasted_iota(jnp.int32, (TILE, N_SHIFT), 1)
    left = pltpu.roll(x, 1, 0)
    right = pltpu.roll(x, TILE - 1, 0)
    edge = jnp.where(is_ctx, TILE - 1, GRID_W - 1)
    left = jnp.where((t & edge) == 0, 0.0, left)
    right = jnp.where((t & edge) == edge, 0.0, right)
    up = jnp.concatenate([up_ref[0].astype(F32), x[:TILE - GRID_W]], axis=0)
    down = jnp.concatenate([x[GRID_W:], dn_ref[0].astype(F32)], axis=0)
    up = jnp.where(t < jnp.where(j == 0, GRID_W, 0), 0.0, up)
    down = jnp.where(t >= jnp.where(j == CTX_TILE - 1, TILE - GRID_W, TILE), 0.0, down)
    sel = jnp.where(is_ctx, lane & 1, lane & 3)
    shifted = jnp.where(sel == 0, left, jnp.where(sel == 1, right, jnp.where(sel == 2, up, down)))
    s = x + (shifted - x) * mu_ref[...]

    r = s[:, 0:RWKV_W]
    k0 = s[:, RWKV_W:2 * RWKV_W]
    v = s[:, 2 * RWKV_W:3 * RWKV_W]
    ones_bd = ones_ref[...]
    kk = k0 * kk_ref_[...]
    norm = jnp.sqrt(_head_sum(kk * kk, ones_bd))
    kk = kk / jnp.maximum(norm, 1e-12)
    r_out[0] = r.astype(BF16)
    v_out[0] = v.astype(BF16)
    kk_out[0] = kk.astype(BF16)
    bonus = jnp.zeros_like(v)
    for d in range(2):
        wl = s[:, 3 * RWKV_W + d * LORA:3 * RWKV_W + (d + 1) * LORA]
        al = s[:, 3 * RWKV_W + (2 + d) * LORA:3 * RWKV_W + (3 + d) * LORA]
        z = w0_ref[d:d + 1, :] + _dot(jnp.tanh(wl).astype(BF16), w2_ref[d].astype(BF16), None)
        lw_out[d, 0] = -np.float32(np.exp(-0.5)) * _sigmoid(z)
        a = _sigmoid(a0_ref[d:d + 1, :] + _dot(al.astype(BF16), a2_ref[d].astype(BF16), None))
        k = k0 * (1.0 + (a - 1.0) * ka_ref[...])
        k_out[d, 0] = k.astype(BF16)
        b_out[d, 0] = (kk * a).astype(BF16)
        bonus = bonus + _head_sum(r * k * rk_ref[...], ones_bd) * v
    bonus_out[0] = bonus


def _even_terms(u, mu, w0, w2, a0, a2, k_k, k_a, r_k, ones_bd):
    tok = jax.ShapeDtypeStruct((BATCH, TOK, RWKV_W), BF16)
    tok2 = jax.ShapeDtypeStruct((2, BATCH, TOK, RWKV_W), BF16)
    tok_f32 = jax.ShapeDtypeStruct((BATCH, TOK, RWKV_W), F32)
    tok2_f32 = jax.ShapeDtypeStruct((2, BATCH, TOK, RWKV_W), F32)
    per = TILE // GRID_W
    last64 = TOK // GRID_W - 1
    full = lambda shape: pl.BlockSpec(shape, lambda b, j: (0,) * len(shape))
    tspec = pl.BlockSpec((1, TILE, RWKV_W), lambda b, j: (b, j, 0))
    t2spec = pl.BlockSpec((2, 1, TILE, RWKV_W), lambda b, j: (0, b, j, 0))
    return pl.pallas_call(
        _even_terms_kernel,
        out_shape=(tok, tok, tok, tok2, tok2, tok2_f32, tok_f32),
        grid=(BATCH, N_TILES),
        in_specs=[
            pl.BlockSpec((1, TILE, N_SHIFT), lambda b, j: (b, j, 0)),
            pl.BlockSpec((1, GRID_W, N_SHIFT), lambda b, j: (b, jnp.maximum(j * per - 1, 0), 0)),
            pl.BlockSpec((1, GRID_W, N_SHIFT), lambda b, j: (b, jnp.minimum(j * per + per, last64), 0)),
            full((1, N_SHIFT)), full((2, RWKV_W)), full((2, LORA, RWKV_W)), full((2, RWKV_W)),
            full((2, LORA, RWKV_W)), full((1, RWKV_W)), full((1, RWKV_W)), full((1, RWKV_W)),
            full((2 * PAIR_W, PAIR_W)),
        ],
        out_specs=(tspec, tspec, tspec, t2spec, t2spec, t2spec, tspec),
        compiler_params=_params("arbitrary", "arbitrary"),
        name="even_terms",
    )(u, u, u, mu.reshape(1, N_SHIFT), w0, w2, a0, a2, k_k.reshape(1, RWKV_W), k_a.reshape(1, RWKV_W),
      r_k.reshape(1, RWKV_W), ones_bd)


def _rw_chunk_index(d, s):
    fwd = jnp.where(s < RW_CTX_CHUNKS, RW_LAT_CHUNKS + s, s - RW_CTX_CHUNKS)
    return jnp.where(d == 0, fwd, RW_NCHUNK - 1 - s)


def _rwkv_scan_kernel(rf_ref, vf_ref, kkf_ref, rb_ref, vb_ref, kkb_ref, kf_ref, bf_ref, lwf_ref, kb_ref, bb_ref,
                      lwb_ref, yf_ref, yb_ref, h_ref):
    s = pl.program_id(0)
    n = PAIR_W

    @pl.when(s == 0)
    def _():
        h_ref[...] = jnp.zeros_like(h_ref)

    c = RW_CHUNK
    ti = lax.broadcasted_iota(jnp.int32, (c, n), 0)
    li = lax.broadcasted_iota(jnp.int32, (c, n), 1)
    si = li & (c - 1)
    strict_d = (si < ti, si > ti)
    incl_d = (si <= ti, si >= ti)
    ident = jnp.where(si == ti, 1.0, 0.0)
    blk8 = (ti >> 3) == (si >> 3)
    merges = [((ti >> (sh + 1)) == (si >> (sh + 1))) & ((ti >> sh) != (si >> sh)) for sh in (3, 4, 5)]
    head0 = li < RWKV_HEAD
    cumb_d = [jnp.where(m[:, :c], 1.0, 0.0).astype(BF16) for m in incl_d]
    ri = lax.broadcasted_iota(jnp.int32, (n, n), 0)
    ci = lax.broadcasted_iota(jnp.int32, (n, n), 1)
    eye = ri == ci
    same_head = (ri >> 6) == (ci >> 6)

    def stack(xb):
        zero = jnp.zeros_like(xb)
        return jnp.concatenate([jnp.where(head0, xb, zero), jnp.where(head0, zero, xb)], axis=0)

    def prod(x, y):
        return _dot(x.astype(BF16), stack(y.astype(BF16)), None)

    dir_refs = ((rf_ref, vf_ref, kkf_ref, kf_ref, bf_ref, lwf_ref, yf_ref),
                (rb_ref, vb_ref, kkb_ref, kb_ref, bb_ref, lwb_ref, yb_ref))
    pairs = range(2 * BATCH * N_PAIRS)
    drn = [p // (BATCH * N_PAIRS) for p in pairs]
    bat = [(p // N_PAIRS) % BATCH for p in pairs]
    col = [slice((p % N_PAIRS) * PAIR_W, (p % N_PAIRS + 1) * PAIR_W) for p in pairs]
    strict = [strict_d[drn[p]] for p in pairs]
    incl = [incl_d[drn[p]] for p in pairs]
    lw_all, lc_all = {}, {}
    for dd in range(2):
        for bb in range(BATCH):
            lw_b = dir_refs[dd][5][0, bb]
            lw_hi = lw_b.astype(BF16)
            lw_mid, lw_lo = _split_bf16(lw_b - lw_hi.astype(F32))
            lw_all[dd, bb] = lw_b
            lc_all[dd, bb] = _dot(jnp.concatenate([cumb_d[dd]] * 3, axis=1),
                                  jnp.concatenate([lw_hi, lw_mid, lw_lo], axis=0), None)
    ar, kb, ke, vb, g_col = [], [], [], [], []
    for p in pairs:
        r_ref, v_ref, kk_ref, k_ref, b_ref = dir_refs[drn[p]][:5]
        lw, lc = lw_all[drn[p], bat[p]][:, col[p]], lc_all[drn[p], bat[p]][:, col[p]]
        lc_end = jnp.sum(lw, axis=0, keepdims=True)
        g_inv = jnp.exp(-lc)
        g_end = jnp.exp(lc_end - lc)
        kc, bc = k_ref[0, bat[p], :, col[p]], b_ref[0, bat[p], :, col[p]]
        at = (-kk_ref[bat[p], :, col[p]] * jnp.exp(lc - lw)).astype(BF16)
        rt = (r_ref[bat[p], :, col[p]] * jnp.exp(lc)).astype(BF16)
        ar.append(jnp.concatenate([at, rt], axis=0))
        kb.append(jnp.concatenate([stack((kc * g_inv).astype(BF16)), stack((bc * g_inv).astype(BF16))], axis=0))
        ke.append(jnp.concatenate([(kc * g_end).astype(BF16), (bc * g_end).astype(BF16)], axis=0))
        vb.append(v_ref[bat[p], :, col[p]].astype(BF16))
        g_col.append(jnp.sum(jnp.where(eye, jnp.broadcast_to(jnp.exp(lc_end), (n, n)), 0.0), axis=1, keepdims=True))

    quad = [_dot_nt(ar[p], kb[p], None) for p in pairs]
    a_b = [jnp.where(strict[p], quad[p][:c, n:], 0.0) for p in pairs]
    akbk = [jnp.concatenate([jnp.where(strict[p], quad[p][:c, :n], 0.0), jnp.where(incl[p], quad[p][c:, :n], 0.0)],
                            axis=0).astype(BF16) for p in pairs]
    b_b = [jnp.where(incl[p], quad[p][c:, n:], 0.0).astype(BF16) for p in pairs]
    h0 = [h_ref[p] for p in pairs]
    arh = [_dot(ar[p], h0[p].astype(BF16), None) for p in pairs]
    abv = [_dot(akbk[p], stack(vb[p]), None) for p in pairs]

    n1 = [jnp.where(blk8, a_b[p], 0.0) for p in pairs]
    n2 = [prod(n1[p], n1[p]) for p in pairs]
    n4 = [prod(n2[p], n2[p]) for p in pairs]
    t12 = [prod(ident + n1[p], ident + n2[p]) for p in pairs]
    tinv = [prod(t12[p], ident + n4[p]) for p in pairs]
    for m in merges:
        off = [jnp.where(m, a_b[p], 0.0) for p in pairs]
        half = [prod(tinv[p], off[p]) for p in pairs]
        tinv = [tinv[p] + prod(half[p], tinv[p]) for p in pairs]

    ub = [_dot(tinv[p].astype(BF16), stack((arh[p][:c] + abv[p][:c]).astype(BF16)), None).astype(BF16) for p in pairs]
    bbu = [_dot(b_b[p], stack(ub[p]), None) for p in pairs]
    upd = [_dot_tn(ke[p], jnp.concatenate([vb[p], ub[p]], axis=0), None) for p in pairs]
    for p in pairs:
        dir_refs[drn[p]][6][bat[p], :, col[p]] = arh[p][c:] + abv[p][c:] + bbu[p]
        h_ref[p] = h0[p] * g_col[p] + jnp.where(same_head, upd[p], 0.0)


def _rwkv_scan(r, v, kk, k2, b2, lw2):
    def tok_spec(d):
        return pl.BlockSpec((BATCH, RW_CHUNK, RWKV_W), lambda s: (0, _rw_chunk_index(d, s), 0))

    def dir_spec(d):
        return pl.BlockSpec((1, BATCH, RW_CHUNK, RWKV_W), lambda s: (d, 0, _rw_chunk_index(d, s), 0))

    y_shape = jax.ShapeDtypeStruct((BATCH, TOK, RWKV_W), F32)
    return pl.pallas_call(
        _rwkv_scan_kernel,
        out_shape=(y_shape, y_shape),
        grid=(RW_NCHUNK,),
        in_specs=[tok_spec(0), tok_spec(0), tok_spec(0), tok_spec(1), tok_spec(1), tok_spec(1),
                  dir_spec(0), dir_spec(0), dir_spec(0), dir_spec(1), dir_spec(1), dir_spec(1)],
        out_specs=(tok_spec(0), tok_spec(1)),
        scratch_shapes=[pltpu.VMEM((2 * BATCH * N_PAIRS, PAIR_W, PAIR_W), F32)],
        compiler_params=_params("arbitrary"),
        name="rwkv_scan",
    )(r, v, kk, r, v, kk, k2, b2, lw2, k2, b2, lw2)


def _dft_mats(n):
    i = np.arange(n)
    ang = 2.0 * np.pi * ((i[:, None] * i[None, :]) % n) / n
    return np.cos(ang).astype(np.float32), np.sin(ang).astype(np.float32)


def _chan_dft_mat():
    c, s = _dft_mats(FOUR_GROUP_W)
    eye = np.eye(FOUR_GROUPS, dtype=np.float32)
    return np.concatenate([np.kron(eye, c), np.kron(eye, s)], axis=1)


def _four_ctx_kernel(u_ref, cs_ref, c_ref, s_ref, o_ref):
    x = u_ref[0][:, RWKV_W:RWKV_W + FOUR_W]
    pq = _dot(x, cs_ref[...])
    scale = np.float32(1.0 / np.sqrt(CTX_LEN * FOUR_GROUP_W))
    o_ref[0] = (_dot(c_ref[...], pq[:, :FOUR_W]) - _dot(s_ref[...], pq[:, FOUR_W:])) * scale


def _four_ctx(u):
    c, s = _dft_mats(CTX_LEN)
    full = lambda shape: pl.BlockSpec(shape, lambda b: (0,) * len(shape))
    return pl.pallas_call(
        _four_ctx_kernel,
        out_shape=jax.ShapeDtypeStruct((BATCH, CTX_LEN, FOUR_W), F32),
        grid=(BATCH,),
        in_specs=[pl.BlockSpec((1, CTX_LEN, TAIL_W), lambda b: (b, CTX_TILE, 0)),
                  full((FOUR_W, 2 * FOUR_W)), full((CTX_LEN, CTX_LEN)), full((CTX_LEN, CTX_LEN))],
        out_specs=pl.BlockSpec((1, CTX_LEN, FOUR_W), lambda b: (b, 0, 0)),
        compiler_params=_params("arbitrary"),
        name="fourier_ctx",
    )(u, jnp.asarray(_chan_dft_mat()), jnp.asarray(c), jnp.asarray(s))


L2_PER_STEP = 8


def _four_lat_a_kernel(u_ref, cs_ref, c_ref, s_ref, tc_ref, ts_ref, zr_ref, zi_ref):
    cs = cs_ref[...]
    cm = c_ref[...]
    sm = s_ref[...]
    for i in range(L2_PER_STEP):
        x = u_ref[0, :, i, RWKV_W:RWKV_W + FOUR_W]
        pq = _dot(x, cs)
        p, q = pq[:, :FOUR_W], pq[:, FOUR_W:]
        zr = _dot(cm, p) - _dot(sm, q)
        zi = -_dot(cm, q) - _dot(sm, p)
        tc, ts = tc_ref[i], ts_ref[i]
        zr_ref[0, i] = zr * tc + zi * ts
        zi_ref[0, i] = zi * tc - zr * ts


def _four_lat_b_kernel(zr_ref, zi_ref, c_ref, s_ref, o_ref):
    scale = np.float32(1.0 / np.sqrt(SEQ * FOUR_GROUP_W))
    o_ref[0] = (_dot(c_ref[...], zr_ref[0]) + _dot(s_ref[...], zi_ref[0])) * scale


def _four_lat(u):
    c1, s1 = _dft_mats(GRID_ROWS)
    c2, s2 = _dft_mats(GRID_W)
    k1 = np.arange(GRID_ROWS)[None, :, None]
    l2 = np.arange(GRID_W)[:, None, None]
    ang = 2.0 * np.pi * (k1 * l2) / SEQ
    tc = np.broadcast_to(np.cos(ang), (GRID_W, GRID_ROWS, FOUR_W)).astype(np.float32)
    ts = np.broadcast_to(np.sin(ang), (GRID_W, GRID_ROWS, FOUR_W)).astype(np.float32)
    u4 = u.reshape(BATCH, TOK // GRID_W, GRID_W, TAIL_W)
    full = lambda shape: pl.BlockSpec(shape, lambda b, g: (0,) * len(shape))
    zshape = jax.ShapeDtypeStruct((BATCH, GRID_W, GRID_ROWS, FOUR_W), F32)
    zspec = pl.BlockSpec((1, L2_PER_STEP, GRID_ROWS, FOUR_W), lambda b, g: (b, g, 0, 0))
    tspec = pl.BlockSpec((L2_PER_STEP, GRID_ROWS, FOUR_W), lambda b, g: (g, 0, 0))
    zr, zi = pl.pallas_call(
        _four_lat_a_kernel,
        out_shape=(zshape, zshape),
        grid=(BATCH, GRID_W // L2_PER_STEP),
        in_specs=[pl.BlockSpec((1, GRID_ROWS, L2_PER_STEP, TAIL_W), lambda b, g: (b, 0, g, 0)),
                  full((FOUR_W, 2 * FOUR_W)), full((GRID_ROWS, GRID_ROWS)), full((GRID_ROWS, GRID_ROWS)),
                  tspec, tspec],
        out_specs=(zspec, zspec),
        compiler_params=_params("arbitrary", "arbitrary"),
        name="fourier_lat_a",
    )(u4, jnp.asarray(_chan_dft_mat()), jnp.asarray(c1), jnp.asarray(s1), jnp.asarray(tc), jnp.asarray(ts))
    ncol = GRID_ROWS * FOUR_W
    cb = 4096
    full2 = lambda shape: pl.BlockSpec(shape, lambda b, g: (0,) * len(shape))
    zb = pl.BlockSpec((1, GRID_W, cb), lambda b, g: (b, 0, g))
    out = pl.pallas_call(
        _four_lat_b_kernel,
        out_shape=jax.ShapeDtypeStruct((BATCH, GRID_W, ncol), F32),
        grid=(BATCH, ncol // cb),
        in_specs=[zb, zb, full2((GRID_W, GRID_W)), full2((GRID_W, GRID_W))],
        out_specs=zb,
        compiler_params=_params("arbitrary", "arbitrary"),
        name="fourier_lat_b",
    )(zr.reshape(BATCH, GRID_W, ncol), zi.reshape(BATCH, GRID_W, ncol), jnp.asarray(c2), jnp.asarray(s2))
    return out.reshape(BATCH, SEQ, FOUR_W)


def _post_residual(x, out, npost, gate):
    y = out * lax.rsqrt(jnp.mean(out * out, axis=-1, keepdims=True) + EPS) * npost
    return x + gate * y


def _even_finish_kernel(yf_ref, yb_ref, bonus_ref, tail_ref, four_ref, x_ref, ones_ref, g_ref, bta_ref, w_ref,
                        np_ref, mod_ref, o_ref):
    o = yf_ref[0] + yb_ref[0] + bonus_ref[0]
    ones_bd = ones_ref[...]
    inv = np.float32(1.0 / RWKV_HEAD)
    mean = _head_sum(o, ones_bd) * inv
    cen = o - mean
    var = _head_sum(cen * cen, ones_bd) * inv
    on = cen * lax.rsqrt(var + RWKV_GN_EPS) * g_ref[...] + bta_ref[...]
    tail = tail_ref[0]
    yr = on * _silu(tail[:, :RWKV_W])
    yf = four_ref[0] * _silu(tail[:, RWKV_W + FOUR_W:])
    z = jnp.concatenate([yr, yf], axis=-1).astype(BF16)
    out = jnp.dot(z, w_ref[...], preferred_element_type=F32)
    o_ref[0] = _post_residual(x_ref[0], out, np_ref[...], mod_ref[0, :, 2 * D_MODEL:])


def _even_finish(yf, yb, bonus, u, four, xcomb, ones_bd, lnx_g, lnx_b, w_out_bf16, npost, mod):
    full = lambda shape: pl.BlockSpec(shape, lambda b, j: (0,) * len(shape))
    return pl.pallas_call(
        _even_finish_kernel,
        out_shape=jax.ShapeDtypeStruct((BATCH, TOK, D_MODEL), F32),
        grid=(BATCH, N_TILES),
        in_specs=[
            pl.BlockSpec((1, TILE, RWKV_W), lambda b, j: (b, j, 0)),
            pl.BlockSpec((1, TILE, RWKV_W), lambda b, j: (b, j, 0)),
            pl.BlockSpec((1, TILE, RWKV_W), lambda b, j: (b, j, 0)),
            pl.BlockSpec((1, TILE, TAIL_W), lambda b, j: (b, j, 0)),
            pl.BlockSpec((1, TILE, FOUR_W), lambda b, j: (b, j, 0)),
            pl.BlockSpec((1, TILE, D_MODEL), lambda b, j: (b, j, 0)),
            full((2 * PAIR_W, PAIR_W)), full((1, RWKV_W)), full((1, RWKV_W)), full((D_MODEL, D_MODEL)),
            full((1, D_MODEL)),
            pl.BlockSpec((1, 1, 3 * D_MODEL), lambda b, j: (_mod_row(b, j), 0, 0)),
        ],
        out_specs=pl.BlockSpec((1, TILE, D_MODEL), lambda b, j: (b, j, 0)),
        compiler_params=_params("arbitrary", "arbitrary"),
        name="even_finish",
    )(yf, yb, bonus, u, four, xcomb, ones_bd, lnx_g.reshape(1, RWKV_W), lnx_b.reshape(1, RWKV_W), w_out_bf16,
      npost.reshape(1, D_MODEL), mod.reshape(8, 1, 3 * D_MODEL))


def _ret_chunk_index(d, s):
    lat = jnp.where(d == 0, s - 1, RET_NCHUNK - 1 - s)
    return jnp.where(s == 0, RET_NCHUNK - 1, lat)


def _ret_scan_kernel(q_ref, k_ref, v_ref, cos_ref, sin_ref, dl_ref, o_ref, st_ref):
    d = pl.program_id(0)
    s = pl.program_id(2)
    c = RET_CHUNK

    @pl.when(s == 0)
    def _():
        st_ref[...] = jnp.zeros_like(st_ref)

    rev = d == 1
    cosf, sinf = cos_ref[...], sin_ref[...]
    ri = lax.broadcasted_iota(jnp.int32, (c, c), 0)
    ci = lax.broadcasted_iota(jnp.int32, (c, c), 1)
    diff = jnp.where(rev, ci - ri, ri - ci)
    mask = diff >= jnp.where(rev, 1, 0)
    dist = jnp.where(mask, diff, 0).astype(F32)
    ti = lax.broadcasted_iota(jnp.int32, (c, RET_HEAD), 0)
    pos = jnp.where(rev, c - 1 - ti, ti).astype(F32)

    def rope(t):
        return t * cosf + pltpu.roll(t, RET_HEAD // 2, 1) * sinf

    heads = range(RET_HEADS)
    col = [slice(h * RET_HEAD, (h + 1) * RET_HEAD) for h in heads]
    lg, q, k, v = [], [], [], []
    for h in heads:
        x = dl_ref[0, h]
        lg.append(jnp.minimum(x, 0.0) - jnp.log(1.0 + jnp.exp(-jnp.abs(x))))
        q.append(rope(q_ref[0, :, col[h]].astype(F32)).astype(BF16))
        k.append(rope(k_ref[0, :, col[h]].astype(F32)) * np.float32(RET_HEAD ** -0.5))
        v.append(v_ref[0, :, col[h]].astype(BF16))
    qk = [_dot_nt(q[h], k[h].astype(BF16), None) for h in heads]
    st = [st_ref[h] for h in heads]
    cross = [_dot(q[h], st[h].astype(BF16), None) for h in heads]
    sc = [(qk[h] * jnp.where(mask, jnp.exp(dist * jnp.broadcast_to(lg[h][:, :1], (c, c))), 0.0)).astype(BF16)
          for h in heads]
    inner = [_dot(sc[h], v[h], None) for h in heads]
    upd = [_dot_tn((k[h] * jnp.exp((c - 1.0 - pos) * lg[h])).astype(BF16), v[h], None) for h in heads]
    for h in heads:
        o_ref[0, 0, :, col[h]] = inner[h] + cross[h] * jnp.exp((pos + 1.0) * lg[h])
        st_ref[h] = st[h] * jnp.exp(np.float32(c) * lg[h]) + upd[h]


def _ret_scan(u2, cosf, sinf, dl):
    def tok_spec(blk):
        return pl.BlockSpec((1, RET_CHUNK, RET_W), lambda d, b, s: (b, _ret_chunk_index(d, s), blk))
    rope_spec = pl.BlockSpec((RET_CHUNK, RET_HEAD), lambda d, b, s: (_ret_chunk_index(d, s), 0))
    return pl.pallas_call(
        _ret_scan_kernel,
        out_shape=jax.ShapeDtypeStruct((2, BATCH, TOK, RET_W), F32),
        grid=(2, BATCH, RET_NCHUNK),
        in_specs=[tok_spec(0), tok_spec(1), tok_spec(2), rope_spec, rope_spec,
                  pl.BlockSpec((1, RET_HEADS, 1, RET_HEAD), lambda d, b, s: (d, 0, 0, 0))],
        out_specs=pl.BlockSpec((1, 1, RET_CHUNK, RET_W), lambda d, b, s: (d, b, _ret_chunk_index(d, s), 0)),
        scratch_shapes=[pltpu.VMEM((RET_HEADS, RET_HEAD, RET_HEAD), F32)],
        compiler_params=_params("arbitrary", "arbitrary", "arbitrary"),
        name="ret_scan",
    )(u2, u2, u2, cosf, sinf, dl)


def _rope_tables():
    pos = np.arange(SEQ)
    half = RET_HEAD // 2
    inv = ROPE_BASE ** (-np.arange(0, half, 2, dtype=np.float32) / half)
    ang = np.concatenate([(pos // GRID_W)[:, None] * inv[None, :], (pos % GRID_W)[:, None] * inv[None, :]], axis=1)
    cos = np.concatenate([np.cos(ang), np.ones((CTX_LEN, half))], axis=0).astype(np.float32)
    sin = np.concatenate([np.sin(ang), np.zeros((CTX_LEN, half))], axis=0).astype(np.float32)
    return np.concatenate([cos, cos], axis=1), np.concatenate([-sin, sin], axis=1)


def _odd_finish_kernel(o_ref, g_ref, x_ref, w_ref, np_ref, mod_ref, out_ref):
    o = o_ref[0, 0] + o_ref[1, 0]
    parts = []
    for h in range(RET_HEADS):
        oh = o[:, h * RET_HEAD:(h + 1) * RET_HEAD]
        parts.append(oh * lax.rsqrt(jnp.mean(oh * oh, axis=-1, keepdims=True) + EPS))
    z = (jnp.concatenate(parts, axis=-1) * _silu(g_ref[0].astype(F32))).astype(BF16)
    out = jnp.dot(z, w_ref[...], preferred_element_type=F32)
    out_ref[0] = _post_residual(x_ref[0], out, np_ref[...], mod_ref[0, :, 2 * D_MODEL:])


def _odd_finish(o2, u2, xcomb, w_out_bf16, npost, mod):
    full = lambda shape: pl.BlockSpec(shape, lambda b, j: (0,) * len(shape))
    return pl.pallas_call(
        _odd_finish_kernel,
        out_shape=jax.ShapeDtypeStruct((BATCH, SEQ, D_MODEL), F32),
        grid=(BATCH, SEQ // TILE),
        in_specs=[
            pl.BlockSpec((2, 1, TILE, RET_W), lambda b, j: (0, b, j, 0)),
            pl.BlockSpec((1, TILE, RET_W), lambda b, j: (b, j, 3)),
            pl.BlockSpec((1, TILE, D_MODEL), lambda b, j: (b, j, 0)),
            full((RET_W, D_MODEL)), full((1, D_MODEL)),
            pl.BlockSpec((1, 1, 3 * D_MODEL), lambda b, j: (b, 0, 0)),
        ],
        out_specs=pl.BlockSpec((1, TILE, D_MODEL), lambda b, j: (b, j, 0)),
        compiler_params=_params("arbitrary", "arbitrary"),
        name="odd_finish",
    )(o2, u2, xcomb, w_out_bf16, npost.reshape(1, D_MODEL), mod.reshape(8, 1, 3 * D_MODEL))


def _head_ones():
    two_heads = np.kron(np.eye(2, dtype=np.float32), np.ones((RWKV_HEAD, RWKV_HEAD), np.float32))
    return np.concatenate([two_heads, two_heads], axis=0)


def kernel(x, c, ctx, c_ctx, ada_w, ada_b, norm_pre, norm_post, ev_w_in, ev_mu, ev_w0, ev_w2, ev_a0, ev_a2, ev_k_k, ev_k_a, ev_r_k, ev_lnx_g, ev_lnx_b, ev_w_out, od_w_in, od_decay_logit, od_w_out):
    cvec = jnp.concatenate([c, c_ctx[None, :], jnp.zeros((8 - BATCH - 1, D_MODEL), F32)], axis=0)
    mod = _ada_mod(cvec, ada_w, ada_b)
    xcomb = jnp.concatenate([x, ctx], axis=1)
    ones_bd = jnp.asarray(_head_ones(), dtype=BF16)

    us, tail = _norm_proj(xcomb, norm_pre[0], mod[0], ev_w_in[0].astype(BF16), ((N_SHIFT, BF16), (TAIL_W, F32)))
    r, v, kk, k2, b2, lw2, bonus = _even_terms(us, ev_mu[0], ev_w0[0], ev_w2[0], ev_a0[0], ev_a2[0], ev_k_k[0],
                                                ev_k_a[0], ev_r_k[0].reshape(RWKV_W), ones_bd)
    yf, yb = _rwkv_scan(r, v, kk, k2, b2, lw2)
    four = jnp.concatenate([_four_lat(tail), _four_ctx(tail)], axis=1)
    xcomb = _even_finish(yf, yb, bonus, tail, four, xcomb, ones_bd, ev_lnx_g[0], ev_lnx_b[0], ev_w_out[0].astype(BF16),
                         norm_post[0], mod[0])

    (u2,) = _norm_proj(xcomb, norm_pre[1], mod[1], od_w_in[0].astype(BF16), ((ODD_IN, BF16),))
    cosf, sinf = _rope_tables()
    dl = jnp.broadcast_to(od_decay_logit[0][:, :, None, None], (2, RET_HEADS, 1, RET_HEAD))
    o2 = _ret_scan(u2, jnp.asarray(cosf), jnp.asarray(sinf), dl)
    return _odd_finish(o2, u2, xcomb, od_w_out[0].astype(BF16), norm_post[1], mod[1])
```

```python
import functools

import numpy as np
import jax
import jax.numpy as jnp
from jax import lax
from jax.experimental import pallas as pl
from jax.experimental.pallas import tpu as pltpu

F32 = jnp.float32
BF16 = jnp.bfloat16
HIGHEST = lax.Precision.HIGHEST

D_MODEL = 1024
BATCH = 2
SEQ = 8192
GRID_W = 64
GRID_ROWS = SEQ // GRID_W
CTX_LEN = 256
TOK = SEQ + CTX_LEN
EPS = 1e-6

RWKV_HEAD = 64
RWKV_HEADS = 12
RWKV_W = RWKV_HEADS * RWKV_HEAD
LORA = 64
FOUR_GROUPS = 4
FOUR_W = 256
FOUR_GROUP_W = 64
N_SHIFT = 3 * RWKV_W + 4 * LORA
EVEN_IN = N_SHIFT + RWKV_W + 2 * FOUR_W
TAIL_W = EVEN_IN - N_SHIFT
RWKV_GN_EPS = 64e-5

RET_HEADS = 8
RET_HEAD = 128
RET_W = 1024
ODD_IN = 4 * RET_W
ROPE_BASE = 10000.0

TILE = 256
N_TILES = TOK // TILE
CTX_TILE = N_TILES - 1
RW_CHUNK = 64
RW_NCHUNK = TOK // RW_CHUNK
RW_CTX_CHUNKS = CTX_LEN // RW_CHUNK
RW_LAT_CHUNKS = SEQ // RW_CHUNK
PAIR_W = 2 * RWKV_HEAD
N_PAIRS = RWKV_W // PAIR_W
RET_CHUNK = 256
RET_NCHUNK = TOK // RET_CHUNK
VMEM_LIMIT = 56 * 1024 * 1024


def _dot(a, b, precision=HIGHEST):
    return jnp.dot(a, b, precision=precision, preferred_element_type=F32)


def _dot_nt(a, b, precision=HIGHEST):
    return lax.dot_general(a, b, (((1,), (1,)), ((), ())), precision=precision, preferred_element_type=F32)


def _dot_tn(a, b, precision=HIGHEST):
    return lax.dot_general(a, b, (((0,), (0,)), ((), ())), precision=precision, preferred_element_type=F32)


def _split_bf16(x):
    hi = x.astype(BF16)
    return hi, (x - hi.astype(F32)).astype(BF16)


def _dot3(a, b):
    ah, al = _split_bf16(a)
    bh, bl = _split_bf16(b)
    return _dot(jnp.concatenate([ah, al], axis=1), jnp.concatenate([bh, bh], axis=0), None) + _dot(ah, bl, None)


def _head_sum(x, ones_bd):
    hi, lo = _split_bf16(x)
    out = []
    for c in range(x.shape[1] // PAIR_W):
        cols = slice(c * PAIR_W, (c + 1) * PAIR_W)
        out.append(_dot(jnp.concatenate([hi[:, cols], lo[:, cols]], axis=1), ones_bd, None))
    return jnp.concatenate(out, axis=1)


def _silu(x):
    return x * (1.0 / (1.0 + jnp.exp(-x)))


def _sigmoid(x):
    return 1.0 / (1.0 + jnp.exp(-x))


def _params(*sem):
    return pltpu.CompilerParams(dimension_semantics=sem, vmem_limit_bytes=VMEM_LIMIT)


def _ada_kernel(c_ref, w_ref, b_ref, o_ref):
    o_ref[0] = _dot(_silu(c_ref[...]), w_ref[0]) + b_ref[0]


def _ada_mod(cvec, ada_w, ada_b):
    depth = ada_w.shape[0]
    nt = 3 * D_MODEL // 1024
    return pl.pallas_call(
        _ada_kernel,
        out_shape=jax.ShapeDtypeStruct((depth, 8, 3 * D_MODEL), F32),
        grid=(depth, nt),
        in_specs=[
            pl.BlockSpec((8, D_MODEL), lambda l, n: (0, 0)),
            pl.BlockSpec((1, D_MODEL, 1024), lambda l, n: (l, 0, n)),
            pl.BlockSpec((1, 1, 1024), lambda l, n: (l, 0, n)),
        ],
        out_specs=pl.BlockSpec((1, 8, 1024), lambda l, n: (l, 0, n)),
        compiler_params=_params("arbitrary", "arbitrary"),
        name="ada_mod",
    )(cvec, ada_w, ada_b.reshape(depth, 1, 3 * D_MODEL))


def _mod_row(b, j):
    return jnp.where(j == CTX_TILE, BATCH, b)


def _tile_rows(lat_ref, ctx_ref):
    return jnp.where(pl.program_id(1) == CTX_TILE, ctx_ref[0], lat_ref[0])


def _lat_ctx_specs(width, lat_is_combined):
    lat = pl.BlockSpec((1, TILE, width), lambda b, j: (b, jnp.minimum(j, CTX_TILE - 1), 0))
    ctx = pl.BlockSpec((1, TILE, width), lambda b, j: (b, CTX_TILE if lat_is_combined else 0, 0))
    return lat, ctx


def _norm_proj_kernel(x_ref, xc_ref, g_ref, mod_ref, w_ref, *o_refs):
    x = _tile_rows(x_ref, xc_ref)
    y = x * lax.rsqrt(jnp.mean(x * x, axis=-1, keepdims=True) + EPS) * g_ref[...]
    shift = mod_ref[0, :, 0:D_MODEL]
    scale = mod_ref[0, :, D_MODEL:2 * D_MODEL]
    h = (y * (1.0 + scale) + shift).astype(BF16)
    start = 0
    for o_ref in o_refs:
        width = o_ref.shape[-1]
        o_ref[0] = jnp.dot(h, w_ref[:, start:start + width], preferred_element_type=F32).astype(o_ref.dtype)
        start += width


def _norm_proj(x_lat, x_ctx, g, mod, w_bf16, out_cols):
    n_out = w_bf16.shape[1]
    assert sum(w for w, _ in out_cols) == n_out
    return pl.pallas_call(
        _norm_proj_kernel,
        out_shape=tuple(jax.ShapeDtypeStruct((BATCH, TOK, w), dt) for w, dt in out_cols),
        grid=(BATCH, N_TILES),
        in_specs=[
            *_lat_ctx_specs(D_MODEL, x_lat is x_ctx),
            pl.BlockSpec((1, D_MODEL), lambda b, j: (0, 0)),
            pl.BlockSpec((1, 1, 3 * D_MODEL), lambda b, j: (_mod_row(b, j), 0, 0)),
            pl.BlockSpec((D_MODEL, n_out), lambda b, j: (0, 0)),
        ],
        out_specs=tuple(pl.BlockSpec((1, TILE, w), lambda b, j: (b, j, 0)) for w, _ in out_cols),
        compiler_params=_params("arbitrary", "arbitrary"),
        name="norm_proj",
    )(x_lat, x_ctx, g.reshape(1, D_MODEL), mod.reshape(8, 1, 3 * D_MODEL), w_bf16)


def _even_terms_kernel(u_ref, up_ref, dn_ref, mu_ref, w0_ref, w2_ref, a0_ref, a2_ref, kk_ref_, ka_ref, rk_ref,
                       ones_ref, r_out, v_out, kk_out, k_out, b_out, lw_out, bonus_out):
    j = pl.program_id(1)
    is_ctx = j == CTX_TILE
    x = u_ref[0].astype(F32)
    left = pltpu.roll(x, 1, 0)
    right = pltpu.roll(x, TILE - 1, 0)
    row8 = lax.broadcasted_iota(jnp.int32, (8, N_SHIFT), 0)
    inner_first = jnp.where(is_ctx, 0, 1)
    inner_last = jnp.where(is_ctx, 8, 7)

    def zero_rows(a, first):
        parts, pos = [], 0
        for g in range(0, TILE, GRID_W):
            at = g if first else g + GRID_W - 8
            if first:
                keep = row8 >= (1 if g == 0 else inner_first)
            else:
                keep = row8 < (7 if g == TILE - GRID_W else inner_last)
            parts += [a[pos:at], jnp.where(keep, a[at:at + 8], 0.0)]
            pos = at + 8
        return jnp.concatenate([p for p in parts + [a[pos:]] if p.shape[0]], axis=0)

    left = zero_rows(left, True)
    right = zero_rows(right, False)
    up_ok = jnp.where(j == 0, 0.0, 1.0)
    dn_ok = jnp.where(j == CTX_TILE - 1, 0.0, 1.0)
    up = jnp.concatenate([up_ref[0].astype(F32) * up_ok, x[:TILE - GRID_W]], axis=0)
    down = jnp.concatenate([x[GRID_W:], dn_ref[0].astype(F32) * dn_ok], axis=0)
    lane = lax.broadcasted_iota(jnp.int32, (1, N_SHIFT), 1)
    sel = jnp.where(is_ctx, lane & 1, lane & 3)
    shifted = jnp.where(sel == 0, left, jnp.where(sel == 1, right, jnp.where(sel == 2, up, down)))
    s = x + (shifted - x) * mu_ref[...]

    r = s[:, 0:RWKV_W]
    k0 = s[:, RWKV_W:2 * RWKV_W]
    v = s[:, 2 * RWKV_W:3 * RWKV_W]
    ones_bd = ones_ref[...]
    kk = k0 * kk_ref_[...]
    kk = kk * lax.rsqrt(jnp.maximum(_head_sum(kk * kk, ones_bd), 1e-24))
    r_out[0] = r.astype(BF16)
    v_out[0] = v.astype(BF16)
    kk_out[0] = kk.astype(BF16)
    bonus = jnp.zeros_like(v)
    for d in range(2):
        wl = s[:, 3 * RWKV_W + d * LORA:3 * RWKV_W + (d + 1) * LORA]
        al = s[:, 3 * RWKV_W + (2 + d) * LORA:3 * RWKV_W + (3 + d) * LORA]
        z = w0_ref[d:d + 1, :] + _dot(jnp.tanh(wl).astype(BF16), w2_ref[d].astype(BF16), None)
        lw_out[d, 0] = -np.float32(np.exp(-0.5)) * _sigmoid(z)
        a = _sigmoid(a0_ref[d:d + 1, :] + _dot(al.astype(BF16), a2_ref[d].astype(BF16), None))
        k = k0 * (1.0 + (a - 1.0) * ka_ref[...])
        k_out[d, 0] = k.astype(BF16)
        b_out[d, 0] = (kk * a).astype(BF16)
        bonus = bonus + _head_sum(r * k * rk_ref[...], ones_bd) * v
    bonus_out[0] = bonus.astype(BF16)


def _even_terms(u, mu, w0, w2, a0, a2, k_k, k_a, r_k, ones_bd):
    tok = jax.ShapeDtypeStruct((BATCH, TOK, RWKV_W), BF16)
    tok2 = jax.ShapeDtypeStruct((2, BATCH, TOK, RWKV_W), BF16)
    tok2_f32 = jax.ShapeDtypeStruct((2, BATCH, TOK, RWKV_W), F32)
    per = TILE // GRID_W
    last64 = TOK // GRID_W - 1
    full = lambda shape: pl.BlockSpec(shape, lambda b, j: (0,) * len(shape))
    tspec = pl.BlockSpec((1, TILE, RWKV_W), lambda b, j: (b, j, 0))
    t2spec = pl.BlockSpec((2, 1, TILE, RWKV_W), lambda b, j: (0, b, j, 0))
    return pl.pallas_call(
        _even_terms_kernel,
        out_shape=(tok, tok, tok, tok2, tok2, tok2_f32, tok),
        grid=(BATCH, N_TILES),
        in_specs=[
            pl.BlockSpec((1, TILE, N_SHIFT), lambda b, j: (b, j, 0)),
            pl.BlockSpec((1, GRID_W, N_SHIFT), lambda b, j: (b, jnp.maximum(j * per - 1, 0), 0)),
            pl.BlockSpec((1, GRID_W, N_SHIFT), lambda b, j: (b, jnp.minimum(j * per + per, last64), 0)),
            full((1, N_SHIFT)), full((2, RWKV_W)), full((2, LORA, RWKV_W)), full((2, RWKV_W)),
            full((2, LORA, RWKV_W)), full((1, RWKV_W)), full((1, RWKV_W)), full((1, RWKV_W)),
            full((2 * PAIR_W, PAIR_W)),
        ],
        out_specs=(tspec, tspec, tspec, t2spec, t2spec, t2spec, tspec),
        compiler_params=_params("arbitrary", "arbitrary"),
        name="even_terms",
    )(u, u, u, mu.reshape(1, N_SHIFT), w0, w2, a0, a2, k_k.reshape(1, RWKV_W), k_a.reshape(1, RWKV_W),
      r_k.reshape(1, RWKV_W), ones_bd)


def _rw_chunk_index(d, s):
    fwd = jnp.where(s < RW_CTX_CHUNKS, RW_LAT_CHUNKS + s, s - RW_CTX_CHUNKS)
    return jnp.where(d == 0, fwd, RW_NCHUNK - 1 - s)


def _rwkv_scan_kernel(rf_ref, vf_ref, kkf_ref, rb_ref, vb_ref, kkb_ref, kf_ref, bf_ref, lwf_ref, kb_ref, bb_ref,
                      lwb_ref, yf_ref, yb_ref, h_ref):
    s = pl.program_id(0)
    n = PAIR_W

    @pl.when(s == 0)
    def _():
        h_ref[...] = jnp.zeros_like(h_ref)

    c = RW_CHUNK
    ti = lax.broadcasted_iota(jnp.int32, (c, n), 0)
    li = lax.broadcasted_iota(jnp.int32, (c, n), 1)
    si = li & (c - 1)
    strict_d = (si < ti, si > ti)
    incl_d = (si <= ti, si >= ti)
    ident = jnp.where(si == ti, 1.0, 0.0)
    blk8 = (ti >> 3) == (si >> 3)
    merges = [((ti >> (sh + 1)) == (si >> (sh + 1))) & ((ti >> sh) != (si >> sh)) for sh in (3, 4, 5)]
    head0 = li < RWKV_HEAD
    cumb_d = [jnp.where(m[:, :c], 1.0, 0.0).astype(BF16) for m in incl_d]
    ri = lax.broadcasted_iota(jnp.int32, (n, n), 0)
    ci = lax.broadcasted_iota(jnp.int32, (n, n), 1)
    eye = ri == ci
    same_head = (ri >> 6) == (ci >> 6)

    def stack(xb):
        zero = jnp.zeros_like(xb)
        return jnp.concatenate([jnp.where(head0, xb, zero), jnp.where(head0, zero, xb)], axis=0)

    def prod(x, y):
        return _dot(x.astype(BF16), stack(y.astype(BF16)), None)

    dir_refs = ((rf_ref, vf_ref, kkf_ref, kf_ref, bf_ref, lwf_ref, yf_ref),
                (rb_ref, vb_ref, kkb_ref, kb_ref, bb_ref, lwb_ref, yb_ref))
    pairs = range(2 * BATCH * N_PAIRS)
    drn = [p // (BATCH * N_PAIRS) for p in pairs]
    bat = [(p // N_PAIRS) % BATCH for p in pairs]
    col = [slice((p % N_PAIRS) * PAIR_W, (p % N_PAIRS + 1) * PAIR_W) for p in pairs]
    strict = [strict_d[drn[p]] for p in pairs]
    incl = [incl_d[drn[p]] for p in pairs]
    lw_all, lc_all = {}, {}
    for dd in range(2):
        for bb in range(BATCH):
            lw_b = dir_refs[dd][5][0, bb]
            lw_hi = lw_b.astype(BF16)
            lw_mid, lw_lo = _split_bf16(lw_b - lw_hi.astype(F32))
            lw_all[dd, bb] = lw_b
            lc_all[dd, bb] = _dot(jnp.concatenate([cumb_d[dd]] * 3, axis=1),
                                  jnp.concatenate([lw_hi, lw_mid, lw_lo], axis=0), None)
    ar, kb, ke, vb, g_col = [], [], [], [], []
    for p in pairs:
        r_ref, v_ref, kk_ref, k_ref, b_ref = dir_refs[drn[p]][:5]
        lw, lc = lw_all[drn[p], bat[p]][:, col[p]], lc_all[drn[p], bat[p]][:, col[p]]
        lc_end = jnp.sum(lw, axis=0, keepdims=True)
        g_inv = jnp.exp(-lc)
        g_end = jnp.exp(lc_end - lc)
        kc, bc = k_ref[0, bat[p], :, col[p]], b_ref[0, bat[p], :, col[p]]
        at = (-kk_ref[bat[p], :, col[p]] * jnp.exp(lc - lw)).astype(BF16)
        rt = (r_ref[bat[p], :, col[p]] * jnp.exp(lc)).astype(BF16)
        ar.append(jnp.concatenate([at, rt], axis=0))
        kb.append(jnp.concatenate([stack((kc * g_inv).astype(BF16)), stack((bc * g_inv).astype(BF16))], axis=0))
        ke.append(jnp.concatenate([(kc * g_end).astype(BF16), (bc * g_end).astype(BF16)], axis=0))
        vb.append(v_ref[bat[p], :, col[p]].astype(BF16))
        g_col.append(jnp.sum(jnp.where(eye, jnp.broadcast_to(jnp.exp(lc_end), (n, n)), 0.0), axis=1, keepdims=True))

    quad = [_dot_nt(ar[p], kb[p], None) for p in pairs]
    a_b = [jnp.where(strict[p], quad[p][:c, n:], 0.0) for p in pairs]
    akbk = [jnp.concatenate([jnp.where(strict[p], quad[p][:c, :n], 0.0), jnp.where(incl[p], quad[p][c:, :n], 0.0)],
                            axis=0).astype(BF16) for p in pairs]
    b_b = [jnp.where(incl[p], quad[p][c:, n:], 0.0).astype(BF16) for p in pairs]
    h0 = [h_ref[p] for p in pairs]
    arh = [_dot(ar[p], h0[p].astype(BF16), None) for p in pairs]
    abv = [_dot(akbk[p], stack(vb[p]), None) for p in pairs]

    n1 = [jnp.where(blk8, a_b[p], 0.0) for p in pairs]
    n2 = [prod(n1[p], n1[p]) for p in pairs]
    n4 = [prod(n2[p], n2[p]) for p in pairs]
    t12 = [prod(ident + n1[p], ident + n2[p]) for p in pairs]
    tinv = [prod(t12[p], ident + n4[p]) for p in pairs]
    for m in merges:
        off = [jnp.where(m, a_b[p], 0.0) for p in pairs]
        half = [prod(tinv[p], off[p]) for p in pairs]
        tinv = [tinv[p] + prod(half[p], tinv[p]) for p in pairs]

    ub = [_dot(tinv[p].astype(BF16), stack((arh[p][:c] + abv[p][:c]).astype(BF16)), None).astype(BF16) for p in pairs]
    bbu = [_dot(b_b[p], stack(ub[p]), None) for p in pairs]
    upd = [_dot_tn(ke[p], jnp.concatenate([vb[p], ub[p]], axis=0), None) for p in pairs]
    for p in pairs:
        dir_refs[drn[p]][6][bat[p], :, col[p]] = (arh[p][c:] + abv[p][c:] + bbu[p]).astype(BF16)
        h_ref[p] = h0[p] * g_col[p] + jnp.where(same_head, upd[p], 0.0)


def _rwkv_scan(r, v, kk, k2, b2, lw2):
    def tok_spec(d):
        return pl.BlockSpec((BATCH, RW_CHUNK, RWKV_W), lambda s: (0, _rw_chunk_index(d, s), 0))

    def dir_spec(d):
        return pl.BlockSpec((1, BATCH, RW_CHUNK, RWKV_W), lambda s: (d, 0, _rw_chunk_index(d, s), 0))

    y_shape = jax.ShapeDtypeStruct((BATCH, TOK, RWKV_W), BF16)
    return pl.pallas_call(
        _rwkv_scan_kernel,
        out_shape=(y_shape, y_shape),
        grid=(RW_NCHUNK,),
        in_specs=[tok_spec(0), tok_spec(0), tok_spec(0), tok_spec(1), tok_spec(1), tok_spec(1),
                  dir_spec(0), dir_spec(0), dir_spec(0), dir_spec(1), dir_spec(1), dir_spec(1)],
        out_specs=(tok_spec(0), tok_spec(1)),
        scratch_shapes=[pltpu.VMEM((2 * BATCH * N_PAIRS, PAIR_W, PAIR_W), F32)],
        compiler_params=_params("arbitrary"),
        name="rwkv_scan",
    )(r, v, kk, r, v, kk, k2, b2, lw2, k2, b2, lw2)


def _dft_mats(n):
    i = np.arange(n)
    ang = 2.0 * np.pi * ((i[:, None] * i[None, :]) % n) / n
    return np.cos(ang).astype(np.float32), np.sin(ang).astype(np.float32)


def _chan_dft_mat():
    c, s = _dft_mats(FOUR_GROUP_W)
    eye = np.eye(FOUR_GROUPS, dtype=np.float32)
    return np.concatenate([np.kron(eye, c), np.kron(eye, s)], axis=1)


def _bdot(a, b):
    return jnp.dot(a.astype(BF16), b.astype(BF16), preferred_element_type=F32)


def _four_ctx_kernel(u_ref, cs_ref, c_ref, s_ref, o_ref):
    pq = _bdot(u_ref[0], cs_ref[...])
    scale = np.float32(1.0 / np.sqrt(CTX_LEN * FOUR_GROUP_W))
    o_ref[0] = (_bdot(c_ref[...], pq[:, :FOUR_W]) - _bdot(s_ref[...], pq[:, FOUR_W:])) * scale


def _four_ctx(u):
    c, s = _dft_mats(CTX_LEN)
    full = lambda shape: pl.BlockSpec(shape, lambda b: (0,) * len(shape))
    return pl.pallas_call(
        _four_ctx_kernel,
        out_shape=jax.ShapeDtypeStruct((BATCH, CTX_LEN, FOUR_W), F32),
        grid=(BATCH,),
        in_specs=[pl.BlockSpec((1, CTX_LEN, FOUR_W), lambda b: (b, CTX_TILE, 0)),
                  full((FOUR_W, 2 * FOUR_W)), full((CTX_LEN, CTX_LEN)), full((CTX_LEN, CTX_LEN))],
        out_specs=pl.BlockSpec((1, CTX_LEN, FOUR_W), lambda b: (b, 0, 0)),
        compiler_params=_params("arbitrary"),
        name="fourier_ctx",
    )(u, jnp.asarray(_chan_dft_mat()), jnp.asarray(c), jnp.asarray(s))


L2_PER_STEP = 8


def _four_lat_a_kernel(u_ref, cs_ref, c_ref, s_ref, tc_ref, ts_ref, zr_ref, zi_ref):
    cs = cs_ref[...]
    cm = c_ref[...]
    sm = s_ref[...]
    for i in range(L2_PER_STEP):
        x = u_ref[0, :, i, :]
        pq = _bdot(x, cs)
        cpq = _bdot(cm, pq)
        spq = _bdot(sm, pq)
        zr = cpq[:, :FOUR_W] - spq[:, FOUR_W:]
        zi = -cpq[:, FOUR_W:] - spq[:, :FOUR_W]
        tc, ts = tc_ref[i], ts_ref[i]
        zr_ref[0, i] = zr * tc + zi * ts
        zi_ref[0, i] = zi * tc - zr * ts


K1_PER_STEP = 8


def _four_lat_b_kernel(zr_ref, zi_ref, cs_ref, o_ref):
    scale = np.float32(1.0 / np.sqrt(SEQ * FOUR_GROUP_W))
    cs = cs_ref[...]
    for i in range(K1_PER_STEP):
        z = jnp.concatenate([zr_ref[0, :, i, :], zi_ref[0, :, i, :]], axis=0)
        o_ref[0, :, i, :] = _bdot(cs, z) * scale


def _four_lat(u):
    c1, s1 = _dft_mats(GRID_ROWS)
    c2, s2 = _dft_mats(GRID_W)
    k1 = np.arange(GRID_ROWS)[None, :, None]
    l2 = np.arange(GRID_W)[:, None, None]
    ang = 2.0 * np.pi * (k1 * l2) / SEQ
    tc = np.broadcast_to(np.cos(ang), (GRID_W, GRID_ROWS, FOUR_W)).astype(np.float32)
    ts = np.broadcast_to(np.sin(ang), (GRID_W, GRID_ROWS, FOUR_W)).astype(np.float32)
    u4 = u.reshape(BATCH, TOK // GRID_W, GRID_W, FOUR_W)
    full = lambda shape: pl.BlockSpec(shape, lambda b, g: (0,) * len(shape))
    zshape = jax.ShapeDtypeStruct((BATCH, GRID_W, GRID_ROWS, FOUR_W), F32)
    zspec = pl.BlockSpec((1, L2_PER_STEP, GRID_ROWS, FOUR_W), lambda b, g: (b, g, 0, 0))
    tspec = pl.BlockSpec((L2_PER_STEP, GRID_ROWS, FOUR_W), lambda b, g: (g, 0, 0))
    zr, zi = pl.pallas_call(
        _four_lat_a_kernel,
        out_shape=(zshape, zshape),
        grid=(BATCH, GRID_W // L2_PER_STEP),
        in_specs=[pl.BlockSpec((1, GRID_ROWS, L2_PER_STEP, FOUR_W), lambda b, g: (b, 0, g, 0)),
                  full((FOUR_W, 2 * FOUR_W)), full((GRID_ROWS, GRID_ROWS)), full((GRID_ROWS, GRID_ROWS)),
                  tspec, tspec],
        out_specs=(zspec, zspec),
        compiler_params=_params("arbitrary", "arbitrary"),
        name="fourier_lat_a",
    )(u4, jnp.asarray(_chan_dft_mat()), jnp.asarray(c1), jnp.asarray(s1), jnp.asarray(tc), jnp.asarray(ts))
    zb = pl.BlockSpec((1, GRID_W, K1_PER_STEP, FOUR_W), lambda b, g: (b, 0, g, 0))
    out = pl.pallas_call(
        _four_lat_b_kernel,
        out_shape=jax.ShapeDtypeStruct((BATCH, GRID_W, GRID_ROWS, FOUR_W), F32),
        grid=(BATCH, GRID_ROWS // K1_PER_STEP),
        in_specs=[zb, zb, full((GRID_W, 2 * GRID_W))],
        out_specs=zb,
        compiler_params=_params("arbitrary", "arbitrary"),
        name="fourier_lat_b",
    )(zr, zi, jnp.asarray(np.concatenate([c2, s2], axis=1)))
    return out.reshape(BATCH, SEQ, FOUR_W)


def _post_residual(x, out, npost, gate):
    y = out * lax.rsqrt(jnp.mean(out * out, axis=-1, keepdims=True) + EPS) * npost
    return x + gate * y


def _even_finish_kernel(yf_ref, yb_ref, bonus_ref, gr_ref, gf_ref, four_ref, fourc_ref, x_ref, xc_ref, ones_ref, g_ref,
                        bta_ref, w_ref, np_ref, mod_ref, o_ref):
    o = yf_ref[0].astype(F32) + yb_ref[0].astype(F32) + bonus_ref[0].astype(F32)
    ones_bd = ones_ref[...]
    inv = np.float32(1.0 / RWKV_HEAD)
    mean = _head_sum(o, ones_bd) * inv
    cen = o - mean
    var = _head_sum(cen * cen, ones_bd) * inv
    on = cen * lax.rsqrt(var + RWKV_GN_EPS) * g_ref[...] + bta_ref[...]
    yr = on * _silu(gr_ref[0].astype(F32))
    yf = _tile_rows(four_ref, fourc_ref) * _silu(gf_ref[0].astype(F32))
    z = jnp.concatenate([yr, yf], axis=-1).astype(BF16)
    out = jnp.dot(z, w_ref[...], preferred_element_type=F32)
    o_ref[0] = _post_residual(_tile_rows(x_ref, xc_ref), out, np_ref[...], mod_ref[0, :, 2 * D_MODEL:])


def _even_finish(yf, yb, bonus, gate_r, gate_f, four_lat, four_ctx, x_lat, x_ctx, ones_bd, lnx_g, lnx_b, w_out_bf16,
                 npost, mod):
    full = lambda shape: pl.BlockSpec(shape, lambda b, j: (0,) * len(shape))
    return pl.pallas_call(
        _even_finish_kernel,
        out_shape=jax.ShapeDtypeStruct((BATCH, TOK, D_MODEL), F32),
        grid=(BATCH, N_TILES),
        in_specs=[
            pl.BlockSpec((1, TILE, RWKV_W), lambda b, j: (b, j, 0)),
            pl.BlockSpec((1, TILE, RWKV_W), lambda b, j: (b, j, 0)),
            pl.BlockSpec((1, TILE, RWKV_W), lambda b, j: (b, j, 0)),
            pl.BlockSpec((1, TILE, RWKV_W), lambda b, j: (b, j, 0)),
            pl.BlockSpec((1, TILE, FOUR_W), lambda b, j: (b, j, 0)),
            *_lat_ctx_specs(FOUR_W, False),
            *_lat_ctx_specs(D_MODEL, False),
            full((2 * PAIR_W, PAIR_W)), full((1, RWKV_W)), full((1, RWKV_W)), full((D_MODEL, D_MODEL)),
            full((1, D_MODEL)),
            pl.BlockSpec((1, 1, 3 * D_MODEL), lambda b, j: (_mod_row(b, j), 0, 0)),
        ],
        out_specs=pl.BlockSpec((1, TILE, D_MODEL), lambda b, j: (b, j, 0)),
        compiler_params=_params("arbitrary", "arbitrary"),
        name="even_finish",
    )(yf, yb, bonus, gate_r, gate_f, four_lat, four_ctx, x_lat, x_ctx, ones_bd, lnx_g.reshape(1, RWKV_W),
      lnx_b.reshape(1, RWKV_W), w_out_bf16, npost.reshape(1, D_MODEL), mod.reshape(8, 1, 3 * D_MODEL))


def _ret_chunk_index(d, s):
    lat = jnp.where(d == 0, s - 1, RET_NCHUNK - 1 - s)
    return jnp.where(s == 0, RET_NCHUNK - 1, lat)


def _ret_scan_kernel(q_ref, k_ref, v_ref, cos_ref, sin_ref, dl_ref, o_ref, st_ref):
    d = pl.program_id(0)
    s = pl.program_id(2)
    c = RET_CHUNK

    @pl.when(s == 0)
    def _():
        st_ref[...] = jnp.zeros_like(st_ref)

    rev = d == 1
    cosf, sinf = cos_ref[...], sin_ref[...]
    ri = lax.broadcasted_iota(jnp.int32, (c, c), 0)
    ci = lax.broadcasted_iota(jnp.int32, (c, c), 1)
    diff = jnp.where(rev, ci - ri, ri - ci)
    mask = diff >= jnp.where(rev, 1, 0)
    dist = jnp.where(mask, diff, 0).astype(F32)
    ti = lax.broadcasted_iota(jnp.int32, (c, RET_HEAD), 0)
    pos = jnp.where(rev, c - 1 - ti, ti).astype(F32)

    def rope(t):
        return t * cosf + pltpu.roll(t, RET_HEAD // 2, 1) * sinf

    heads = range(RET_HEADS)
    col = [slice(h * RET_HEAD, (h + 1) * RET_HEAD) for h in heads]
    lg, q, k, v = [], [], [], []
    for h in heads:
        x = dl_ref[0, h]
        lg.append((jnp.minimum(x, 0.0) - jnp.log(1.0 + jnp.exp(-jnp.abs(x)))) * np.float32(np.log2(np.e)))
        q.append(rope(q_ref[0, :, col[h]].astype(F32)).astype(BF16))
        k.append(rope(k_ref[0, :, col[h]].astype(F32)) * np.float32(RET_HEAD ** -0.5))
        v.append(v_ref[0, :, col[h]].astype(BF16))
    qk = [_dot_nt(q[h], k[h].astype(BF16), None) for h in heads]
    st = [st_ref[h] for h in heads]
    cross = [_dot(q[h], st[h].astype(BF16), None) for h in heads]
    sc = [(qk[h] * jnp.where(mask, jnp.exp2(dist * jnp.broadcast_to(lg[h][:, :1], (c, c))), 0.0)).astype(BF16)
          for h in heads]
    inner = [_dot(sc[h], v[h], None) for h in heads]
    upd = [_dot_tn((k[h] * jnp.exp2((c - 1.0 - pos) * lg[h])).astype(BF16), v[h], None) for h in heads]
    for h in heads:
        o_ref[0, 0, :, col[h]] = (inner[h] + cross[h] * jnp.exp2((pos + 1.0) * lg[h])).astype(o_ref.dtype)
        st_ref[h] = st[h] * jnp.exp2(np.float32(c) * lg[h]) + upd[h]


def _ret_scan(u2, cosf, sinf, dl):
    def tok_spec(blk):
        return pl.BlockSpec((1, RET_CHUNK, RET_W), lambda d, b, s: (b, _ret_chunk_index(d, s), blk))
    rope_spec = pl.BlockSpec((RET_CHUNK, RET_HEAD), lambda d, b, s: (_ret_chunk_index(d, s), 0))
    return pl.pallas_call(
        _ret_scan_kernel,
        out_shape=jax.ShapeDtypeStruct((2, BATCH, TOK, RET_W), BF16),
        grid=(2, BATCH, RET_NCHUNK),
        in_specs=[tok_spec(0), tok_spec(1), tok_spec(2), rope_spec, rope_spec,
                  pl.BlockSpec((1, RET_HEADS, 1, RET_HEAD), lambda d, b, s: (d, 0, 0, 0))],
        out_specs=pl.BlockSpec((1, 1, RET_CHUNK, RET_W), lambda d, b, s: (d, b, _ret_chunk_index(d, s), 0)),
        scratch_shapes=[pltpu.VMEM((RET_HEADS, RET_HEAD, RET_HEAD), F32)],
        compiler_params=_params("arbitrary", "arbitrary", "arbitrary"),
        name="ret_scan",
    )(u2, u2, u2, cosf, sinf, dl)


def _rope_tables():
    pos = np.arange(SEQ)
    half = RET_HEAD // 2
    inv = ROPE_BASE ** (-np.arange(0, half, 2, dtype=np.float32) / half)
    ang = np.concatenate([(pos // GRID_W)[:, None] * inv[None, :], (pos % GRID_W)[:, None] * inv[None, :]], axis=1)
    cos = np.concatenate([np.cos(ang), np.ones((CTX_LEN, half))], axis=0).astype(np.float32)
    sin = np.concatenate([np.sin(ang), np.zeros((CTX_LEN, half))], axis=0).astype(np.float32)
    return np.concatenate([cos, cos], axis=1), np.concatenate([-sin, sin], axis=1)


def _odd_finish_kernel(o_ref, g_ref, x_ref, w_ref, np_ref, mod_ref, out_ref):
    o = o_ref[0, 0].astype(F32) + o_ref[1, 0].astype(F32)
    parts = []
    for h in range(RET_HEADS):
        oh = o[:, h * RET_HEAD:(h + 1) * RET_HEAD]
        parts.append(oh * lax.rsqrt(jnp.mean(oh * oh, axis=-1, keepdims=True) + EPS))
    z = (jnp.concatenate(parts, axis=-1) * _silu(g_ref[0].astype(F32))).astype(BF16)
    out = jnp.dot(z, w_ref[...], preferred_element_type=F32)
    out_ref[0] = _post_residual(x_ref[0], out, np_ref[...], mod_ref[0, :, 2 * D_MODEL:])


def _odd_finish(o2, u2, xcomb, w_out_bf16, npost, mod):
    full = lambda shape: pl.BlockSpec(shape, lambda b, j: (0,) * len(shape))
    return pl.pallas_call(
        _odd_finish_kernel,
        out_shape=jax.ShapeDtypeStruct((BATCH, SEQ, D_MODEL), F32),
        grid=(BATCH, SEQ // TILE),
        in_specs=[
            pl.BlockSpec((2, 1, TILE, RET_W), lambda b, j: (0, b, j, 0)),
            pl.BlockSpec((1, TILE, RET_W), lambda b, j: (b, j, 3)),
            pl.BlockSpec((1, TILE, D_MODEL), lambda b, j: (b, j, 0)),
            full((RET_W, D_MODEL)), full((1, D_MODEL)),
            pl.BlockSpec((1, 1, 3 * D_MODEL), lambda b, j: (b, 0, 0)),
        ],
        out_specs=pl.BlockSpec((1, TILE, D_MODEL), lambda b, j: (b, j, 0)),
        compiler_params=_params("arbitrary", "arbitrary"),
        name="odd_finish",
    )(o2, u2, xcomb, w_out_bf16, npost.reshape(1, D_MODEL), mod.reshape(8, 1, 3 * D_MODEL))


def _head_ones():
    two_heads = np.kron(np.eye(2, dtype=np.float32), np.ones((RWKV_HEAD, RWKV_HEAD), np.float32))
    return np.concatenate([two_heads, two_heads], axis=0)


def kernel(x, c, ctx, c_ctx, ada_w, ada_b, norm_pre, norm_post, ev_w_in, ev_mu, ev_w0, ev_w2, ev_a0, ev_a2, ev_k_k, ev_k_a, ev_r_k, ev_lnx_g, ev_lnx_b, ev_w_out, od_w_in, od_decay_logit, od_w_out):
    cvec = jnp.concatenate([c, c_ctx[None, :], jnp.zeros((8 - BATCH - 1, D_MODEL), F32)], axis=0)
    mod = _ada_mod(cvec, ada_w, ada_b)
    ones_bd = jnp.asarray(_head_ones(), dtype=BF16)

    us, gate_r, four_in, gate_f = _norm_proj(
        x, ctx, norm_pre[0], mod[0], ev_w_in[0].astype(BF16),
        ((N_SHIFT, BF16), (RWKV_W, BF16), (FOUR_W, F32), (FOUR_W, BF16)))
    r, v, kk, k2, b2, lw2, bonus = _even_terms(us, ev_mu[0], ev_w0[0], ev_w2[0], ev_a0[0], ev_a2[0], ev_k_k[0],
                                                ev_k_a[0], ev_r_k[0].reshape(RWKV_W), ones_bd)
    yf, yb = _rwkv_scan(r, v, kk, k2, b2, lw2)
    xcomb = _even_finish(yf, yb, bonus, gate_r, gate_f, _four_lat(four_in), _four_ctx(four_in), x, ctx, ones_bd,
                         ev_lnx_g[0], ev_lnx_b[0], ev_w_out[0].astype(BF16), norm_post[0], mod[0])

    (u2,) = _norm_proj(xcomb, xcomb, norm_pre[1], mod[1], od_w_in[0].astype(BF16), ((ODD_IN, BF16),))
    cosf, sinf = _rope_tables()
    dl = jnp.broadcast_to(od_decay_logit[0][:, :, None, None], (2, RET_HEADS, 1, RET_HEAD))
    o2 = _ret_scan(u2, jnp.asarray(cosf), jnp.asarray(sinf), dl)
    return _odd_finish(o2, u2, xcomb, od_w_out[0].astype(BF16), norm_post[1], mod[1])
```

```python
import functools

import numpy as np
import jax
import jax.numpy as jnp
from jax import lax
from jax.experimental import pallas as pl
from jax.experimental.pallas import tpu as pltpu

F32 = jnp.float32
BF16 = jnp.bfloat16
HIGHEST = lax.Precision.HIGHEST

D_MODEL = 1024
BATCH = 2
SEQ = 8192
GRID_W = 64
GRID_ROWS = SEQ // GRID_W
CTX_LEN = 256
TOK = SEQ + CTX_LEN
EPS = 1e-6

RWKV_HEAD = 64
RWKV_HEADS = 12
RWKV_W = RWKV_HEADS * RWKV_HEAD
LORA = 64
FOUR_GROUPS = 4
FOUR_W = 256
FOUR_GROUP_W = 64
N_SHIFT = 3 * RWKV_W + 4 * LORA
EVEN_IN = N_SHIFT + RWKV_W + 2 * FOUR_W
TAIL_W = EVEN_IN - N_SHIFT
RWKV_GN_EPS = 64e-5

RET_HEADS = 8
RET_HEAD = 128
RET_W = 1024
ODD_IN = 4 * RET_W
ROPE_BASE = 10000.0

TILE = 256
N_TILES = TOK // TILE
CTX_TILE = N_TILES - 1
RW_CHUNK = 64
RW_NCHUNK = TOK // RW_CHUNK
RW_CTX_CHUNKS = CTX_LEN // RW_CHUNK
RW_LAT_CHUNKS = SEQ // RW_CHUNK
PAIR_W = 2 * RWKV_HEAD
N_PAIRS = RWKV_W // PAIR_W
RET_CHUNK = 256
RET_NCHUNK = TOK // RET_CHUNK
VMEM_LIMIT = 56 * 1024 * 1024


def _dot(a, b, precision=HIGHEST):
    return jnp.dot(a, b, precision=precision, preferred_element_type=F32)


def _dot_nt(a, b, precision=HIGHEST):
    return lax.dot_general(a, b, (((1,), (1,)), ((), ())), precision=precision, preferred_element_type=F32)


def _dot_tn(a, b, precision=HIGHEST):
    return lax.dot_general(a, b, (((0,), (0,)), ((), ())), precision=precision, preferred_element_type=F32)


def _split_bf16(x):
    hi = x.astype(BF16)
    return hi, (x - hi.astype(F32)).astype(BF16)


def _dot3(a, b):
    ah, al = _split_bf16(a)
    bh, bl = _split_bf16(b)
    return _dot(jnp.concatenate([ah, al], axis=1), jnp.concatenate([bh, bh], axis=0), None) + _dot(ah, bl, None)


def _head_sum(x, ones_bd):
    hi, lo = _split_bf16(x)
    out = []
    for c in range(x.shape[1] // PAIR_W):
        cols = slice(c * PAIR_W, (c + 1) * PAIR_W)
        out.append(_dot(jnp.concatenate([hi[:, cols], lo[:, cols]], axis=1), ones_bd, None))
    return jnp.concatenate(out, axis=1)


def _silu(x):
    return x * (1.0 / (1.0 + jnp.exp(-x)))


def _sigmoid(x):
    return 1.0 / (1.0 + jnp.exp(-x))


def _params(*sem):
    return pltpu.CompilerParams(dimension_semantics=sem, vmem_limit_bytes=VMEM_LIMIT)


def _ada_kernel(c_ref, w_ref, b_ref, o_ref):
    o_ref[0] = _dot(_silu(c_ref[...]), w_ref[0]) + b_ref[0]


def _ada_mod(cvec, ada_w, ada_b):
    depth = ada_w.shape[0]
    nt = 3 * D_MODEL // 1024
    return pl.pallas_call(
        _ada_kernel,
        out_shape=jax.ShapeDtypeStruct((depth, 8, 3 * D_MODEL), F32),
        grid=(depth, nt),
        in_specs=[
            pl.BlockSpec((8, D_MODEL), lambda l, n: (0, 0)),
            pl.BlockSpec((1, D_MODEL, 1024), lambda l, n: (l, 0, n)),
            pl.BlockSpec((1, 1, 1024), lambda l, n: (l, 0, n)),
        ],
        out_specs=pl.BlockSpec((1, 8, 1024), lambda l, n: (l, 0, n)),
        compiler_params=_params("arbitrary", "arbitrary"),
        name="ada_mod",
    )(cvec, ada_w, ada_b.reshape(depth, 1, 3 * D_MODEL))


def _mod_row(b, j):
    return jnp.where(j == CTX_TILE, BATCH, b)


def _tile_rows(lat_ref, ctx_ref, tile):
    return jnp.where(tile == CTX_TILE, ctx_ref[0], lat_ref[0])


def _lat_ctx_specs(width, lat_is_combined):
    lat = pl.BlockSpec((1, TILE, width), lambda b, j: (b, jnp.minimum(j, CTX_TILE - 1), 0))
    ctx = pl.BlockSpec((1, TILE, width), lambda b, j: (b, CTX_TILE if lat_is_combined else 0, 0))
    return lat, ctx


def _norm_mod_project(x, g_ref, mod_ref, w_ref, o_refs):
    y = x * lax.rsqrt(jnp.mean(x * x, axis=-1, keepdims=True) + EPS) * g_ref[...]
    shift = mod_ref[0, :, 0:D_MODEL]
    scale = mod_ref[0, :, D_MODEL:2 * D_MODEL]
    h = (y * (1.0 + scale) + shift).astype(BF16)
    start = 0
    for o_ref in o_refs:
        width = o_ref.shape[-1]
        o_ref[0] = jnp.dot(h, w_ref[:, start:start + width], preferred_element_type=F32).astype(o_ref.dtype)
        start += width


def _norm_proj_kernel(x_ref, xc_ref, g_ref, mod_ref, w_ref, *o_refs):
    _norm_mod_project(_tile_rows(x_ref, xc_ref, pl.program_id(1)), g_ref, mod_ref, w_ref, o_refs)


def _norm_proj(x_lat, x_ctx, g, mod, w_bf16, out_cols):
    n_out = w_bf16.shape[1]
    assert sum(w for w, _ in out_cols) == n_out
    return pl.pallas_call(
        _norm_proj_kernel,
        out_shape=tuple(jax.ShapeDtypeStruct((BATCH, TOK, w), dt) for w, dt in out_cols),
        grid=(BATCH, N_TILES),
        in_specs=[
            *_lat_ctx_specs(D_MODEL, x_lat is x_ctx),
            pl.BlockSpec((1, D_MODEL), lambda b, j: (0, 0)),
            pl.BlockSpec((1, 1, 3 * D_MODEL), lambda b, j: (_mod_row(b, j), 0, 0)),
            pl.BlockSpec((D_MODEL, n_out), lambda b, j: (0, 0)),
        ],
        out_specs=tuple(pl.BlockSpec((1, TILE, w), lambda b, j: (b, j, 0)) for w, _ in out_cols),
        compiler_params=_params("arbitrary", "arbitrary"),
        name="norm_proj",
    )(x_lat, x_ctx, g.reshape(1, D_MODEL), mod.reshape(8, 1, 3 * D_MODEL), w_bf16)


PT_LAG = 2


def _proj_terms_kernel(x_ref, xc_ref, g_ref, mod_ref, w_ref, mu_ref, w0_ref, w2_ref, a0_ref, a2_ref, kk_ref_, ka_ref,
                       rk_ref, ones_ref, gr_out, four_out, gf_out, r_out, v_out, kk_out, k_out, b_out, lw_out,
                       bonus_out, cur_ref, nxt_ref, halo_ref):
    step = pl.program_id(1)
    j = step - PT_LAG

    @pl.when(step == 0)
    def _():
        cur_ref[...] = jnp.zeros_like(cur_ref)
        nxt_ref[...] = jnp.zeros_like(nxt_ref)
        halo_ref[...] = jnp.zeros_like(halo_ref)

    y = _tile_rows(x_ref, xc_ref, jnp.minimum(step, N_TILES - 1))
    y = y * lax.rsqrt(jnp.mean(y * y, axis=-1, keepdims=True) + EPS) * g_ref[...]
    h = (y * (1.0 + mod_ref[0, :, D_MODEL:2 * D_MODEL]) + mod_ref[0, :, 0:D_MODEL]).astype(BF16)

    def project(lo, hi):
        return jnp.dot(h, w_ref[:, lo:hi], preferred_element_type=F32)

    is_ctx = j == CTX_TILE
    cur_u = cur_ref[...]
    nxt_u = nxt_ref[...]
    up_rows = halo_ref[...]
    dn_rows = nxt_u[0:GRID_W, :]
    x = cur_u.astype(F32)
    p_shift = [project(0, 1024).astype(BF16), project(1024, 2048).astype(BF16)]
    left = pltpu.roll(x, 1, 0)
    right = pltpu.roll(x, TILE - 1, 0)
    row8 = lax.broadcasted_iota(jnp.int32, (8, N_SHIFT), 0)
    inner_first = jnp.where(is_ctx, 0, 1)
    inner_last = jnp.where(is_ctx, 8, 7)

    def zero_rows(a, first):
        parts, pos = [], 0
        for g in range(0, TILE, GRID_W):
            at = g if first else g + GRID_W - 8
            if first:
                keep = row8 >= (1 if g == 0 else inner_first)
            else:
                keep = row8 < (7 if g == TILE - GRID_W else inner_last)
            parts += [a[pos:at], jnp.where(keep, a[at:at + 8], 0.0)]
            pos = at + 8
        return jnp.concatenate([p for p in parts + [a[pos:]] if p.shape[0]], axis=0)

    left = zero_rows(left, True)
    right = zero_rows(right, False)
    up_ok = jnp.where(j == 0, 0.0, 1.0)
    dn_ok = jnp.where(j == CTX_TILE - 1, 0.0, 1.0)
    up = jnp.concatenate([up_rows.astype(F32) * up_ok, x[:TILE - GRID_W]], axis=0)
    down = jnp.concatenate([x[GRID_W:], dn_rows.astype(F32) * dn_ok], axis=0)
    lane = lax.broadcasted_iota(jnp.int32, (1, N_SHIFT), 1)
    sel = jnp.where(is_ctx, lane & 1, lane & 3)
    shifted = jnp.where(sel == 0, left, jnp.where(sel == 1, right, jnp.where(sel == 2, up, down)))
    s = x + (shifted - x) * mu_ref[...]

    r = s[:, 0:RWKV_W]
    k0 = s[:, RWKV_W:2 * RWKV_W]
    v = s[:, 2 * RWKV_W:3 * RWKV_W]
    ones_bd = ones_ref[...]
    kk = k0 * kk_ref_[...]
    kk = kk * lax.rsqrt(jnp.maximum(_head_sum(kk * kk, ones_bd), 1e-24))
    r_out[0] = r.astype(BF16)
    v_out[0] = v.astype(BF16)
    kk_out[0] = kk.astype(BF16)
    bonus = jnp.zeros_like(v)
    for d in range(2):
        wl = s[:, 3 * RWKV_W + d * LORA:3 * RWKV_W + (d + 1) * LORA]
        al = s[:, 3 * RWKV_W + (2 + d) * LORA:3 * RWKV_W + (3 + d) * LORA]
        z = w0_ref[d:d + 1, :] + _dot(jnp.tanh(wl).astype(BF16), w2_ref[d].astype(BF16), None)
        lw_out[d, 0] = -np.float32(np.exp(-0.5)) * _sigmoid(z)
        a = _sigmoid(a0_ref[d:d + 1, :] + _dot(al.astype(BF16), a2_ref[d].astype(BF16), None))
        k = k0 * (1.0 + (a - 1.0) * ka_ref[...])
        k_out[d, 0] = k.astype(BF16)
        b_out[d, 0] = (kk * a).astype(BF16)
        if d == 0:
            p_shift.append(project(2048, N_SHIFT).astype(BF16))
            gr_out[0] = project(N_SHIFT, N_SHIFT + RWKV_W).astype(BF16)
        bonus = bonus + _head_sum(r * k * rk_ref[...], ones_bd) * v
    bonus_out[0] = bonus.astype(BF16)
    four_out[0] = project(N_SHIFT + RWKV_W, N_SHIFT + RWKV_W + FOUR_W)
    gf_out[0] = project(N_SHIFT + RWKV_W + FOUR_W, EVEN_IN).astype(BF16)

    halo_ref[...] = cur_u[TILE - GRID_W:, :]
    cur_ref[...] = nxt_u
    nxt_ref[...] = jnp.concatenate(p_shift, axis=1)


def _proj_terms(x, ctx, g, mod, w_bf16, mu, w0, w2, a0, a2, k_k, k_a, r_k, ones_bd):
    tok = jax.ShapeDtypeStruct((BATCH, TOK, RWKV_W), BF16)
    tok2 = jax.ShapeDtypeStruct((2, BATCH, TOK, RWKV_W), BF16)
    tok2_f32 = jax.ShapeDtypeStruct((2, BATCH, TOK, RWKV_W), F32)
    full = lambda shape: pl.BlockSpec(shape, lambda b, j: (0,) * len(shape))
    cur = lambda j: jnp.minimum(j, N_TILES - 1)
    lag = lambda j: jnp.maximum(j - PT_LAG, 0)
    cur_spec = lambda w: pl.BlockSpec((1, TILE, w), lambda b, j: (b, cur(j), 0))
    tspec = pl.BlockSpec((1, TILE, RWKV_W), lambda b, j: (b, lag(j), 0))
    t2spec = pl.BlockSpec((2, 1, TILE, RWKV_W), lambda b, j: (0, b, lag(j), 0))
    return pl.pallas_call(
        _proj_terms_kernel,
        out_shape=(tok, jax.ShapeDtypeStruct((BATCH, TOK, FOUR_W), F32), jax.ShapeDtypeStruct((BATCH, TOK, FOUR_W), BF16),
                   tok, tok, tok, tok2, tok2, tok2_f32, tok),
        grid=(BATCH, N_TILES + PT_LAG),
        in_specs=[
            *_lat_ctx_specs(D_MODEL, False),
            full((1, D_MODEL)),
            pl.BlockSpec((1, 1, 3 * D_MODEL), lambda b, j: (_mod_row(b, cur(j)), 0, 0)),
            full((D_MODEL, EVEN_IN)),
            full((1, N_SHIFT)), full((2, RWKV_W)), full((2, LORA, RWKV_W)), full((2, RWKV_W)),
            full((2, LORA, RWKV_W)), full((1, RWKV_W)), full((1, RWKV_W)), full((1, RWKV_W)),
            full((2 * PAIR_W, PAIR_W)),
        ],
        out_specs=(cur_spec(RWKV_W), cur_spec(FOUR_W), cur_spec(FOUR_W),
                   tspec, tspec, tspec, t2spec, t2spec, t2spec, tspec),
        scratch_shapes=[pltpu.VMEM((TILE, N_SHIFT), BF16), pltpu.VMEM((TILE, N_SHIFT), BF16),
                        pltpu.VMEM((GRID_W, N_SHIFT), BF16)],
        compiler_params=_params("arbitrary", "arbitrary"),
        name="proj_terms",
    )(x, ctx, g.reshape(1, D_MODEL), mod.reshape(8, 1, 3 * D_MODEL), w_bf16, mu.reshape(1, N_SHIFT), w0, w2, a0, a2,
      k_k.reshape(1, RWKV_W), k_a.reshape(1, RWKV_W), r_k.reshape(1, RWKV_W), ones_bd)


def _rw_chunk_index(d, s):
    fwd = jnp.where(s < RW_CTX_CHUNKS, RW_LAT_CHUNKS + s, s - RW_CTX_CHUNKS)
    return jnp.where(d == 0, fwd, RW_NCHUNK - 1 - s)


def _rwkv_scan_kernel(rf_ref, vf_ref, kkf_ref, rb_ref, vb_ref, kkb_ref, kf_ref, bf_ref, lwf_ref, kb_ref, bb_ref,
                      lwb_ref, yf_ref, yb_ref, h_ref):
    s = pl.program_id(0)
    n = PAIR_W

    @pl.when(s == 0)
    def _():
        h_ref[...] = jnp.zeros_like(h_ref)

    c = RW_CHUNK
    ti = lax.broadcasted_iota(jnp.int32, (c, n), 0)
    li = lax.broadcasted_iota(jnp.int32, (c, n), 1)
    si = li & (c - 1)
    strict_d = (si < ti, si > ti)
    incl_d = (si <= ti, si >= ti)
    ident = jnp.where(si == ti, 1.0, 0.0)
    blk8 = (ti >> 3) == (si >> 3)
    merges = [((ti >> (sh + 1)) == (si >> (sh + 1))) & ((ti >> sh) != (si >> sh)) for sh in (3, 4, 5)]
    head0 = li < RWKV_HEAD
    cumb_d = [jnp.where(m[:, :c], 1.0, 0.0).astype(BF16) for m in incl_d]
    ri = lax.broadcasted_iota(jnp.int32, (n, n), 0)
    ci = lax.broadcasted_iota(jnp.int32, (n, n), 1)
    eye = ri == ci
    same_head = (ri >> 6) == (ci >> 6)

    def stack(xb):
        zero = jnp.zeros_like(xb)
        return jnp.concatenate([jnp.where(head0, xb, zero), jnp.where(head0, zero, xb)], axis=0)

    def prod(x, y):
        return _dot(x.astype(BF16), stack(y.astype(BF16)), None)

    dir_refs = ((rf_ref, vf_ref, kkf_ref, kf_ref, bf_ref, lwf_ref, yf_ref),
                (rb_ref, vb_ref, kkb_ref, kb_ref, bb_ref, lwb_ref, yb_ref))
    pairs = range(2 * BATCH * N_PAIRS)
    drn = [p // (BATCH * N_PAIRS) for p in pairs]
    bat = [(p // N_PAIRS) % BATCH for p in pairs]
    col = [slice((p % N_PAIRS) * PAIR_W, (p % N_PAIRS + 1) * PAIR_W) for p in pairs]
    strict = [strict_d[drn[p]] for p in pairs]
    incl = [incl_d[drn[p]] for p in pairs]
    lw_all, lc_all = {}, {}
    for dd in range(2):
        for bb in range(BATCH):
            lw_b = dir_refs[dd][5][0, bb]
            lw_hi = lw_b.astype(BF16)
            lw_mid, lw_lo = _split_bf16(lw_b - lw_hi.astype(F32))
            lw_all[dd, bb] = lw_b
            lc_all[dd, bb] = _dot(jnp.concatenate([cumb_d[dd]] * 3, axis=1),
                                  jnp.concatenate([lw_hi, lw_mid, lw_lo], axis=0), None)
    ar, kb, ke, vb, g_col = [], [], [], [], []
    for p in pairs:
        r_ref, v_ref, kk_ref, k_ref, b_ref = dir_refs[drn[p]][:5]
        lw, lc = lw_all[drn[p], bat[p]][:, col[p]], lc_all[drn[p], bat[p]][:, col[p]]
        lc_end = jnp.sum(lw, axis=0, keepdims=True)
        g_inv = jnp.exp(-lc)
        g_end = jnp.exp(lc_end - lc)
        kc, bc = k_ref[0, bat[p], :, col[p]], b_ref[0, bat[p], :, col[p]]
        at = (-kk_ref[bat[p], :, col[p]] * jnp.exp(lc - lw)).astype(BF16)
        rt = (r_ref[bat[p], :, col[p]] * jnp.exp(lc)).astype(BF16)
        ar.append(jnp.concatenate([at, rt], axis=0))
        kb.append(jnp.concatenate([stack((kc * g_inv).astype(BF16)), stack((bc * g_inv).astype(BF16))], axis=0))
        ke.append(jnp.concatenate([(kc * g_end).astype(BF16), (bc * g_end).astype(BF16)], axis=0))
        vb.append(v_ref[bat[p], :, col[p]].astype(BF16))
        g_col.append(jnp.sum(jnp.where(eye, jnp.broadcast_to(jnp.exp(lc_end), (n, n)), 0.0), axis=1, keepdims=True))

    quad = [_dot_nt(ar[p], kb[p], None) for p in pairs]
    a_b = [jnp.where(strict[p], quad[p][:c, n:], 0.0) for p in pairs]
    akbk = [jnp.concatenate([jnp.where(strict[p], quad[p][:c, :n], 0.0), jnp.where(incl[p], quad[p][c:, :n], 0.0)],
                            axis=0).astype(BF16) for p in pairs]
    b_b = [jnp.where(incl[p], quad[p][c:, n:], 0.0).astype(BF16) for p in pairs]
    h0 = [h_ref[p] for p in pairs]
    arh = [_dot(ar[p], h0[p].astype(BF16), None) for p in pairs]
    abv = [_dot(akbk[p], stack(vb[p]), None) for p in pairs]

    n1 = [jnp.where(blk8, a_b[p], 0.0) for p in pairs]
    n2 = [prod(n1[p], n1[p]) for p in pairs]
    n4 = [prod(n2[p], n2[p]) for p in pairs]
    t12 = [prod(ident + n1[p], ident + n2[p]) for p in pairs]
    tinv = [prod(t12[p], ident + n4[p]) for p in pairs]
    for m in merges:
        off = [jnp.where(m, a_b[p], 0.0) for p in pairs]
        half = [prod(tinv[p], off[p]) for p in pairs]
        tinv = [tinv[p] + prod(half[p], tinv[p]) for p in pairs]

    ub = [_dot(tinv[p].astype(BF16), stack((arh[p][:c] + abv[p][:c]).astype(BF16)), None).astype(BF16) for p in pairs]
    bbu = [_dot(b_b[p], stack(ub[p]), None) for p in pairs]
    upd = [_dot_tn(ke[p], jnp.concatenate([vb[p], ub[p]], axis=0), None) for p in pairs]
    for p in pairs:
        dir_refs[drn[p]][6][bat[p], :, col[p]] = (arh[p][c:] + abv[p][c:] + bbu[p]).astype(BF16)
        h_ref[p] = h0[p] * g_col[p] + jnp.where(same_head, upd[p], 0.0)


def _rwkv_scan(r, v, kk, k2, b2, lw2):
    def tok_spec(d):
        return pl.BlockSpec((BATCH, RW_CHUNK, RWKV_W), lambda s: (0, _rw_chunk_index(d, s), 0))

    def dir_spec(d):
        return pl.BlockSpec((1, BATCH, RW_CHUNK, RWKV_W), lambda s: (d, 0, _rw_chunk_index(d, s), 0))

    y_shape = jax.ShapeDtypeStruct((BATCH, TOK, RWKV_W), BF16)
    return pl.pallas_call(
        _rwkv_scan_kernel,
        out_shape=(y_shape, y_shape),
        grid=(RW_NCHUNK,),
        in_specs=[tok_spec(0), tok_spec(0), tok_spec(0), tok_spec(1), tok_spec(1), tok_spec(1),
                  dir_spec(0), dir_spec(0), dir_spec(0), dir_spec(1), dir_spec(1), dir_spec(1)],
        out_specs=(tok_spec(0), tok_spec(1)),
        scratch_shapes=[pltpu.VMEM((2 * BATCH * N_PAIRS, PAIR_W, PAIR_W), F32)],
        compiler_params=_params("arbitrary"),
        name="rwkv_scan",
    )(r, v, kk, r, v, kk, k2, b2, lw2, k2, b2, lw2)


def _dft_mats(n):
    i = np.arange(n)
    ang = 2.0 * np.pi * ((i[:, None] * i[None, :]) % n) / n
    return np.cos(ang).astype(np.float32), np.sin(ang).astype(np.float32)


def _chan_dft_mat():
    c, s = _dft_mats(FOUR_GROUP_W)
    eye = np.eye(FOUR_GROUPS, dtype=np.float32)
    return np.concatenate([np.kron(eye, c), np.kron(eye, s)], axis=1)


def _bdot(a, b):
    return jnp.dot(a.astype(BF16), b.astype(BF16), preferred_element_type=F32)


def _four_ctx_kernel(u_ref, cs_ref, c_ref, s_ref, o_ref):
    pq = _bdot(u_ref[0], cs_ref[...])
    scale = np.float32(1.0 / np.sqrt(CTX_LEN * FOUR_GROUP_W))
    o_ref[0] = (_bdot(c_ref[...], pq[:, :FOUR_W]) - _bdot(s_ref[...], pq[:, FOUR_W:])) * scale


def _four_ctx(u):
    c, s = _dft_mats(CTX_LEN)
    full = lambda shape: pl.BlockSpec(shape, lambda b: (0,) * len(shape))
    return pl.pallas_call(
        _four_ctx_kernel,
        out_shape=jax.ShapeDtypeStruct((BATCH, CTX_LEN, FOUR_W), F32),
        grid=(BATCH,),
        in_specs=[pl.BlockSpec((1, CTX_LEN, FOUR_W), lambda b: (b, CTX_TILE, 0)),
                  full((FOUR_W, 2 * FOUR_W)), full((CTX_LEN, CTX_LEN)), full((CTX_LEN, CTX_LEN))],
        out_specs=pl.BlockSpec((1, CTX_LEN, FOUR_W), lambda b: (b, 0, 0)),
        compiler_params=_params("arbitrary"),
        name="fourier_ctx",
    )(u, jnp.asarray(_chan_dft_mat()), jnp.asarray(c), jnp.asarray(s))


L2_PER_STEP = 8


def _four_lat_a_kernel(u_ref, cs_ref, c_ref, s_ref, tc_ref, ts_ref, zr_ref, zi_ref):
    cs = cs_ref[...]
    cm = c_ref[...]
    sm = s_ref[...]
    for i in range(L2_PER_STEP):
        x = u_ref[0, :, i, :]
        pq = _bdot(x, cs)
        cpq = _bdot(cm, pq)
        spq = _bdot(sm, pq)
        zr = cpq[:, :FOUR_W] - spq[:, FOUR_W:]
        zi = -cpq[:, FOUR_W:] - spq[:, :FOUR_W]
        tc, ts = tc_ref[i], ts_ref[i]
        zr_ref[0, i] = zr * tc + zi * ts
        zi_ref[0, i] = zi * tc - zr * ts


K1_PER_STEP = 8


def _four_lat_b_kernel(zr_ref, zi_ref, cs_ref, o_ref):
    scale = np.float32(1.0 / np.sqrt(SEQ * FOUR_GROUP_W))
    cs = cs_ref[...]
    for i in range(K1_PER_STEP):
        z = jnp.concatenate([zr_ref[0, :, i, :], zi_ref[0, :, i, :]], axis=0)
        o_ref[0, :, i, :] = _bdot(cs, z) * scale


def _four_lat(u):
    c1, s1 = _dft_mats(GRID_ROWS)
    c2, s2 = _dft_mats(GRID_W)
    k1 = np.arange(GRID_ROWS)[None, :, None]
    l2 = np.arange(GRID_W)[:, None, None]
    ang = 2.0 * np.pi * (k1 * l2) / SEQ
    tc = np.broadcast_to(np.cos(ang), (GRID_W, GRID_ROWS, FOUR_W)).astype(np.float32)
    ts = np.broadcast_to(np.sin(ang), (GRID_W, GRID_ROWS, FOUR_W)).astype(np.float32)
    u4 = u.reshape(BATCH, TOK // GRID_W, GRID_W, FOUR_W)
    full = lambda shape: pl.BlockSpec(shape, lambda b, g: (0,) * len(shape))
    zshape = jax.ShapeDtypeStruct((BATCH, GRID_W, GRID_ROWS, FOUR_W), F32)
    zspec = pl.BlockSpec((1, L2_PER_STEP, GRID_ROWS, FOUR_W), lambda b, g: (b, g, 0, 0))
    tspec = pl.BlockSpec((L2_PER_STEP, GRID_ROWS, FOUR_W), lambda b, g: (g, 0, 0))
    zr, zi = pl.pallas_call(
        _four_lat_a_kernel,
        out_shape=(zshape, zshape),
        grid=(BATCH, GRID_W // L2_PER_STEP),
        in_specs=[pl.BlockSpec((1, GRID_ROWS, L2_PER_STEP, FOUR_W), lambda b, g: (b, 0, g, 0)),
                  full((FOUR_W, 2 * FOUR_W)), full((GRID_ROWS, GRID_ROWS)), full((GRID_ROWS, GRID_ROWS)),
                  tspec, tspec],
        out_specs=(zspec, zspec),
        compiler_params=_params("arbitrary", "arbitrary"),
        name="fourier_lat_a",
    )(u4, jnp.asarray(_chan_dft_mat()), jnp.asarray(c1), jnp.asarray(s1), jnp.asarray(tc), jnp.asarray(ts))
    zb = pl.BlockSpec((1, GRID_W, K1_PER_STEP, FOUR_W), lambda b, g: (b, 0, g, 0))
    out = pl.pallas_call(
        _four_lat_b_kernel,
        out_shape=jax.ShapeDtypeStruct((BATCH, GRID_W, GRID_ROWS, FOUR_W), F32),
        grid=(BATCH, GRID_ROWS // K1_PER_STEP),
        in_specs=[zb, zb, full((GRID_W, 2 * GRID_W))],
        out_specs=zb,
        compiler_params=_params("arbitrary", "arbitrary"),
        name="fourier_lat_b",
    )(zr, zi, jnp.asarray(np.concatenate([c2, s2], axis=1)))
    return out.reshape(BATCH, SEQ, FOUR_W)


def _post_residual(x, out, npost, gate):
    y = out * lax.rsqrt(jnp.mean(out * out, axis=-1, keepdims=True) + EPS) * npost
    return x + gate * y


def _even_finish_kernel(yf_ref, yb_ref, bonus_ref, gr_ref, gf_ref, four_ref, fourc_ref, x_ref, xc_ref, ones_ref, g_ref,
                        bta_ref, w_ref, np_ref, mod_ref, g1_ref, mod1_ref, w1_ref, o_ref, u1_ref, x1_ref):
    j = pl.program_id(1)
    tile = jnp.minimum(j, N_TILES - 1)

    @pl.when(j == 0)
    def _():
        x1_ref[...] = jnp.zeros_like(x1_ref)

    _norm_mod_project(x1_ref[...], g1_ref, mod1_ref, w1_ref, (u1_ref,))
    o = yf_ref[0].astype(F32) + yb_ref[0].astype(F32) + bonus_ref[0].astype(F32)
    ones_bd = ones_ref[...]
    inv = np.float32(1.0 / RWKV_HEAD)
    mean = _head_sum(o, ones_bd) * inv
    cen = o - mean
    var = _head_sum(cen * cen, ones_bd) * inv
    on = cen * lax.rsqrt(var + RWKV_GN_EPS) * g_ref[...] + bta_ref[...]
    yr = on * _silu(gr_ref[0].astype(F32))
    yf = _tile_rows(four_ref, fourc_ref, tile) * _silu(gf_ref[0].astype(F32))
    z = jnp.concatenate([yr, yf], axis=-1).astype(BF16)
    out = jnp.dot(z, w_ref[...], preferred_element_type=F32)
    x1 = _post_residual(_tile_rows(x_ref, xc_ref, tile), out, np_ref[...], mod_ref[0, :, 2 * D_MODEL:])
    o_ref[0] = x1
    x1_ref[...] = x1


def _even_finish(yf, yb, bonus, gate_r, gate_f, four_lat, four_ctx, x_lat, x_ctx, ones_bd, lnx_g, lnx_b, w_out_bf16,
                 npost, mod, next_g, next_mod, next_w_bf16):
    full = lambda shape: pl.BlockSpec(shape, lambda b, j: (0,) * len(shape))
    n_next = next_w_bf16.shape[1]
    cur = lambda j: jnp.minimum(j, N_TILES - 1)
    prev = lambda j: jnp.maximum(j - 1, 0)
    cur_spec = lambda w: pl.BlockSpec((1, TILE, w), lambda b, j: (b, cur(j), 0))
    mod_spec = lambda t: pl.BlockSpec((1, 1, 3 * D_MODEL), lambda b, j: (_mod_row(b, t(j)), 0, 0))
    return pl.pallas_call(
        _even_finish_kernel,
        out_shape=(jax.ShapeDtypeStruct((BATCH, TOK, D_MODEL), F32), jax.ShapeDtypeStruct((BATCH, TOK, n_next), BF16)),
        grid=(BATCH, N_TILES + 1),
        in_specs=[
            cur_spec(RWKV_W), cur_spec(RWKV_W), cur_spec(RWKV_W), cur_spec(RWKV_W), cur_spec(FOUR_W),
            *_lat_ctx_specs(FOUR_W, False),
            *_lat_ctx_specs(D_MODEL, False),
            full((2 * PAIR_W, PAIR_W)), full((1, RWKV_W)), full((1, RWKV_W)), full((D_MODEL, D_MODEL)),
            full((1, D_MODEL)), mod_spec(cur),
            full((1, D_MODEL)), mod_spec(prev), full((D_MODEL, n_next)),
        ],
        out_specs=(cur_spec(D_MODEL), pl.BlockSpec((1, TILE, n_next), lambda b, j: (b, prev(j), 0))),
        scratch_shapes=[pltpu.VMEM((TILE, D_MODEL), F32)],
        compiler_params=_params("arbitrary", "arbitrary"),
        name="even_finish",
    )(yf, yb, bonus, gate_r, gate_f, four_lat, four_ctx, x_lat, x_ctx, ones_bd, lnx_g.reshape(1, RWKV_W),
      lnx_b.reshape(1, RWKV_W), w_out_bf16, npost.reshape(1, D_MODEL), mod.reshape(8, 1, 3 * D_MODEL),
      next_g.reshape(1, D_MODEL), next_mod.reshape(8, 1, 3 * D_MODEL), next_w_bf16)


def _ret_chunk_index(d, s):
    lat = jnp.where(d == 0, s - 1, RET_NCHUNK - 1 - s)
    return jnp.where(s == 0, RET_NCHUNK - 1, lat)


def _ret_scan_kernel(q_ref, k_ref, v_ref, cos_ref, sin_ref, dl_ref, o_ref, st_ref):
    d = pl.program_id(0)
    s = pl.program_id(2)
    c = RET_CHUNK

    @pl.when(s == 0)
    def _():
        st_ref[...] = jnp.zeros_like(st_ref)

    rev = d == 1
    cosf, sinf = cos_ref[...], sin_ref[...]
    ri = lax.broadcasted_iota(jnp.int32, (c, c), 0)
    ci = lax.broadcasted_iota(jnp.int32, (c, c), 1)
    diff = jnp.where(rev, ci - ri, ri - ci)
    mask = diff >= jnp.where(rev, 1, 0)
    dist = jnp.where(mask, diff, 0).astype(F32)
    ti = lax.broadcasted_iota(jnp.int32, (c, RET_HEAD), 0)
    pos = jnp.where(rev, c - 1 - ti, ti).astype(F32)

    def rope(t):
        return t * cosf + pltpu.roll(t, RET_HEAD // 2, 1) * sinf

    heads = range(RET_HEADS)
    col = [slice(h * RET_HEAD, (h + 1) * RET_HEAD) for h in heads]
    lg, q, k, v = [], [], [], []
    for h in heads:
        x = dl_ref[0, h]
        lg.append((jnp.minimum(x, 0.0) - jnp.log(1.0 + jnp.exp(-jnp.abs(x)))) * np.float32(np.log2(np.e)))
        q.append(rope(q_ref[0, :, col[h]].astype(F32)).astype(BF16))
        k.append(rope(k_ref[0, :, col[h]].astype(F32)) * np.float32(RET_HEAD ** -0.5))
        v.append(v_ref[0, :, col[h]].astype(BF16))
    qk = [_dot_nt(q[h], k[h].astype(BF16), None) for h in heads]
    st = [st_ref[h] for h in heads]
    cross = [_dot(q[h], st[h].astype(BF16), None) for h in heads]
    sc = [(qk[h] * jnp.where(mask, jnp.exp2(dist * jnp.broadcast_to(lg[h][:, :1], (c, c))), 0.0)).astype(BF16)
          for h in heads]
    inner = [_dot(sc[h], v[h], None) for h in heads]
    upd = [_dot_tn((k[h] * jnp.exp2((c - 1.0 - pos) * lg[h])).astype(BF16), v[h], None) for h in heads]
    for h in heads:
        o_ref[0, 0, :, col[h]] = (inner[h] + cross[h] * jnp.exp2((pos + 1.0) * lg[h])).astype(o_ref.dtype)
        st_ref[h] = st[h] * jnp.exp2(np.float32(c) * lg[h]) + upd[h]


def _ret_scan(u2, cosf, sinf, dl):
    def tok_spec(blk):
        return pl.BlockSpec((1, RET_CHUNK, RET_W), lambda d, b, s: (b, _ret_chunk_index(d, s), blk))
    rope_spec = pl.BlockSpec((RET_CHUNK, RET_HEAD), lambda d, b, s: (_ret_chunk_index(d, s), 0))
    return pl.pallas_call(
        _ret_scan_kernel,
        out_shape=jax.ShapeDtypeStruct((2, BATCH, TOK, RET_W), BF16),
        grid=(2, BATCH, RET_NCHUNK),
        in_specs=[tok_spec(0), tok_spec(1), tok_spec(2), rope_spec, rope_spec,
                  pl.BlockSpec((1, RET_HEADS, 1, RET_HEAD), lambda d, b, s: (d, 0, 0, 0))],
        out_specs=pl.BlockSpec((1, 1, RET_CHUNK, RET_W), lambda d, b, s: (d, b, _ret_chunk_index(d, s), 0)),
        scratch_shapes=[pltpu.VMEM((RET_HEADS, RET_HEAD, RET_HEAD), F32)],
        compiler_params=_params("arbitrary", "arbitrary", "arbitrary"),
        name="ret_scan",
    )(u2, u2, u2, cosf, sinf, dl)


def _rope_tables():
    pos = np.arange(SEQ)
    half = RET_HEAD // 2
    inv = ROPE_BASE ** (-np.arange(0, half, 2, dtype=np.float32) / half)
    ang = np.concatenate([(pos // GRID_W)[:, None] * inv[None, :], (pos % GRID_W)[:, None] * inv[None, :]], axis=1)
    cos = np.concatenate([np.cos(ang), np.ones((CTX_LEN, half))], axis=0).astype(np.float32)
    sin = np.concatenate([np.sin(ang), np.zeros((CTX_LEN, half))], axis=0).astype(np.float32)
    return np.concatenate([cos, cos], axis=1), np.concatenate([-sin, sin], axis=1)


def _odd_finish_kernel(o_ref, g_ref, x_ref, w_ref, np_ref, mod_ref, out_ref):
    o = o_ref[0, 0].astype(F32) + o_ref[1, 0].astype(F32)
    parts = []
    for h in range(RET_HEADS):
        oh = o[:, h * RET_HEAD:(h + 1) * RET_HEAD]
        parts.append(oh * lax.rsqrt(jnp.mean(oh * oh, axis=-1, keepdims=True) + EPS))
    z = (jnp.concatenate(parts, axis=-1) * _silu(g_ref[0].astype(F32))).astype(BF16)
    out = jnp.dot(z, w_ref[...], preferred_element_type=F32)
    out_ref[0] = _post_residual(x_ref[0], out, np_ref[...], mod_ref[0, :, 2 * D_MODEL:])


def _odd_finish(o2, u2, xcomb, w_out_bf16, npost, mod):
    full = lambda shape: pl.BlockSpec(shape, lambda b, j: (0,) * len(shape))
    return pl.pallas_call(
        _odd_finish_kernel,
        out_shape=jax.ShapeDtypeStruct((BATCH, SEQ, D_MODEL), F32),
        grid=(BATCH, SEQ // TILE),
        in_specs=[
            pl.BlockSpec((2, 1, TILE, RET_W), lambda b, j: (0, b, j, 0)),
            pl.BlockSpec((1, TILE, RET_W), lambda b, j: (b, j, 3)),
            pl.BlockSpec((1, TILE, D_MODEL), lambda b, j: (b, j, 0)),
            full((RET_W, D_MODEL)), full((1, D_MODEL)),
            pl.BlockSpec((1, 1, 3 * D_MODEL), lambda b, j: (b, 0, 0)),
        ],
        out_specs=pl.BlockSpec((1, TILE, D_MODEL), lambda b, j: (b, j, 0)),
        compiler_params=_params("arbitrary", "arbitrary"),
        name="odd_finish",
    )(o2, u2, xcomb, w_out_bf16, npost.reshape(1, D_MODEL), mod.reshape(8, 1, 3 * D_MODEL))


def _head_ones():
    two_heads = np.kron(np.eye(2, dtype=np.float32), np.ones((RWKV_HEAD, RWKV_HEAD), np.float32))
    return np.concatenate([two_heads, two_heads], axis=0)


def kernel(x, c, ctx, c_ctx, ada_w, ada_b, norm_pre, norm_post, ev_w_in, ev_mu, ev_w0, ev_w2, ev_a0, ev_a2, ev_k_k, ev_k_a, ev_r_k, ev_lnx_g, ev_lnx_b, ev_w_out, od_w_in, od_decay_logit, od_w_out):
    cvec = jnp.concatenate([c, c_ctx[None, :], jnp.zeros((8 - BATCH - 1, D_MODEL), F32)], axis=0)
    mod = _ada_mod(cvec, ada_w, ada_b)
    ones_bd = jnp.asarray(_head_ones(), dtype=BF16)

    gate_r, four_in, gate_f, r, v, kk, k2, b2, lw2, bonus = _proj_terms(
        x, ctx, norm_pre[0], mod[0], ev_w_in[0].astype(BF16), ev_mu[0], ev_w0[0], ev_w2[0], ev_a0[0], ev_a2[0],
        ev_k_k[0], ev_k_a[0], ev_r_k[0].reshape(RWKV_W), ones_bd)
    yf, yb = _rwkv_scan(r, v, kk, k2, b2, lw2)
    xcomb, u2 = _even_finish(yf, yb, bonus, gate_r, gate_f, _four_lat(four_in), _four_ctx(four_in), x, ctx, ones_bd,
                             ev_lnx_g[0], ev_lnx_b[0], ev_w_out[0].astype(BF16), norm_post[0], mod[0],
                             norm_pre[1], mod[1], od_w_in[0].astype(BF16))

    cosf, sinf = _rope_tables()
    dl = jnp.broadcast_to(od_decay_logit[0][:, :, None, None], (2, RET_HEADS, 1, RET_HEAD))
    o2 = _ret_scan(u2, jnp.asarray(cosf), jnp.asarray(sinf), dl)
    return _odd_finish(o2, u2, xcomb, od_w_out[0].astype(BF16), norm_post[1], mod[1])
```

```python
import functools

import numpy as np
import jax
import jax.numpy as jnp
from jax import lax
from jax.experimental import pallas as pl
from jax.experimental.pallas import tpu as pltpu

F32 = jnp.float32
BF16 = jnp.bfloat16
HIGHEST = lax.Precision.HIGHEST

D_MODEL = 1024
BATCH = 2
SEQ = 8192
GRID_W = 64
GRID_ROWS = SEQ // GRID_W
CTX_LEN = 256
TOK = SEQ + CTX_LEN
EPS = 1e-6

RWKV_HEAD = 64
RWKV_HEADS = 12
RWKV_W = RWKV_HEADS * RWKV_HEAD
LORA = 64
FOUR_GROUPS = 4
FOUR_W = 256
FOUR_GROUP_W = 64
N_SHIFT = 3 * RWKV_W + 4 * LORA
EVEN_IN = N_SHIFT + RWKV_W + 2 * FOUR_W
TAIL_W = EVEN_IN - N_SHIFT
RWKV_GN_EPS = 64e-5

RET_HEADS = 8
RET_HEAD = 128
RET_W = 1024
ODD_IN = 4 * RET_W
ROPE_BASE = 10000.0

TILE = 256
N_TILES = TOK // TILE
CTX_TILE = N_TILES - 1
RW_CHUNK = 64
RW_NCHUNK = TOK // RW_CHUNK
RW_CTX_CHUNKS = CTX_LEN // RW_CHUNK
RW_LAT_CHUNKS = SEQ // RW_CHUNK
PAIR_W = 2 * RWKV_HEAD
N_PAIRS = RWKV_W // PAIR_W
RET_CHUNK = 256
RET_NCHUNK = TOK // RET_CHUNK
VMEM_LIMIT = 56 * 1024 * 1024


def _dot(a, b, precision=HIGHEST):
    return jnp.dot(a, b, precision=precision, preferred_element_type=F32)


def _dot_nt(a, b, precision=HIGHEST):
    return lax.dot_general(a, b, (((1,), (1,)), ((), ())), precision=precision, preferred_element_type=F32)


def _dot_tn(a, b, precision=HIGHEST):
    return lax.dot_general(a, b, (((0,), (0,)), ((), ())), precision=precision, preferred_element_type=F32)


def _split_bf16(x):
    hi = x.astype(BF16)
    return hi, (x - hi.astype(F32)).astype(BF16)


def _dot3(a, b):
    ah, al = _split_bf16(a)
    bh, bl = _split_bf16(b)
    return _dot(jnp.concatenate([ah, al], axis=1), jnp.concatenate([bh, bh], axis=0), None) + _dot(ah, bl, None)


def _head_sum(x, ones_bd):
    xb = x.astype(BF16)
    out = []
    for c in range(x.shape[1] // PAIR_W):
        out.append(_dot(xb[:, c * PAIR_W:(c + 1) * PAIR_W], ones_bd, None))
    return jnp.concatenate(out, axis=1)


def _silu(x):
    return x * (1.0 / (1.0 + jnp.exp(-x)))


def _sigmoid(x):
    return 1.0 / (1.0 + jnp.exp(-x))


def _params(*sem):
    return pltpu.CompilerParams(dimension_semantics=sem, vmem_limit_bytes=VMEM_LIMIT)


def _ada_kernel(c_ref, w_ref, b_ref, o_ref):
    o_ref[0] = _dot(_silu(c_ref[...]), w_ref[0]) + b_ref[0]


def _ada_mod(cvec, ada_w, ada_b):
    depth = ada_w.shape[0]
    nt = 3 * D_MODEL // 1024
    return pl.pallas_call(
        _ada_kernel,
        out_shape=jax.ShapeDtypeStruct((depth, 8, 3 * D_MODEL), F32),
        grid=(depth, nt),
        in_specs=[
            pl.BlockSpec((8, D_MODEL), lambda l, n: (0, 0)),
            pl.BlockSpec((1, D_MODEL, 1024), lambda l, n: (l, 0, n)),
            pl.BlockSpec((1, 1, 1024), lambda l, n: (l, 0, n)),
        ],
        out_specs=pl.BlockSpec((1, 8, 1024), lambda l, n: (l, 0, n)),
        compiler_params=_params("arbitrary", "arbitrary"),
        name="ada_mod",
    )(cvec, ada_w, ada_b.reshape(depth, 1, 3 * D_MODEL))


def _mod_row(b, j):
    return jnp.where(j == CTX_TILE, BATCH, b)


def _tile_rows(lat_ref, ctx_ref, tile):
    return jnp.where(tile == CTX_TILE, ctx_ref[0], lat_ref[0])


def _lat_ctx_specs(width, lat_is_combined):
    lat = pl.BlockSpec((1, TILE, width), lambda b, j: (b, jnp.minimum(j, CTX_TILE - 1), 0))
    ctx = pl.BlockSpec((1, TILE, width), lambda b, j: (b, CTX_TILE if lat_is_combined else 0, 0))
    return lat, ctx


def _norm_mod_project(x, g_ref, mod_ref, w_ref, o_refs):
    y = x * lax.rsqrt(jnp.mean(x * x, axis=-1, keepdims=True) + EPS) * g_ref[...]
    shift = mod_ref[0, :, 0:D_MODEL]
    scale = mod_ref[0, :, D_MODEL:2 * D_MODEL]
    h = (y * (1.0 + scale) + shift).astype(BF16)
    start = 0
    for o_ref in o_refs:
        width = o_ref.shape[-1]
        o_ref[0] = jnp.dot(h, w_ref[:, start:start + width], preferred_element_type=F32).astype(o_ref.dtype)
        start += width


def _norm_proj_kernel(x_ref, xc_ref, g_ref, mod_ref, w_ref, *o_refs):
    _norm_mod_project(_tile_rows(x_ref, xc_ref, pl.program_id(1)), g_ref, mod_ref, w_ref, o_refs)


def _norm_proj(x_lat, x_ctx, g, mod, w_bf16, out_cols):
    n_out = w_bf16.shape[1]
    assert sum(w for w, _ in out_cols) == n_out
    return pl.pallas_call(
        _norm_proj_kernel,
        out_shape=tuple(jax.ShapeDtypeStruct((BATCH, TOK, w), dt) for w, dt in out_cols),
        grid=(BATCH, N_TILES),
        in_specs=[
            *_lat_ctx_specs(D_MODEL, x_lat is x_ctx),
            pl.BlockSpec((1, D_MODEL), lambda b, j: (0, 0)),
            pl.BlockSpec((1, 1, 3 * D_MODEL), lambda b, j: (_mod_row(b, j), 0, 0)),
            pl.BlockSpec((D_MODEL, n_out), lambda b, j: (0, 0)),
        ],
        out_specs=tuple(pl.BlockSpec((1, TILE, w), lambda b, j: (b, j, 0)) for w, _ in out_cols),
        compiler_params=_params("arbitrary", "arbitrary"),
        name="norm_proj",
    )(x_lat, x_ctx, g.reshape(1, D_MODEL), mod.reshape(8, 1, 3 * D_MODEL), w_bf16)


PT_LAG = 2


def _proj_terms_kernel(x_ref, xc_ref, g_ref, mod_ref, w_ref, mu_ref, w0_ref, w2_ref, a0_ref, a2_ref, kk_ref_, ka_ref,
                       rk_ref, ones_ref, gr_out, four_out, gf_out, r_out, v_out, kk_out, k_out, b_out, lw_out,
                       bonus_out, cur_ref, nxt_ref, halo_ref):
    step = pl.program_id(1)
    j = step - PT_LAG

    @pl.when(step == 0)
    def _():
        cur_ref[...] = jnp.zeros_like(cur_ref)
        nxt_ref[...] = jnp.zeros_like(nxt_ref)
        halo_ref[...] = jnp.zeros_like(halo_ref)

    y = _tile_rows(x_ref, xc_ref, jnp.minimum(step, N_TILES - 1))
    y = y * lax.rsqrt(jnp.mean(y * y, axis=-1, keepdims=True) + EPS) * g_ref[...]
    h = (y * (1.0 + mod_ref[0, :, D_MODEL:2 * D_MODEL]) + mod_ref[0, :, 0:D_MODEL]).astype(BF16)

    def project(lo, hi):
        return jnp.dot(h, w_ref[:, lo:hi], preferred_element_type=F32)

    is_ctx = j == CTX_TILE
    cur_u = cur_ref[...]
    nxt_u = nxt_ref[...]
    up_rows = halo_ref[...]
    dn_rows = nxt_u[0:GRID_W, :]
    x = cur_u.astype(F32)
    p_shift = [project(0, 1024).astype(BF16), project(1024, 2048).astype(BF16)]
    left = pltpu.roll(x, 1, 0)
    right = pltpu.roll(x, TILE - 1, 0)
    row8 = lax.broadcasted_iota(jnp.int32, (8, N_SHIFT), 0)
    inner_first = jnp.where(is_ctx, 0, 1)
    inner_last = jnp.where(is_ctx, 8, 7)

    def zero_rows(a, first):
        parts, pos = [], 0
        for g in range(0, TILE, GRID_W):
            at = g if first else g + GRID_W - 8
            if first:
                keep = row8 >= (1 if g == 0 else inner_first)
            else:
                keep = row8 < (7 if g == TILE - GRID_W else inner_last)
            parts += [a[pos:at], jnp.where(keep, a[at:at + 8], 0.0)]
            pos = at + 8
        return jnp.concatenate([p for p in parts + [a[pos:]] if p.shape[0]], axis=0)

    left = zero_rows(left, True)
    right = zero_rows(right, False)
    up_ok = jnp.where(j == 0, 0.0, 1.0)
    dn_ok = jnp.where(j == CTX_TILE - 1, 0.0, 1.0)
    up = jnp.concatenate([up_rows.astype(F32) * up_ok, x[:TILE - GRID_W]], axis=0)
    down = jnp.concatenate([x[GRID_W:], dn_rows.astype(F32) * dn_ok], axis=0)
    lane = lax.broadcasted_iota(jnp.int32, (1, N_SHIFT), 1)
    sel = jnp.where(is_ctx, lane & 1, lane & 3)
    shifted = jnp.where(sel == 0, left, jnp.where(sel == 1, right, jnp.where(sel == 2, up, down)))
    s = x + (shifted - x) * mu_ref[...]

    r = s[:, 0:RWKV_W]
    k0 = s[:, RWKV_W:2 * RWKV_W]
    v = s[:, 2 * RWKV_W:3 * RWKV_W]
    ones_bd = ones_ref[...]
    kk = k0 * kk_ref_[...]
    kk = kk * lax.rsqrt(jnp.maximum(_head_sum(kk * kk, ones_bd), 1e-24))
    r_out[0] = r.astype(BF16)
    v_out[0] = v.astype(BF16)
    kk_out[0] = kk.astype(BF16)
    bonus = jnp.zeros_like(v)
    for d in range(2):
        wl = s[:, 3 * RWKV_W + d * LORA:3 * RWKV_W + (d + 1) * LORA]
        al = s[:, 3 * RWKV_W + (2 + d) * LORA:3 * RWKV_W + (3 + d) * LORA]
        z = w0_ref[d:d + 1, :] + _dot(jnp.tanh(wl).astype(BF16), w2_ref[d].astype(BF16), None)
        lw_out[d, 0] = -np.float32(np.exp(-0.5)) * _sigmoid(z)
        a = _sigmoid(a0_ref[d:d + 1, :] + _dot(al.astype(BF16), a2_ref[d].astype(BF16), None))
        k = k0 * (1.0 + (a - 1.0) * ka_ref[...])
        k_out[d, 0] = k.astype(BF16)
        b_out[d, 0] = (kk * a).astype(BF16)
        if d == 0:
            p_shift.append(project(2048, N_SHIFT).astype(BF16))
            gr_out[0] = project(N_SHIFT, N_SHIFT + RWKV_W).astype(BF16)
        bonus = bonus + _head_sum(r * k * rk_ref[...], ones_bd) * v
    bonus_out[0] = bonus.astype(BF16)
    four_out[0] = project(N_SHIFT + RWKV_W, N_SHIFT + RWKV_W + FOUR_W)
    gf_out[0] = project(N_SHIFT + RWKV_W + FOUR_W, EVEN_IN).astype(BF16)

    halo_ref[...] = cur_u[TILE - GRID_W:, :]
    cur_ref[...] = nxt_u
    nxt_ref[...] = jnp.concatenate(p_shift, axis=1)


def _proj_terms(x, ctx, g, mod, w_bf16, mu, w0, w2, a0, a2, k_k, k_a, r_k, ones_bd):
    tok = jax.ShapeDtypeStruct((BATCH, TOK, RWKV_W), BF16)
    tok2 = jax.ShapeDtypeStruct((2, BATCH, TOK, RWKV_W), BF16)
    tok2_f32 = jax.ShapeDtypeStruct((2, BATCH, TOK, RWKV_W), F32)
    full = lambda shape: pl.BlockSpec(shape, lambda b, j: (0,) * len(shape))
    cur = lambda j: jnp.minimum(j, N_TILES - 1)
    lag = lambda j: jnp.maximum(j - PT_LAG, 0)
    cur_spec = lambda w: pl.BlockSpec((1, TILE, w), lambda b, j: (b, cur(j), 0))
    tspec = pl.BlockSpec((1, TILE, RWKV_W), lambda b, j: (b, lag(j), 0))
    t2spec = pl.BlockSpec((2, 1, TILE, RWKV_W), lambda b, j: (0, b, lag(j), 0))
    return pl.pallas_call(
        _proj_terms_kernel,
        out_shape=(tok, jax.ShapeDtypeStruct((BATCH, TOK, FOUR_W), F32), jax.ShapeDtypeStruct((BATCH, TOK, FOUR_W), BF16),
                   tok, tok, tok, tok2, tok2, tok2_f32, tok),
        grid=(BATCH, N_TILES + PT_LAG),
        in_specs=[
            *_lat_ctx_specs(D_MODEL, False),
            full((1, D_MODEL)),
            pl.BlockSpec((1, 1, 3 * D_MODEL), lambda b, j: (_mod_row(b, cur(j)), 0, 0)),
            full((D_MODEL, EVEN_IN)),
            full((1, N_SHIFT)), full((2, RWKV_W)), full((2, LORA, RWKV_W)), full((2, RWKV_W)),
            full((2, LORA, RWKV_W)), full((1, RWKV_W)), full((1, RWKV_W)), full((1, RWKV_W)),
            full((PAIR_W, PAIR_W)),
        ],
        out_specs=(cur_spec(RWKV_W), cur_spec(FOUR_W), cur_spec(FOUR_W),
                   tspec, tspec, tspec, t2spec, t2spec, t2spec, tspec),
        scratch_shapes=[pltpu.VMEM((TILE, N_SHIFT), BF16), pltpu.VMEM((TILE, N_SHIFT), BF16),
                        pltpu.VMEM((GRID_W, N_SHIFT), BF16)],
        compiler_params=_params("arbitrary", "arbitrary"),
        name="proj_terms",
    )(x, ctx, g.reshape(1, D_MODEL), mod.reshape(8, 1, 3 * D_MODEL), w_bf16, mu.reshape(1, N_SHIFT), w0, w2, a0, a2,
      k_k.reshape(1, RWKV_W), k_a.reshape(1, RWKV_W), r_k.reshape(1, RWKV_W), ones_bd)


def _rw_chunk_index(d, s):
    fwd = jnp.where(s < RW_CTX_CHUNKS, RW_LAT_CHUNKS + s, s - RW_CTX_CHUNKS)
    return jnp.where(d == 0, fwd, RW_NCHUNK - 1 - s)


def _rwkv_scan_kernel(rf_ref, vf_ref, kkf_ref, rb_ref, vb_ref, kkb_ref, kf_ref, bf_ref, lwf_ref, kb_ref, bb_ref,
                      lwb_ref, yf_ref, yb_ref, h_ref):
    s = pl.program_id(0)
    n = PAIR_W

    @pl.when(s == 0)
    def _():
        h_ref[...] = jnp.zeros_like(h_ref)

    c = RW_CHUNK
    ti = lax.broadcasted_iota(jnp.int32, (c, n), 0)
    li = lax.broadcasted_iota(jnp.int32, (c, n), 1)
    si = li & (c - 1)
    strict_d = (si < ti, si > ti)
    incl_d = (si <= ti, si >= ti)
    ident = jnp.where(si == ti, 1.0, 0.0)
    blk8 = (ti >> 3) == (si >> 3)
    merges = [((ti >> (sh + 1)) == (si >> (sh + 1))) & ((ti >> sh) != (si >> sh)) for sh in (3, 4, 5)]
    head0 = li < RWKV_HEAD
    cumb_d = [jnp.where(m[:, :c], 1.0, 0.0).astype(BF16) for m in incl_d]
    ri = lax.broadcasted_iota(jnp.int32, (n, n), 0)
    ci = lax.broadcasted_iota(jnp.int32, (n, n), 1)
    eye = ri == ci
    same_head = (ri >> 6) == (ci >> 6)

    def stack(xb):
        zero = jnp.zeros_like(xb)
        return jnp.concatenate([jnp.where(head0, xb, zero), jnp.where(head0, zero, xb)], axis=0)

    def prod(x, y):
        return _dot(x.astype(BF16), stack(y.astype(BF16)), None)

    dir_refs = ((rf_ref, vf_ref, kkf_ref, kf_ref, bf_ref, lwf_ref, yf_ref),
                (rb_ref, vb_ref, kkb_ref, kb_ref, bb_ref, lwb_ref, yb_ref))
    pairs = range(2 * BATCH * N_PAIRS)
    drn = [p // (BATCH * N_PAIRS) for p in pairs]
    bat = [(p // N_PAIRS) % BATCH for p in pairs]
    col = [slice((p % N_PAIRS) * PAIR_W, (p % N_PAIRS + 1) * PAIR_W) for p in pairs]
    strict = [strict_d[drn[p]] for p in pairs]
    incl = [incl_d[drn[p]] for p in pairs]
    lw_all, lc_all = {}, {}
    for dd in range(2):
        for bb in range(BATCH):
            lw_b = dir_refs[dd][5][0, bb]
            lw_hi = lw_b.astype(BF16)
            lw_mid, lw_lo = _split_bf16(lw_b - lw_hi.astype(F32))
            lw_all[dd, bb] = lw_b
            lc_all[dd, bb] = _dot(jnp.concatenate([cumb_d[dd]] * 3, axis=1),
                                  jnp.concatenate([lw_hi, lw_mid, lw_lo], axis=0), None)
    ar, kb, ke, vb, g_col = [], [], [], [], []
    for p in pairs:
        r_ref, v_ref, kk_ref, k_ref, b_ref = dir_refs[drn[p]][:5]
        lw, lc = lw_all[drn[p], bat[p]][:, col[p]], lc_all[drn[p], bat[p]][:, col[p]]
        lc_end = jnp.sum(lw, axis=0, keepdims=True)
        g_inv = jnp.exp(-lc)
        g_end = jnp.exp(lc_end - lc)
        kc, bc = k_ref[0, bat[p], :, col[p]], b_ref[0, bat[p], :, col[p]]
        at = (-kk_ref[bat[p], :, col[p]] * jnp.exp(lc - lw)).astype(BF16)
        rt = (r_ref[bat[p], :, col[p]] * jnp.exp(lc)).astype(BF16)
        ar.append(jnp.concatenate([at, rt], axis=0))
        kb.append(jnp.concatenate([stack((kc * g_inv).astype(BF16)), stack((bc * g_inv).astype(BF16))], axis=0))
        ke.append(jnp.concatenate([(kc * g_end).astype(BF16), (bc * g_end).astype(BF16)], axis=0))
        vb.append(v_ref[bat[p], :, col[p]].astype(BF16))
        g_col.append(jnp.sum(jnp.where(eye, jnp.broadcast_to(jnp.exp(lc_end), (n, n)), 0.0), axis=1, keepdims=True))

    quad = [_dot_nt(ar[p], kb[p], None) for p in pairs]
    a_b = [jnp.where(strict[p], quad[p][:c, n:], 0.0) for p in pairs]
    akbk = [jnp.concatenate([jnp.where(strict[p], quad[p][:c, :n], 0.0), jnp.where(incl[p], quad[p][c:, :n], 0.0)],
                            axis=0).astype(BF16) for p in pairs]
    b_b = [jnp.where(incl[p], quad[p][c:, n:], 0.0).astype(BF16) for p in pairs]
    h0 = [h_ref[p] for p in pairs]
    arh = [_dot(ar[p], h0[p].astype(BF16), None) for p in pairs]
    abv = [_dot(akbk[p], stack(vb[p]), None) for p in pairs]

    n1 = [jnp.where(blk8, a_b[p], 0.0) for p in pairs]
    n2 = [prod(n1[p], n1[p]) for p in pairs]
    n4 = [prod(n2[p], n2[p]) for p in pairs]
    t12 = [prod(ident + n1[p], ident + n2[p]) for p in pairs]
    tinv = [prod(t12[p], ident + n4[p]) for p in pairs]
    for m in merges:
        off = [jnp.where(m, a_b[p], 0.0) for p in pairs]
        half = [prod(tinv[p], off[p]) for p in pairs]
        tinv = [tinv[p] + prod(half[p], tinv[p]) for p in pairs]

    ub = [_dot(tinv[p].astype(BF16), stack((arh[p][:c] + abv[p][:c]).astype(BF16)), None).astype(BF16) for p in pairs]
    bbu = [_dot(b_b[p], stack(ub[p]), None) for p in pairs]
    upd = [_dot_tn(ke[p], jnp.concatenate([vb[p], ub[p]], axis=0), None) for p in pairs]
    for p in pairs:
        dir_refs[drn[p]][6][bat[p], :, col[p]] = (arh[p][c:] + abv[p][c:] + bbu[p]).astype(BF16)
        h_ref[p] = h0[p] * g_col[p] + jnp.where(same_head, upd[p], 0.0)


def _rwkv_scan(r, v, kk, k2, b2, lw2):
    def tok_spec(d):
        return pl.BlockSpec((BATCH, RW_CHUNK, RWKV_W), lambda s: (0, _rw_chunk_index(d, s), 0))

    def dir_spec(d):
        return pl.BlockSpec((1, BATCH, RW_CHUNK, RWKV_W), lambda s: (d, 0, _rw_chunk_index(d, s), 0))

    y_shape = jax.ShapeDtypeStruct((BATCH, TOK, RWKV_W), BF16)
    return pl.pallas_call(
        _rwkv_scan_kernel,
        out_shape=(y_shape, y_shape),
        grid=(RW_NCHUNK,),
        in_specs=[tok_spec(0), tok_spec(0), tok_spec(0), tok_spec(1), tok_spec(1), tok_spec(1),
                  dir_spec(0), dir_spec(0), dir_spec(0), dir_spec(1), dir_spec(1), dir_spec(1)],
        out_specs=(tok_spec(0), tok_spec(1)),
        scratch_shapes=[pltpu.VMEM((2 * BATCH * N_PAIRS, PAIR_W, PAIR_W), F32)],
        compiler_params=_params("arbitrary"),
        name="rwkv_scan",
    )(r, v, kk, r, v, kk, k2, b2, lw2, k2, b2, lw2)


def _dft_mats(n):
    i = np.arange(n)
    ang = 2.0 * np.pi * ((i[:, None] * i[None, :]) % n) / n
    return np.cos(ang).astype(np.float32), np.sin(ang).astype(np.float32)


def _chan_dft_mat():
    c, s = _dft_mats(FOUR_GROUP_W)
    eye = np.eye(FOUR_GROUPS, dtype=np.float32)
    return np.concatenate([np.kron(eye, c), np.kron(eye, s)], axis=1)


def _bdot(a, b):
    return jnp.dot(a.astype(BF16), b.astype(BF16), preferred_element_type=F32)


def _four_ctx_kernel(u_ref, cs_ref, c_ref, s_ref, o_ref):
    pq = _bdot(u_ref[0], cs_ref[...])
    scale = np.float32(1.0 / np.sqrt(CTX_LEN * FOUR_GROUP_W))
    o_ref[0] = (_bdot(c_ref[...], pq[:, :FOUR_W]) - _bdot(s_ref[...], pq[:, FOUR_W:])) * scale


def _four_ctx(u):
    c, s = _dft_mats(CTX_LEN)
    full = lambda shape: pl.BlockSpec(shape, lambda b: (0,) * len(shape))
    return pl.pallas_call(
        _four_ctx_kernel,
        out_shape=jax.ShapeDtypeStruct((BATCH, CTX_LEN, FOUR_W), F32),
        grid=(BATCH,),
        in_specs=[pl.BlockSpec((1, CTX_LEN, FOUR_W), lambda b: (b, CTX_TILE, 0)),
                  full((FOUR_W, 2 * FOUR_W)), full((CTX_LEN, CTX_LEN)), full((CTX_LEN, CTX_LEN))],
        out_specs=pl.BlockSpec((1, CTX_LEN, FOUR_W), lambda b: (b, 0, 0)),
        compiler_params=_params("arbitrary"),
        name="fourier_ctx",
    )(u, jnp.asarray(_chan_dft_mat()), jnp.asarray(c), jnp.asarray(s))


L2_PER_STEP = 8


def _four_lat_a_kernel(u_ref, cs_ref, c_ref, s_ref, tc_ref, ts_ref, zr_ref, zi_ref):
    cs = cs_ref[...]
    cm = c_ref[...]
    sm = s_ref[...]
    for i in range(L2_PER_STEP):
        x = u_ref[0, :, i, :]
        pq = _bdot(x, cs)
        cpq = _bdot(cm, pq)
        spq = _bdot(sm, pq)
        zr = cpq[:, :FOUR_W] - spq[:, FOUR_W:]
        zi = -cpq[:, FOUR_W:] - spq[:, :FOUR_W]
        tc, ts = tc_ref[i], ts_ref[i]
        zr_ref[0, i] = zr * tc + zi * ts
        zi_ref[0, i] = zi * tc - zr * ts


K1_PER_STEP = 8


def _four_lat_b_kernel(zr_ref, zi_ref, cs_ref, o_ref):
    scale = np.float32(1.0 / np.sqrt(SEQ * FOUR_GROUP_W))
    cs = cs_ref[...]
    for i in range(K1_PER_STEP):
        z = jnp.concatenate([zr_ref[0, :, i, :], zi_ref[0, :, i, :]], axis=0)
        o_ref[0, :, i, :] = _bdot(cs, z) * scale


def _four_lat(u):
    c1, s1 = _dft_mats(GRID_ROWS)
    c2, s2 = _dft_mats(GRID_W)
    k1 = np.arange(GRID_ROWS)[None, :, None]
    l2 = np.arange(GRID_W)[:, None, None]
    ang = 2.0 * np.pi * (k1 * l2) / SEQ
    tc = np.broadcast_to(np.cos(ang), (GRID_W, GRID_ROWS, FOUR_W)).astype(np.float32)
    ts = np.broadcast_to(np.sin(ang), (GRID_W, GRID_ROWS, FOUR_W)).astype(np.float32)
    u4 = u.reshape(BATCH, TOK // GRID_W, GRID_W, FOUR_W)
    full = lambda shape: pl.BlockSpec(shape, lambda b, g: (0,) * len(shape))
    zshape = jax.ShapeDtypeStruct((BATCH, GRID_W, GRID_ROWS, FOUR_W), F32)
    zspec = pl.BlockSpec((1, L2_PER_STEP, GRID_ROWS, FOUR_W), lambda b, g: (b, g, 0, 0))
    tspec = pl.BlockSpec((L2_PER_STEP, GRID_ROWS, FOUR_W), lambda b, g: (g, 0, 0))
    zr, zi = pl.pallas_call(
        _four_lat_a_kernel,
        out_shape=(zshape, zshape),
        grid=(BATCH, GRID_W // L2_PER_STEP),
        in_specs=[pl.BlockSpec((1, GRID_ROWS, L2_PER_STEP, FOUR_W), lambda b, g: (b, 0, g, 0)),
                  full((FOUR_W, 2 * FOUR_W)), full((GRID_ROWS, GRID_ROWS)), full((GRID_ROWS, GRID_ROWS)),
                  tspec, tspec],
        out_specs=(zspec, zspec),
        compiler_params=_params("arbitrary", "arbitrary"),
        name="fourier_lat_a",
    )(u4, jnp.asarray(_chan_dft_mat()), jnp.asarray(c1), jnp.asarray(s1), jnp.asarray(tc), jnp.asarray(ts))
    zb = pl.BlockSpec((1, GRID_W, K1_PER_STEP, FOUR_W), lambda b, g: (b, 0, g, 0))
    out = pl.pallas_call(
        _four_lat_b_kernel,
        out_shape=jax.ShapeDtypeStruct((BATCH, GRID_W, GRID_ROWS, FOUR_W), F32),
        grid=(BATCH, GRID_ROWS // K1_PER_STEP),
        in_specs=[zb, zb, full((GRID_W, 2 * GRID_W))],
        out_specs=zb,
        compiler_params=_params("arbitrary", "arbitrary"),
        name="fourier_lat_b",
    )(zr, zi, jnp.asarray(np.concatenate([c2, s2], axis=1)))
    return out.reshape(BATCH, SEQ, FOUR_W)


def _post_residual(x, out, npost, gate):
    y = out * lax.rsqrt(jnp.mean(out * out, axis=-1, keepdims=True) + EPS) * npost
    return x + gate * y


def _retention_project(x, g_ref, mod_ref, w_ref, cos_ref, sin_ref, u_ref):
    y = x * lax.rsqrt(jnp.mean(x * x, axis=-1, keepdims=True) + EPS) * g_ref[...]
    h = (y * (1.0 + mod_ref[0, :, D_MODEL:2 * D_MODEL]) + mod_ref[0, :, 0:D_MODEL]).astype(BF16)
    cosf, sinf = cos_ref[...], sin_ref[...]
    for c in range(RET_HEADS):
        t2 = jnp.dot(h, w_ref[:, c * 2 * RET_HEAD:(c + 1) * 2 * RET_HEAD], preferred_element_type=F32)
        for i in range(2):
            cols = slice((2 * c + i) * RET_HEAD, (2 * c + i + 1) * RET_HEAD)
            t = t2[:, i * RET_HEAD:(i + 1) * RET_HEAD]
            t = t * cosf + pltpu.roll(t, RET_HEAD // 2, 1) * sinf
            if 2 * c + i >= RET_HEADS:
                t = t * np.float32(RET_HEAD ** -0.5)
            u_ref[0, :, cols] = t.astype(u_ref.dtype)
    for c in range(2, 4):
        cols = slice(c * RET_W, (c + 1) * RET_W)
        u_ref[0, :, cols] = jnp.dot(h, w_ref[:, cols], preferred_element_type=F32).astype(u_ref.dtype)


def _even_finish_kernel(yf_ref, yb_ref, bonus_ref, gr_ref, gf_ref, four_ref, fourc_ref, x_ref, xc_ref, ones_ref, g_ref,
                        bta_ref, w_ref, np_ref, mod_ref, g1_ref, mod1_ref, w1_ref, cos_ref, sin_ref, o_ref, u1_ref,
                        x1_ref):
    j = pl.program_id(1)
    tile = jnp.minimum(j, N_TILES - 1)

    @pl.when(j == 0)
    def _():
        x1_ref[...] = jnp.zeros_like(x1_ref)

    _retention_project(x1_ref[...], g1_ref, mod1_ref, w1_ref, cos_ref, sin_ref, u1_ref)
    o = yf_ref[0].astype(F32) + yb_ref[0].astype(F32) + bonus_ref[0].astype(F32)
    ones_bd = ones_ref[...]
    inv = np.float32(1.0 / RWKV_HEAD)
    mean = _head_sum(o, ones_bd) * inv
    cen = o - mean
    var = _head_sum(cen * cen, ones_bd) * inv
    on = cen * lax.rsqrt(var + RWKV_GN_EPS) * g_ref[...] + bta_ref[...]
    yr = on * _silu(gr_ref[0].astype(F32))
    yf = _tile_rows(four_ref, fourc_ref, tile) * _silu(gf_ref[0].astype(F32))
    z = jnp.concatenate([yr, yf], axis=-1).astype(BF16)
    out = jnp.dot(z, w_ref[...], preferred_element_type=F32)
    x1 = _post_residual(_tile_rows(x_ref, xc_ref, tile), out, np_ref[...], mod_ref[0, :, 2 * D_MODEL:])
    o_ref[0] = x1
    x1_ref[...] = x1


def _even_finish(yf, yb, bonus, gate_r, gate_f, four_lat, four_ctx, x_lat, x_ctx, ones_bd, lnx_g, lnx_b, w_out_bf16,
                 npost, mod, next_g, next_mod, next_w_bf16, cosf, sinf):
    full = lambda shape: pl.BlockSpec(shape, lambda b, j: (0,) * len(shape))
    n_next = next_w_bf16.shape[1]
    cur = lambda j: jnp.minimum(j, N_TILES - 1)
    prev = lambda j: jnp.maximum(j - 1, 0)
    cur_spec = lambda w: pl.BlockSpec((1, TILE, w), lambda b, j: (b, cur(j), 0))
    mod_spec = lambda t: pl.BlockSpec((1, 1, 3 * D_MODEL), lambda b, j: (_mod_row(b, t(j)), 0, 0))
    return pl.pallas_call(
        _even_finish_kernel,
        out_shape=(jax.ShapeDtypeStruct((BATCH, TOK, D_MODEL), F32), jax.ShapeDtypeStruct((BATCH, TOK, n_next), BF16)),
        grid=(BATCH, N_TILES + 1),
        in_specs=[
            cur_spec(RWKV_W), cur_spec(RWKV_W), cur_spec(RWKV_W), cur_spec(RWKV_W), cur_spec(FOUR_W),
            *_lat_ctx_specs(FOUR_W, False),
            *_lat_ctx_specs(D_MODEL, False),
            full((PAIR_W, PAIR_W)), full((1, RWKV_W)), full((1, RWKV_W)), full((D_MODEL, D_MODEL)),
            full((1, D_MODEL)), mod_spec(cur),
            full((1, D_MODEL)), mod_spec(prev), full((D_MODEL, n_next)),
            pl.BlockSpec((TILE, RET_HEAD), lambda b, j: (prev(j), 0)),
            pl.BlockSpec((TILE, RET_HEAD), lambda b, j: (prev(j), 0)),
        ],
        out_specs=(cur_spec(D_MODEL), pl.BlockSpec((1, TILE, n_next), lambda b, j: (b, prev(j), 0))),
        scratch_shapes=[pltpu.VMEM((TILE, D_MODEL), F32)],
        compiler_params=_params("arbitrary", "arbitrary"),
        name="even_finish",
    )(yf, yb, bonus, gate_r, gate_f, four_lat, four_ctx, x_lat, x_ctx, ones_bd, lnx_g.reshape(1, RWKV_W),
      lnx_b.reshape(1, RWKV_W), w_out_bf16, npost.reshape(1, D_MODEL), mod.reshape(8, 1, 3 * D_MODEL),
      next_g.reshape(1, D_MODEL), next_mod.reshape(8, 1, 3 * D_MODEL), next_w_bf16, cosf, sinf)


def _ret_chunk_index(d, s):
    lat = jnp.where(d == 0, s - 1, RET_NCHUNK - 1 - s)
    return jnp.where(s == 0, RET_NCHUNK - 1, lat)


def _ret_scan_kernel(q_ref, k_ref, v_ref, dl_ref, o_ref, st_ref):
    d = pl.program_id(0)
    s = pl.program_id(2)
    c = RET_CHUNK

    @pl.when(s == 0)
    def _():
        st_ref[...] = jnp.zeros_like(st_ref)

    rev = d == 1
    ri = lax.broadcasted_iota(jnp.int32, (c, c), 0)
    ci = lax.broadcasted_iota(jnp.int32, (c, c), 1)
    diff = jnp.where(rev, ci - ri, ri - ci)
    mask = diff >= jnp.where(rev, 1, 0)
    dist = jnp.where(mask, diff, 0).astype(F32)
    ti = lax.broadcasted_iota(jnp.int32, (c, RET_HEAD), 0)
    pos = jnp.where(rev, c - 1 - ti, ti).astype(F32)

    heads = range(RET_HEADS)
    col = [slice(h * RET_HEAD, (h + 1) * RET_HEAD) for h in heads]
    lg, q, k, v = [], [], [], []
    for h in heads:
        x = dl_ref[0, h]
        lg.append((jnp.minimum(x, 0.0) - jnp.log(1.0 + jnp.exp(-jnp.abs(x)))) * np.float32(np.log2(np.e)))
        q.append(q_ref[0, :, col[h]])
        k.append(k_ref[0, :, col[h]])
        v.append(v_ref[0, :, col[h]])
    qk = [_dot_nt(q[h], k[h], None) for h in heads]
    st = [st_ref[h] for h in heads]
    cross = [_dot(q[h], st[h].astype(BF16), None) for h in heads]
    sc = [(qk[h] * jnp.where(mask, jnp.exp2(dist * jnp.broadcast_to(lg[h][:, :1], (c, c))), 0.0)).astype(BF16)
          for h in heads]
    inner = [_dot(sc[h], v[h], None) for h in heads]
    upd = [_dot_tn((k[h].astype(F32) * jnp.exp2((c - 1.0 - pos) * lg[h])).astype(BF16), v[h], None) for h in heads]
    for h in heads:
        o_ref[0, 0, :, col[h]] = (inner[h] + cross[h] * jnp.exp2((pos + 1.0) * lg[h])).astype(o_ref.dtype)
        st_ref[h] = st[h] * jnp.exp2(np.float32(c) * lg[h]) + upd[h]


def _ret_scan(u2, dl):
    def tok_spec(blk):
        return pl.BlockSpec((1, RET_CHUNK, RET_W), lambda d, b, s: (b, _ret_chunk_index(d, s), blk))
    return pl.pallas_call(
        _ret_scan_kernel,
        out_shape=jax.ShapeDtypeStruct((2, BATCH, TOK, RET_W), BF16),
        grid=(2, BATCH, RET_NCHUNK),
        in_specs=[tok_spec(0), tok_spec(1), tok_spec(2),
                  pl.BlockSpec((1, RET_HEADS, 1, RET_HEAD), lambda d, b, s: (d, 0, 0, 0))],
        out_specs=pl.BlockSpec((1, 1, RET_CHUNK, RET_W), lambda d, b, s: (d, b, _ret_chunk_index(d, s), 0)),
        scratch_shapes=[pltpu.VMEM((RET_HEADS, RET_HEAD, RET_HEAD), F32)],
        compiler_params=_params("arbitrary", "arbitrary", "arbitrary"),
        name="ret_scan",
    )(u2, u2, u2, dl)


def _rope_tables():
    pos = np.arange(SEQ)
    half = RET_HEAD // 2
    inv = ROPE_BASE ** (-np.arange(0, half, 2, dtype=np.float32) / half)
    ang = np.concatenate([(pos // GRID_W)[:, None] * inv[None, :], (pos % GRID_W)[:, None] * inv[None, :]], axis=1)
    cos = np.concatenate([np.cos(ang), np.ones((CTX_LEN, half))], axis=0).astype(np.float32)
    sin = np.concatenate([np.sin(ang), np.zeros((CTX_LEN, half))], axis=0).astype(np.float32)
    return np.concatenate([cos, cos], axis=1), np.concatenate([-sin, sin], axis=1)


def _odd_finish_kernel(o_ref, g_ref, x_ref, w_ref, np_ref, mod_ref, out_ref):
    o = o_ref[0, 0].astype(F32) + o_ref[1, 0].astype(F32)
    parts = []
    for h in range(RET_HEADS):
        oh = o[:, h * RET_HEAD:(h + 1) * RET_HEAD]
        parts.append(oh * lax.rsqrt(jnp.mean(oh * oh, axis=-1, keepdims=True) + EPS))
    z = (jnp.concatenate(parts, axis=-1) * _silu(g_ref[0].astype(F32))).astype(BF16)
    out = jnp.dot(z, w_ref[...], preferred_element_type=F32)
    out_ref[0] = _post_residual(x_ref[0], out, np_ref[...], mod_ref[0, :, 2 * D_MODEL:])


def _odd_finish(o2, u2, xcomb, w_out_bf16, npost, mod):
    full = lambda shape: pl.BlockSpec(shape, lambda b, j: (0,) * len(shape))
    return pl.pallas_call(
        _odd_finish_kernel,
        out_shape=jax.ShapeDtypeStruct((BATCH, SEQ, D_MODEL), F32),
        grid=(BATCH, SEQ // TILE),
        in_specs=[
            pl.BlockSpec((2, 1, TILE, RET_W), lambda b, j: (0, b, j, 0)),
            pl.BlockSpec((1, TILE, RET_W), lambda b, j: (b, j, 3)),
            pl.BlockSpec((1, TILE, D_MODEL), lambda b, j: (b, j, 0)),
            full((RET_W, D_MODEL)), full((1, D_MODEL)),
            pl.BlockSpec((1, 1, 3 * D_MODEL), lambda b, j: (b, 0, 0)),
        ],
        out_specs=pl.BlockSpec((1, TILE, D_MODEL), lambda b, j: (b, j, 0)),
        compiler_params=_params("arbitrary", "arbitrary"),
        name="odd_finish",
    )(o2, u2, xcomb, w_out_bf16, npost.reshape(1, D_MODEL), mod.reshape(8, 1, 3 * D_MODEL))


def _head_ones():
    return np.kron(np.eye(2, dtype=np.float32), np.ones((RWKV_HEAD, RWKV_HEAD), np.float32))


def kernel(x, c, ctx, c_ctx, ada_w, ada_b, norm_pre, norm_post, ev_w_in, ev_mu, ev_w0, ev_w2, ev_a0, ev_a2, ev_k_k, ev_k_a, ev_r_k, ev_lnx_g, ev_lnx_b, ev_w_out, od_w_in, od_decay_logit, od_w_out):
    cvec = jnp.concatenate([c, c_ctx[None, :], jnp.zeros((8 - BATCH - 1, D_MODEL), F32)], axis=0)
    mod = _ada_mod(cvec, ada_w, ada_b)
    ones_bd = jnp.asarray(_head_ones(), dtype=BF16)

    gate_r, four_in, gate_f, r, v, kk, k2, b2, lw2, bonus = _proj_terms(
        x, ctx, norm_pre[0], mod[0], ev_w_in[0].astype(BF16), ev_mu[0], ev_w0[0], ev_w2[0], ev_a0[0], ev_a2[0],
        ev_k_k[0], ev_k_a[0], ev_r_k[0].reshape(RWKV_W), ones_bd)
    yf, yb = _rwkv_scan(r, v, kk, k2, b2, lw2)
    cosf, sinf = _rope_tables()
    xcomb, u2 = _even_finish(yf, yb, bonus, gate_r, gate_f, _four_lat(four_in), _four_ctx(four_in), x, ctx, ones_bd,
                             ev_lnx_g[0], ev_lnx_b[0], ev_w_out[0].astype(BF16), norm_post[0], mod[0],
                             norm_pre[1], mod[1], od_w_in[0].astype(BF16), jnp.asarray(cosf), jnp.asarray(sinf))

    dl = jnp.broadcast_to(od_decay_logit[0][:, :, None, None], (2, RET_HEADS, 1, RET_HEAD))
    o2 = _ret_scan(u2, dl)
    return _odd_finish(o2, u2, xcomb, od_w_out[0].astype(BF16), norm_post[1], mod[1])
```

```python
import numpy as np
import jax
import jax.numpy as jnp
from jax import lax
from jax.experimental import pallas as pl
from jax.experimental.pallas import tpu as pltpu

F32 = jnp.float32
BF16 = jnp.bfloat16
HIGHEST = lax.Precision.HIGHEST

D_MODEL = 1024
BATCH = 2
SEQ = 8192
GRID_W = 64
GRID_ROWS = SEQ // GRID_W
CTX_LEN = 256
TOK = SEQ + CTX_LEN
EPS = 1e-6

RWKV_HEAD = 64
RWKV_HEADS = 12
RWKV_W = RWKV_HEADS * RWKV_HEAD
LORA = 64
FOUR_GROUPS = 4
FOUR_W = 256
FOUR_GROUP_W = 64
N_SHIFT = 3 * RWKV_W + 4 * LORA
EVEN_IN = N_SHIFT + RWKV_W + 2 * FOUR_W
TAIL_W = EVEN_IN - N_SHIFT
RWKV_GN_EPS = 64e-5

RET_HEADS = 8
RET_HEAD = 128
RET_W = 1024
ODD_IN = 4 * RET_W
ROPE_BASE = 10000.0

TILE = 256
N_TILES = TOK // TILE
CTX_TILE = N_TILES - 1
RW_CHUNK = 64
RW_NCHUNK = TOK // RW_CHUNK
RW_CTX_CHUNKS = CTX_LEN // RW_CHUNK
RW_LAT_CHUNKS = SEQ // RW_CHUNK
PAIR_W = 2 * RWKV_HEAD
N_PAIRS = RWKV_W // PAIR_W
RET_CHUNK = 256
RET_NCHUNK = TOK // RET_CHUNK
VMEM_LIMIT = 56 * 1024 * 1024


def _dot(a, b, precision=HIGHEST):
    return jnp.dot(a, b, precision=precision, preferred_element_type=F32)


def _dot_nt(a, b, precision=HIGHEST):
    return lax.dot_general(a, b, (((1,), (1,)), ((), ())), precision=precision, preferred_element_type=F32)


def _dot_tn(a, b, precision=HIGHEST):
    return lax.dot_general(a, b, (((0,), (0,)), ((), ())), precision=precision, preferred_element_type=F32)


def _split_bf16(x):
    hi = x.astype(BF16)
    return hi, (x - hi.astype(F32)).astype(BF16)


def _head_sum(x, ones_bd):
    xb = x.astype(BF16)
    out = []
    for c in range(x.shape[1] // PAIR_W):
        out.append(_dot(xb[:, c * PAIR_W:(c + 1) * PAIR_W], ones_bd, None))
    return jnp.concatenate(out, axis=1)


def _silu(x):
    return x * (1.0 / (1.0 + jnp.exp(-x)))


def _sigmoid(x):
    return 1.0 / (1.0 + jnp.exp(-x))


def _params(*sem):
    return pltpu.CompilerParams(dimension_semantics=sem, vmem_limit_bytes=VMEM_LIMIT)


def _ada_kernel(c_ref, w_ref, b_ref, o_ref):
    o_ref[0] = _dot(_silu(c_ref[...]), w_ref[0]) + b_ref[0]


def _ada_mod(cvec, ada_w, ada_b):
    depth = ada_w.shape[0]
    nt = 3 * D_MODEL // 1024
    return pl.pallas_call(
        _ada_kernel,
        out_shape=jax.ShapeDtypeStruct((depth, 8, 3 * D_MODEL), F32),
        grid=(depth, nt),
        in_specs=[
            pl.BlockSpec((8, D_MODEL), lambda l, n: (0, 0)),
            pl.BlockSpec((1, D_MODEL, 1024), lambda l, n: (l, 0, n)),
            pl.BlockSpec((1, 1, 1024), lambda l, n: (l, 0, n)),
        ],
        out_specs=pl.BlockSpec((1, 8, 1024), lambda l, n: (l, 0, n)),
        compiler_params=_params("arbitrary", "arbitrary"),
        name="ada_mod",
    )(cvec, ada_w, ada_b.reshape(depth, 1, 3 * D_MODEL))


def _mod_row(b, j):
    return jnp.where(j == CTX_TILE, BATCH, b)


def _tile_rows(lat_ref, ctx_ref, tile):
    return jnp.where(tile == CTX_TILE, ctx_ref[0], lat_ref[0])


def _lat_ctx_specs(width):
    lat = pl.BlockSpec((1, TILE, width), lambda b, j: (b, jnp.minimum(j, CTX_TILE - 1), 0))
    ctx = pl.BlockSpec((1, TILE, width), lambda b, j: (b, 0, 0))
    return lat, ctx


PT_LAG = 2


def _proj_terms_kernel(x_ref, xc_ref, g_ref, mod_ref, w_ref, mu_ref, w0_ref, w2_ref, a0_ref, a2_ref, kk_ref_, ka_ref,
                       rk_ref, ones_ref, gr_out, four_out, gf_out, r_out, v_out, kk_out, k_out, b_out, lw_out,
                       bonus_out, cur_ref, nxt_ref, halo_ref):
    step = pl.program_id(1)
    j = step - PT_LAG

    @pl.when(step == 0)
    def _():
        cur_ref[...] = jnp.zeros_like(cur_ref)
        nxt_ref[...] = jnp.zeros_like(nxt_ref)
        halo_ref[...] = jnp.zeros_like(halo_ref)

    y = _tile_rows(x_ref, xc_ref, jnp.minimum(step, N_TILES - 1))
    gain = g_ref[...] * (1.0 + mod_ref[0, :, D_MODEL:2 * D_MODEL])
    h = (y * lax.rsqrt(jnp.mean(y * y, axis=-1, keepdims=True) + EPS) * gain + mod_ref[0, :, 0:D_MODEL]).astype(BF16)

    def project(lo, hi):
        return jnp.dot(h, w_ref[:, lo:hi], preferred_element_type=F32)

    is_ctx = j == CTX_TILE
    cur_u = cur_ref[...]
    nxt_u = nxt_ref[...]
    up_rows = halo_ref[...]
    dn_rows = nxt_u[0:GRID_W, :]
    x = cur_u.astype(F32)
    p_shift = [project(0, 1024).astype(BF16), project(1024, 2048).astype(BF16)]
    left = pltpu.roll(x, 1, 0)
    right = pltpu.roll(x, TILE - 1, 0)
    row8 = lax.broadcasted_iota(jnp.int32, (8, N_SHIFT), 0)
    inner_first = jnp.where(is_ctx, 0, 1)
    inner_last = jnp.where(is_ctx, 8, 7)

    def zero_rows(a, first):
        parts, pos = [], 0
        for g in range(0, TILE, GRID_W):
            at = g if first else g + GRID_W - 8
            if first:
                keep = row8 >= (1 if g == 0 else inner_first)
            else:
                keep = row8 < (7 if g == TILE - GRID_W else inner_last)
            parts += [a[pos:at], jnp.where(keep, a[at:at + 8], 0.0)]
            pos = at + 8
        return jnp.concatenate([p for p in parts + [a[pos:]] if p.shape[0]], axis=0)

    left = zero_rows(left, True)
    right = zero_rows(right, False)
    up_ok = jnp.where(j == 0, 0.0, 1.0)
    dn_ok = jnp.where(j == CTX_TILE - 1, 0.0, 1.0)
    up = jnp.concatenate([up_rows.astype(F32) * up_ok, x[:TILE - GRID_W]], axis=0)
    down = jnp.concatenate([x[GRID_W:], dn_rows.astype(F32) * dn_ok], axis=0)
    lane = lax.broadcasted_iota(jnp.int32, (1, N_SHIFT), 1)
    sel = jnp.where(is_ctx, lane & 1, lane & 3)
    shifted = jnp.where(sel == 0, left, jnp.where(sel == 1, right, jnp.where(sel == 2, up, down)))
    s = x + (shifted - x) * mu_ref[...]

    r = s[:, 0:RWKV_W]
    k0 = s[:, RWKV_W:2 * RWKV_W]
    v = s[:, 2 * RWKV_W:3 * RWKV_W]
    ones_bd = ones_ref[...]
    kk = k0 * kk_ref_[...]
    kk = kk * lax.rsqrt(jnp.maximum(_head_sum(kk * kk, ones_bd), 1e-24))
    r_out[0] = r.astype(BF16)
    v_out[0] = v.astype(BF16)
    kk_out[0] = kk.astype(BF16)
    bonus = jnp.zeros_like(v)
    for d in range(2):
        wl = s[:, 3 * RWKV_W + d * LORA:3 * RWKV_W + (d + 1) * LORA]
        al = s[:, 3 * RWKV_W + (2 + d) * LORA:3 * RWKV_W + (3 + d) * LORA]
        z = w0_ref[d:d + 1, :] + _dot(jnp.tanh(wl).astype(BF16), w2_ref[d].astype(BF16), None)
        lw_out[d, 0] = -np.float32(np.exp(-0.5)) * _sigmoid(z)
        a = _sigmoid(a0_ref[d:d + 1, :] + _dot(al.astype(BF16), a2_ref[d].astype(BF16), None))
        k = k0 * (1.0 + (a - 1.0) * ka_ref[...])
        k_out[d, 0] = k.astype(BF16)
        b_out[d, 0] = (kk * a).astype(BF16)
        if d == 0:
            p_shift.append(project(2048, N_SHIFT).astype(BF16))
            gr_out[0] = project(N_SHIFT, N_SHIFT + RWKV_W).astype(BF16)
        bonus = bonus + _head_sum(r * k * rk_ref[...], ones_bd) * v
    bonus_out[0] = bonus.astype(BF16)
    four_out[0] = project(N_SHIFT + RWKV_W, N_SHIFT + RWKV_W + FOUR_W)
    gf_out[0] = project(N_SHIFT + RWKV_W + FOUR_W, EVEN_IN).astype(BF16)

    halo_ref[...] = cur_u[TILE - GRID_W:, :]
    cur_ref[...] = nxt_u
    nxt_ref[...] = jnp.concatenate(p_shift, axis=1)


def _proj_terms(x, ctx, g, mod, w_bf16, mu, w0, w2, a0, a2, k_k, k_a, r_k, ones_bd):
    tok = jax.ShapeDtypeStruct((BATCH, TOK, RWKV_W), BF16)
    tok2 = jax.ShapeDtypeStruct((2, BATCH, TOK, RWKV_W), BF16)
    tok2_f32 = jax.ShapeDtypeStruct((2, BATCH, TOK, RWKV_W), F32)
    full = lambda shape: pl.BlockSpec(shape, lambda b, j: (0,) * len(shape))
    cur = lambda j: jnp.minimum(j, N_TILES - 1)
    lag = lambda j: jnp.maximum(j - PT_LAG, 0)
    cur_spec = lambda w: pl.BlockSpec((1, TILE, w), lambda b, j: (b, cur(j), 0))
    tspec = pl.BlockSpec((1, TILE, RWKV_W), lambda b, j: (b, lag(j), 0))
    t2spec = pl.BlockSpec((2, 1, TILE, RWKV_W), lambda b, j: (0, b, lag(j), 0))
    return pl.pallas_call(
        _proj_terms_kernel,
        out_shape=(tok, jax.ShapeDtypeStruct((BATCH, TOK, FOUR_W), F32), jax.ShapeDtypeStruct((BATCH, TOK, FOUR_W), BF16),
                   tok, tok, tok, tok2, tok2, tok2_f32, tok),
        grid=(BATCH, N_TILES + PT_LAG),
        in_specs=[
            *_lat_ctx_specs(D_MODEL),
            full((1, D_MODEL)),
            pl.BlockSpec((1, 1, 3 * D_MODEL), lambda b, j: (_mod_row(b, cur(j)), 0, 0)),
            full((D_MODEL, EVEN_IN)),
            full((1, N_SHIFT)), full((2, RWKV_W)), full((2, LORA, RWKV_W)), full((2, RWKV_W)),
            full((2, LORA, RWKV_W)), full((1, RWKV_W)), full((1, RWKV_W)), full((1, RWKV_W)),
            full((PAIR_W, PAIR_W)),
        ],
        out_specs=(cur_spec(RWKV_W), cur_spec(FOUR_W), cur_spec(FOUR_W),
                   tspec, tspec, tspec, t2spec, t2spec, t2spec, tspec),
        scratch_shapes=[pltpu.VMEM((TILE, N_SHIFT), BF16), pltpu.VMEM((TILE, N_SHIFT), BF16),
                        pltpu.VMEM((GRID_W, N_SHIFT), BF16)],
        compiler_params=_params("arbitrary", "arbitrary"),
        name="proj_terms",
    )(x, ctx, g.reshape(1, D_MODEL), mod.reshape(8, 1, 3 * D_MODEL), w_bf16, mu.reshape(1, N_SHIFT), w0, w2, a0, a2,
      k_k.reshape(1, RWKV_W), k_a.reshape(1, RWKV_W), r_k.reshape(1, RWKV_W), ones_bd)


def _rw_chunk_index(d, s):
    fwd = jnp.where(s < RW_CTX_CHUNKS, RW_LAT_CHUNKS + s, s - RW_CTX_CHUNKS)
    return jnp.where(d == 0, fwd, RW_NCHUNK - 1 - s)


def _rwkv_scan_kernel(rf_ref, vf_ref, kkf_ref, rb_ref, vb_ref, kkb_ref, kf_ref, bf_ref, lwf_ref, kb_ref, bb_ref,
                      lwb_ref, yf_ref, yb_ref, h_ref):
    s = pl.program_id(0)
    n = PAIR_W

    @pl.when(s == 0)
    def _():
        h_ref[...] = jnp.zeros_like(h_ref)

    c = RW_CHUNK
    ti = lax.broadcasted_iota(jnp.int32, (c, n), 0)
    li = lax.broadcasted_iota(jnp.int32, (c, n), 1)
    si = li & (c - 1)
    strict_d = (si < ti, si > ti)
    incl_d = (si <= ti, si >= ti)
    ident = jnp.where(si == ti, 1.0, 0.0)
    blk8 = (ti >> 3) == (si >> 3)
    merges = [((ti >> (sh + 1)) == (si >> (sh + 1))) & ((ti >> sh) != (si >> sh)) for sh in (3, 4, 5)]
    head0 = li < RWKV_HEAD
    cumb_d = [jnp.where(m[:, :c], 1.0, 0.0).astype(BF16) for m in incl_d]
    ri = lax.broadcasted_iota(jnp.int32, (n, n), 0)
    ci = lax.broadcasted_iota(jnp.int32, (n, n), 1)
    eye = ri == ci
    same_head = (ri >> 6) == (ci >> 6)

    def stack(xb):
        zero = jnp.zeros_like(xb)
        return jnp.concatenate([jnp.where(head0, xb, zero), jnp.where(head0, zero, xb)], axis=0)

    def prod(x, y):
        return _dot(x.astype(BF16), stack(y.astype(BF16)), None)

    dir_refs = ((rf_ref, vf_ref, kkf_ref, kf_ref, bf_ref, lwf_ref, yf_ref),
                (rb_ref, vb_ref, kkb_ref, kb_ref, bb_ref, lwb_ref, yb_ref))
    pairs = range(2 * BATCH * N_PAIRS)
    drn = [p // (BATCH * N_PAIRS) for p in pairs]
    bat = [(p // N_PAIRS) % BATCH for p in pairs]
    col = [slice((p % N_PAIRS) * PAIR_W, (p % N_PAIRS + 1) * PAIR_W) for p in pairs]
    strict = [strict_d[drn[p]] for p in pairs]
    incl = [incl_d[drn[p]] for p in pairs]
    lw_all, lc_all = {}, {}
    for dd in range(2):
        for bb in range(BATCH):
            lw_b = dir_refs[dd][5][0, bb]
            lw_hi = lw_b.astype(BF16)
            lw_mid, lw_lo = _split_bf16(lw_b - lw_hi.astype(F32))
            lw_all[dd, bb] = lw_b
            lc_all[dd, bb] = _dot(jnp.concatenate([cumb_d[dd]] * 3, axis=1),
                                  jnp.concatenate([lw_hi, lw_mid, lw_lo], axis=0), None)
    ar, kb, ke, vb, g_col = [], [], [], [], []
    for p in pairs:
        r_ref, v_ref, kk_ref, k_ref, b_ref = dir_refs[drn[p]][:5]
        lw, lc = lw_all[drn[p], bat[p]][:, col[p]], lc_all[drn[p], bat[p]][:, col[p]]
        lc_end = jnp.sum(lw, axis=0, keepdims=True)
        g_inv = jnp.exp(-lc)
        g_end = jnp.exp(lc_end - lc)
        kc, bc = k_ref[0, bat[p], :, col[p]], b_ref[0, bat[p], :, col[p]]
        at = (-kk_ref[bat[p], :, col[p]] * jnp.exp(lc - lw)).astype(BF16)
        rt = (r_ref[bat[p], :, col[p]] * jnp.exp(lc)).astype(BF16)
        ar.append(jnp.concatenate([at, rt], axis=0))
        kb.append(jnp.concatenate([stack((kc * g_inv).astype(BF16)), stack((bc * g_inv).astype(BF16))], axis=0))
        ke.append(jnp.concatenate([(kc * g_end).astype(BF16), (bc * g_end).astype(BF16)], axis=0))
        vb.append(v_ref[bat[p], :, col[p]].astype(BF16))
        g_col.append(jnp.sum(jnp.where(eye, jnp.broadcast_to(jnp.exp(lc_end), (n, n)), 0.0), axis=1, keepdims=True))

    quad = [_dot_nt(ar[p], kb[p], None) for p in pairs]
    a_b = [jnp.where(strict[p], quad[p][:c, n:], 0.0) for p in pairs]
    akbk = [jnp.concatenate([jnp.where(strict[p], quad[p][:c, :n], 0.0), jnp.where(incl[p], quad[p][c:, :n], 0.0)],
                            axis=0).astype(BF16) for p in pairs]
    b_b = [jnp.where(incl[p], quad[p][c:, n:], 0.0).astype(BF16) for p in pairs]
    h0 = [h_ref[p] for p in pairs]
    arh = [_dot(ar[p], h0[p].astype(BF16), None) for p in pairs]
    abv = [_dot(akbk[p], stack(vb[p]), None) for p in pairs]

    n1 = [jnp.where(blk8, a_b[p], 0.0) for p in pairs]
    n2 = [prod(n1[p], n1[p]) for p in pairs]
    n4 = [prod(n2[p], n2[p]) for p in pairs]
    t12 = [prod(ident + n1[p], ident + n2[p]) for p in pairs]
    tinv = [prod(t12[p], ident + n4[p]) for p in pairs]
    for m in merges:
        off = [jnp.where(m, a_b[p], 0.0) for p in pairs]
        half = [prod(tinv[p], off[p]) for p in pairs]
        tinv = [tinv[p] + prod(half[p], tinv[p]) for p in pairs]

    ub = [_dot(tinv[p].astype(BF16), stack((arh[p][:c] + abv[p][:c]).astype(BF16)), None).astype(BF16) for p in pairs]
    bbu = [_dot(b_b[p], stack(ub[p]), None) for p in pairs]
    upd = [_dot_tn(ke[p], jnp.concatenate([vb[p], ub[p]], axis=0), None) for p in pairs]
    for p in pairs:
        dir_refs[drn[p]][6][bat[p], :, col[p]] = (arh[p][c:] + abv[p][c:] + bbu[p]).astype(BF16)
        h_ref[p] = h0[p] * g_col[p] + jnp.where(same_head, upd[p], 0.0)


def _rwkv_scan(r, v, kk, k2, b2, lw2):
    def tok_spec(d):
        return pl.BlockSpec((BATCH, RW_CHUNK, RWKV_W), lambda s: (0, _rw_chunk_index(d, s), 0))

    def dir_spec(d):
        return pl.BlockSpec((1, BATCH, RW_CHUNK, RWKV_W), lambda s: (d, 0, _rw_chunk_index(d, s), 0))

    y_shape = jax.ShapeDtypeStruct((BATCH, TOK, RWKV_W), BF16)
    return pl.pallas_call(
        _rwkv_scan_kernel,
        out_shape=(y_shape, y_shape),
        grid=(RW_NCHUNK,),
        in_specs=[tok_spec(0), tok_spec(0), tok_spec(0), tok_spec(1), tok_spec(1), tok_spec(1),
                  dir_spec(0), dir_spec(0), dir_spec(0), dir_spec(1), dir_spec(1), dir_spec(1)],
        out_specs=(tok_spec(0), tok_spec(1)),
        scratch_shapes=[pltpu.VMEM((2 * BATCH * N_PAIRS, PAIR_W, PAIR_W), F32)],
        compiler_params=_params("arbitrary"),
        name="rwkv_scan",
    )(r, v, kk, r, v, kk, k2, b2, lw2, k2, b2, lw2)


def _dft_mats(n):
    i = np.arange(n)
    ang = 2.0 * np.pi * ((i[:, None] * i[None, :]) % n) / n
    return np.cos(ang).astype(np.float32), np.sin(ang).astype(np.float32)


def _chan_dft_mat():
    c, s = _dft_mats(FOUR_GROUP_W)
    eye = np.eye(FOUR_GROUPS, dtype=np.float32)
    return np.concatenate([np.kron(eye, c), np.kron(eye, s)], axis=1)


def _bdot(a, b):
    return jnp.dot(a.astype(BF16), b.astype(BF16), preferred_element_type=F32)


def _four_ctx_kernel(u_ref, cs_ref, c_ref, s_ref, o_ref):
    pq = _bdot(u_ref[0], cs_ref[...])
    scale = np.float32(1.0 / np.sqrt(CTX_LEN * FOUR_GROUP_W))
    o_ref[0] = (_bdot(c_ref[...], pq[:, :FOUR_W]) - _bdot(s_ref[...], pq[:, FOUR_W:])) * scale


def _four_ctx(u):
    c, s = _dft_mats(CTX_LEN)
    full = lambda shape: pl.BlockSpec(shape, lambda b: (0,) * len(shape))
    return pl.pallas_call(
        _four_ctx_kernel,
        out_shape=jax.ShapeDtypeStruct((BATCH, CTX_LEN, FOUR_W), F32),
        grid=(BATCH,),
        in_specs=[pl.BlockSpec((1, CTX_LEN, FOUR_W), lambda b: (b, CTX_TILE, 0)),
                  full((FOUR_W, 2 * FOUR_W)), full((CTX_LEN, CTX_LEN)), full((CTX_LEN, CTX_LEN))],
        out_specs=pl.BlockSpec((1, CTX_LEN, FOUR_W), lambda b: (b, 0, 0)),
        compiler_params=_params("arbitrary"),
        name="fourier_ctx",
    )(u, jnp.asarray(_chan_dft_mat()), jnp.asarray(c), jnp.asarray(s))


L2_PER_STEP = 16


def _four_lat_a_kernel(u_ref, cs_ref, c_ref, s_ref, tc_ref, ts_ref, zr_ref, zi_ref):
    cs = cs_ref[...]
    cm = c_ref[...]
    sm = s_ref[...]
    for i in range(L2_PER_STEP):
        x = u_ref[0, :, i, :]
        pq = _bdot(x, cs)
        cpq = _bdot(cm, pq)
        spq = _bdot(sm, pq)
        zr = cpq[:, :FOUR_W] - spq[:, FOUR_W:]
        zi = -cpq[:, FOUR_W:] - spq[:, :FOUR_W]
        tc, ts = tc_ref[i], ts_ref[i]
        zr_ref[0, i] = zr * tc + zi * ts
        zi_ref[0, i] = zi * tc - zr * ts


K1_PER_STEP = 16


def _four_lat_b_kernel(zr_ref, zi_ref, cs_ref, o_ref):
    scale = np.float32(1.0 / np.sqrt(SEQ * FOUR_GROUP_W))
    cs = cs_ref[...]
    for i in range(K1_PER_STEP):
        z = jnp.concatenate([zr_ref[0, :, i, :], zi_ref[0, :, i, :]], axis=0)
        o_ref[0, :, i, :] = _bdot(cs, z) * scale


def _four_lat(u):
    c1, s1 = _dft_mats(GRID_ROWS)
    c2, s2 = _dft_mats(GRID_W)
    k1 = np.arange(GRID_ROWS)[None, :, None]
    l2 = np.arange(GRID_W)[:, None, None]
    ang = 2.0 * np.pi * (k1 * l2) / SEQ
    tc = np.broadcast_to(np.cos(ang), (GRID_W, GRID_ROWS, FOUR_W)).astype(np.float32)
    ts = np.broadcast_to(np.sin(ang), (GRID_W, GRID_ROWS, FOUR_W)).astype(np.float32)
    u4 = u.reshape(BATCH, TOK // GRID_W, GRID_W, FOUR_W)
    full = lambda shape: pl.BlockSpec(shape, lambda b, g: (0,) * len(shape))
    zshape = jax.ShapeDtypeStruct((BATCH, GRID_W, GRID_ROWS, FOUR_W), F32)
    zspec = pl.BlockSpec((1, L2_PER_STEP, GRID_ROWS, FOUR_W), lambda b, g: (b, g, 0, 0))
    tspec = pl.BlockSpec((L2_PER_STEP, GRID_ROWS, FOUR_W), lambda b, g: (g, 0, 0))
    zr, zi = pl.pallas_call(
        _four_lat_a_kernel,
        out_shape=(zshape, zshape),
        grid=(BATCH, GRID_W // L2_PER_STEP),
        in_specs=[pl.BlockSpec((1, GRID_ROWS, L2_PER_STEP, FOUR_W), lambda b, g: (b, 0, g, 0)),
                  full((FOUR_W, 2 * FOUR_W)), full((GRID_ROWS, GRID_ROWS)), full((GRID_ROWS, GRID_ROWS)),
                  tspec, tspec],
        out_specs=(zspec, zspec),
        compiler_params=_params("arbitrary", "arbitrary"),
        name="fourier_lat_a",
    )(u4, jnp.asarray(_chan_dft_mat()), jnp.asarray(c1), jnp.asarray(s1), jnp.asarray(tc), jnp.asarray(ts))
    zb = pl.BlockSpec((1, GRID_W, K1_PER_STEP, FOUR_W), lambda b, g: (b, 0, g, 0))
    out = pl.pallas_call(
        _four_lat_b_kernel,
        out_shape=jax.ShapeDtypeStruct((BATCH, GRID_W, GRID_ROWS, FOUR_W), F32),
        grid=(BATCH, GRID_ROWS // K1_PER_STEP),
        in_specs=[zb, zb, full((GRID_W, 2 * GRID_W))],
        out_specs=zb,
        compiler_params=_params("arbitrary", "arbitrary"),
        name="fourier_lat_b",
    )(zr, zi, jnp.asarray(np.concatenate([c2, s2], axis=1)))
    return out.reshape(BATCH, SEQ, FOUR_W)


def _post_residual(x, out, npost, gate):
    y = out * lax.rsqrt(jnp.mean(out * out, axis=-1, keepdims=True) + EPS) * npost
    return x + gate * y


def _retention_project(x, g_ref, mod_ref, w_ref, cos_ref, sin_ref, u_ref):
    gain = g_ref[...] * (1.0 + mod_ref[0, :, D_MODEL:2 * D_MODEL])
    h = (x * lax.rsqrt(jnp.mean(x * x, axis=-1, keepdims=True) + EPS) * gain + mod_ref[0, :, 0:D_MODEL]).astype(BF16)
    cosf, sinf = cos_ref[...], sin_ref[...]
    for c in range(RET_HEADS):
        t2 = jnp.dot(h, w_ref[:, c * 2 * RET_HEAD:(c + 1) * 2 * RET_HEAD], preferred_element_type=F32)
        for i in range(2):
            cols = slice((2 * c + i) * RET_HEAD, (2 * c + i + 1) * RET_HEAD)
            t = t2[:, i * RET_HEAD:(i + 1) * RET_HEAD]
            t = t * cosf + pltpu.roll(t, RET_HEAD // 2, 1) * sinf
            if 2 * c + i >= RET_HEADS:
                t = t * np.float32(RET_HEAD ** -0.5)
            u_ref[0, :, cols] = t.astype(u_ref.dtype)
    for c in range(2, 4):
        cols = slice(c * RET_W, (c + 1) * RET_W)
        u_ref[0, :, cols] = jnp.dot(h, w_ref[:, cols], preferred_element_type=F32).astype(u_ref.dtype)


def _even_finish_kernel(yf_ref, yb_ref, bonus_ref, gr_ref, gf_ref, four_ref, fourc_ref, x_ref, xc_ref, ones_ref, g_ref,
                        bta_ref, w_ref, np_ref, mod_ref, g1_ref, mod1_ref, w1_ref, cos_ref, sin_ref, o_ref, u1_ref,
                        x1_ref):
    j = pl.program_id(1)
    tile = jnp.minimum(j, N_TILES - 1)

    @pl.when(j == 0)
    def _():
        x1_ref[...] = jnp.zeros_like(x1_ref)

    _retention_project(x1_ref[...], g1_ref, mod1_ref, w1_ref, cos_ref, sin_ref, u1_ref)
    o = yf_ref[0].astype(F32) + yb_ref[0].astype(F32) + bonus_ref[0].astype(F32)
    ones_bd = ones_ref[...]
    inv = np.float32(1.0 / RWKV_HEAD)
    mean = _head_sum(o, ones_bd) * inv
    cen = o - mean
    var = _head_sum(cen * cen, ones_bd) * inv
    on = cen * lax.rsqrt(var + RWKV_GN_EPS) * g_ref[...] + bta_ref[...]
    yr = on * _silu(gr_ref[0].astype(F32))
    yf = _tile_rows(four_ref, fourc_ref, tile) * _silu(gf_ref[0].astype(F32))
    z = jnp.concatenate([yr, yf], axis=-1).astype(BF16)
    out = jnp.dot(z, w_ref[...], preferred_element_type=F32)
    x1 = _post_residual(_tile_rows(x_ref, xc_ref, tile), out, np_ref[...], mod_ref[0, :, 2 * D_MODEL:])
    o_ref[0] = x1
    x1_ref[...] = x1


def _even_finish(yf, yb, bonus, gate_r, gate_f, four_lat, four_ctx, x_lat, x_ctx, ones_bd, lnx_g, lnx_b, w_out_bf16,
                 npost, mod, next_g, next_mod, next_w_bf16, cosf, sinf):
    full = lambda shape: pl.BlockSpec(shape, lambda b, j: (0,) * len(shape))
    n_next = next_w_bf16.shape[1]
    cur = lambda j: jnp.minimum(j, N_TILES - 1)
    prev = lambda j: jnp.maximum(j - 1, 0)
    cur_spec = lambda w: pl.BlockSpec((1, TILE, w), lambda b, j: (b, cur(j), 0))
    mod_spec = lambda t: pl.BlockSpec((1, 1, 3 * D_MODEL), lambda b, j: (_mod_row(b, t(j)), 0, 0))
    return pl.pallas_call(
        _even_finish_kernel,
        out_shape=(jax.ShapeDtypeStruct((BATCH, TOK, D_MODEL), F32), jax.ShapeDtypeStruct((BATCH, TOK, n_next), BF16)),
        grid=(BATCH, N_TILES + 1),
        in_specs=[
            cur_spec(RWKV_W), cur_spec(RWKV_W), cur_spec(RWKV_W), cur_spec(RWKV_W), cur_spec(FOUR_W),
            *_lat_ctx_specs(FOUR_W),
            *_lat_ctx_specs(D_MODEL),
            full((PAIR_W, PAIR_W)), full((1, RWKV_W)), full((1, RWKV_W)), full((D_MODEL, D_MODEL)),
            full((1, D_MODEL)), mod_spec(cur),
            full((1, D_MODEL)), mod_spec(prev), full((D_MODEL, n_next)),
            pl.BlockSpec((TILE, RET_HEAD), lambda b, j: (prev(j), 0)),
            pl.BlockSpec((TILE, RET_HEAD), lambda b, j: (prev(j), 0)),
        ],
        out_specs=(cur_spec(D_MODEL), pl.BlockSpec((1, TILE, n_next), lambda b, j: (b, prev(j), 0))),
        scratch_shapes=[pltpu.VMEM((TILE, D_MODEL), F32)],
        compiler_params=_params("arbitrary", "arbitrary"),
        name="even_finish",
    )(yf, yb, bonus, gate_r, gate_f, four_lat, four_ctx, x_lat, x_ctx, ones_bd, lnx_g.reshape(1, RWKV_W),
      lnx_b.reshape(1, RWKV_W), w_out_bf16, npost.reshape(1, D_MODEL), mod.reshape(8, 1, 3 * D_MODEL),
      next_g.reshape(1, D_MODEL), next_mod.reshape(8, 1, 3 * D_MODEL), next_w_bf16, cosf, sinf)


def _ret_chunk_index(d, s):
    lat = jnp.where(d == 0, s - 1, RET_NCHUNK - 1 - s)
    return jnp.where(s == 0, RET_NCHUNK - 1, lat)


def _ret_scan_kernel(qf_ref, kf_ref, vf_ref, qb_ref, kb_ref, vb_ref, dl_ref, of_ref, ob_ref, st_ref):
    s = pl.program_id(1)
    c = RET_CHUNK

    @pl.when(s == 0)
    def _():
        st_ref[...] = jnp.zeros_like(st_ref)

    ri = lax.broadcasted_iota(jnp.int32, (c, c), 0)
    ci = lax.broadcasted_iota(jnp.int32, (c, c), 1)
    ti = lax.broadcasted_iota(jnp.int32, (c, RET_HEAD), 0)
    masks = (ri >= ci, ci > ri)
    dists = (jnp.where(masks[0], ri - ci, 0).astype(F32), jnp.where(masks[1], ci - ri, 0).astype(F32))
    poss = (ti.astype(F32), (c - 1 - ti).astype(F32))
    dir_refs = ((qf_ref, kf_ref, vf_ref, of_ref), (qb_ref, kb_ref, vb_ref, ob_ref))

    streams = range(2 * RET_HEADS)
    drn = [n // RET_HEADS for n in streams]
    col = [slice((n % RET_HEADS) * RET_HEAD, (n % RET_HEADS + 1) * RET_HEAD) for n in streams]
    lg, q, k, v = [], [], [], []
    for n in streams:
        x = dl_ref[drn[n], n % RET_HEADS]
        lg.append((jnp.minimum(x, 0.0) - jnp.log(1.0 + jnp.exp(-jnp.abs(x)))) * np.float32(np.log2(np.e)))
        q.append(dir_refs[drn[n]][0][0, :, col[n]])
        k.append(dir_refs[drn[n]][1][0, :, col[n]])
        v.append(dir_refs[drn[n]][2][0, :, col[n]])
    qk = [_dot_nt(q[n], k[n], None) for n in streams]
    st = [st_ref[n] for n in streams]
    cross = [_dot(q[n], st[n].astype(BF16), None) for n in streams]
    sc = [(qk[n] * jnp.where(masks[drn[n]], jnp.exp2(dists[drn[n]] * jnp.broadcast_to(lg[n][:, :1], (c, c))),
                             0.0)).astype(BF16) for n in streams]
    inner = [_dot(sc[n], v[n], None) for n in streams]
    upd = [_dot_tn((k[n].astype(F32) * jnp.exp2((c - 1.0 - poss[drn[n]]) * lg[n])).astype(BF16), v[n], None)
           for n in streams]
    for n in streams:
        o_ref = dir_refs[drn[n]][3]
        o_ref[0, :, col[n]] = (inner[n] + cross[n] * jnp.exp2((poss[drn[n]] + 1.0) * lg[n])).astype(o_ref.dtype)
        st_ref[n] = st[n] * jnp.exp2(np.float32(c) * lg[n]) + upd[n]


def _ret_scan(u2, dl):
    def tok_spec(d, blk):
        return pl.BlockSpec((1, RET_CHUNK, RET_W), lambda b, s: (b, _ret_chunk_index(d, s), blk))
    o_shape = jax.ShapeDtypeStruct((BATCH, TOK, RET_W), BF16)
    return pl.pallas_call(
        _ret_scan_kernel,
        out_shape=(o_shape, o_shape),
        grid=(BATCH, RET_NCHUNK),
        in_specs=[tok_spec(0, 0), tok_spec(0, 1), tok_spec(0, 2), tok_spec(1, 0), tok_spec(1, 1), tok_spec(1, 2),
                  pl.BlockSpec((2, RET_HEADS, 1, RET_HEAD), lambda b, s: (0, 0, 0, 0))],
        out_specs=(tok_spec(0, 0), tok_spec(1, 0)),
        scratch_shapes=[pltpu.VMEM((2 * RET_HEADS, RET_HEAD, RET_HEAD), F32)],
        compiler_params=_params("arbitrary", "arbitrary"),
        name="ret_scan",
    )(u2, u2, u2, u2, u2, u2, dl)


def _rope_tables():
    pos = np.arange(SEQ)
    half = RET_HEAD // 2
    inv = ROPE_BASE ** (-np.arange(0, half, 2, dtype=np.float32) / half)
    ang = np.concatenate([(pos // GRID_W)[:, None] * inv[None, :], (pos % GRID_W)[:, None] * inv[None, :]], axis=1)
    cos = np.concatenate([np.cos(ang), np.ones((CTX_LEN, half))], axis=0).astype(np.float32)
    sin = np.concatenate([np.sin(ang), np.zeros((CTX_LEN, half))], axis=0).astype(np.float32)
    return np.concatenate([cos, cos], axis=1), np.concatenate([-sin, sin], axis=1)


def _odd_finish_kernel(of_ref, ob_ref, g_ref, x_ref, w_ref, np_ref, mod_ref, out_ref):
    o = of_ref[0].astype(F32) + ob_ref[0].astype(F32)
    parts = []
    for h in range(RET_HEADS):
        oh = o[:, h * RET_HEAD:(h + 1) * RET_HEAD]
        parts.append(oh * lax.rsqrt(jnp.mean(oh * oh, axis=-1, keepdims=True) + EPS))
    z = (jnp.concatenate(parts, axis=-1) * _silu(g_ref[0].astype(F32))).astype(BF16)
    out = jnp.dot(z, w_ref[...], preferred_element_type=F32)
    out_ref[0] = _post_residual(x_ref[0], out, np_ref[...], mod_ref[0, :, 2 * D_MODEL:])


ODD_TILE = 512


def _odd_finish(o_f, o_b, u2, xcomb, w_out_bf16, npost, mod):
    full = lambda shape: pl.BlockSpec(shape, lambda b, j: (0,) * len(shape))
    tile = lambda w, blk: pl.BlockSpec((1, ODD_TILE, w), lambda b, j: (b, j, blk))
    return pl.pallas_call(
        _odd_finish_kernel,
        out_shape=jax.ShapeDtypeStruct((BATCH, SEQ, D_MODEL), F32),
        grid=(BATCH, SEQ // ODD_TILE),
        in_specs=[
            tile(RET_W, 0), tile(RET_W, 0), tile(RET_W, 3), tile(D_MODEL, 0),
            full((RET_W, D_MODEL)), full((1, D_MODEL)),
            pl.BlockSpec((1, 1, 3 * D_MODEL), lambda b, j: (b, 0, 0)),
        ],
        out_specs=tile(D_MODEL, 0),
        compiler_params=_params("arbitrary", "arbitrary"),
        name="odd_finish",
    )(o_f, o_b, u2, xcomb, w_out_bf16, npost.reshape(1, D_MODEL), mod.reshape(8, 1, 3 * D_MODEL))


def _head_ones():
    return np.kron(np.eye(2, dtype=np.float32), np.ones((RWKV_HEAD, RWKV_HEAD), np.float32))


def kernel(x, c, ctx, c_ctx, ada_w, ada_b, norm_pre, norm_post, ev_w_in, ev_mu, ev_w0, ev_w2, ev_a0, ev_a2, ev_k_k, ev_k_a, ev_r_k, ev_lnx_g, ev_lnx_b, ev_w_out, od_w_in, od_decay_logit, od_w_out):
    cvec = jnp.concatenate([c, c_ctx[None, :], jnp.zeros((8 - BATCH - 1, D_MODEL), F32)], axis=0)
    mod = _ada_mod(cvec, ada_w, ada_b)
    ones_bd = jnp.asarray(_head_ones(), dtype=BF16)

    gate_r, four_in, gate_f, r, v, kk, k2, b2, lw2, bonus = _proj_terms(
        x, ctx, norm_pre[0], mod[0], ev_w_in[0].astype(BF16), ev_mu[0], ev_w0[0], ev_w2[0], ev_a0[0], ev_a2[0],
        ev_k_k[0], ev_k_a[0], ev_r_k[0].reshape(RWKV_W), ones_bd)
    yf, yb = _rwkv_scan(r, v, kk, k2, b2, lw2)
    cosf, sinf = _rope_tables()
    xcomb, u2 = _even_finish(yf, yb, bonus, gate_r, gate_f, _four_lat(four_in), _four_ctx(four_in), x, ctx, ones_bd,
                             ev_lnx_g[0], ev_lnx_b[0], ev_w_out[0].astype(BF16), norm_post[0], mod[0],
                             norm_pre[1], mod[1], od_w_in[0].astype(BF16), jnp.asarray(cosf), jnp.asarray(sinf))

    dl = jnp.broadcast_to(od_decay_logit[0][:, :, None, None], (2, RET_HEADS, 1, RET_HEAD))
    o_f, o_b = _ret_scan(u2, dl)
    return _odd_finish(o_f, o_b, u2, xcomb, od_w_out[0].astype(BF16), norm_post[1], mod[1])
```

```python
import numpy as np
import jax
import jax.numpy as jnp
from jax import lax
from jax.experimental import pallas as pl
from jax.experimental.pallas import tpu as pltpu

F32 = jnp.float32
BF16 = jnp.bfloat16
HIGHEST = lax.Precision.HIGHEST

D_MODEL = 1024
BATCH = 2
SEQ = 8192
GRID_W = 64
GRID_ROWS = SEQ // GRID_W
CTX_LEN = 256
TOK = SEQ + CTX_LEN
EPS = 1e-6

RWKV_HEAD = 64
RWKV_HEADS = 12
RWKV_W = RWKV_HEADS * RWKV_HEAD
LORA = 64
FOUR_GROUPS = 4
FOUR_W = 256
FOUR_GROUP_W = 64
N_SHIFT = 3 * RWKV_W + 4 * LORA
EVEN_IN = N_SHIFT + RWKV_W + 2 * FOUR_W
TAIL_W = EVEN_IN - N_SHIFT
RWKV_GN_EPS = 64e-5

RET_HEADS = 8
RET_HEAD = 128
RET_W = 1024
ODD_IN = 4 * RET_W
ROPE_BASE = 10000.0

TILE = 256
N_TILES = TOK // TILE
CTX_TILE = N_TILES - 1
RW_CHUNK = 64
RW_NCHUNK = TOK // RW_CHUNK
RW_CTX_CHUNKS = CTX_LEN // RW_CHUNK
RW_LAT_CHUNKS = SEQ // RW_CHUNK
PAIR_W = 2 * RWKV_HEAD
N_PAIRS = RWKV_W // PAIR_W
RET_CHUNK = 256
RET_NCHUNK = TOK // RET_CHUNK
VMEM_LIMIT = 56 * 1024 * 1024


def _dot(a, b, precision=HIGHEST):
    return jnp.dot(a, b, precision=precision, preferred_element_type=F32)


def _dot_nt(a, b, precision=HIGHEST):
    return lax.dot_general(a, b, (((1,), (1,)), ((), ())), precision=precision, preferred_element_type=F32)


def _dot_tn(a, b, precision=HIGHEST):
    return lax.dot_general(a, b, (((0,), (0,)), ((), ())), precision=precision, preferred_element_type=F32)


def _split_bf16(x):
    hi = x.astype(BF16)
    return hi, (x - hi.astype(F32)).astype(BF16)


def _head_sum(x, ones_bd):
    xb = x.astype(BF16)
    out = []
    for c in range(x.shape[1] // PAIR_W):
        out.append(_dot(xb[:, c * PAIR_W:(c + 1) * PAIR_W], ones_bd, None))
    return jnp.concatenate(out, axis=1)


def _silu(x):
    return x * (1.0 / (1.0 + jnp.exp(-x)))


def _sigmoid(x):
    return 1.0 / (1.0 + jnp.exp(-x))


def _params(*sem):
    return pltpu.CompilerParams(dimension_semantics=sem, vmem_limit_bytes=VMEM_LIMIT)


def _ada_kernel(c_ref, w_ref, b_ref, o_ref):
    o_ref[0] = _dot(_silu(c_ref[...]), w_ref[0]) + b_ref[0]


def _ada_mod(cvec, ada_w, ada_b):
    depth = ada_w.shape[0]
    nt = 3 * D_MODEL // 1024
    return pl.pallas_call(
        _ada_kernel,
        out_shape=jax.ShapeDtypeStruct((depth, 8, 3 * D_MODEL), F32),
        grid=(depth, nt),
        in_specs=[
            pl.BlockSpec((8, D_MODEL), lambda l, n: (0, 0)),
            pl.BlockSpec((1, D_MODEL, 1024), lambda l, n: (l, 0, n)),
            pl.BlockSpec((1, 1, 1024), lambda l, n: (l, 0, n)),
        ],
        out_specs=pl.BlockSpec((1, 8, 1024), lambda l, n: (l, 0, n)),
        compiler_params=_params("arbitrary", "arbitrary"),
        name="ada_mod",
    )(cvec, ada_w, ada_b.reshape(depth, 1, 3 * D_MODEL))


def _mod_row(b, j):
    return jnp.where(j == CTX_TILE, BATCH, b)


def _tile_rows(lat_ref, ctx_ref, tile):
    return jnp.where(tile == CTX_TILE, ctx_ref[0], lat_ref[0])


def _lat_ctx_specs(width):
    lat = pl.BlockSpec((1, TILE, width), lambda b, j: (b, jnp.minimum(j, CTX_TILE - 1), 0))
    ctx = pl.BlockSpec((1, TILE, width), lambda b, j: (b, 0, 0))
    return lat, ctx


PT_LAG = 2
PROJ_BLOCK = 256


def _proj_terms_kernel(x_ref, xc_ref, g_ref, mod_ref, w_ref, mu_ref, w0_ref, w2_ref, a0_ref, a2_ref, kk_ref_, ka_ref,
                       rk_ref, ones_ref, gr_out, four_out, gf_out, r_out, v_out, kk_out, k_out, b_out, lw_out,
                       bonus_out, cur_ref, nxt_ref, halo_ref):
    step = pl.program_id(1)
    j = step - PT_LAG

    @pl.when(step == 0)
    def _():
        cur_ref[...] = jnp.zeros_like(cur_ref)
        nxt_ref[...] = jnp.zeros_like(nxt_ref)
        halo_ref[...] = jnp.zeros_like(halo_ref)

    y = _tile_rows(x_ref, xc_ref, jnp.minimum(step, N_TILES - 1))
    gain = g_ref[...] * (1.0 + mod_ref[0, :, D_MODEL:2 * D_MODEL])
    h = (y * lax.rsqrt(jnp.mean(y * y, axis=-1, keepdims=True) + EPS) * gain + mod_ref[0, :, 0:D_MODEL]).astype(BF16)

    is_ctx = j == CTX_TILE
    cur_u = cur_ref[...]
    nxt_u = nxt_ref[...]
    up_rows = halo_ref[...]
    dn_rows = nxt_u[0:GRID_W, :]
    halo_ref[...] = cur_u[TILE - GRID_W:, :]
    cur_ref[...] = nxt_u
    x = cur_u.astype(F32)

    blocks = iter(range(EVEN_IN // PROJ_BLOCK))
    targets = ((nxt_ref, 0, N_SHIFT), (gr_out, N_SHIFT, RWKV_W), (four_out, N_SHIFT + RWKV_W, FOUR_W),
               (gf_out, N_SHIFT + RWKV_W + FOUR_W, FOUR_W))

    def issue(count):
        for _ in range(count):
            lo = next(blocks) * PROJ_BLOCK
            acc = jnp.dot(h, w_ref[:, lo:lo + PROJ_BLOCK], preferred_element_type=F32)
            for ref, start, width in targets:
                if start <= lo < start + width:
                    if ref is nxt_ref:
                        ref[:, lo - start:lo - start + PROJ_BLOCK] = acc.astype(ref.dtype)
                    else:
                        ref[0, :, lo - start:lo - start + PROJ_BLOCK] = acc.astype(ref.dtype)

    left = pltpu.roll(x, 1, 0)
    right = pltpu.roll(x, TILE - 1, 0)
    row8 = lax.broadcasted_iota(jnp.int32, (8, N_SHIFT), 0)
    inner_first = jnp.where(is_ctx, 0, 1)
    inner_last = jnp.where(is_ctx, 8, 7)

    def zero_rows(a, first):
        parts, pos = [], 0
        for g in range(0, TILE, GRID_W):
            at = g if first else g + GRID_W - 8
            if first:
                keep = row8 >= (1 if g == 0 else inner_first)
            else:
                keep = row8 < (7 if g == TILE - GRID_W else inner_last)
            parts += [a[pos:at], jnp.where(keep, a[at:at + 8], 0.0)]
            pos = at + 8
        return jnp.concatenate([p for p in parts + [a[pos:]] if p.shape[0]], axis=0)

    left = zero_rows(left, True)
    right = zero_rows(right, False)
    up_ok = jnp.where(j == 0, 0.0, 1.0)
    dn_ok = jnp.where(j == CTX_TILE - 1, 0.0, 1.0)
    up = jnp.concatenate([up_rows.astype(F32) * up_ok, x[:TILE - GRID_W]], axis=0)
    down = jnp.concatenate([x[GRID_W:], dn_rows.astype(F32) * dn_ok], axis=0)
    lane = lax.broadcasted_iota(jnp.int32, (1, N_SHIFT), 1)
    sel = jnp.where(is_ctx, lane & 1, lane & 3)
    shifted = jnp.where(sel == 0, left, jnp.where(sel == 1, right, jnp.where(sel == 2, up, down)))
    s = x + (shifted - x) * mu_ref[...]
    issue(6)

    r = s[:, 0:RWKV_W]
    k0 = s[:, RWKV_W:2 * RWKV_W]
    v = s[:, 2 * RWKV_W:3 * RWKV_W]
    ones_bd = ones_ref[...]
    kk = k0 * kk_ref_[...]
    kk = kk * lax.rsqrt(jnp.maximum(_head_sum(kk * kk, ones_bd), 1e-24))
    issue(2)
    r_out[0] = r.astype(BF16)
    v_out[0] = v.astype(BF16)
    kk_out[0] = kk.astype(BF16)
    bonus = jnp.zeros_like(v)
    for d in range(2):
        wl = s[:, 3 * RWKV_W + d * LORA:3 * RWKV_W + (d + 1) * LORA]
        al = s[:, 3 * RWKV_W + (2 + d) * LORA:3 * RWKV_W + (3 + d) * LORA]
        z = w0_ref[d:d + 1, :] + _dot(jnp.tanh(wl).astype(BF16), w2_ref[d].astype(BF16), None)
        lw_out[d, 0] = -np.float32(np.exp(-0.5)) * _sigmoid(z)
        a = _sigmoid(a0_ref[d:d + 1, :] + _dot(al.astype(BF16), a2_ref[d].astype(BF16), None))
        issue(2)
        k = k0 * (1.0 + (a - 1.0) * ka_ref[...])
        k_out[d, 0] = k.astype(BF16)
        b_out[d, 0] = (kk * a).astype(BF16)
        bonus = bonus + _head_sum(r * k * rk_ref[...], ones_bd) * v
        issue(2 if d == 0 else 1)
    bonus_out[0] = bonus.astype(BF16)


def _proj_terms(x, ctx, g, mod, w_bf16, mu, w0, w2, a0, a2, k_k, k_a, r_k, ones_bd):
    tok = jax.ShapeDtypeStruct((BATCH, TOK, RWKV_W), BF16)
    tok2 = jax.ShapeDtypeStruct((2, BATCH, TOK, RWKV_W), BF16)
    tok2_f32 = jax.ShapeDtypeStruct((2, BATCH, TOK, RWKV_W), F32)
    full = lambda shape: pl.BlockSpec(shape, lambda b, j: (0,) * len(shape))
    cur = lambda j: jnp.minimum(j, N_TILES - 1)
    lag = lambda j: jnp.maximum(j - PT_LAG, 0)
    cur_spec = lambda w: pl.BlockSpec((1, TILE, w), lambda b, j: (b, cur(j), 0))
    tspec = pl.BlockSpec((1, TILE, RWKV_W), lambda b, j: (b, lag(j), 0))
    t2spec = pl.BlockSpec((2, 1, TILE, RWKV_W), lambda b, j: (0, b, lag(j), 0))
    return pl.pallas_call(
        _proj_terms_kernel,
        out_shape=(tok, jax.ShapeDtypeStruct((BATCH, TOK, FOUR_W), F32), jax.ShapeDtypeStruct((BATCH, TOK, FOUR_W), BF16),
                   tok, tok, tok, tok2, tok2, tok2_f32, tok),
        grid=(BATCH, N_TILES + PT_LAG),
        in_specs=[
            *_lat_ctx_specs(D_MODEL),
            full((1, D_MODEL)),
            pl.BlockSpec((1, 1, 3 * D_MODEL), lambda b, j: (_mod_row(b, cur(j)), 0, 0)),
            full((D_MODEL, EVEN_IN)),
            full((1, N_SHIFT)), full((2, RWKV_W)), full((2, LORA, RWKV_W)), full((2, RWKV_W)),
            full((2, LORA, RWKV_W)), full((1, RWKV_W)), full((1, RWKV_W)), full((1, RWKV_W)),
            full((PAIR_W, PAIR_W)),
        ],
        out_specs=(cur_spec(RWKV_W), cur_spec(FOUR_W), cur_spec(FOUR_W),
                   tspec, tspec, tspec, t2spec, t2spec, t2spec, tspec),
        scratch_shapes=[pltpu.VMEM((TILE, N_SHIFT), BF16), pltpu.VMEM((TILE, N_SHIFT), BF16),
                        pltpu.VMEM((GRID_W, N_SHIFT), BF16)],
        compiler_params=_params("arbitrary", "arbitrary"),
        name="proj_terms",
    )(x, ctx, g.reshape(1, D_MODEL), mod.reshape(8, 1, 3 * D_MODEL), w_bf16, mu.reshape(1, N_SHIFT), w0, w2, a0, a2,
      k_k.reshape(1, RWKV_W), k_a.reshape(1, RWKV_W), r_k.reshape(1, RWKV_W), ones_bd)


def _rw_chunk_index(d, s):
    fwd = jnp.where(s < RW_CTX_CHUNKS, RW_LAT_CHUNKS + s, s - RW_CTX_CHUNKS)
    return jnp.where(d == 0, fwd, RW_NCHUNK - 1 - s)


def _rwkv_scan_kernel(rf_ref, vf_ref, kkf_ref, rb_ref, vb_ref, kkb_ref, kf_ref, bf_ref, lwf_ref, kb_ref, bb_ref,
                      lwb_ref, yf_ref, yb_ref, h_ref):
    s = pl.program_id(0)
    n = PAIR_W

    @pl.when(s == 0)
    def _():
        h_ref[...] = jnp.zeros_like(h_ref)

    c = RW_CHUNK
    ti = lax.broadcasted_iota(jnp.int32, (c, n), 0)
    li = lax.broadcasted_iota(jnp.int32, (c, n), 1)
    si = li & (c - 1)
    strict_d = (si < ti, si > ti)
    incl_d = (si <= ti, si >= ti)
    ident = jnp.where(si == ti, 1.0, 0.0)
    blk8 = (ti >> 3) == (si >> 3)
    merges = [((ti >> (sh + 1)) == (si >> (sh + 1))) & ((ti >> sh) != (si >> sh)) for sh in (3, 4, 5)]
    head0 = li < RWKV_HEAD
    cumb_d = [jnp.where(m[:, :c], 1.0, 0.0).astype(BF16) for m in incl_d]
    ri = lax.broadcasted_iota(jnp.int32, (n, n), 0)
    ci = lax.broadcasted_iota(jnp.int32, (n, n), 1)
    eye = ri == ci
    same_head = (ri >> 6) == (ci >> 6)

    def stack(xb):
        zero = jnp.zeros_like(xb)
        return jnp.concatenate([jnp.where(head0, xb, zero), jnp.where(head0, zero, xb)], axis=0)

    def prod(x, y):
        return _dot(x.astype(BF16), stack(y.astype(BF16)), None)

    dir_refs = ((rf_ref, vf_ref, kkf_ref, kf_ref, bf_ref, lwf_ref, yf_ref),
                (rb_ref, vb_ref, kkb_ref, kb_ref, bb_ref, lwb_ref, yb_ref))
    pairs = range(2 * BATCH * N_PAIRS)
    drn = [p // (BATCH * N_PAIRS) for p in pairs]
    bat = [(p // N_PAIRS) % BATCH for p in pairs]
    col = [slice((p % N_PAIRS) * PAIR_W, (p % N_PAIRS + 1) * PAIR_W) for p in pairs]
    strict = [strict_d[drn[p]] for p in pairs]
    incl = [incl_d[drn[p]] for p in pairs]
    lw_all, lc_all = {}, {}
    for dd in range(2):
        for bb in range(BATCH):
            lw_b = dir_refs[dd][5][0, bb]
            lw_hi = lw_b.astype(BF16)
            lw_mid, lw_lo = _split_bf16(lw_b - lw_hi.astype(F32))
            lw_all[dd, bb] = lw_b
            lc_all[dd, bb] = _dot(jnp.concatenate([cumb_d[dd]] * 3, axis=1),
                                  jnp.concatenate([lw_hi, lw_mid, lw_lo], axis=0), None)
    ar, kb, ke, vb, g_col = [], [], [], [], []
    for p in pairs:
        r_ref, v_ref, kk_ref, k_ref, b_ref = dir_refs[drn[p]][:5]
        lw, lc = lw_all[drn[p], bat[p]][:, col[p]], lc_all[drn[p], bat[p]][:, col[p]]
        lc_end = jnp.sum(lw, axis=0, keepdims=True)
        g_inv = jnp.exp(-lc)
        g_end = jnp.exp(lc_end - lc)
        kc, bc = k_ref[0, bat[p], :, col[p]], b_ref[0, bat[p], :, col[p]]
        at = (-kk_ref[bat[p], :, col[p]] * jnp.exp(lc - lw)).astype(BF16)
        rt = (r_ref[bat[p], :, col[p]] * jnp.exp(lc)).astype(BF16)
        ar.append(jnp.concatenate([at, rt], axis=0))
        kb.append(jnp.concatenate([stack((kc * g_inv).astype(BF16)), stack((bc * g_inv).astype(BF16))], axis=0))
        ke.append(jnp.concatenate([(kc * g_end).astype(BF16), (bc * g_end).astype(BF16)], axis=0))
        vb.append(v_ref[bat[p], :, col[p]].astype(BF16))
        g_col.append(jnp.sum(jnp.where(eye, jnp.broadcast_to(jnp.exp(lc_end), (n, n)), 0.0), axis=1, keepdims=True))

    quad = [_dot_nt(ar[p], kb[p], None) for p in pairs]
    a_b = [jnp.where(strict[p], quad[p][:c, n:], 0.0) for p in pairs]
    akbk = [jnp.concatenate([jnp.where(strict[p], quad[p][:c, :n], 0.0), jnp.where(incl[p], quad[p][c:, :n], 0.0)],
                            axis=0).astype(BF16) for p in pairs]
    b_b = [jnp.where(incl[p], quad[p][c:, n:], 0.0).astype(BF16) for p in pairs]
    h0 = [h_ref[p] for p in pairs]
    arh = [_dot(ar[p], h0[p].astype(BF16), None) for p in pairs]
    abv = [_dot(akbk[p], stack(vb[p]), None) for p in pairs]

    n1 = [jnp.where(blk8, a_b[p], 0.0) for p in pairs]
    n2 = [prod(n1[p], n1[p]) for p in pairs]
    both = [prod(jnp.concatenate([n2[p], ident + n1[p]], axis=0), n2[p]) for p in pairs]
    n4 = [both[p][:c] for p in pairs]
    t12 = [ident + n1[p] + both[p][c:] for p in pairs]
    tinv = [t12[p] + prod(t12[p], n4[p]) for p in pairs]
    for m in merges:
        off = [jnp.where(m, a_b[p], 0.0) for p in pairs]
        half = [prod(tinv[p], off[p]) for p in pairs]
        tinv = [tinv[p] + prod(half[p], tinv[p]) for p in pairs]

    ub = [_dot(tinv[p].astype(BF16), stack((arh[p][:c] + abv[p][:c]).astype(BF16)), None).astype(BF16) for p in pairs]
    bbu = [_dot(b_b[p], stack(ub[p]), None) for p in pairs]
    upd = [_dot_tn(ke[p], jnp.concatenate([vb[p], ub[p]], axis=0), None) for p in pairs]
    for p in pairs:
        dir_refs[drn[p]][6][bat[p], :, col[p]] = (arh[p][c:] + abv[p][c:] + bbu[p]).astype(BF16)
        h_ref[p] = h0[p] * g_col[p] + jnp.where(same_head, upd[p], 0.0)


def _rwkv_scan(r, v, kk, k2, b2, lw2):
    def tok_spec(d):
        return pl.BlockSpec((BATCH, RW_CHUNK, RWKV_W), lambda s: (0, _rw_chunk_index(d, s), 0))

    def dir_spec(d):
        return pl.BlockSpec((1, BATCH, RW_CHUNK, RWKV_W), lambda s: (d, 0, _rw_chunk_index(d, s), 0))

    y_shape = jax.ShapeDtypeStruct((BATCH, TOK, RWKV_W), BF16)
    return pl.pallas_call(
        _rwkv_scan_kernel,
        out_shape=(y_shape, y_shape),
        grid=(RW_NCHUNK,),
        in_specs=[tok_spec(0), tok_spec(0), tok_spec(0), tok_spec(1), tok_spec(1), tok_spec(1),
                  dir_spec(0), dir_spec(0), dir_spec(0), dir_spec(1), dir_spec(1), dir_spec(1)],
        out_specs=(tok_spec(0), tok_spec(1)),
        scratch_shapes=[pltpu.VMEM((2 * BATCH * N_PAIRS, PAIR_W, PAIR_W), F32)],
        compiler_params=_params("arbitrary"),
        name="rwkv_scan",
    )(r, v, kk, r, v, kk, k2, b2, lw2, k2, b2, lw2)


def _dft_mats(n):
    i = np.arange(n)
    ang = 2.0 * np.pi * ((i[:, None] * i[None, :]) % n) / n
    return np.cos(ang).astype(np.float32), np.sin(ang).astype(np.float32)


def _chan_dft_mat():
    c, s = _dft_mats(FOUR_GROUP_W)
    eye = np.eye(FOUR_GROUPS, dtype=np.float32)
    return np.concatenate([np.kron(eye, c), np.kron(eye, s)], axis=1)


def _bdot(a, b):
    return jnp.dot(a.astype(BF16), b.astype(BF16), preferred_element_type=F32)


def _four_ctx_kernel(u_ref, cs_ref, c_ref, s_ref, o_ref):
    pq = _bdot(u_ref[0], cs_ref[...])
    scale = np.float32(1.0 / np.sqrt(CTX_LEN * FOUR_GROUP_W))
    o_ref[0] = (_bdot(c_ref[...], pq[:, :FOUR_W]) - _bdot(s_ref[...], pq[:, FOUR_W:])) * scale


def _four_ctx(u):
    c, s = _dft_mats(CTX_LEN)
    full = lambda shape: pl.BlockSpec(shape, lambda b: (0,) * len(shape))
    return pl.pallas_call(
        _four_ctx_kernel,
        out_shape=jax.ShapeDtypeStruct((BATCH, CTX_LEN, FOUR_W), F32),
        grid=(BATCH,),
        in_specs=[pl.BlockSpec((1, CTX_LEN, FOUR_W), lambda b: (b, CTX_TILE, 0)),
                  full((FOUR_W, 2 * FOUR_W)), full((CTX_LEN, CTX_LEN)), full((CTX_LEN, CTX_LEN))],
        out_specs=pl.BlockSpec((1, CTX_LEN, FOUR_W), lambda b: (b, 0, 0)),
        compiler_params=_params("arbitrary"),
        name="fourier_ctx",
    )(u, jnp.asarray(_chan_dft_mat()), jnp.asarray(c), jnp.asarray(s))


L2_PER_STEP = 16


def _four_lat_a_kernel(u_ref, cs_ref, c_ref, s_ref, tc_ref, ts_ref, zr_ref, zi_ref):
    cs = cs_ref[...]
    cm = c_ref[...]
    sm = s_ref[...]
    for i in range(L2_PER_STEP):
        x = u_ref[0, :, i, :]
        pq = _bdot(x, cs)
        cpq = _bdot(cm, pq)
        spq = _bdot(sm, pq)
        zr = cpq[:, :FOUR_W] - spq[:, FOUR_W:]
        zi = -cpq[:, FOUR_W:] - spq[:, :FOUR_W]
        tc, ts = tc_ref[i], ts_ref[i]
        zr_ref[0, i] = zr * tc + zi * ts
        zi_ref[0, i] = zi * tc - zr * ts


K1_PER_STEP = 16


def _four_lat_b_kernel(zr_ref, zi_ref, cs_ref, o_ref):
    scale = np.float32(1.0 / np.sqrt(SEQ * FOUR_GROUP_W))
    cs = cs_ref[...]
    for i in range(K1_PER_STEP):
        z = jnp.concatenate([zr_ref[0, :, i, :], zi_ref[0, :, i, :]], axis=0)
        o_ref[0, :, i, :] = _bdot(cs, z) * scale


def _four_lat(u):
    c1, s1 = _dft_mats(GRID_ROWS)
    c2, s2 = _dft_mats(GRID_W)
    k1 = np.arange(GRID_ROWS)[None, :, None]
    l2 = np.arange(GRID_W)[:, None, None]
    ang = 2.0 * np.pi * (k1 * l2) / SEQ
    tc = np.broadcast_to(np.cos(ang), (GRID_W, GRID_ROWS, FOUR_W)).astype(np.float32)
    ts = np.broadcast_to(np.sin(ang), (GRID_W, GRID_ROWS, FOUR_W)).astype(np.float32)
    u4 = u.reshape(BATCH, TOK // GRID_W, GRID_W, FOUR_W)
    full = lambda shape: pl.BlockSpec(shape, lambda b, g: (0,) * len(shape))
    zshape = jax.ShapeDtypeStruct((BATCH, GRID_W, GRID_ROWS, FOUR_W), F32)
    zspec = pl.BlockSpec((1, L2_PER_STEP, GRID_ROWS, FOUR_W), lambda b, g: (b, g, 0, 0))
    tspec = pl.BlockSpec((L2_PER_STEP, GRID_ROWS, FOUR_W), lambda b, g: (g, 0, 0))
    zr, zi = pl.pallas_call(
        _four_lat_a_kernel,
        out_shape=(zshape, zshape),
        grid=(BATCH, GRID_W // L2_PER_STEP),
        in_specs=[pl.BlockSpec((1, GRID_ROWS, L2_PER_STEP, FOUR_W), lambda b, g: (b, 0, g, 0)),
                  full((FOUR_W, 2 * FOUR_W)), full((GRID_ROWS, GRID_ROWS)), full((GRID_ROWS, GRID_ROWS)),
                  tspec, tspec],
        out_specs=(zspec, zspec),
        compiler_params=_params("arbitrary", "arbitrary"),
        name="fourier_lat_a",
    )(u4, jnp.asarray(_chan_dft_mat()), jnp.asarray(c1), jnp.asarray(s1), jnp.asarray(tc), jnp.asarray(ts))
    zb = pl.BlockSpec((1, GRID_W, K1_PER_STEP, FOUR_W), lambda b, g: (b, 0, g, 0))
    out = pl.pallas_call(
        _four_lat_b_kernel,
        out_shape=jax.ShapeDtypeStruct((BATCH, GRID_W, GRID_ROWS, FOUR_W), F32),
        grid=(BATCH, GRID_ROWS // K1_PER_STEP),
        in_specs=[zb, zb, full((GRID_W, 2 * GRID_W))],
        out_specs=zb,
        compiler_params=_params("arbitrary", "arbitrary"),
        name="fourier_lat_b",
    )(zr, zi, jnp.asarray(np.concatenate([c2, s2], axis=1)))
    return out.reshape(BATCH, SEQ, FOUR_W)


def _post_residual(x, out, npost, gate):
    y = out * lax.rsqrt(jnp.mean(out * out, axis=-1, keepdims=True) + EPS) * npost
    return x + gate * y


def _retention_project(x, g_ref, mod_ref, w_ref, cos_ref, sin_ref, u_ref):
    gain = g_ref[...] * (1.0 + mod_ref[0, :, D_MODEL:2 * D_MODEL])
    h = (x * lax.rsqrt(jnp.mean(x * x, axis=-1, keepdims=True) + EPS) * gain + mod_ref[0, :, 0:D_MODEL]).astype(BF16)
    cosf, sinf = cos_ref[...], sin_ref[...]
    blocks = iter(range(ODD_IN // PROJ_BLOCK))

    def issue(count):
        for _ in range(count):
            lo = next(blocks) * PROJ_BLOCK
            acc = jnp.dot(h, w_ref[:, lo:lo + PROJ_BLOCK], preferred_element_type=F32)
            if lo >= 2 * RET_W:
                u_ref[0, :, lo:lo + PROJ_BLOCK] = acc.astype(u_ref.dtype)
                continue
            for i in range(PROJ_BLOCK // RET_HEAD):
                t = acc[:, i * RET_HEAD:(i + 1) * RET_HEAD]
                t = t * cosf + pltpu.roll(t, RET_HEAD // 2, 1) * sinf
                if lo >= RET_W:
                    t = t * np.float32(RET_HEAD ** -0.5)
                u_ref[0, :, lo + i * RET_HEAD:lo + (i + 1) * RET_HEAD] = t.astype(u_ref.dtype)

    return issue


def _even_finish_kernel(yf_ref, yb_ref, bonus_ref, gr_ref, gf_ref, four_ref, fourc_ref, x_ref, xc_ref, ones_ref, g_ref,
                        bta_ref, w_ref, np_ref, mod_ref, g1_ref, mod1_ref, w1_ref, cos_ref, sin_ref, o_ref, u1_ref,
                        x1_ref):
    j = pl.program_id(1)
    tile = jnp.minimum(j, N_TILES - 1)

    @pl.when(j == 0)
    def _():
        x1_ref[...] = jnp.zeros_like(x1_ref)

    issue = _retention_project(x1_ref[...], g1_ref, mod1_ref, w1_ref, cos_ref, sin_ref, u1_ref)
    issue(3)
    o = yf_ref[0].astype(F32) + yb_ref[0].astype(F32) + bonus_ref[0].astype(F32)
    ones_bd = ones_ref[...]
    inv = np.float32(1.0 / RWKV_HEAD)
    mean = _head_sum(o, ones_bd) * inv
    issue(3)
    cen = o - mean
    var = _head_sum(cen * cen, ones_bd) * inv
    issue(3)
    on = cen * lax.rsqrt(var + RWKV_GN_EPS) * g_ref[...] + bta_ref[...]
    yr = on * _silu(gr_ref[0].astype(F32))
    yf = _tile_rows(four_ref, fourc_ref, tile) * _silu(gf_ref[0].astype(F32))
    z = jnp.concatenate([yr, yf], axis=-1).astype(BF16)
    out = jnp.dot(z, w_ref[...], preferred_element_type=F32)
    issue(ODD_IN // PROJ_BLOCK - 9)
    x1 = _post_residual(_tile_rows(x_ref, xc_ref, tile), out, np_ref[...], mod_ref[0, :, 2 * D_MODEL:])
    o_ref[0] = x1
    x1_ref[...] = x1


def _even_finish(yf, yb, bonus, gate_r, gate_f, four_lat, four_ctx, x_lat, x_ctx, ones_bd, lnx_g, lnx_b, w_out_bf16,
                 npost, mod, next_g, next_mod, next_w_bf16, cosf, sinf):
    full = lambda shape: pl.BlockSpec(shape, lambda b, j: (0,) * len(shape))
    n_next = next_w_bf16.shape[1]
    cur = lambda j: jnp.minimum(j, N_TILES - 1)
    prev = lambda j: jnp.maximum(j - 1, 0)
    cur_spec = lambda w: pl.BlockSpec((1, TILE, w), lambda b, j: (b, cur(j), 0))
    mod_spec = lambda t: pl.BlockSpec((1, 1, 3 * D_MODEL), lambda b, j: (_mod_row(b, t(j)), 0, 0))
    return pl.pallas_call(
        _even_finish_kernel,
        out_shape=(jax.ShapeDtypeStruct((BATCH, TOK, D_MODEL), F32), jax.ShapeDtypeStruct((BATCH, TOK, n_next), BF16)),
        grid=(BATCH, N_TILES + 1),
        in_specs=[
            cur_spec(RWKV_W), cur_spec(RWKV_W), cur_spec(RWKV_W), cur_spec(RWKV_W), cur_spec(FOUR_W),
            *_lat_ctx_specs(FOUR_W),
            *_lat_ctx_specs(D_MODEL),
            full((PAIR_W, PAIR_W)), full((1, RWKV_W)), full((1, RWKV_W)), full((D_MODEL, D_MODEL)),
            full((1, D_MODEL)), mod_spec(cur),
            full((1, D_MODEL)), mod_spec(prev), full((D_MODEL, n_next)),
            pl.BlockSpec((TILE, RET_HEAD), lambda b, j: (prev(j), 0)),
            pl.BlockSpec((TILE, RET_HEAD), lambda b, j: (prev(j), 0)),
        ],
        out_specs=(cur_spec(D_MODEL), pl.BlockSpec((1, TILE, n_next), lambda b, j: (b, prev(j), 0))),
        scratch_shapes=[pltpu.VMEM((TILE, D_MODEL), F32)],
        compiler_params=_params("arbitrary", "arbitrary"),
        name="even_finish",
    )(yf, yb, bonus, gate_r, gate_f, four_lat, four_ctx, x_lat, x_ctx, ones_bd, lnx_g.reshape(1, RWKV_W),
      lnx_b.reshape(1, RWKV_W), w_out_bf16, npost.reshape(1, D_MODEL), mod.reshape(8, 1, 3 * D_MODEL),
      next_g.reshape(1, D_MODEL), next_mod.reshape(8, 1, 3 * D_MODEL), next_w_bf16, cosf, sinf)


def _ret_chunk_index(d, s):
    lat = jnp.where(d == 0, s - 1, RET_NCHUNK - 1 - s)
    return jnp.where(s == 0, RET_NCHUNK - 1, lat)


def _ret_scan_kernel(qf_ref, kf_ref, vf_ref, qb_ref, kb_ref, vb_ref, dl_ref, of_ref, ob_ref, st_ref):
    s = pl.program_id(1)
    c = RET_CHUNK

    @pl.when(s == 0)
    def _():
        st_ref[...] = jnp.zeros_like(st_ref)

    ri = lax.broadcasted_iota(jnp.int32, (c, c), 0)
    ci = lax.broadcasted_iota(jnp.int32, (c, c), 1)
    ti = lax.broadcasted_iota(jnp.int32, (c, RET_HEAD), 0)
    masks = (ri >= ci, ci > ri)
    dists = (jnp.where(masks[0], ri - ci, 0).astype(F32), jnp.where(masks[1], ci - ri, 0).astype(F32))
    poss = (ti.astype(F32), (c - 1 - ti).astype(F32))
    dir_refs = ((qf_ref, kf_ref, vf_ref, of_ref), (qb_ref, kb_ref, vb_ref, ob_ref))

    streams = range(2 * RET_HEADS)
    drn = [n // RET_HEADS for n in streams]
    col = [slice((n % RET_HEADS) * RET_HEAD, (n % RET_HEADS + 1) * RET_HEAD) for n in streams]
    lg, q, k, v = [], [], [], []
    for n in streams:
        x = dl_ref[drn[n], n % RET_HEADS]
        lg.append((jnp.minimum(x, 0.0) - jnp.log(1.0 + jnp.exp(-jnp.abs(x)))) * np.float32(np.log2(np.e)))
        q.append(dir_refs[drn[n]][0][0, :, col[n]])
        k.append(dir_refs[drn[n]][1][0, :, col[n]])
        v.append(dir_refs[drn[n]][2][0, :, col[n]])
    qk = [_dot_nt(q[n], k[n], None) for n in streams]
    st = [st_ref[n] for n in streams]
    cross = [_dot(q[n], st[n].astype(BF16), None) for n in streams]
    sc = [(qk[n] * jnp.where(masks[drn[n]], jnp.exp2(dists[drn[n]] * jnp.broadcast_to(lg[n][:, :1], (c, c))),
                             0.0)).astype(BF16) for n in streams]
    inner = [_dot(sc[n], v[n], None) for n in streams]
    upd = [_dot_tn((k[n].astype(F32) * jnp.exp2((c - 1.0 - poss[drn[n]]) * lg[n])).astype(BF16), v[n], None)
           for n in streams]
    for n in streams:
        o_ref = dir_refs[drn[n]][3]
        o_ref[0, :, col[n]] = (inner[n] + cross[n] * jnp.exp2((poss[drn[n]] + 1.0) * lg[n])).astype(o_ref.dtype)
        st_ref[n] = st[n] * jnp.exp2(np.float32(c) * lg[n]) + upd[n]


def _ret_scan(u2, dl):
    def tok_spec(d, blk):
        return pl.BlockSpec((1, RET_CHUNK, RET_W), lambda b, s: (b, _ret_chunk_index(d, s), blk))
    o_shape = jax.ShapeDtypeStruct((BATCH, TOK, RET_W), BF16)
    return pl.pallas_call(
        _ret_scan_kernel,
        out_shape=(o_shape, o_shape),
        grid=(BATCH, RET_NCHUNK),
        in_specs=[tok_spec(0, 0), tok_spec(0, 1), tok_spec(0, 2), tok_spec(1, 0), tok_spec(1, 1), tok_spec(1, 2),
                  pl.BlockSpec((2, RET_HEADS, 1, RET_HEAD), lambda b, s: (0, 0, 0, 0))],
        out_specs=(tok_spec(0, 0), tok_spec(1, 0)),
        scratch_shapes=[pltpu.VMEM((2 * RET_HEADS, RET_HEAD, RET_HEAD), F32)],
        compiler_params=_params("arbitrary", "arbitrary"),
        name="ret_scan",
    )(u2, u2, u2, u2, u2, u2, dl)


def _rope_tables():
    pos = np.arange(SEQ)
    half = RET_HEAD // 2
    inv = ROPE_BASE ** (-np.arange(0, half, 2, dtype=np.float32) / half)
    ang = np.concatenate([(pos // GRID_W)[:, None] * inv[None, :], (pos % GRID_W)[:, None] * inv[None, :]], axis=1)
    cos = np.concatenate([np.cos(ang), np.ones((CTX_LEN, half))], axis=0).astype(np.float32)
    sin = np.concatenate([np.sin(ang), np.zeros((CTX_LEN, half))], axis=0).astype(np.float32)
    return np.concatenate([cos, cos], axis=1), np.concatenate([-sin, sin], axis=1)


def _odd_finish_kernel(of_ref, ob_ref, g_ref, x_ref, w_ref, np_ref, mod_ref, out_ref):
    o = of_ref[0].astype(F32) + ob_ref[0].astype(F32)
    parts = []
    for h in range(RET_HEADS):
        oh = o[:, h * RET_HEAD:(h + 1) * RET_HEAD]
        parts.append(oh * lax.rsqrt(jnp.mean(oh * oh, axis=-1, keepdims=True) + EPS))
    z = (jnp.concatenate(parts, axis=-1) * _silu(g_ref[0].astype(F32))).astype(BF16)
    out = jnp.dot(z, w_ref[...], preferred_element_type=F32)
    out_ref[0] = _post_residual(x_ref[0], out, np_ref[...], mod_ref[0, :, 2 * D_MODEL:])


ODD_TILE = 512


def _odd_finish(o_f, o_b, u2, xcomb, w_out_bf16, npost, mod):
    full = lambda shape: pl.BlockSpec(shape, lambda b, j: (0,) * len(shape))
    tile = lambda w, blk: pl.BlockSpec((1, ODD_TILE, w), lambda b, j: (b, j, blk))
    return pl.pallas_call(
        _odd_finish_kernel,
        out_shape=jax.ShapeDtypeStruct((BATCH, SEQ, D_MODEL), F32),
        grid=(BATCH, SEQ // ODD_TILE),
        in_specs=[
            tile(RET_W, 0), tile(RET_W, 0), tile(RET_W, 3), tile(D_MODEL, 0),
            full((RET_W, D_MODEL)), full((1, D_MODEL)),
            pl.BlockSpec((1, 1, 3 * D_MODEL), lambda b, j: (b, 0, 0)),
        ],
        out_specs=tile(D_MODEL, 0),
        compiler_params=_params("arbitrary", "arbitrary"),
        name="odd_finish",
    )(o_f, o_b, u2, xcomb, w_out_bf16, npost.reshape(1, D_MODEL), mod.reshape(8, 1, 3 * D_MODEL))


def _head_ones():
    return np.kron(np.eye(2, dtype=np.float32), np.ones((RWKV_HEAD, RWKV_HEAD), np.float32))


def kernel(x, c, ctx, c_ctx, ada_w, ada_b, norm_pre, norm_post, ev_w_in, ev_mu, ev_w0, ev_w2, ev_a0, ev_a2, ev_k_k, ev_k_a, ev_r_k, ev_lnx_g, ev_lnx_b, ev_w_out, od_w_in, od_decay_logit, od_w_out):
    cvec = jnp.concatenate([c, c_ctx[None, :], jnp.zeros((8 - BATCH - 1, D_MODEL), F32)], axis=0)
    mod = _ada_mod(cvec, ada_w, ada_b)
    ones_bd = jnp.asarray(_head_ones(), dtype=BF16)

    gate_r, four_in, gate_f, r, v, kk, k2, b2, lw2, bonus = _proj_terms(
        x, ctx, norm_pre[0], mod[0], ev_w_in[0].astype(BF16), ev_mu[0], ev_w0[0], ev_w2[0], ev_a0[0], ev_a2[0],
        ev_k_k[0], ev_k_a[0], ev_r_k[0].reshape(RWKV_W), ones_bd)
    yf, yb = _rwkv_scan(r, v, kk, k2, b2, lw2)
    cosf, sinf = _rope_tables()
    xcomb, u2 = _even_finish(yf, yb, bonus, gate_r, gate_f, _four_lat(four_in), _four_ctx(four_in), x, ctx, ones_bd,
                             ev_lnx_g[0], ev_lnx_b[0], ev_w_out[0].astype(BF16), norm_post[0], mod[0],
                             norm_pre[1], mod[1], od_w_in[0].astype(BF16), jnp.asarray(cosf), jnp.asarray(sinf))

    dl = jnp.broadcast_to(od_decay_logit[0][:, :, None, None], (2, RET_HEADS, 1, RET_HEAD))
    o_f, o_b = _ret_scan(u2, dl)
    return _odd_finish(o_f, o_b, u2, xcomb, od_w_out[0].astype(BF16), norm_post[1], mod[1])
```

```python
import functools

import numpy as np
import jax
import jax.numpy as jnp
from jax import lax
from jax.experimental import pallas as pl
from jax.experimental.pallas import tpu as pltpu

F32 = jnp.float32
BF16 = jnp.bfloat16
HIGHEST = lax.Precision.HIGHEST

D_MODEL = 1024
BATCH = 2
SEQ = 8192
GRID_W = 64
GRID_ROWS = SEQ // GRID_W
CTX_LEN = 256
TOK = SEQ + CTX_LEN
EPS = 1e-6

RWKV_HEAD = 64
RWKV_HEADS = 12
RWKV_W = RWKV_HEADS * RWKV_HEAD
LORA = 64
FOUR_GROUPS = 4
FOUR_W = 256
FOUR_GROUP_W = 64
N_SHIFT = 3 * RWKV_W + 4 * LORA
EVEN_IN = N_SHIFT + RWKV_W + 2 * FOUR_W
TAIL_W = EVEN_IN - N_SHIFT
RWKV_GN_EPS = 64e-5

RET_HEADS = 8
RET_HEAD = 128
RET_W = 1024
ODD_IN = 4 * RET_W
ROPE_BASE = 10000.0

TILE = 256
N_TILES = TOK // TILE
CTX_TILE = N_TILES - 1
RW_CHUNK = 64
RW_CHUNKS_PER_STEP = 4
RW_NBLOCK = TOK // (RW_CHUNK * RW_CHUNKS_PER_STEP)
RW_CTX_BLOCKS = CTX_LEN // (RW_CHUNK * RW_CHUNKS_PER_STEP)
PAIR_W = 2 * RWKV_HEAD
N_PAIRS = RWKV_W // PAIR_W
RET_CHUNK = 256
RET_NCHUNK = TOK // RET_CHUNK
VMEM_LIMIT = 56 * 1024 * 1024


def _dot(a, b, precision=HIGHEST):
    return jnp.dot(a, b, precision=precision, preferred_element_type=F32)


def _dot_nt(a, b, precision=HIGHEST):
    return lax.dot_general(a, b, (((1,), (1,)), ((), ())), precision=precision, preferred_element_type=F32)


def _dot_tn(a, b, precision=HIGHEST):
    return lax.dot_general(a, b, (((0,), (0,)), ((), ())), precision=precision, preferred_element_type=F32)


def _split_bf16(x):
    hi = x.astype(BF16)
    return hi, (x - hi.astype(F32)).astype(BF16)


def _head_sum(x, ones_bd):
    xb = x.astype(BF16)
    out = []
    for c in range(x.shape[1] // PAIR_W):
        out.append(_dot(xb[:, c * PAIR_W:(c + 1) * PAIR_W], ones_bd, None))
    return jnp.concatenate(out, axis=1)


def _silu(x):
    return x * (1.0 / (1.0 + jnp.exp(-x)))


def _sigmoid(x):
    return 1.0 / (1.0 + jnp.exp(-x))


def _params(*sem):
    return pltpu.CompilerParams(dimension_semantics=sem, vmem_limit_bytes=VMEM_LIMIT)


def _ada_kernel(c_ref, w_ref, b_ref, o_ref):
    o_ref[0] = _dot(_silu(c_ref[...]), w_ref[0]) + b_ref[0]


def _ada_mod(cvec, ada_w, ada_b):
    depth = ada_w.shape[0]
    nt = 3 * D_MODEL // 1024
    return pl.pallas_call(
        _ada_kernel,
        out_shape=jax.ShapeDtypeStruct((depth, 8, 3 * D_MODEL), F32),
        grid=(depth, nt),
        in_specs=[
            pl.BlockSpec((8, D_MODEL), lambda l, n: (0, 0)),
            pl.BlockSpec((1, D_MODEL, 1024), lambda l, n: (l, 0, n)),
            pl.BlockSpec((1, 1, 1024), lambda l, n: (l, 0, n)),
        ],
        out_specs=pl.BlockSpec((1, 8, 1024), lambda l, n: (l, 0, n)),
        compiler_params=_params("arbitrary", "arbitrary"),
        name="ada_mod",
    )(cvec, ada_w, ada_b.reshape(depth, 1, 3 * D_MODEL))


def _mod_row(b, j):
    return jnp.where(j == CTX_TILE, BATCH, b)


def _tile_rows(lat_ref, ctx_ref, tile):
    return jnp.where(tile == CTX_TILE, ctx_ref[0], lat_ref[0])


def _lat_ctx_specs(width):
    lat = pl.BlockSpec((1, TILE, width), lambda b, j: (b, jnp.minimum(j, CTX_TILE - 1), 0))
    ctx = pl.BlockSpec((1, TILE, width), lambda b, j: (b, 0, 0))
    return lat, ctx


PT_LAG = 2
PROJ_BLOCK = 256


def _proj_terms_kernel(x_ref, xc_ref, g_ref, mod_ref, w_ref, mu_ref, w0_ref, w2_ref, a0_ref, a2_ref, kk_ref_, ka_ref,
                       rk_ref, ones_ref, gr_out, four_out, gf_out, r_out, v_out, kk_out, k_out, b_out, lw_out,
                       bonus_out, cur_ref, nxt_ref, halo_ref):
    step = pl.program_id(1)
    j = step - PT_LAG

    @pl.when(step == 0)
    def _():
        cur_ref[...] = jnp.zeros_like(cur_ref)
        nxt_ref[...] = jnp.zeros_like(nxt_ref)
        halo_ref[...] = jnp.zeros_like(halo_ref)

    y = _tile_rows(x_ref, xc_ref, jnp.minimum(step, N_TILES - 1))
    gain = g_ref[...] * (1.0 + mod_ref[0, :, D_MODEL:2 * D_MODEL])
    h = (y * lax.rsqrt(jnp.mean(y * y, axis=-1, keepdims=True) + EPS) * gain + mod_ref[0, :, 0:D_MODEL]).astype(BF16)

    is_ctx = j == CTX_TILE
    cur_u = cur_ref[...]
    nxt_u = nxt_ref[...]
    up_rows = halo_ref[...]
    dn_rows = nxt_u[0:GRID_W, :]
    halo_ref[...] = cur_u[TILE - GRID_W:, :]
    cur_ref[...] = nxt_u
    x = cur_u.astype(F32)

    blocks = iter(range(EVEN_IN // PROJ_BLOCK))
    targets = ((nxt_ref, 0, N_SHIFT), (gr_out, N_SHIFT, RWKV_W), (four_out, N_SHIFT + RWKV_W, FOUR_W),
               (gf_out, N_SHIFT + RWKV_W + FOUR_W, FOUR_W))

    def issue(count):
        for _ in range(count):
            lo = next(blocks) * PROJ_BLOCK
            acc = jnp.dot(h, w_ref[:, lo:lo + PROJ_BLOCK], preferred_element_type=F32)
            for ref, start, width in targets:
                if start <= lo < start + width:
                    if ref is nxt_ref:
                        ref[:, lo - start:lo - start + PROJ_BLOCK] = acc.astype(ref.dtype)
                    else:
                        ref[0, :, lo - start:lo - start + PROJ_BLOCK] = acc.astype(ref.dtype)

    left = pltpu.roll(x, 1, 0)
    right = pltpu.roll(x, TILE - 1, 0)
    row8 = lax.broadcasted_iota(jnp.int32, (8, N_SHIFT), 0)
    inner_first = jnp.where(is_ctx, 0, 1)
    inner_last = jnp.where(is_ctx, 8, 7)

    def zero_rows(a, first):
        parts, pos = [], 0
        for g in range(0, TILE, GRID_W):
            at = g if first else g + GRID_W - 8
            if first:
                keep = row8 >= (1 if g == 0 else inner_first)
            else:
                keep = row8 < (7 if g == TILE - GRID_W else inner_last)
            parts += [a[pos:at], jnp.where(keep, a[at:at + 8], 0.0)]
            pos = at + 8
        return jnp.concatenate([p for p in parts + [a[pos:]] if p.shape[0]], axis=0)

    left = zero_rows(left, True)
    right = zero_rows(right, False)
    up_ok = jnp.where(j == 0, 0.0, 1.0)
    dn_ok = jnp.where(j == CTX_TILE - 1, 0.0, 1.0)
    up = jnp.concatenate([up_rows.astype(F32) * up_ok, x[:TILE - GRID_W]], axis=0)
    down = jnp.concatenate([x[GRID_W:], dn_rows.astype(F32) * dn_ok], axis=0)
    lane = lax.broadcasted_iota(jnp.int32, (1, N_SHIFT), 1)
    sel = jnp.where(is_ctx, lane & 1, lane & 3)
    shifted = jnp.where(sel == 0, left, jnp.where(sel == 1, right, jnp.where(sel == 2, up, down)))
    s = x + (shifted - x) * mu_ref[...]
    issue(6)

    r = s[:, 0:RWKV_W]
    k0 = s[:, RWKV_W:2 * RWKV_W]
    v = s[:, 2 * RWKV_W:3 * RWKV_W]
    ones_bd = ones_ref[...]
    kk = k0 * kk_ref_[...]
    kk = kk * lax.rsqrt(jnp.maximum(_head_sum(kk * kk, ones_bd), 1e-24))
    issue(2)
    r_out[0] = r.astype(BF16)
    v_out[0] = v.astype(BF16)
    kk_out[0] = kk.astype(BF16)
    bonus = jnp.zeros_like(v)
    for d in range(2):
        wl = s[:, 3 * RWKV_W + d * LORA:3 * RWKV_W + (d + 1) * LORA]
        al = s[:, 3 * RWKV_W + (2 + d) * LORA:3 * RWKV_W + (3 + d) * LORA]
        z = w0_ref[d:d + 1, :] + _dot(jnp.tanh(wl).astype(BF16), w2_ref[d].astype(BF16), None)
        lw_out[d, 0] = -np.float32(np.exp(-0.5)) * _sigmoid(z)
        a = _sigmoid(a0_ref[d:d + 1, :] + _dot(al.astype(BF16), a2_ref[d].astype(BF16), None))
        issue(2)
        k = k0 * (1.0 + (a - 1.0) * ka_ref[...])
        k_out[d, 0] = k.astype(BF16)
        b_out[d, 0] = (kk * a).astype(BF16)
        bonus = bonus + _head_sum(r * k * rk_ref[...], ones_bd) * v
        issue(2 if d == 0 else 1)
    bonus_out[0] = bonus.astype(BF16)


def _proj_terms(x, ctx, g, mod, w_bf16, mu, w0, w2, a0, a2, k_k, k_a, r_k, ones_bd):
    tok = jax.ShapeDtypeStruct((BATCH, TOK, RWKV_W), BF16)
    tok2 = jax.ShapeDtypeStruct((2, BATCH, TOK, RWKV_W), BF16)
    tok2_f32 = jax.ShapeDtypeStruct((2, BATCH, TOK, RWKV_W), F32)
    full = lambda shape: pl.BlockSpec(shape, lambda b, j: (0,) * len(shape))
    cur = lambda j: jnp.minimum(j, N_TILES - 1)
    lag = lambda j: jnp.maximum(j - PT_LAG, 0)
    cur_spec = lambda w: pl.BlockSpec((1, TILE, w), lambda b, j: (b, cur(j), 0))
    tspec = pl.BlockSpec((1, TILE, RWKV_W), lambda b, j: (b, lag(j), 0))
    t2spec = pl.BlockSpec((2, 1, TILE, RWKV_W), lambda b, j: (0, b, lag(j), 0))
    return pl.pallas_call(
        _proj_terms_kernel,
        out_shape=(tok, jax.ShapeDtypeStruct((BATCH, TOK, FOUR_W), F32), jax.ShapeDtypeStruct((BATCH, TOK, FOUR_W), BF16),
                   tok, tok, tok, tok2, tok2, tok2_f32, tok),
        grid=(BATCH, N_TILES + PT_LAG),
        in_specs=[
            *_lat_ctx_specs(D_MODEL),
            full((1, D_MODEL)),
            pl.BlockSpec((1, 1, 3 * D_MODEL), lambda b, j: (_mod_row(b, cur(j)), 0, 0)),
            full((D_MODEL, EVEN_IN)),
            full((1, N_SHIFT)), full((2, RWKV_W)), full((2, LORA, RWKV_W)), full((2, RWKV_W)),
            full((2, LORA, RWKV_W)), full((1, RWKV_W)), full((1, RWKV_W)), full((1, RWKV_W)),
            full((PAIR_W, PAIR_W)),
        ],
        out_specs=(cur_spec(RWKV_W), cur_spec(FOUR_W), cur_spec(FOUR_W),
                   tspec, tspec, tspec, t2spec, t2spec, t2spec, tspec),
        scratch_shapes=[pltpu.VMEM((TILE, N_SHIFT), BF16), pltpu.VMEM((TILE, N_SHIFT), BF16),
                        pltpu.VMEM((GRID_W, N_SHIFT), BF16)],
        compiler_params=_params("arbitrary", "arbitrary"),
        name="proj_terms",
    )(x, ctx, g.reshape(1, D_MODEL), mod.reshape(8, 1, 3 * D_MODEL), w_bf16, mu.reshape(1, N_SHIFT), w0, w2, a0, a2,
      k_k.reshape(1, RWKV_W), k_a.reshape(1, RWKV_W), r_k.reshape(1, RWKV_W), ones_bd)


def _rw_block_index(d, s):
    lat = jnp.where(d == 0, s - RW_CTX_BLOCKS, RW_NBLOCK - 1 - s)
    ctx = RW_NBLOCK - RW_CTX_BLOCKS + jnp.where(d == 0, s, RW_CTX_BLOCKS - 1 - s)
    return jnp.where(s < RW_CTX_BLOCKS, ctx, lat)


def _rwkv_scan_kernel(rf_ref, vf_ref, kkf_ref, rb_ref, vb_ref, kkb_ref, kf_ref, bf_ref, lwf_ref, kb_ref, bb_ref,
                      lwb_ref, yf_ref, yb_ref, h_ref):
    s = pl.program_id(0)
    n = PAIR_W

    @pl.when(s == 0)
    def _():
        h_ref[...] = jnp.zeros_like(h_ref)

    c = RW_CHUNK
    ti = lax.broadcasted_iota(jnp.int32, (c, n), 0)
    li = lax.broadcasted_iota(jnp.int32, (c, n), 1)
    si = li & (c - 1)
    strict_d = (si < ti, si > ti)
    incl_d = (si <= ti, si >= ti)
    ident = jnp.where(si == ti, 1.0, 0.0)
    blk8 = (ti >> 3) == (si >> 3)
    merges = [((ti >> (sh + 1)) == (si >> (sh + 1))) & ((ti >> sh) != (si >> sh)) for sh in (3, 4, 5)]
    head0 = li < RWKV_HEAD
    cumb_d = [jnp.where(m[:, :c], 1.0, 0.0).astype(BF16) for m in incl_d]
    ri = lax.broadcasted_iota(jnp.int32, (n, n), 0)
    ci = lax.broadcasted_iota(jnp.int32, (n, n), 1)
    eye = ri == ci
    same_head = (ri >> 6) == (ci >> 6)

    def stack(xb):
        zero = jnp.zeros_like(xb)
        return jnp.concatenate([jnp.where(head0, xb, zero), jnp.where(head0, zero, xb)], axis=0)

    def prod(x, y):
        return _dot(x.astype(BF16), stack(y.astype(BF16)), None)

    dir_refs = ((rf_ref, vf_ref, kkf_ref, kf_ref, bf_ref, lwf_ref, yf_ref),
                (rb_ref, vb_ref, kkb_ref, kb_ref, bb_ref, lwb_ref, yb_ref))
    pairs = range(2 * BATCH * N_PAIRS)
    drn = [p // (BATCH * N_PAIRS) for p in pairs]
    bat = [(p // N_PAIRS) % BATCH for p in pairs]
    col = [slice((p % N_PAIRS) * PAIR_W, (p % N_PAIRS + 1) * PAIR_W) for p in pairs]
    strict = [strict_d[drn[p]] for p in pairs]
    incl = [incl_d[drn[p]] for p in pairs]
    lax.fori_loop(0, RW_CHUNKS_PER_STEP, functools.partial(
        _rwkv_chunk, dir_refs=dir_refs, h_ref=h_ref, pairs=pairs, drn=drn, bat=bat, col=col, strict=strict, incl=incl,
        cumb_d=cumb_d, ident=ident, blk8=blk8, merges=merges, eye=eye, same_head=same_head, stack=stack, prod=prod), 0)


def _rwkv_chunk(i, carry, *, dir_refs, h_ref, pairs, drn, bat, col, strict, incl, cumb_d, ident, blk8, merges, eye,
                same_head, stack, prod):
    c, n = RW_CHUNK, PAIR_W
    rows_d = (pl.ds(pl.multiple_of(i * c, c), c), pl.ds(pl.multiple_of((RW_CHUNKS_PER_STEP - 1 - i) * c, c), c))
    rows = [rows_d[drn[p]] for p in pairs]
    lw_all, lc_all = {}, {}
    for dd in range(2):
        for bb in range(BATCH):
            lw_b = dir_refs[dd][5][0, bb, rows_d[dd], :]
            lw_hi = lw_b.astype(BF16)
            lw_mid, lw_lo = _split_bf16(lw_b - lw_hi.astype(F32))
            lw_all[dd, bb] = lw_b
            lc_all[dd, bb] = _dot(jnp.concatenate([cumb_d[dd]] * 3, axis=1),
                                  jnp.concatenate([lw_hi, lw_mid, lw_lo], axis=0), None)
    ar, kb, ke, vb, g_col = [], [], [], [], []
    for p in pairs:
        r_ref, v_ref, kk_ref, k_ref, b_ref = dir_refs[drn[p]][:5]
        lw, lc = lw_all[drn[p], bat[p]][:, col[p]], lc_all[drn[p], bat[p]][:, col[p]]
        lc_end = jnp.sum(lw, axis=0, keepdims=True)
        g_inv = jnp.exp(-lc)
        g_end = jnp.exp(lc_end - lc)
        kc, bc = k_ref[0, bat[p], rows[p], col[p]], b_ref[0, bat[p], rows[p], col[p]]
        at = (-kk_ref[bat[p], rows[p], col[p]] * jnp.exp(lc - lw)).astype(BF16)
        rt = (r_ref[bat[p], rows[p], col[p]] * jnp.exp(lc)).astype(BF16)
        ar.append(jnp.concatenate([at, rt], axis=0))
        kb.append(jnp.concatenate([stack((kc * g_inv).astype(BF16)), stack((bc * g_inv).astype(BF16))], axis=0))
        ke.append(jnp.concatenate([(kc * g_end).astype(BF16), (bc * g_end).astype(BF16)], axis=0))
        vb.append(v_ref[bat[p], rows[p], col[p]].astype(BF16))
        g_col.append(jnp.sum(jnp.where(eye, jnp.broadcast_to(jnp.exp(lc_end), (n, n)), 0.0), axis=1, keepdims=True))

    quad = [_dot_nt(ar[p], kb[p], None) for p in pairs]
    a_b = [jnp.where(strict[p], quad[p][:c, n:], 0.0) for p in pairs]
    akbk = [jnp.concatenate([jnp.where(strict[p], quad[p][:c, :n], 0.0), jnp.where(incl[p], quad[p][c:, :n], 0.0)],
                            axis=0).astype(BF16) for p in pairs]
    b_b = [jnp.where(incl[p], quad[p][c:, n:], 0.0).astype(BF16) for p in pairs]
    h0 = [h_ref[p] for p in pairs]
    arh = [_dot(ar[p], h0[p].astype(BF16), None) for p in pairs]
    abv = [_dot(akbk[p], stack(vb[p]), None) for p in pairs]

    n1 = [jnp.where(blk8, a_b[p], 0.0) for p in pairs]
    n2 = [prod(n1[p], n1[p]) for p in pairs]
    both = [prod(jnp.concatenate([n2[p], ident + n1[p]], axis=0), n2[p]) for p in pairs]
    n4 = [both[p][:c] for p in pairs]
    t12 = [ident + n1[p] + both[p][c:] for p in pairs]
    tinv = [t12[p] + prod(t12[p], n4[p]) for p in pairs]
    for m in merges:
        off = [jnp.where(m, a_b[p], 0.0) for p in pairs]
        half = [prod(tinv[p], off[p]) for p in pairs]
        tinv = [tinv[p] + prod(half[p], tinv[p]) for p in pairs]

    ub = [_dot(tinv[p].astype(BF16), stack((arh[p][:c] + abv[p][:c]).astype(BF16)), None).astype(BF16) for p in pairs]
    bbu = [_dot(b_b[p], stack(ub[p]), None) for p in pairs]
    upd = [_dot_tn(ke[p], jnp.concatenate([vb[p], ub[p]], axis=0), None) for p in pairs]
    for p in pairs:
        dir_refs[drn[p]][6][bat[p], rows[p], col[p]] = (arh[p][c:] + abv[p][c:] + bbu[p]).astype(BF16)
        h_ref[p] = h0[p] * g_col[p] + jnp.where(same_head, upd[p], 0.0)
    return carry


def _rwkv_scan(r, v, kk, k2, b2, lw2):
    rows = RW_CHUNK * RW_CHUNKS_PER_STEP

    def tok_spec(d):
        return pl.BlockSpec((BATCH, rows, RWKV_W), lambda s: (0, _rw_block_index(d, s), 0))

    def dir_spec(d):
        return pl.BlockSpec((1, BATCH, rows, RWKV_W), lambda s: (d, 0, _rw_block_index(d, s), 0))

    y_shape = jax.ShapeDtypeStruct((BATCH, TOK, RWKV_W), BF16)
    return pl.pallas_call(
        _rwkv_scan_kernel,
        out_shape=(y_shape, y_shape),
        grid=(RW_NBLOCK,),
        in_specs=[tok_spec(0), tok_spec(0), tok_spec(0), tok_spec(1), tok_spec(1), tok_spec(1),
                  dir_spec(0), dir_spec(0), dir_spec(0), dir_spec(1), dir_spec(1), dir_spec(1)],
        out_specs=(tok_spec(0), tok_spec(1)),
        scratch_shapes=[pltpu.VMEM((2 * BATCH * N_PAIRS, PAIR_W, PAIR_W), F32)],
        compiler_params=_params("arbitrary"),
        name="rwkv_scan",
    )(r, v, kk, r, v, kk, k2, b2, lw2, k2, b2, lw2)


def _dft_mats(n):
    i = np.arange(n)
    ang = 2.0 * np.pi * ((i[:, None] * i[None, :]) % n) / n
    return np.cos(ang).astype(np.float32), np.sin(ang).astype(np.float32)


def _chan_dft_mat():
    c, s = _dft_mats(FOUR_GROUP_W)
    eye = np.eye(FOUR_GROUPS, dtype=np.float32)
    return np.concatenate([np.kron(eye, c), np.kron(eye, s)], axis=1)


def _bdot(a, b):
    return jnp.dot(a.astype(BF16), b.astype(BF16), preferred_element_type=F32)


def _four_ctx_kernel(u_ref, cs_ref, c_ref, s_ref, o_ref):
    pq = _bdot(u_ref[0], cs_ref[...])
    scale = np.float32(1.0 / np.sqrt(CTX_LEN * FOUR_GROUP_W))
    o_ref[0] = (_bdot(c_ref[...], pq[:, :FOUR_W]) - _bdot(s_ref[...], pq[:, FOUR_W:])) * scale


def _four_ctx(u):
    c, s = _dft_mats(CTX_LEN)
    full = lambda shape: pl.BlockSpec(shape, lambda b: (0,) * len(shape))
    return pl.pallas_call(
        _four_ctx_kernel,
        out_shape=jax.ShapeDtypeStruct((BATCH, CTX_LEN, FOUR_W), F32),
        grid=(BATCH,),
        in_specs=[pl.BlockSpec((1, CTX_LEN, FOUR_W), lambda b: (b, CTX_TILE, 0)),
                  full((FOUR_W, 2 * FOUR_W)), full((CTX_LEN, CTX_LEN)), full((CTX_LEN, CTX_LEN))],
        out_specs=pl.BlockSpec((1, CTX_LEN, FOUR_W), lambda b: (b, 0, 0)),
        compiler_params=_params("arbitrary"),
        name="fourier_ctx",
    )(u, jnp.asarray(_chan_dft_mat()), jnp.asarray(c), jnp.asarray(s))


L2_PER_STEP = 16


def _four_lat_a_kernel(u_ref, cs_ref, c_ref, s_ref, tc_ref, ts_ref, zr_ref, zi_ref):
    cs = cs_ref[...]
    cm = c_ref[...]
    sm = s_ref[...]
    for i in range(L2_PER_STEP):
        x = u_ref[0, :, i, :]
        pq = _bdot(x, cs)
        cpq = _bdot(cm, pq)
        spq = _bdot(sm, pq)
        zr = cpq[:, :FOUR_W] - spq[:, FOUR_W:]
        zi = -cpq[:, FOUR_W:] - spq[:, :FOUR_W]
        tc, ts = tc_ref[i], ts_ref[i]
        zr_ref[0, i] = zr * tc + zi * ts
        zi_ref[0, i] = zi * tc - zr * ts


K1_PER_STEP = 16


def _four_lat_b_kernel(zr_ref, zi_ref, cs_ref, o_ref):
    scale = np.float32(1.0 / np.sqrt(SEQ * FOUR_GROUP_W))
    cs = cs_ref[...]
    for i in range(K1_PER_STEP):
        z = jnp.concatenate([zr_ref[0, :, i, :], zi_ref[0, :, i, :]], axis=0)
        o_ref[0, :, i, :] = _bdot(cs, z) * scale


def _four_lat(u):
    c1, s1 = _dft_mats(GRID_ROWS)
    c2, s2 = _dft_mats(GRID_W)
    k1 = np.arange(GRID_ROWS)[None, :, None]
    l2 = np.arange(GRID_W)[:, None, None]
    ang = 2.0 * np.pi * (k1 * l2) / SEQ
    tc = np.broadcast_to(np.cos(ang), (GRID_W, GRID_ROWS, FOUR_W)).astype(np.float32)
    ts = np.broadcast_to(np.sin(ang), (GRID_W, GRID_ROWS, FOUR_W)).astype(np.float32)
    u4 = u.reshape(BATCH, TOK // GRID_W, GRID_W, FOUR_W)
    full = lambda shape: pl.BlockSpec(shape, lambda b, g: (0,) * len(shape))
    zshape = jax.ShapeDtypeStruct((BATCH, GRID_W, GRID_ROWS, FOUR_W), F32)
    zspec = pl.BlockSpec((1, L2_PER_STEP, GRID_ROWS, FOUR_W), lambda b, g: (b, g, 0, 0))
    tspec = pl.BlockSpec((L2_PER_STEP, GRID_ROWS, FOUR_W), lambda b, g: (g, 0, 0))
    zr, zi = pl.pallas_call(
        _four_lat_a_kernel,
        out_shape=(zshape, zshape),
        grid=(BATCH, GRID_W // L2_PER_STEP),
        in_specs=[pl.BlockSpec((1, GRID_ROWS, L2_PER_STEP, FOUR_W), lambda b, g: (b, 0, g, 0)),
                  full((FOUR_W, 2 * FOUR_W)), full((GRID_ROWS, GRID_ROWS)), full((GRID_ROWS, GRID_ROWS)),
                  tspec, tspec],
        out_specs=(zspec, zspec),
        compiler_params=_params("arbitrary", "arbitrary"),
        name="fourier_lat_a",
    )(u4, jnp.asarray(_chan_dft_mat()), jnp.asarray(c1), jnp.asarray(s1), jnp.asarray(tc), jnp.asarray(ts))
    zb = pl.BlockSpec((1, GRID_W, K1_PER_STEP, FOUR_W), lambda b, g: (b, 0, g, 0))
    out = pl.pallas_call(
        _four_lat_b_kernel,
        out_shape=jax.ShapeDtypeStruct((BATCH, GRID_W, GRID_ROWS, FOUR_W), F32),
        grid=(BATCH, GRID_ROWS // K1_PER_STEP),
        in_specs=[zb, zb, full((GRID_W, 2 * GRID_W))],
        out_specs=zb,
        compiler_params=_params("arbitrary", "arbitrary"),
        name="fourier_lat_b",
    )(zr, zi, jnp.asarray(np.concatenate([c2, s2], axis=1)))
    return out.reshape(BATCH, SEQ, FOUR_W)


def _post_residual(x, out, npost, gate):
    y = out * lax.rsqrt(jnp.mean(out * out, axis=-1, keepdims=True) + EPS) * npost
    return x + gate * y


def _retention_project(x, g_ref, mod_ref, w_ref, cos_ref, sin_ref, u_ref):
    gain = g_ref[...] * (1.0 + mod_ref[0, :, D_MODEL:2 * D_MODEL])
    h = (x * lax.rsqrt(jnp.mean(x * x, axis=-1, keepdims=True) + EPS) * gain + mod_ref[0, :, 0:D_MODEL]).astype(BF16)
    cosf, sinf = cos_ref[...], sin_ref[...]
    blocks = iter(range(ODD_IN // PROJ_BLOCK))

    def issue(count):
        for _ in range(count):
            lo = next(blocks) * PROJ_BLOCK
            acc = jnp.dot(h, w_ref[:, lo:lo + PROJ_BLOCK], preferred_element_type=F32)
            if lo >= 2 * RET_W:
                u_ref[0, :, lo:lo + PROJ_BLOCK] = acc.astype(u_ref.dtype)
                continue
            for i in range(PROJ_BLOCK // RET_HEAD):
                t = acc[:, i * RET_HEAD:(i + 1) * RET_HEAD]
                t = t * cosf + pltpu.roll(t, RET_HEAD // 2, 1) * sinf
                if lo >= RET_W:
                    t = t * np.float32(RET_HEAD ** -0.5)
                u_ref[0, :, lo + i * RET_HEAD:lo + (i + 1) * RET_HEAD] = t.astype(u_ref.dtype)

    return issue


def _even_finish_kernel(yf_ref, yb_ref, bonus_ref, gr_ref, gf_ref, four_ref, fourc_ref, x_ref, xc_ref, ones_ref, g_ref,
                        bta_ref, w_ref, np_ref, mod_ref, g1_ref, mod1_ref, w1_ref, cos_ref, sin_ref, o_ref, u1_ref,
                        x1_ref):
    j = pl.program_id(1)
    tile = jnp.minimum(j, N_TILES - 1)

    @pl.when(j == 0)
    def _():
        x1_ref[...] = jnp.zeros_like(x1_ref)

    issue = _retention_project(x1_ref[...], g1_ref, mod1_ref, w1_ref, cos_ref, sin_ref, u1_ref)
    issue(3)
    o = yf_ref[0].astype(F32) + yb_ref[0].astype(F32) + bonus_ref[0].astype(F32)
    ones_bd = ones_ref[...]
    inv = np.float32(1.0 / RWKV_HEAD)
    mean = _head_sum(o, ones_bd) * inv
    issue(3)
    cen = o - mean
    var = _head_sum(cen * cen, ones_bd) * inv
    issue(3)
    on = cen * lax.rsqrt(var + RWKV_GN_EPS) * g_ref[...] + bta_ref[...]
    yr = on * _silu(gr_ref[0].astype(F32))
    yf = _tile_rows(four_ref, fourc_ref, tile) * _silu(gf_ref[0].astype(F32))
    z = jnp.concatenate([yr, yf], axis=-1).astype(BF16)
    out = jnp.dot(z, w_ref[...], preferred_element_type=F32)
    issue(ODD_IN // PROJ_BLOCK - 9)
    x1 = _post_residual(_tile_rows(x_ref, xc_ref, tile), out, np_ref[...], mod_ref[0, :, 2 * D_MODEL:])
    o_ref[0] = x1
    x1_ref[...] = x1


def _even_finish(yf, yb, bonus, gate_r, gate_f, four_lat, four_ctx, x_lat, x_ctx, ones_bd, lnx_g, lnx_b, w_out_bf16,
                 npost, mod, next_g, next_mod, next_w_bf16, cosf, sinf):
    full = lambda shape: pl.BlockSpec(shape, lambda b, j: (0,) * len(shape))
    n_next = next_w_bf16.shape[1]
    cur = lambda j: jnp.minimum(j, N_TILES - 1)
    prev = lambda j: jnp.maximum(j - 1, 0)
    cur_spec = lambda w: pl.BlockSpec((1, TILE, w), lambda b, j: (b, cur(j), 0))
    mod_spec = lambda t: pl.BlockSpec((1, 1, 3 * D_MODEL), lambda b, j: (_mod_row(b, t(j)), 0, 0))
    return pl.pallas_call(
        _even_finish_kernel,
        out_shape=(jax.ShapeDtypeStruct((BATCH, TOK, D_MODEL), F32), jax.ShapeDtypeStruct((BATCH, TOK, n_next), BF16)),
        grid=(BATCH, N_TILES + 1),
        in_specs=[
            cur_spec(RWKV_W), cur_spec(RWKV_W), cur_spec(RWKV_W), cur_spec(RWKV_W), cur_spec(FOUR_W),
            *_lat_ctx_specs(FOUR_W),
            *_lat_ctx_specs(D_MODEL),
            full((PAIR_W, PAIR_W)), full((1, RWKV_W)), full((1, RWKV_W)), full((D_MODEL, D_MODEL)),
            full((1, D_MODEL)), mod_spec(cur),
            full((1, D_MODEL)), mod_spec(prev), full((D_MODEL, n_next)),
            pl.BlockSpec((TILE, RET_HEAD), lambda b, j: (prev(j), 0)),
            pl.BlockSpec((TILE, RET_HEAD), lambda b, j: (prev(j), 0)),
        ],
        out_specs=(cur_spec(D_MODEL), pl.BlockSpec((1, TILE, n_next), lambda b, j: (b, prev(j), 0))),
        scratch_shapes=[pltpu.VMEM((TILE, D_MODEL), F32)],
        compiler_params=_params("arbitrary", "arbitrary"),
        name="even_finish",
    )(yf, yb, bonus, gate_r, gate_f, four_lat, four_ctx, x_lat, x_ctx, ones_bd, lnx_g.reshape(1, RWKV_W),
      lnx_b.reshape(1, RWKV_W), w_out_bf16, npost.reshape(1, D_MODEL), mod.reshape(8, 1, 3 * D_MODEL),
      next_g.reshape(1, D_MODEL), next_mod.reshape(8, 1, 3 * D_MODEL), next_w_bf16, cosf, sinf)


def _ret_chunk_index(d, s):
    lat = jnp.where(d == 0, s - 1, RET_NCHUNK - 1 - s)
    return jnp.where(s == 0, RET_NCHUNK - 1, lat)


def _ret_scan_kernel(qf_ref, kf_ref, vf_ref, qb_ref, kb_ref, vb_ref, dl_ref, of_ref, ob_ref, st_ref):
    s = pl.program_id(1)
    c = RET_CHUNK

    @pl.when(s == 0)
    def _():
        st_ref[...] = jnp.zeros_like(st_ref)

    ri = lax.broadcasted_iota(jnp.int32, (c, c), 0)
    ci = lax.broadcasted_iota(jnp.int32, (c, c), 1)
    ti = lax.broadcasted_iota(jnp.int32, (c, RET_HEAD), 0)
    masks = (ri >= ci, ci > ri)
    dists = (jnp.where(masks[0], ri - ci, 0).astype(F32), jnp.where(masks[1], ci - ri, 0).astype(F32))
    poss = (ti.astype(F32), (c - 1 - ti).astype(F32))
    dir_refs = ((qf_ref, kf_ref, vf_ref, of_ref), (qb_ref, kb_ref, vb_ref, ob_ref))

    streams = range(2 * RET_HEADS)
    drn = [n // RET_HEADS for n in streams]
    col = [slice((n % RET_HEADS) * RET_HEAD, (n % RET_HEADS + 1) * RET_HEAD) for n in streams]
    lg, q, k, v = [], [], [], []
    for n in streams:
        x = dl_ref[drn[n], n % RET_HEADS]
        lg.append((jnp.minimum(x, 0.0) - jnp.log(1.0 + jnp.exp(-jnp.abs(x)))) * np.float32(np.log2(np.e)))
        q.append(dir_refs[drn[n]][0][0, :, col[n]])
        k.append(dir_refs[drn[n]][1][0, :, col[n]])
        v.append(dir_refs[drn[n]][2][0, :, col[n]])
    qk = [_dot_nt(q[n], k[n], None) for n in streams]
    st = [st_ref[n] for n in streams]
    cross = [_dot(q[n], st[n].astype(BF16), None) for n in streams]
    sc = [(qk[n] * jnp.where(masks[drn[n]], jnp.exp2(dists[drn[n]] * jnp.broadcast_to(lg[n][:, :1], (c, c))),
                             0.0)).astype(BF16) for n in streams]
    inner = [_dot(sc[n], v[n], None) for n in streams]
    upd = [_dot_tn((k[n].astype(F32) * jnp.exp2((c - 1.0 - poss[drn[n]]) * lg[n])).astype(BF16), v[n], None)
           for n in streams]
    for n in streams:
        o_ref = dir_refs[drn[n]][3]
        o_ref[0, :, col[n]] = (inner[n] + cross[n] * jnp.exp2((poss[drn[n]] + 1.0) * lg[n])).astype(o_ref.dtype)
        st_ref[n] = st[n] * jnp.exp2(np.float32(c) * lg[n]) + upd[n]


def _ret_scan(u2, dl):
    def tok_spec(d, blk):
        return pl.BlockSpec((1, RET_CHUNK, RET_W), lambda b, s: (b, _ret_chunk_index(d, s), blk))
    o_shape = jax.ShapeDtypeStruct((BATCH, TOK, RET_W), BF16)
    return pl.pallas_call(
        _ret_scan_kernel,
        out_shape=(o_shape, o_shape),
        grid=(BATCH, RET_NCHUNK),
        in_specs=[tok_spec(0, 0), tok_spec(0, 1), tok_spec(0, 2), tok_spec(1, 0), tok_spec(1, 1), tok_spec(1, 2),
                  pl.BlockSpec((2, RET_HEADS, 1, RET_HEAD), lambda b, s: (0, 0, 0, 0))],
        out_specs=(tok_spec(0, 0), tok_spec(1, 0)),
        scratch_shapes=[pltpu.VMEM((2 * RET_HEADS, RET_HEAD, RET_HEAD), F32)],
        compiler_params=_params("arbitrary", "arbitrary"),
        name="ret_scan",
    )(u2, u2, u2, u2, u2, u2, dl)


def _rope_tables():
    pos = np.arange(SEQ)
    half = RET_HEAD // 2
    inv = ROPE_BASE ** (-np.arange(0, half, 2, dtype=np.float32) / half)
    ang = np.concatenate([(pos // GRID_W)[:, None] * inv[None, :], (pos % GRID_W)[:, None] * inv[None, :]], axis=1)
    cos = np.concatenate([np.cos(ang), np.ones((CTX_LEN, half))], axis=0).astype(np.float32)
    sin = np.concatenate([np.sin(ang), np.zeros((CTX_LEN, half))], axis=0).astype(np.float32)
    return np.concatenate([cos, cos], axis=1), np.concatenate([-sin, sin], axis=1)


def _odd_finish_kernel(of_ref, ob_ref, g_ref, x_ref, w_ref, np_ref, mod_ref, out_ref):
    o = of_ref[0].astype(F32) + ob_ref[0].astype(F32)
    parts = []
    ones = jnp.ones((RET_HEAD, RET_HEAD), BF16)
    for h in range(RET_HEADS):
        oh = o[:, h * RET_HEAD:(h + 1) * RET_HEAD]
        ms = _dot((oh * oh).astype(BF16), ones, None) * np.float32(1.0 / RET_HEAD)
        parts.append(oh * lax.rsqrt(ms + EPS))
    z = (jnp.concatenate(parts, axis=-1) * _silu(g_ref[0].astype(F32))).astype(BF16)
    out = jnp.dot(z, w_ref[...], preferred_element_type=F32)
    out_ref[0] = _post_residual(x_ref[0], out, np_ref[...], mod_ref[0, :, 2 * D_MODEL:])


ODD_TILE = 512


def _odd_finish(o_f, o_b, u2, xcomb, w_out_bf16, npost, mod):
    full = lambda shape: pl.BlockSpec(shape, lambda b, j: (0,) * len(shape))
    tile = lambda w, blk: pl.BlockSpec((1, ODD_TILE, w), lambda b, j: (b, j, blk))
    return pl.pallas_call(
        _odd_finish_kernel,
        out_shape=jax.ShapeDtypeStruct((BATCH, SEQ, D_MODEL), F32),
        grid=(BATCH, SEQ // ODD_TILE),
        in_specs=[
            tile(RET_W, 0), tile(RET_W, 0), tile(RET_W, 3), tile(D_MODEL, 0),
            full((RET_W, D_MODEL)), full((1, D_MODEL)),
            pl.BlockSpec((1, 1, 3 * D_MODEL), lambda b, j: (b, 0, 0)),
        ],
        out_specs=tile(D_MODEL, 0),
        compiler_params=_params("arbitrary", "arbitrary"),
        name="odd_finish",
    )(o_f, o_b, u2, xcomb, w_out_bf16, npost.reshape(1, D_MODEL), mod.reshape(8, 1, 3 * D_MODEL))


def _head_ones():
    return np.kron(np.eye(2, dtype=np.float32), np.ones((RWKV_HEAD, RWKV_HEAD), np.float32))


def kernel(x, c, ctx, c_ctx, ada_w, ada_b, norm_pre, norm_post, ev_w_in, ev_mu, ev_w0, ev_w2, ev_a0, ev_a2, ev_k_k, ev_k_a, ev_r_k, ev_lnx_g, ev_lnx_b, ev_w_out, od_w_in, od_decay_logit, od_w_out):
    cvec = jnp.concatenate([c, c_ctx[None, :], jnp.zeros((8 - BATCH - 1, D_MODEL), F32)], axis=0)
    mod = _ada_mod(cvec, ada_w, ada_b)
    ones_bd = jnp.asarray(_head_ones(), dtype=BF16)

    gate_r, four_in, gate_f, r, v, kk, k2, b2, lw2, bonus = _proj_terms(
        x, ctx, norm_pre[0], mod[0], ev_w_in[0].astype(BF16), ev_mu[0], ev_w0[0], ev_w2[0], ev_a0[0], ev_a2[0],
        ev_k_k[0], ev_k_a[0], ev_r_k[0].reshape(RWKV_W), ones_bd)
    yf, yb = _rwkv_scan(r, v, kk, k2, b2, lw2)
    cosf, sinf = _rope_tables()
    xcomb, u2 = _even_finish(yf, yb, bonus, gate_r, gate_f, _four_lat(four_in), _four_ctx(four_in), x, ctx, ones_bd,
                             ev_lnx_g[0], ev_lnx_b[0], ev_w_out[0].astype(BF16), norm_post[0], mod[0],
                             norm_pre[1], mod[1], od_w_in[0].astype(BF16), jnp.asarray(cosf), jnp.asarray(sinf))

    dl = jnp.broadcast_to(od_decay_logit[0][:, :, None, None], (2, RET_HEADS, 1, RET_HEAD))
    o_f, o_b = _ret_scan(u2, dl)
    return _odd_finish(o_f, o_b, u2, xcomb, od_w_out[0].astype(BF16), norm_post[1], mod[1])
```

```python
import functools

import numpy as np
import jax
import jax.numpy as jnp
from jax import lax
from jax.experimental import pallas as pl
from jax.experimental.pallas import tpu as pltpu

F32 = jnp.float32
BF16 = jnp.bfloat16
HIGHEST = lax.Precision.HIGHEST

D_MODEL = 1024
BATCH = 2
SEQ = 8192
GRID_W = 64
GRID_ROWS = SEQ // GRID_W
CTX_LEN = 256
TOK = SEQ + CTX_LEN
EPS = 1e-6

RWKV_HEAD = 64
RWKV_HEADS = 12
RWKV_W = RWKV_HEADS * RWKV_HEAD
LORA = 64
FOUR_GROUPS = 4
FOUR_W = 256
FOUR_GROUP_W = 64
N_SHIFT = 3 * RWKV_W + 4 * LORA
EVEN_IN = N_SHIFT + RWKV_W + 2 * FOUR_W
TAIL_W = EVEN_IN - N_SHIFT
RWKV_GN_EPS = 64e-5

RET_HEADS = 8
RET_HEAD = 128
RET_W = 1024
ODD_IN = 4 * RET_W
ROPE_BASE = 10000.0

TILE = 256
N_TILES = TOK // TILE
CTX_TILE = N_TILES - 1
RW_CHUNK = 64
RW_CHUNKS_PER_STEP = 4
RW_NBLOCK = TOK // (RW_CHUNK * RW_CHUNKS_PER_STEP)
RW_CTX_BLOCKS = CTX_LEN // (RW_CHUNK * RW_CHUNKS_PER_STEP)
PAIR_W = 2 * RWKV_HEAD
N_PAIRS = RWKV_W // PAIR_W
RET_CHUNK = 256
RET_SUB = 128
RET_NCHUNK = TOK // RET_CHUNK
VMEM_LIMIT = 56 * 1024 * 1024


def _dot(a, b, precision=HIGHEST):
    return jnp.dot(a, b, precision=precision, preferred_element_type=F32)


def _dot_nt(a, b, precision=HIGHEST):
    return lax.dot_general(a, b, (((1,), (1,)), ((), ())), precision=precision, preferred_element_type=F32)


def _dot_tn(a, b, precision=HIGHEST):
    return lax.dot_general(a, b, (((0,), (0,)), ((), ())), precision=precision, preferred_element_type=F32)


def _split_bf16(x):
    hi = x.astype(BF16)
    return hi, (x - hi.astype(F32)).astype(BF16)


def _head_sum(x, ones_bd):
    xb = x.astype(BF16)
    out = []
    for c in range(x.shape[1] // PAIR_W):
        out.append(_dot(xb[:, c * PAIR_W:(c + 1) * PAIR_W], ones_bd, None))
    return jnp.concatenate(out, axis=1)


def _silu(x):
    return x * (1.0 / (1.0 + jnp.exp(-x)))


def _sigmoid(x):
    return 1.0 / (1.0 + jnp.exp(-x))


def _params(*sem):
    return pltpu.CompilerParams(dimension_semantics=sem, vmem_limit_bytes=VMEM_LIMIT)


def _ada_kernel(c_ref, w_ref, b_ref, o_ref):
    o_ref[0] = _dot(_silu(c_ref[...]), w_ref[0]) + b_ref[0]


def _ada_mod(cvec, ada_w, ada_b):
    depth = ada_w.shape[0]
    nt = 3 * D_MODEL // 1024
    return pl.pallas_call(
        _ada_kernel,
        out_shape=jax.ShapeDtypeStruct((depth, 8, 3 * D_MODEL), F32),
        grid=(depth, nt),
        in_specs=[
            pl.BlockSpec((8, D_MODEL), lambda l, n: (0, 0)),
            pl.BlockSpec((1, D_MODEL, 1024), lambda l, n: (l, 0, n)),
            pl.BlockSpec((1, 1, 1024), lambda l, n: (l, 0, n)),
        ],
        out_specs=pl.BlockSpec((1, 8, 1024), lambda l, n: (l, 0, n)),
        compiler_params=_params("arbitrary", "arbitrary"),
        name="ada_mod",
    )(cvec, ada_w, ada_b.reshape(depth, 1, 3 * D_MODEL))


def _mod_row(b, j):
    return jnp.where(j == CTX_TILE, BATCH, b)


def _tile_rows(lat_ref, ctx_ref, tile):
    return jnp.where(tile == CTX_TILE, ctx_ref[0], lat_ref[0])


def _lat_ctx_specs(width):
    lat = pl.BlockSpec((1, TILE, width), lambda b, j: (b, jnp.minimum(j, CTX_TILE - 1), 0))
    ctx = pl.BlockSpec((1, TILE, width), lambda b, j: (b, 0, 0))
    return lat, ctx


PT_LAG = 2
PROJ_BLOCK = 256


def _proj_terms_kernel(x_ref, xc_ref, g_ref, mod_ref, w_ref, mu_ref, w0_ref, w2_ref, a0_ref, a2_ref, kk_ref_, ka_ref,
                       rk_ref, ones_ref, gr_out, four_out, gf_out, r_out, v_out, kk_out, k_out, b_out, lw_out,
                       bonus_out, cur_ref, nxt_ref, halo_ref):
    step = pl.program_id(1)
    j = step - PT_LAG

    @pl.when(step == 0)
    def _():
        cur_ref[...] = jnp.zeros_like(cur_ref)
        nxt_ref[...] = jnp.zeros_like(nxt_ref)
        halo_ref[...] = jnp.zeros_like(halo_ref)

    y = _tile_rows(x_ref, xc_ref, jnp.minimum(step, N_TILES - 1))
    gain = g_ref[...] * (1.0 + mod_ref[0, :, D_MODEL:2 * D_MODEL])
    h = (y * lax.rsqrt(jnp.mean(y * y, axis=-1, keepdims=True) + EPS) * gain + mod_ref[0, :, 0:D_MODEL]).astype(BF16)

    is_ctx = j == CTX_TILE
    cur_u = cur_ref[...]
    nxt_u = nxt_ref[...]
    up_rows = halo_ref[...]
    dn_rows = nxt_u[0:GRID_W, :]
    halo_ref[...] = cur_u[TILE - GRID_W:, :]
    cur_ref[...] = nxt_u
    x = cur_u.astype(F32)

    blocks = iter(range(EVEN_IN // PROJ_BLOCK))
    targets = ((nxt_ref, 0, N_SHIFT), (gr_out, N_SHIFT, RWKV_W), (four_out, N_SHIFT + RWKV_W, FOUR_W),
               (gf_out, N_SHIFT + RWKV_W + FOUR_W, FOUR_W))

    def issue(count):
        for _ in range(count):
            lo = next(blocks) * PROJ_BLOCK
            acc = jnp.dot(h, w_ref[:, lo:lo + PROJ_BLOCK], preferred_element_type=F32)
            for ref, start, width in targets:
                if start <= lo < start + width:
                    if ref is nxt_ref:
                        ref[:, lo - start:lo - start + PROJ_BLOCK] = acc.astype(ref.dtype)
                    else:
                        ref[0, :, lo - start:lo - start + PROJ_BLOCK] = acc.astype(ref.dtype)

    left = pltpu.roll(x, 1, 0)
    right = pltpu.roll(x, TILE - 1, 0)
    row8 = lax.broadcasted_iota(jnp.int32, (8, N_SHIFT), 0)
    inner_first = jnp.where(is_ctx, 0, 1)
    inner_last = jnp.where(is_ctx, 8, 7)

    def zero_rows(a, first):
        parts, pos = [], 0
        for g in range(0, TILE, GRID_W):
            at = g if first else g + GRID_W - 8
            if first:
                keep = row8 >= (1 if g == 0 else inner_first)
            else:
                keep = row8 < (7 if g == TILE - GRID_W else inner_last)
            parts += [a[pos:at], jnp.where(keep, a[at:at + 8], 0.0)]
            pos = at + 8
        return jnp.concatenate([p for p in parts + [a[pos:]] if p.shape[0]], axis=0)

    left = zero_rows(left, True)
    right = zero_rows(right, False)
    up_ok = jnp.where(j == 0, 0.0, 1.0)
    dn_ok = jnp.where(j == CTX_TILE - 1, 0.0, 1.0)
    up = jnp.concatenate([up_rows.astype(F32) * up_ok, x[:TILE - GRID_W]], axis=0)
    down = jnp.concatenate([x[GRID_W:], dn_rows.astype(F32) * dn_ok], axis=0)
    lane = lax.broadcasted_iota(jnp.int32, (1, N_SHIFT), 1)
    sel = jnp.where(is_ctx, lane & 1, lane & 3)
    shifted = jnp.where(sel == 0, left, jnp.where(sel == 1, right, jnp.where(sel == 2, up, down)))
    s = x + (shifted - x) * mu_ref[...]
    issue(6)

    r = s[:, 0:RWKV_W]
    k0 = s[:, RWKV_W:2 * RWKV_W]
    v = s[:, 2 * RWKV_W:3 * RWKV_W]
    ones_bd = ones_ref[...]
    kk = k0 * kk_ref_[...]
    kk = kk * lax.rsqrt(jnp.maximum(_head_sum(kk * kk, ones_bd), 1e-24))
    issue(2)
    r_out[0] = r.astype(BF16)
    v_out[0] = v.astype(BF16)
    kk_out[0] = kk.astype(BF16)
    bonus = jnp.zeros_like(v)
    for d in range(2):
        wl = s[:, 3 * RWKV_W + d * LORA:3 * RWKV_W + (d + 1) * LORA]
        al = s[:, 3 * RWKV_W + (2 + d) * LORA:3 * RWKV_W + (3 + d) * LORA]
        z = w0_ref[d:d + 1, :] + _dot(jnp.tanh(wl).astype(BF16), w2_ref[d].astype(BF16), None)
        lw_out[d, 0] = -np.float32(np.exp(-0.5)) * _sigmoid(z)
        a = _sigmoid(a0_ref[d:d + 1, :] + _dot(al.astype(BF16), a2_ref[d].astype(BF16), None))
        issue(2)
        k = k0 * (1.0 + (a - 1.0) * ka_ref[...])
        k_out[d, 0] = k.astype(BF16)
        b_out[d, 0] = (kk * a).astype(BF16)
        bonus = bonus + _head_sum(r * k * rk_ref[...], ones_bd) * v
        issue(2 if d == 0 else 1)
    bonus_out[0] = bonus.astype(BF16)


def _proj_terms(x, ctx, g, mod, w_bf16, mu, w0, w2, a0, a2, k_k, k_a, r_k, ones_bd):
    tok = jax.ShapeDtypeStruct((BATCH, TOK, RWKV_W), BF16)
    tok2 = jax.ShapeDtypeStruct((2, BATCH, TOK, RWKV_W), BF16)
    tok2_f32 = jax.ShapeDtypeStruct((2, BATCH, TOK, RWKV_W), F32)
    full = lambda shape: pl.BlockSpec(shape, lambda b, j: (0,) * len(shape))
    cur = lambda j: jnp.minimum(j, N_TILES - 1)
    lag = lambda j: jnp.maximum(j - PT_LAG, 0)
    cur_spec = lambda w: pl.BlockSpec((1, TILE, w), lambda b, j: (b, cur(j), 0))
    tspec = pl.BlockSpec((1, TILE, RWKV_W), lambda b, j: (b, lag(j), 0))
    t2spec = pl.BlockSpec((2, 1, TILE, RWKV_W), lambda b, j: (0, b, lag(j), 0))
    return pl.pallas_call(
        _proj_terms_kernel,
        out_shape=(tok, jax.ShapeDtypeStruct((BATCH, TOK, FOUR_W), F32), jax.ShapeDtypeStruct((BATCH, TOK, FOUR_W), BF16),
                   tok, tok, tok, tok2, tok2, tok2_f32, tok),
        grid=(BATCH, N_TILES + PT_LAG),
        in_specs=[
            *_lat_ctx_specs(D_MODEL),
            full((1, D_MODEL)),
            pl.BlockSpec((1, 1, 3 * D_MODEL), lambda b, j: (_mod_row(b, cur(j)), 0, 0)),
            full((D_MODEL, EVEN_IN)),
            full((1, N_SHIFT)), full((2, RWKV_W)), full((2, LORA, RWKV_W)), full((2, RWKV_W)),
            full((2, LORA, RWKV_W)), full((1, RWKV_W)), full((1, RWKV_W)), full((1, RWKV_W)),
            full((PAIR_W, PAIR_W)),
        ],
        out_specs=(cur_spec(RWKV_W), cur_spec(FOUR_W), cur_spec(FOUR_W),
                   tspec, tspec, tspec, t2spec, t2spec, t2spec, tspec),
        scratch_shapes=[pltpu.VMEM((TILE, N_SHIFT), BF16), pltpu.VMEM((TILE, N_SHIFT), BF16),
                        pltpu.VMEM((GRID_W, N_SHIFT), BF16)],
        compiler_params=_params("arbitrary", "arbitrary"),
        name="proj_terms",
    )(x, ctx, g.reshape(1, D_MODEL), mod.reshape(8, 1, 3 * D_MODEL), w_bf16, mu.reshape(1, N_SHIFT), w0, w2, a0, a2,
      k_k.reshape(1, RWKV_W), k_a.reshape(1, RWKV_W), r_k.reshape(1, RWKV_W), ones_bd)


def _rw_block_index(d, s):
    lat = jnp.where(d == 0, s - RW_CTX_BLOCKS, RW_NBLOCK - 1 - s)
    ctx = RW_NBLOCK - RW_CTX_BLOCKS + jnp.where(d == 0, s, RW_CTX_BLOCKS - 1 - s)
    return jnp.where(s < RW_CTX_BLOCKS, ctx, lat)


def _rwkv_scan_kernel(rf_ref, vf_ref, kkf_ref, rb_ref, vb_ref, kkb_ref, kf_ref, bf_ref, lwf_ref, kb_ref, bb_ref,
                      lwb_ref, yf_ref, yb_ref, h_ref):
    s = pl.program_id(0)
    n = PAIR_W

    @pl.when(s == 0)
    def _():
        h_ref[...] = jnp.zeros_like(h_ref)

    c = RW_CHUNK
    ti = lax.broadcasted_iota(jnp.int32, (c, n), 0)
    li = lax.broadcasted_iota(jnp.int32, (c, n), 1)
    si = li & (c - 1)
    strict_d = (si < ti, si > ti)
    incl_d = (si <= ti, si >= ti)
    ident = jnp.where(si == ti, 1.0, 0.0)
    blk8 = (ti >> 3) == (si >> 3)
    merges = [((ti >> (sh + 1)) == (si >> (sh + 1))) & ((ti >> sh) != (si >> sh)) for sh in (3, 4, 5)]
    head0 = li < RWKV_HEAD
    cumb_d = [jnp.where(m[:, :c], 1.0, 0.0).astype(BF16) for m in incl_d]
    ri = lax.broadcasted_iota(jnp.int32, (n, n), 0)
    ci = lax.broadcasted_iota(jnp.int32, (n, n), 1)
    eye = ri == ci

    def stack(xb):
        zero = jnp.zeros_like(xb)
        return jnp.concatenate([jnp.where(head0, xb, zero), jnp.where(head0, zero, xb)], axis=0)

    def prod(x, y):
        return _dot(x.astype(BF16), stack(y.astype(BF16)), None)

    dir_refs = ((rf_ref, vf_ref, kkf_ref, kf_ref, bf_ref, lwf_ref, yf_ref),
                (rb_ref, vb_ref, kkb_ref, kb_ref, bb_ref, lwb_ref, yb_ref))
    pairs = range(2 * BATCH * N_PAIRS)
    drn = [p // (BATCH * N_PAIRS) for p in pairs]
    bat = [(p // N_PAIRS) % BATCH for p in pairs]
    col = [slice((p % N_PAIRS) * PAIR_W, (p % N_PAIRS + 1) * PAIR_W) for p in pairs]
    strict = [strict_d[drn[p]] for p in pairs]
    incl = [incl_d[drn[p]] for p in pairs]
    lax.fori_loop(0, RW_CHUNKS_PER_STEP, functools.partial(
        _rwkv_chunk, dir_refs=dir_refs, h_ref=h_ref, pairs=pairs, drn=drn, bat=bat, col=col, strict=strict, incl=incl,
        cumb_d=cumb_d, ident=ident, blk8=blk8, merges=merges, eye=eye, stack=stack, prod=prod), 0)


def _rwkv_chunk(i, carry, *, dir_refs, h_ref, pairs, drn, bat, col, strict, incl, cumb_d, ident, blk8, merges, eye,
                stack, prod):
    c, n = RW_CHUNK, PAIR_W
    rows_d = (pl.ds(pl.multiple_of(i * c, c), c), pl.ds(pl.multiple_of((RW_CHUNKS_PER_STEP - 1 - i) * c, c), c))
    rows = [rows_d[drn[p]] for p in pairs]
    lw_all, lc_all = {}, {}
    for dd in range(2):
        for bb in range(BATCH):
            lw_b = dir_refs[dd][5][0, bb, rows_d[dd], :]
            lw_hi = lw_b.astype(BF16)
            lw_mid, lw_lo = _split_bf16(lw_b - lw_hi.astype(F32))
            lw_all[dd, bb] = lw_b
            lc_all[dd, bb] = _dot(jnp.concatenate([cumb_d[dd]] * 3, axis=1),
                                  jnp.concatenate([lw_hi, lw_mid, lw_lo], axis=0), None)
    ar, kbt, vb, g_col = [], [], [], []
    for p in pairs:
        r_ref, v_ref, kk_ref, k_ref, b_ref = dir_refs[drn[p]][:5]
        lw, lc = lw_all[drn[p], bat[p]][:, col[p]], lc_all[drn[p], bat[p]][:, col[p]]
        lc_end = jnp.sum(lw, axis=0, keepdims=True)
        g_inv = jnp.exp(-lc)
        kc, bc = k_ref[0, bat[p], rows[p], col[p]], b_ref[0, bat[p], rows[p], col[p]]
        at = (-kk_ref[bat[p], rows[p], col[p]] * jnp.exp(lc - lw)).astype(BF16)
        rt = (r_ref[bat[p], rows[p], col[p]] * jnp.exp(lc)).astype(BF16)
        ar.append(jnp.concatenate([at, rt], axis=0))
        kb = jnp.concatenate([stack((kc * g_inv).astype(BF16)), stack((bc * g_inv).astype(BF16))], axis=0)
        kbt.append(kb.T)
        vb.append(v_ref[bat[p], rows[p], col[p]].astype(BF16))
        g_col.append(jnp.sum(jnp.where(eye, jnp.broadcast_to(jnp.exp(lc_end), (n, n)), 0.0), axis=1, keepdims=True))

    quad = [_dot(ar[p], kbt[p], None) for p in pairs]
    a_b = [jnp.where(strict[p], quad[p][:c, n:], 0.0) for p in pairs]
    akbk = [jnp.concatenate([jnp.where(strict[p], quad[p][:c, :n], 0.0), jnp.where(incl[p], quad[p][c:, :n], 0.0)],
                            axis=0).astype(BF16) for p in pairs]
    b_b = [jnp.where(incl[p], quad[p][c:, n:], 0.0).astype(BF16) for p in pairs]
    h0 = [h_ref[p] for p in pairs]
    arh = [_dot(ar[p], h0[p].astype(BF16), None) for p in pairs]
    abv = [_dot(jnp.concatenate([akbk[p], kbt[p][:, :n]], axis=0), stack(vb[p]), None) for p in pairs]

    n1 = [jnp.where(blk8, a_b[p], 0.0) for p in pairs]
    n2 = [prod(n1[p], n1[p]) for p in pairs]
    both = [prod(jnp.concatenate([n2[p], ident + n1[p]], axis=0), n2[p]) for p in pairs]
    n4 = [both[p][:c] for p in pairs]
    t12 = [ident + n1[p] + both[p][c:] for p in pairs]
    tinv = [t12[p] + prod(t12[p], n4[p]) for p in pairs]
    for m in merges:
        off = [jnp.where(m, a_b[p], 0.0) for p in pairs]
        half = [prod(tinv[p], off[p]) for p in pairs]
        tinv = [tinv[p] + prod(half[p], tinv[p]) for p in pairs]

    ub = [_dot(tinv[p].astype(BF16), stack((arh[p][:c] + abv[p][:c]).astype(BF16)), None).astype(BF16) for p in pairs]
    bbu = [_dot(jnp.concatenate([b_b[p], kbt[p][:, n:]], axis=0), stack(ub[p]), None) for p in pairs]
    for p in pairs:
        dir_refs[drn[p]][6][bat[p], rows[p], col[p]] = (arh[p][c:] + abv[p][c:2 * c] + bbu[p][:c]).astype(BF16)
        h_ref[p] = (h0[p] + abv[p][2 * c:] + bbu[p][c:]) * g_col[p]
    return carry


def _rwkv_scan(r, v, kk, k2, b2, lw2):
    rows = RW_CHUNK * RW_CHUNKS_PER_STEP

    def tok_spec(d):
        return pl.BlockSpec((BATCH, rows, RWKV_W), lambda s: (0, _rw_block_index(d, s), 0))

    def dir_spec(d):
        return pl.BlockSpec((1, BATCH, rows, RWKV_W), lambda s: (d, 0, _rw_block_index(d, s), 0))

    y_shape = jax.ShapeDtypeStruct((BATCH, TOK, RWKV_W), BF16)
    return pl.pallas_call(
        _rwkv_scan_kernel,
        out_shape=(y_shape, y_shape),
        grid=(RW_NBLOCK,),
        in_specs=[tok_spec(0), tok_spec(0), tok_spec(0), tok_spec(1), tok_spec(1), tok_spec(1),
                  dir_spec(0), dir_spec(0), dir_spec(0), dir_spec(1), dir_spec(1), dir_spec(1)],
        out_specs=(tok_spec(0), tok_spec(1)),
        scratch_shapes=[pltpu.VMEM((2 * BATCH * N_PAIRS, PAIR_W, PAIR_W), F32)],
        compiler_params=_params("arbitrary"),
        name="rwkv_scan",
    )(r, v, kk, r, v, kk, k2, b2, lw2, k2, b2, lw2)


def _dft_mats(n):
    i = np.arange(n)
    ang = 2.0 * np.pi * ((i[:, None] * i[None, :]) % n) / n
    return np.cos(ang).astype(np.float32), np.sin(ang).astype(np.float32)


def _chan_dft_mat():
    c, s = _dft_mats(FOUR_GROUP_W)
    eye = np.eye(FOUR_GROUPS, dtype=np.float32)
    return np.concatenate([np.kron(eye, c), np.kron(eye, s)], axis=1)


def _bdot(a, b):
    return jnp.dot(a.astype(BF16), b.astype(BF16), preferred_element_type=F32)


def _four_ctx_kernel(u_ref, cs_ref, c_ref, s_ref, o_ref):
    pq = _bdot(u_ref[0], cs_ref[...])
    scale = np.float32(1.0 / np.sqrt(CTX_LEN * FOUR_GROUP_W))
    o_ref[0] = (_bdot(c_ref[...], pq[:, :FOUR_W]) - _bdot(s_ref[...], pq[:, FOUR_W:])) * scale


def _four_ctx(u):
    c, s = _dft_mats(CTX_LEN)
    full = lambda shape: pl.BlockSpec(shape, lambda b: (0,) * len(shape))
    return pl.pallas_call(
        _four_ctx_kernel,
        out_shape=jax.ShapeDtypeStruct((BATCH, CTX_LEN, FOUR_W), F32),
        grid=(BATCH,),
        in_specs=[pl.BlockSpec((1, CTX_LEN, FOUR_W), lambda b: (b, CTX_TILE, 0)),
                  full((FOUR_W, 2 * FOUR_W)), full((CTX_LEN, CTX_LEN)), full((CTX_LEN, CTX_LEN))],
        out_specs=pl.BlockSpec((1, CTX_LEN, FOUR_W), lambda b: (b, 0, 0)),
        compiler_params=_params("arbitrary"),
        name="fourier_ctx",
    )(u, jnp.asarray(_chan_dft_mat()), jnp.asarray(c), jnp.asarray(s))


L2_PER_STEP = 16


def _four_lat_a_kernel(u_ref, cs_ref, c_ref, s_ref, tc_ref, ts_ref, zr_ref, zi_ref):
    cs = cs_ref[...]
    cm = c_ref[...]
    sm = s_ref[...]
    for i in range(L2_PER_STEP):
        x = u_ref[0, :, i, :]
        pq = _bdot(x, cs)
        cpq = _bdot(cm, pq)
        spq = _bdot(sm, pq)
        zr = cpq[:, :FOUR_W] - spq[:, FOUR_W:]
        zi = -cpq[:, FOUR_W:] - spq[:, :FOUR_W]
        tc, ts = tc_ref[i], ts_ref[i]
        zr_ref[0, i] = zr * tc + zi * ts
        zi_ref[0, i] = zi * tc - zr * ts


K1_PER_STEP = 16


def _four_lat_b_kernel(zr_ref, zi_ref, cs_ref, o_ref):
    scale = np.float32(1.0 / np.sqrt(SEQ * FOUR_GROUP_W))
    cs = cs_ref[...]
    for i in range(K1_PER_STEP):
        z = jnp.concatenate([zr_ref[0, :, i, :], zi_ref[0, :, i, :]], axis=0)
        o_ref[0, :, i, :] = _bdot(cs, z) * scale


def _four_lat(u):
    c1, s1 = _dft_mats(GRID_ROWS)
    c2, s2 = _dft_mats(GRID_W)
    k1 = np.arange(GRID_ROWS)[None, :, None]
    l2 = np.arange(GRID_W)[:, None, None]
    ang = 2.0 * np.pi * (k1 * l2) / SEQ
    tc = np.broadcast_to(np.cos(ang), (GRID_W, GRID_ROWS, FOUR_W)).astype(np.float32)
    ts = np.broadcast_to(np.sin(ang), (GRID_W, GRID_ROWS, FOUR_W)).astype(np.float32)
    u4 = u.reshape(BATCH, TOK // GRID_W, GRID_W, FOUR_W)
    full = lambda shape: pl.BlockSpec(shape, lambda b, g: (0,) * len(shape))
    zshape = jax.ShapeDtypeStruct((BATCH, GRID_W, GRID_ROWS, FOUR_W), F32)
    zspec = pl.BlockSpec((1, L2_PER_STEP, GRID_ROWS, FOUR_W), lambda b, g: (b, g, 0, 0))
    tspec = pl.BlockSpec((L2_PER_STEP, GRID_ROWS, FOUR_W), lambda b, g: (g, 0, 0))
    zr, zi = pl.pallas_call(
        _four_lat_a_kernel,
        out_shape=(zshape, zshape),
        grid=(BATCH, GRID_W // L2_PER_STEP),
        in_specs=[pl.BlockSpec((1, GRID_ROWS, L2_PER_STEP, FOUR_W), lambda b, g: (b, 0, g, 0)),
                  full((FOUR_W, 2 * FOUR_W)), full((GRID_ROWS, GRID_ROWS)), full((GRID_ROWS, GRID_ROWS)),
                  tspec, tspec],
        out_specs=(zspec, zspec),
        compiler_params=_params("arbitrary", "arbitrary"),
        name="fourier_lat_a",
    )(u4, jnp.asarray(_chan_dft_mat()), jnp.asarray(c1), jnp.asarray(s1), jnp.asarray(tc), jnp.asarray(ts))
    zb = pl.BlockSpec((1, GRID_W, K1_PER_STEP, FOUR_W), lambda b, g: (b, 0, g, 0))
    out = pl.pallas_call(
        _four_lat_b_kernel,
        out_shape=jax.ShapeDtypeStruct((BATCH, GRID_W, GRID_ROWS, FOUR_W), F32),
        grid=(BATCH, GRID_ROWS // K1_PER_STEP),
        in_specs=[zb, zb, full((GRID_W, 2 * GRID_W))],
        out_specs=zb,
        compiler_params=_params("arbitrary", "arbitrary"),
        name="fourier_lat_b",
    )(zr, zi, jnp.asarray(np.concatenate([c2, s2], axis=1)))
    return out.reshape(BATCH, SEQ, FOUR_W)


def _post_residual(x, out, npost, gate):
    y = out * lax.rsqrt(jnp.mean(out * out, axis=-1, keepdims=True) + EPS) * npost
    return x + gate * y


def _retention_project(x, g_ref, mod_ref, w_ref, cos_ref, sin_ref, u_ref):
    gain = g_ref[...] * (1.0 + mod_ref[0, :, D_MODEL:2 * D_MODEL])
    h = (x * lax.rsqrt(jnp.mean(x * x, axis=-1, keepdims=True) + EPS) * gain + mod_ref[0, :, 0:D_MODEL]).astype(BF16)
    cosf, sinf = cos_ref[...], sin_ref[...]
    blocks = iter(range(ODD_IN // PROJ_BLOCK))

    def issue(count):
        for _ in range(count):
            lo = next(blocks) * PROJ_BLOCK
            acc = jnp.dot(h, w_ref[:, lo:lo + PROJ_BLOCK], preferred_element_type=F32)
            if lo >= 2 * RET_W:
                u_ref[0, :, lo:lo + PROJ_BLOCK] = acc.astype(u_ref.dtype)
                continue
            for i in range(PROJ_BLOCK // RET_HEAD):
                t = acc[:, i * RET_HEAD:(i + 1) * RET_HEAD]
                t = t * cosf + pltpu.roll(t, RET_HEAD // 2, 1) * sinf
                if lo >= RET_W:
                    t = t * np.float32(RET_HEAD ** -0.5)
                u_ref[0, :, lo + i * RET_HEAD:lo + (i + 1) * RET_HEAD] = t.astype(u_ref.dtype)

    return issue


def _even_finish_kernel(yf_ref, yb_ref, bonus_ref, gr_ref, gf_ref, four_ref, fourc_ref, x_ref, xc_ref, ones_ref, g_ref,
                        bta_ref, w_ref, np_ref, mod_ref, g1_ref, mod1_ref, w1_ref, cos_ref, sin_ref, o_ref, u1_ref,
                        x1_ref):
    j = pl.program_id(1)
    tile = jnp.minimum(j, N_TILES - 1)

    @pl.when(j == 0)
    def _():
        x1_ref[...] = jnp.zeros_like(x1_ref)

    issue = _retention_project(x1_ref[...], g1_ref, mod1_ref, w1_ref, cos_ref, sin_ref, u1_ref)
    issue(3)
    o = yf_ref[0].astype(F32) + yb_ref[0].astype(F32) + bonus_ref[0].astype(F32)
    ones_bd = ones_ref[...]
    inv = np.float32(1.0 / RWKV_HEAD)
    mean = _head_sum(o, ones_bd) * inv
    issue(3)
    cen = o - mean
    var = _head_sum(cen * cen, ones_bd) * inv
    issue(3)
    on = cen * lax.rsqrt(var + RWKV_GN_EPS) * g_ref[...] + bta_ref[...]
    yr = on * _silu(gr_ref[0].astype(F32))
    yf = _tile_rows(four_ref, fourc_ref, tile) * _silu(gf_ref[0].astype(F32))
    z = jnp.concatenate([yr, yf], axis=-1).astype(BF16)
    out = jnp.dot(z, w_ref[...], preferred_element_type=F32)
    issue(ODD_IN // PROJ_BLOCK - 9)
    x1 = _post_residual(_tile_rows(x_ref, xc_ref, tile), out, np_ref[...], mod_ref[0, :, 2 * D_MODEL:])
    o_ref[0] = x1
    x1_ref[...] = x1


def _even_finish(yf, yb, bonus, gate_r, gate_f, four_lat, four_ctx, x_lat, x_ctx, ones_bd, lnx_g, lnx_b, w_out_bf16,
                 npost, mod, next_g, next_mod, next_w_bf16, cosf, sinf):
    full = lambda shape: pl.BlockSpec(shape, lambda b, j: (0,) * len(shape))
    n_next = next_w_bf16.shape[1]
    cur = lambda j: jnp.minimum(j, N_TILES - 1)
    prev = lambda j: jnp.maximum(j - 1, 0)
    cur_spec = lambda w: pl.BlockSpec((1, TILE, w), lambda b, j: (b, cur(j), 0))
    mod_spec = lambda t: pl.BlockSpec((1, 1, 3 * D_MODEL), lambda b, j: (_mod_row(b, t(j)), 0, 0))
    return pl.pallas_call(
        _even_finish_kernel,
        out_shape=(jax.ShapeDtypeStruct((BATCH, TOK, D_MODEL), F32), jax.ShapeDtypeStruct((BATCH, TOK, n_next), BF16)),
        grid=(BATCH, N_TILES + 1),
        in_specs=[
            cur_spec(RWKV_W), cur_spec(RWKV_W), cur_spec(RWKV_W), cur_spec(RWKV_W), cur_spec(FOUR_W),
            *_lat_ctx_specs(FOUR_W),
            *_lat_ctx_specs(D_MODEL),
            full((PAIR_W, PAIR_W)), full((1, RWKV_W)), full((1, RWKV_W)), full((D_MODEL, D_MODEL)),
            full((1, D_MODEL)), mod_spec(cur),
            full((1, D_MODEL)), mod_spec(prev), full((D_MODEL, n_next)),
            pl.BlockSpec((TILE, RET_HEAD), lambda b, j: (prev(j), 0)),
            pl.BlockSpec((TILE, RET_HEAD), lambda b, j: (prev(j), 0)),
        ],
        out_specs=(cur_spec(D_MODEL), pl.BlockSpec((1, TILE, n_next), lambda b, j: (b, prev(j), 0))),
        scratch_shapes=[pltpu.VMEM((TILE, D_MODEL), F32)],
        compiler_params=_params("arbitrary", "arbitrary"),
        name="even_finish",
    )(yf, yb, bonus, gate_r, gate_f, four_lat, four_ctx, x_lat, x_ctx, ones_bd, lnx_g.reshape(1, RWKV_W),
      lnx_b.reshape(1, RWKV_W), w_out_bf16, npost.reshape(1, D_MODEL), mod.reshape(8, 1, 3 * D_MODEL),
      next_g.reshape(1, D_MODEL), next_mod.reshape(8, 1, 3 * D_MODEL), next_w_bf16, cosf, sinf)


def _ret_chunk_index(d, s):
    lat = jnp.where(d == 0, s - 1, RET_NCHUNK - 1 - s)
    return jnp.where(s == 0, RET_NCHUNK - 1, lat)


def _ret_scan_kernel(qf_ref, kf_ref, vf_ref, qb_ref, kb_ref, vb_ref, dl_ref, of_ref, ob_ref, st_ref):
    s = pl.program_id(1)
    c = RET_SUB

    @pl.when(s == 0)
    def _():
        st_ref[...] = jnp.zeros_like(st_ref)

    ri = lax.broadcasted_iota(jnp.int32, (c, c), 0)
    ci = lax.broadcasted_iota(jnp.int32, (c, c), 1)
    ti = lax.broadcasted_iota(jnp.int32, (c, RET_HEAD), 0)
    masks = (ri >= ci, ci > ri)
    dists = (jnp.where(masks[0], ri - ci, 0).astype(F32), jnp.where(masks[1], ci - ri, 0).astype(F32))
    poss = (ti.astype(F32), (c - 1 - ti).astype(F32))
    dir_refs = ((qf_ref, kf_ref, vf_ref, of_ref), (qb_ref, kb_ref, vb_ref, ob_ref))
    streams = range(2 * RET_HEADS)
    drn = [n // RET_HEADS for n in streams]
    lg = []
    for n in streams:
        x = dl_ref[drn[n], n % RET_HEADS]
        lg.append((jnp.minimum(x, 0.0) - jnp.log(1.0 + jnp.exp(-jnp.abs(x)))) * np.float32(np.log2(np.e)))
    lax.fori_loop(0, RET_CHUNK // RET_SUB, functools.partial(
        _ret_sub_chunk, dir_refs=dir_refs, st_ref=st_ref, streams=streams, drn=drn, lg=lg, masks=masks, dists=dists,
        poss=poss), 0)


def _ret_sub_chunk(i, carry, *, dir_refs, st_ref, streams, drn, lg, masks, dists, poss):
    c = RET_SUB
    last = RET_CHUNK // RET_SUB - 1
    rows_d = (pl.ds(pl.multiple_of(i * c, c), c), pl.ds(pl.multiple_of((last - i) * c, c), c))
    rows = [rows_d[drn[n]] for n in streams]
    col = [slice((n % RET_HEADS) * RET_HEAD, (n % RET_HEADS + 1) * RET_HEAD) for n in streams]
    q = [dir_refs[drn[n]][0][0, rows[n], col[n]] for n in streams]
    k = [dir_refs[drn[n]][1][0, rows[n], col[n]] for n in streams]
    v = [dir_refs[drn[n]][2][0, rows[n], col[n]] for n in streams]
    qk = [_dot_nt(q[n], k[n], None) for n in streams]
    st = [st_ref[n] for n in streams]
    cross = [_dot(q[n], st[n].astype(BF16), None) for n in streams]
    sc = [(qk[n] * jnp.where(masks[drn[n]], jnp.exp2(dists[drn[n]] * jnp.broadcast_to(lg[n][:, :1], (c, c))),
                             0.0)).astype(BF16) for n in streams]
    inner = [_dot(sc[n], v[n], None) for n in streams]
    upd = [_dot_tn((k[n].astype(F32) * jnp.exp2((c - 1.0 - poss[drn[n]]) * lg[n])).astype(BF16), v[n], None)
           for n in streams]
    for n in streams:
        o_ref = dir_refs[drn[n]][3]
        o_ref[0, rows[n], col[n]] = (inner[n] + cross[n] * jnp.exp2((poss[drn[n]] + 1.0) * lg[n])).astype(o_ref.dtype)
        st_ref[n] = st[n] * jnp.exp2(np.float32(c) * lg[n]) + upd[n]
    return carry


def _ret_scan(u2, dl):
    def tok_spec(d, blk):
        return pl.BlockSpec((1, RET_CHUNK, RET_W), lambda b, s: (b, _ret_chunk_index(d, s), blk))
    o_shape = jax.ShapeDtypeStruct((BATCH, TOK, RET_W), BF16)
    return pl.pallas_call(
        _ret_scan_kernel,
        out_shape=(o_shape, o_shape),
        grid=(BATCH, RET_NCHUNK),
        in_specs=[tok_spec(0, 0), tok_spec(0, 1), tok_spec(0, 2), tok_spec(1, 0), tok_spec(1, 1), tok_spec(1, 2),
                  pl.BlockSpec((2, RET_HEADS, 1, RET_HEAD), lambda b, s: (0, 0, 0, 0))],
        out_specs=(tok_spec(0, 0), tok_spec(1, 0)),
        scratch_shapes=[pltpu.VMEM((2 * RET_HEADS, RET_HEAD, RET_HEAD), F32)],
        compiler_params=_params("arbitrary", "arbitrary"),
        name="ret_scan",
    )(u2, u2, u2, u2, u2, u2, dl)


def _rope_tables():
    pos = np.arange(SEQ)
    half = RET_HEAD // 2
    inv = ROPE_BASE ** (-np.arange(0, half, 2, dtype=np.float32) / half)
    ang = np.concatenate([(pos // GRID_W)[:, None] * inv[None, :], (pos % GRID_W)[:, None] * inv[None, :]], axis=1)
    cos = np.concatenate([np.cos(ang), np.ones((CTX_LEN, half))], axis=0).astype(np.float32)
    sin = np.concatenate([np.sin(ang), np.zeros((CTX_LEN, half))], axis=0).astype(np.float32)
    return np.concatenate([cos, cos], axis=1), np.concatenate([-sin, sin], axis=1)


def _odd_finish_kernel(of_ref, ob_ref, g_ref, x_ref, w_ref, np_ref, mod_ref, out_ref):
    o = of_ref[0].astype(F32) + ob_ref[0].astype(F32)
    parts = []
    ones = jnp.ones((RET_HEAD, RET_HEAD), BF16)
    for h in range(RET_HEADS):
        oh = o[:, h * RET_HEAD:(h + 1) * RET_HEAD]
        ms = _dot((oh * oh).astype(BF16), ones, None) * np.float32(1.0 / RET_HEAD)
        parts.append(oh * lax.rsqrt(ms + EPS))
    z = (jnp.concatenate(parts, axis=-1) * _silu(g_ref[0].astype(F32))).astype(BF16)
    out = jnp.dot(z, w_ref[...], preferred_element_type=F32)
    out_ref[0] = _post_residual(x_ref[0], out, np_ref[...], mod_ref[0, :, 2 * D_MODEL:])


ODD_TILE = 512


def _odd_finish(o_f, o_b, u2, xcomb, w_out_bf16, npost, mod):
    full = lambda shape: pl.BlockSpec(shape, lambda b, j: (0,) * len(shape))
    tile = lambda w, blk: pl.BlockSpec((1, ODD_TILE, w), lambda b, j: (b, j, blk))
    return pl.pallas_call(
        _odd_finish_kernel,
        out_shape=jax.ShapeDtypeStruct((BATCH, SEQ, D_MODEL), F32),
        grid=(BATCH, SEQ // ODD_TILE),
        in_specs=[
            tile(RET_W, 0), tile(RET_W, 0), tile(RET_W, 3), tile(D_MODEL, 0),
            full((RET_W, D_MODEL)), full((1, D_MODEL)),
            pl.BlockSpec((1, 1, 3 * D_MODEL), lambda b, j: (b, 0, 0)),
        ],
        out_specs=tile(D_MODEL, 0),
        compiler_params=_params("arbitrary", "arbitrary"),
        name="odd_finish",
    )(o_f, o_b, u2, xcomb, w_out_bf16, npost.reshape(1, D_MODEL), mod.reshape(8, 1, 3 * D_MODEL))


def _head_ones():
    return np.kron(np.eye(2, dtype=np.float32), np.ones((RWKV_HEAD, RWKV_HEAD), np.float32))


def kernel(x, c, ctx, c_ctx, ada_w, ada_b, norm_pre, norm_post, ev_w_in, ev_mu, ev_w0, ev_w2, ev_a0, ev_a2, ev_k_k, ev_k_a, ev_r_k, ev_lnx_g, ev_lnx_b, ev_w_out, od_w_in, od_decay_logit, od_w_out):
    cvec = jnp.concatenate([c, c_ctx[None, :], jnp.zeros((8 - BATCH - 1, D_MODEL), F32)], axis=0)
    mod = _ada_mod(cvec, ada_w, ada_b)
    ones_bd = jnp.asarray(_head_ones(), dtype=BF16)

    gate_r, four_in, gate_f, r, v, kk, k2, b2, lw2, bonus = _proj_terms(
        x, ctx, norm_pre[0], mod[0], ev_w_in[0].astype(BF16), ev_mu[0], ev_w0[0], ev_w2[0], ev_a0[0], ev_a2[0],
        ev_k_k[0], ev_k_a[0], ev_r_k[0].reshape(RWKV_W), ones_bd)
    yf, yb = _rwkv_scan(r, v, kk, k2, b2, lw2)
    cosf, sinf = _rope_tables()
    xcomb, u2 = _even_finish(yf, yb, bonus, gate_r, gate_f, _four_lat(four_in), _four_ctx(four_in), x, ctx, ones_bd,
                             ev_lnx_g[0], ev_lnx_b[0], ev_w_out[0].astype(BF16), norm_post[0], mod[0],
                             norm_pre[1], mod[1], od_w_in[0].astype(BF16), jnp.asarray(cosf), jnp.asarray(sinf))

    dl = jnp.broadcast_to(od_decay_logit[0][:, :, None, None], (2, RET_HEADS, 1, RET_HEAD))
    o_f, o_b = _ret_scan(u2, dl)
    return _odd_finish(o_f, o_b, u2, xcomb, od_w_out[0].astype(BF16), norm_post[1], mod[1])
```

```python
import functools

import numpy as np
import jax
import jax.numpy as jnp
from jax import lax
from jax.experimental import pallas as pl
from jax.experimental.pallas import tpu as pltpu

F32 = jnp.float32
BF16 = jnp.bfloat16

D_MODEL = 1024
BATCH = 2
SEQ = 8192
GRID_W = 64
GRID_ROWS = SEQ // GRID_W
CTX_LEN = 256
TOK = SEQ + CTX_LEN
EPS = 1e-6

RWKV_HEAD = 64
RWKV_HEADS = 12
RWKV_W = RWKV_HEADS * RWKV_HEAD
LORA = 64
FOUR_GROUPS = 4
FOUR_W = 256
FOUR_GROUP_W = 64
N_SHIFT = 3 * RWKV_W + 4 * LORA
EVEN_IN = N_SHIFT + RWKV_W + 2 * FOUR_W
RWKV_GN_EPS = 64e-5

RET_HEADS = 8
RET_HEAD = 128
RET_W = 1024
ODD_IN = 4 * RET_W
ROPE_BASE = 10000.0

TILE = 256
N_TILES = TOK // TILE
CTX_TILE = N_TILES - 1
RW_CHUNK = 64
RW_CHUNKS_PER_STEP = 4
RW_NBLOCK = TOK // (RW_CHUNK * RW_CHUNKS_PER_STEP)
RW_CTX_BLOCKS = CTX_LEN // (RW_CHUNK * RW_CHUNKS_PER_STEP)
PAIR_W = 2 * RWKV_HEAD
N_PAIRS = RWKV_W // PAIR_W
RET_CHUNK = 256
RET_SUB = 128
RET_NCHUNK = TOK // RET_CHUNK
VMEM_LIMIT = 56 * 1024 * 1024


def _dot(a, b, precision=None):
    return jnp.dot(a, b, precision=precision, preferred_element_type=F32)


def _dot_nt(a, b, precision=None):
    return lax.dot_general(a, b, (((1,), (1,)), ((), ())), precision=precision, preferred_element_type=F32)


def _dot_tn(a, b, precision=None):
    return lax.dot_general(a, b, (((0,), (0,)), ((), ())), precision=precision, preferred_element_type=F32)


def _split_bf16(x):
    hi = x.astype(BF16)
    return hi, (x - hi.astype(F32)).astype(BF16)


def _head_sum(x, ones_bd):
    xb = x.astype(BF16)
    out = []
    for c in range(x.shape[1] // PAIR_W):
        out.append(_dot(xb[:, c * PAIR_W:(c + 1) * PAIR_W], ones_bd, None))
    return jnp.concatenate(out, axis=1)


def _silu(x):
    return x * (1.0 / (1.0 + jnp.exp(-x)))


def _sigmoid(x):
    return 1.0 / (1.0 + jnp.exp(-x))


def _params(*sem):
    return pltpu.CompilerParams(dimension_semantics=sem, vmem_limit_bytes=VMEM_LIMIT)


def _ada_kernel(c_ref, w_ref, b_ref, o_ref):
    o_ref[0] = _dot(_silu(c_ref[...]), w_ref[0], lax.Precision.HIGHEST) + b_ref[0]


def _ada_mod(cvec, ada_w, ada_b):
    depth = ada_w.shape[0]
    nt = 3 * D_MODEL // 1024
    return pl.pallas_call(
        _ada_kernel,
        out_shape=jax.ShapeDtypeStruct((depth, 8, 3 * D_MODEL), F32),
        grid=(depth, nt),
        in_specs=[
            pl.BlockSpec((8, D_MODEL), lambda l, n: (0, 0)),
            pl.BlockSpec((1, D_MODEL, 1024), lambda l, n: (l, 0, n)),
            pl.BlockSpec((1, 1, 1024), lambda l, n: (l, 0, n)),
        ],
        out_specs=pl.BlockSpec((1, 8, 1024), lambda l, n: (l, 0, n)),
        compiler_params=_params("arbitrary", "arbitrary"),
        name="ada_mod",
    )(cvec, ada_w, ada_b.reshape(depth, 1, 3 * D_MODEL))


def _mod_row(b, j):
    return jnp.where(j == CTX_TILE, BATCH, b)


def _tile_rows(lat_ref, ctx_ref, tile):
    return jnp.where(tile == CTX_TILE, ctx_ref[0], lat_ref[0])


def _lat_ctx_specs(width):
    lat = pl.BlockSpec((1, TILE, width), lambda b, j: (b, jnp.minimum(j, CTX_TILE - 1), 0))
    ctx = pl.BlockSpec((1, TILE, width), lambda b, j: (b, 0, 0))
    return lat, ctx


PT_LAG = 2
PROJ_BLOCK = 256


def _proj_terms_kernel(x_ref, xc_ref, g_ref, mod_ref, w_ref, mu_ref, w0_ref, w2_ref, a0_ref, a2_ref, kk_ref_, ka_ref,
                       rk_ref, ones_ref, gr_out, four_out, gf_out, r_out, v_out, kk_out, k_out, b_out, lw_out,
                       bonus_out, cur_ref, nxt_ref, halo_ref):
    step = pl.program_id(1)
    j = step - PT_LAG

    @pl.when(step == 0)
    def _():
        cur_ref[...] = jnp.zeros_like(cur_ref)
        nxt_ref[...] = jnp.zeros_like(nxt_ref)
        halo_ref[...] = jnp.zeros_like(halo_ref)

    y = _tile_rows(x_ref, xc_ref, jnp.minimum(step, N_TILES - 1))
    gain = g_ref[...] * (1.0 + mod_ref[0, :, D_MODEL:2 * D_MODEL])
    h = (y * lax.rsqrt(jnp.mean(y * y, axis=-1, keepdims=True) + EPS) * gain + mod_ref[0, :, 0:D_MODEL]).astype(BF16)

    is_ctx = j == CTX_TILE
    cur_u = cur_ref[...]
    nxt_u = nxt_ref[...]
    up_rows = halo_ref[...]
    dn_rows = nxt_u[0:GRID_W, :]
    halo_ref[...] = cur_u[TILE - GRID_W:, :]
    cur_ref[...] = nxt_u
    x = cur_u.astype(F32)

    blocks = iter(range(EVEN_IN // PROJ_BLOCK))
    targets = ((nxt_ref, 0, N_SHIFT), (gr_out, N_SHIFT, RWKV_W), (four_out, N_SHIFT + RWKV_W, FOUR_W),
               (gf_out, N_SHIFT + RWKV_W + FOUR_W, FOUR_W))

    def issue(count):
        for _ in range(count):
            lo = next(blocks) * PROJ_BLOCK
            acc = jnp.dot(h, w_ref[:, lo:lo + PROJ_BLOCK], preferred_element_type=F32)
            for ref, start, width in targets:
                if start <= lo < start + width:
                    if ref is nxt_ref:
                        ref[:, lo - start:lo - start + PROJ_BLOCK] = acc.astype(ref.dtype)
                    else:
                        ref[0, :, lo - start:lo - start + PROJ_BLOCK] = acc.astype(ref.dtype)

    left = pltpu.roll(x, 1, 0)
    right = pltpu.roll(x, TILE - 1, 0)
    row8 = lax.broadcasted_iota(jnp.int32, (8, N_SHIFT), 0)
    inner_first = jnp.where(is_ctx, 0, 1)
    inner_last = jnp.where(is_ctx, 8, 7)

    def zero_rows(a, first):
        parts, pos = [], 0
        for g in range(0, TILE, GRID_W):
            at = g if first else g + GRID_W - 8
            if first:
                keep = row8 >= (1 if g == 0 else inner_first)
            else:
                keep = row8 < (7 if g == TILE - GRID_W else inner_last)
            parts += [a[pos:at], jnp.where(keep, a[at:at + 8], 0.0)]
            pos = at + 8
        return jnp.concatenate([p for p in parts + [a[pos:]] if p.shape[0]], axis=0)

    left = zero_rows(left, True)
    right = zero_rows(right, False)
    up_ok = jnp.where(j == 0, 0.0, 1.0)
    dn_ok = jnp.where(j == CTX_TILE - 1, 0.0, 1.0)
    up = jnp.concatenate([up_rows.astype(F32) * up_ok, x[:TILE - GRID_W]], axis=0)
    down = jnp.concatenate([x[GRID_W:], dn_rows.astype(F32) * dn_ok], axis=0)
    lane = lax.broadcasted_iota(jnp.int32, (1, N_SHIFT), 1)
    sel = jnp.where(is_ctx, lane & 1, lane & 3)
    shifted = jnp.where(sel == 0, left, jnp.where(sel == 1, right, jnp.where(sel == 2, up, down)))
    s = x + (shifted - x) * mu_ref[...]
    issue(6)

    r = s[:, 0:RWKV_W]
    k0 = s[:, RWKV_W:2 * RWKV_W]
    v = s[:, 2 * RWKV_W:3 * RWKV_W]
    ones_bd = ones_ref[...]
    kk = k0 * kk_ref_[...]
    kk = kk * lax.rsqrt(jnp.maximum(_head_sum(kk * kk, ones_bd), 1e-24))
    issue(2)
    r_out[0] = r.astype(BF16)
    v_out[0] = v.astype(BF16)
    kk_out[0] = kk.astype(BF16)
    bonus = jnp.zeros_like(v)
    for d in range(2):
        wl = s[:, 3 * RWKV_W + d * LORA:3 * RWKV_W + (d + 1) * LORA]
        al = s[:, 3 * RWKV_W + (2 + d) * LORA:3 * RWKV_W + (3 + d) * LORA]
        z = w0_ref[d:d + 1, :] + _dot(jnp.tanh(wl).astype(BF16), w2_ref[d].astype(BF16), None)
        lw_out[d, 0] = -np.float32(np.exp(-0.5)) * _sigmoid(z)
        a = _sigmoid(a0_ref[d:d + 1, :] + _dot(al.astype(BF16), a2_ref[d].astype(BF16), None))
        issue(2)
        k = k0 * (1.0 + (a - 1.0) * ka_ref[...])
        k_out[d, 0] = k.astype(BF16)
        b_out[d, 0] = (kk * a).astype(BF16)
        bonus = bonus + _head_sum(r * k * rk_ref[...], ones_bd) * v
        issue(2 if d == 0 else 1)
    bonus_out[0] = bonus.astype(BF16)


def _proj_terms(x, ctx, g, mod, w_bf16, mu, w0, w2, a0, a2, k_k, k_a, r_k, ones_bd):
    tok = jax.ShapeDtypeStruct((BATCH, TOK, RWKV_W), BF16)
    tok2 = jax.ShapeDtypeStruct((2, BATCH, TOK, RWKV_W), BF16)
    tok2_f32 = jax.ShapeDtypeStruct((2, BATCH, TOK, RWKV_W), F32)
    full = lambda shape: pl.BlockSpec(shape, lambda b, j: (0,) * len(shape))
    cur = lambda j: jnp.minimum(j, N_TILES - 1)
    lag = lambda j: jnp.maximum(j - PT_LAG, 0)
    cur_spec = lambda w: pl.BlockSpec((1, TILE, w), lambda b, j: (b, cur(j), 0))
    tspec = pl.BlockSpec((1, TILE, RWKV_W), lambda b, j: (b, lag(j), 0))
    t2spec = pl.BlockSpec((2, 1, TILE, RWKV_W), lambda b, j: (0, b, lag(j), 0))
    return pl.pallas_call(
        _proj_terms_kernel,
        out_shape=(tok, jax.ShapeDtypeStruct((BATCH, TOK, FOUR_W), F32), jax.ShapeDtypeStruct((BATCH, TOK, FOUR_W), BF16),
                   tok, tok, tok, tok2, tok2, tok2_f32, tok),
        grid=(BATCH, N_TILES + PT_LAG),
        in_specs=[
            *_lat_ctx_specs(D_MODEL),
            full((1, D_MODEL)),
            pl.BlockSpec((1, 1, 3 * D_MODEL), lambda b, j: (_mod_row(b, cur(j)), 0, 0)),
            full((D_MODEL, EVEN_IN)),
            full((1, N_SHIFT)), full((2, RWKV_W)), full((2, LORA, RWKV_W)), full((2, RWKV_W)),
            full((2, LORA, RWKV_W)), full((1, RWKV_W)), full((1, RWKV_W)), full((1, RWKV_W)),
            full((PAIR_W, PAIR_W)),
        ],
        out_specs=(cur_spec(RWKV_W), cur_spec(FOUR_W), cur_spec(FOUR_W),
                   tspec, tspec, tspec, t2spec, t2spec, t2spec, tspec),
        scratch_shapes=[pltpu.VMEM((TILE, N_SHIFT), BF16), pltpu.VMEM((TILE, N_SHIFT), BF16),
                        pltpu.VMEM((GRID_W, N_SHIFT), BF16)],
        compiler_params=_params("arbitrary", "arbitrary"),
        name="proj_terms",
    )(x, ctx, g.reshape(1, D_MODEL), mod.reshape(8, 1, 3 * D_MODEL), w_bf16, mu.reshape(1, N_SHIFT), w0, w2, a0, a2,
      k_k.reshape(1, RWKV_W), k_a.reshape(1, RWKV_W), r_k.reshape(1, RWKV_W), ones_bd)


def _rw_block_index(d, s):
    lat = jnp.where(d == 0, s - RW_CTX_BLOCKS, RW_NBLOCK - 1 - s)
    ctx = RW_NBLOCK - RW_CTX_BLOCKS + jnp.where(d == 0, s, RW_CTX_BLOCKS - 1 - s)
    return jnp.where(s < RW_CTX_BLOCKS, ctx, lat)


def _rwkv_scan_kernel(rf_ref, vf_ref, kkf_ref, rb_ref, vb_ref, kkb_ref, kf_ref, bf_ref, lwf_ref, kb_ref, bb_ref,
                      lwb_ref, yf_ref, yb_ref, h_ref):
    s = pl.program_id(0)
    n = PAIR_W

    @pl.when(s == 0)
    def _():
        h_ref[...] = jnp.zeros_like(h_ref)

    c = RW_CHUNK
    ti = lax.broadcasted_iota(jnp.int32, (c, n), 0)
    li = lax.broadcasted_iota(jnp.int32, (c, n), 1)
    si = li & (c - 1)
    strict_d = (si < ti, si > ti)
    incl_d = (si <= ti, si >= ti)
    ident = jnp.where(si == ti, 1.0, 0.0)
    blk8 = (ti >> 3) == (si >> 3)
    merges = [((ti >> (sh + 1)) == (si >> (sh + 1))) & ((ti >> sh) != (si >> sh)) for sh in (3, 4, 5)]
    head0 = li < RWKV_HEAD
    cumb_d = [jnp.where(m[:, :c], 1.0, 0.0).astype(BF16) for m in incl_d]
    ri = lax.broadcasted_iota(jnp.int32, (n, n), 0)
    ci = lax.broadcasted_iota(jnp.int32, (n, n), 1)
    eye = ri == ci

    def stack(xb):
        zero = jnp.zeros_like(xb)
        return jnp.concatenate([jnp.where(head0, xb, zero), jnp.where(head0, zero, xb)], axis=0)

    def prod(x, y):
        return _dot(x.astype(BF16), stack(y.astype(BF16)), None)

    dir_refs = ((rf_ref, vf_ref, kkf_ref, kf_ref, bf_ref, lwf_ref, yf_ref),
                (rb_ref, vb_ref, kkb_ref, kb_ref, bb_ref, lwb_ref, yb_ref))
    pairs = range(2 * BATCH * N_PAIRS)
    drn = [p // (BATCH * N_PAIRS) for p in pairs]
    bat = [(p // N_PAIRS) % BATCH for p in pairs]
    col = [slice((p % N_PAIRS) * PAIR_W, (p % N_PAIRS + 1) * PAIR_W) for p in pairs]
    strict = [strict_d[drn[p]] for p in pairs]
    incl = [incl_d[drn[p]] for p in pairs]
    lax.fori_loop(0, RW_CHUNKS_PER_STEP, functools.partial(
        _rwkv_chunk, dir_refs=dir_refs, h_ref=h_ref, pairs=pairs, drn=drn, bat=bat, col=col, strict=strict, incl=incl,
        cumb_d=cumb_d, ident=ident, blk8=blk8, merges=merges, eye=eye, stack=stack, prod=prod), 0)


def _rwkv_chunk(i, carry, *, dir_refs, h_ref, pairs, drn, bat, col, strict, incl, cumb_d, ident, blk8, merges, eye,
                stack, prod):
    c, n = RW_CHUNK, PAIR_W
    rows_d = (pl.ds(pl.multiple_of(i * c, c), c), pl.ds(pl.multiple_of((RW_CHUNKS_PER_STEP - 1 - i) * c, c), c))
    rows = [rows_d[drn[p]] for p in pairs]
    lw_all, lc_all = {}, {}
    for dd in range(2):
        for bb in range(BATCH):
            lw_b = dir_refs[dd][5][0, bb, rows_d[dd], :]
            lw_hi = lw_b.astype(BF16)
            lw_mid, lw_lo = _split_bf16(lw_b - lw_hi.astype(F32))
            lw_all[dd, bb] = lw_b
            lc_all[dd, bb] = _dot(jnp.concatenate([cumb_d[dd]] * 3, axis=1),
                                  jnp.concatenate([lw_hi, lw_mid, lw_lo], axis=0), None)
    ar, kbt, vb, g_col = [], [], [], []
    for p in pairs:
        r_ref, v_ref, kk_ref, k_ref, b_ref = dir_refs[drn[p]][:5]
        lw, lc = lw_all[drn[p], bat[p]][:, col[p]], lc_all[drn[p], bat[p]][:, col[p]]
        lc_end = jnp.sum(lw, axis=0, keepdims=True)
        g_inv = jnp.exp(-lc)
        kc, bc = k_ref[0, bat[p], rows[p], col[p]], b_ref[0, bat[p], rows[p], col[p]]
        at = (-kk_ref[bat[p], rows[p], col[p]] * jnp.exp(lc - lw)).astype(BF16)
        rt = (r_ref[bat[p], rows[p], col[p]] * jnp.exp(lc)).astype(BF16)
        ar.append(jnp.concatenate([at, rt], axis=0))
        kb = jnp.concatenate([stack((kc * g_inv).astype(BF16)), stack((bc * g_inv).astype(BF16))], axis=0)
        kbt.append(kb.T)
        vb.append(v_ref[bat[p], rows[p], col[p]].astype(BF16))
        g_col.append(jnp.sum(jnp.where(eye, jnp.broadcast_to(jnp.exp(lc_end), (n, n)), 0.0), axis=1, keepdims=True))

    quad = [_dot(ar[p], kbt[p], None) for p in pairs]
    a_b = [jnp.where(strict[p], quad[p][:c, n:], 0.0) for p in pairs]
    akbk = [jnp.concatenate([jnp.where(strict[p], quad[p][:c, :n], 0.0), jnp.where(incl[p], quad[p][c:, :n], 0.0)],
                            axis=0).astype(BF16) for p in pairs]
    b_b = [jnp.where(incl[p], quad[p][c:, n:], 0.0).astype(BF16) for p in pairs]
    h0 = [h_ref[p] for p in pairs]
    arh = [_dot(ar[p], h0[p].astype(BF16), None) for p in pairs]
    abv = [_dot(jnp.concatenate([akbk[p], kbt[p][:, :n]], axis=0), stack(vb[p]), None) for p in pairs]

    n1 = [jnp.where(blk8, a_b[p], 0.0) for p in pairs]
    n2 = [prod(n1[p], n1[p]) for p in pairs]
    both = [prod(jnp.concatenate([n2[p], ident + n1[p]], axis=0), n2[p]) for p in pairs]
    n4 = [both[p][:c] for p in pairs]
    t12 = [ident + n1[p] + both[p][c:] for p in pairs]
    tinv = [t12[p] + prod(t12[p], n4[p]) for p in pairs]
    for m in merges:
        off = [jnp.where(m, a_b[p], 0.0) for p in pairs]
        half = [prod(tinv[p], off[p]) for p in pairs]
        tinv = [tinv[p] + prod(half[p], tinv[p]) for p in pairs]

    ub = [_dot(tinv[p].astype(BF16), stack((arh[p][:c] + abv[p][:c]).astype(BF16)), None).astype(BF16) for p in pairs]
    bbu = [_dot(jnp.concatenate([b_b[p], kbt[p][:, n:]], axis=0), stack(ub[p]), None) for p in pairs]
    for p in pairs:
        dir_refs[drn[p]][6][bat[p], rows[p], col[p]] = (arh[p][c:] + abv[p][c:2 * c] + bbu[p][:c]).astype(BF16)
        h_ref[p] = (h0[p] + abv[p][2 * c:] + bbu[p][c:]) * g_col[p]
    return carry


def _rwkv_scan(r, v, kk, k2, b2, lw2):
    rows = RW_CHUNK * RW_CHUNKS_PER_STEP

    def tok_spec(d):
        return pl.BlockSpec((BATCH, rows, RWKV_W), lambda s: (0, _rw_block_index(d, s), 0))

    def dir_spec(d):
        return pl.BlockSpec((1, BATCH, rows, RWKV_W), lambda s: (d, 0, _rw_block_index(d, s), 0))

    y_shape = jax.ShapeDtypeStruct((BATCH, TOK, RWKV_W), BF16)
    return pl.pallas_call(
        _rwkv_scan_kernel,
        out_shape=(y_shape, y_shape),
        grid=(RW_NBLOCK,),
        in_specs=[tok_spec(0), tok_spec(0), tok_spec(0), tok_spec(1), tok_spec(1), tok_spec(1),
                  dir_spec(0), dir_spec(0), dir_spec(0), dir_spec(1), dir_spec(1), dir_spec(1)],
        out_specs=(tok_spec(0), tok_spec(1)),
        scratch_shapes=[pltpu.VMEM((2 * BATCH * N_PAIRS, PAIR_W, PAIR_W), F32)],
        compiler_params=_params("arbitrary"),
        name="rwkv_scan",
    )(r, v, kk, r, v, kk, k2, b2, lw2, k2, b2, lw2)


def _dft_mats(n):
    i = np.arange(n)
    ang = 2.0 * np.pi * ((i[:, None] * i[None, :]) % n) / n
    return np.cos(ang).astype(np.float32), np.sin(ang).astype(np.float32)


def _chan_dft_mat():
    c, s = _dft_mats(FOUR_GROUP_W)
    eye = np.eye(FOUR_GROUPS, dtype=np.float32)
    return np.concatenate([np.kron(eye, c), np.kron(eye, s)], axis=1)


def _bdot(a, b):
    return jnp.dot(a.astype(BF16), b.astype(BF16), preferred_element_type=F32)


def _four_ctx_kernel(u_ref, cs_ref, c_ref, s_ref, o_ref):
    pq = _bdot(u_ref[0], cs_ref[...])
    scale = np.float32(1.0 / np.sqrt(CTX_LEN * FOUR_GROUP_W))
    o_ref[0] = (_bdot(c_ref[...], pq[:, :FOUR_W]) - _bdot(s_ref[...], pq[:, FOUR_W:])) * scale


def _four_ctx(u):
    c, s = _dft_mats(CTX_LEN)
    full = lambda shape: pl.BlockSpec(shape, lambda b: (0,) * len(shape))
    return pl.pallas_call(
        _four_ctx_kernel,
        out_shape=jax.ShapeDtypeStruct((BATCH, CTX_LEN, FOUR_W), F32),
        grid=(BATCH,),
        in_specs=[pl.BlockSpec((1, CTX_LEN, FOUR_W), lambda b: (b, CTX_TILE, 0)),
                  full((FOUR_W, 2 * FOUR_W)), full((CTX_LEN, CTX_LEN)), full((CTX_LEN, CTX_LEN))],
        out_specs=pl.BlockSpec((1, CTX_LEN, FOUR_W), lambda b: (b, 0, 0)),
        compiler_params=_params("arbitrary"),
        name="fourier_ctx",
    )(u, jnp.asarray(_chan_dft_mat()), jnp.asarray(c), jnp.asarray(s))


L2_PER_STEP = 16


def _four_lat_a_kernel(u_ref, cs_ref, c_ref, s_ref, tc_ref, ts_ref, zr_ref, zi_ref):
    cs = cs_ref[...]
    cm = c_ref[...]
    sm = s_ref[...]
    for i in range(L2_PER_STEP):
        x = u_ref[0, :, i, :]
        pq = _bdot(x, cs)
        cpq = _bdot(cm, pq)
        spq = _bdot(sm, pq)
        zr = cpq[:, :FOUR_W] - spq[:, FOUR_W:]
        zi = -cpq[:, FOUR_W:] - spq[:, :FOUR_W]
        tc, ts = tc_ref[i], ts_ref[i]
        zr_ref[0, i] = zr * tc + zi * ts
        zi_ref[0, i] = zi * tc - zr * ts


K1_PER_STEP = 16


def _four_lat_b_kernel(zr_ref, zi_ref, cs_ref, o_ref):
    scale = np.float32(1.0 / np.sqrt(SEQ * FOUR_GROUP_W))
    cs = cs_ref[...]
    for i in range(K1_PER_STEP):
        z = jnp.concatenate([zr_ref[0, :, i, :], zi_ref[0, :, i, :]], axis=0)
        o_ref[0, :, i, :] = _bdot(cs, z) * scale


def _four_lat(u):
    c1, s1 = _dft_mats(GRID_ROWS)
    c2, s2 = _dft_mats(GRID_W)
    k1 = np.arange(GRID_ROWS)[None, :, None]
    l2 = np.arange(GRID_W)[:, None, None]
    ang = 2.0 * np.pi * (k1 * l2) / SEQ
    tc = np.broadcast_to(np.cos(ang), (GRID_W, GRID_ROWS, FOUR_W)).astype(np.float32)
    ts = np.broadcast_to(np.sin(ang), (GRID_W, GRID_ROWS, FOUR_W)).astype(np.float32)
    u4 = u.reshape(BATCH, TOK // GRID_W, GRID_W, FOUR_W)
    full = lambda shape: pl.BlockSpec(shape, lambda b, g: (0,) * len(shape))
    zshape = jax.ShapeDtypeStruct((BATCH, GRID_W, GRID_ROWS, FOUR_W), F32)
    zspec = pl.BlockSpec((1, L2_PER_STEP, GRID_ROWS, FOUR_W), lambda b, g: (b, g, 0, 0))
    tspec = pl.BlockSpec((L2_PER_STEP, GRID_ROWS, FOUR_W), lambda b, g: (g, 0, 0))
    zr, zi = pl.pallas_call(
        _four_lat_a_kernel,
        out_shape=(zshape, zshape),
        grid=(BATCH, GRID_W // L2_PER_STEP),
        in_specs=[pl.BlockSpec((1, GRID_ROWS, L2_PER_STEP, FOUR_W), lambda b, g: (b, 0, g, 0)),
                  full((FOUR_W, 2 * FOUR_W)), full((GRID_ROWS, GRID_ROWS)), full((GRID_ROWS, GRID_ROWS)),
                  tspec, tspec],
        out_specs=(zspec, zspec),
        compiler_params=_params("arbitrary", "arbitrary"),
        name="fourier_lat_a",
    )(u4, jnp.asarray(_chan_dft_mat()), jnp.asarray(c1), jnp.asarray(s1), jnp.asarray(tc), jnp.asarray(ts))
    zb = pl.BlockSpec((1, GRID_W, K1_PER_STEP, FOUR_W), lambda b, g: (b, 0, g, 0))
    out = pl.pallas_call(
        _four_lat_b_kernel,
        out_shape=jax.ShapeDtypeStruct((BATCH, GRID_W, GRID_ROWS, FOUR_W), F32),
        grid=(BATCH, GRID_ROWS // K1_PER_STEP),
        in_specs=[zb, zb, full((GRID_W, 2 * GRID_W))],
        out_specs=zb,
        compiler_params=_params("arbitrary", "arbitrary"),
        name="fourier_lat_b",
    )(zr, zi, jnp.asarray(np.concatenate([c2, s2], axis=1)))
    return out.reshape(BATCH, SEQ, FOUR_W)


def _post_residual(x, out, npost, gate):
    y = out * lax.rsqrt(jnp.mean(out * out, axis=-1, keepdims=True) + EPS) * npost
    return x + gate * y


def _retention_project(x, g_ref, mod_ref, w_ref, cos_ref, sin_ref, u_ref):
    gain = g_ref[...] * (1.0 + mod_ref[0, :, D_MODEL:2 * D_MODEL])
    h = (x * lax.rsqrt(jnp.mean(x * x, axis=-1, keepdims=True) + EPS) * gain + mod_ref[0, :, 0:D_MODEL]).astype(BF16)
    cosf, sinf = cos_ref[...], sin_ref[...]
    blocks = iter(range(ODD_IN // PROJ_BLOCK))

    def issue(count):
        for _ in range(count):
            lo = next(blocks) * PROJ_BLOCK
            acc = jnp.dot(h, w_ref[:, lo:lo + PROJ_BLOCK], preferred_element_type=F32)
            if lo >= 2 * RET_W:
                u_ref[0, :, lo:lo + PROJ_BLOCK] = acc.astype(u_ref.dtype)
                continue
            for i in range(PROJ_BLOCK // RET_HEAD):
                t = acc[:, i * RET_HEAD:(i + 1) * RET_HEAD]
                t = t * cosf + pltpu.roll(t, RET_HEAD // 2, 1) * sinf
                if lo >= RET_W:
                    t = t * np.float32(RET_HEAD ** -0.5)
                u_ref[0, :, lo + i * RET_HEAD:lo + (i + 1) * RET_HEAD] = t.astype(u_ref.dtype)

    return issue


def _even_finish_kernel(yf_ref, yb_ref, bonus_ref, gr_ref, gf_ref, four_ref, fourc_ref, x_ref, xc_ref, ones_ref, g_ref,
                        bta_ref, w_ref, np_ref, mod_ref, g1_ref, mod1_ref, w1_ref, cos_ref, sin_ref, o_ref, u1_ref,
                        x1_ref):
    j = pl.program_id(1)
    tile = jnp.minimum(j, N_TILES - 1)

    @pl.when(j == 0)
    def _():
        x1_ref[...] = jnp.zeros_like(x1_ref)

    issue = _retention_project(x1_ref[...], g1_ref, mod1_ref, w1_ref, cos_ref, sin_ref, u1_ref)
    issue(3)
    o = yf_ref[0].astype(F32) + yb_ref[0].astype(F32) + bonus_ref[0].astype(F32)
    ones_bd = ones_ref[...]
    inv = np.float32(1.0 / RWKV_HEAD)
    mean = _head_sum(o, ones_bd) * inv
    issue(3)
    cen = o - mean
    var = _head_sum(cen * cen, ones_bd) * inv
    issue(3)
    on = cen * lax.rsqrt(var + RWKV_GN_EPS) * g_ref[...] + bta_ref[...]
    yr = on * _silu(gr_ref[0].astype(F32))
    yf = _tile_rows(four_ref, fourc_ref, tile) * _silu(gf_ref[0].astype(F32))
    z = jnp.concatenate([yr, yf], axis=-1).astype(BF16)
    out = jnp.dot(z, w_ref[...], preferred_element_type=F32)
    issue(ODD_IN // PROJ_BLOCK - 9)
    x1 = _post_residual(_tile_rows(x_ref, xc_ref, tile), out, np_ref[...], mod_ref[0, :, 2 * D_MODEL:])
    o_ref[0] = x1
    x1_ref[...] = x1


def _even_finish(yf, yb, bonus, gate_r, gate_f, four_lat, four_ctx, x_lat, x_ctx, ones_bd, lnx_g, lnx_b, w_out_bf16,
                 npost, mod, next_g, next_mod, next_w_bf16, cosf, sinf):
    full = lambda shape: pl.BlockSpec(shape, lambda b, j: (0,) * len(shape))
    n_next = next_w_bf16.shape[1]
    cur = lambda j: jnp.minimum(j, N_TILES - 1)
    prev = lambda j: jnp.maximum(j - 1, 0)
    cur_spec = lambda w: pl.BlockSpec((1, TILE, w), lambda b, j: (b, cur(j), 0))
    mod_spec = lambda t: pl.BlockSpec((1, 1, 3 * D_MODEL), lambda b, j: (_mod_row(b, t(j)), 0, 0))
    return pl.pallas_call(
        _even_finish_kernel,
        out_shape=(jax.ShapeDtypeStruct((BATCH, TOK, D_MODEL), F32), jax.ShapeDtypeStruct((BATCH, TOK, n_next), BF16)),
        grid=(BATCH, N_TILES + 1),
        in_specs=[
            cur_spec(RWKV_W), cur_spec(RWKV_W), cur_spec(RWKV_W), cur_spec(RWKV_W), cur_spec(FOUR_W),
            *_lat_ctx_specs(FOUR_W),
            *_lat_ctx_specs(D_MODEL),
            full((PAIR_W, PAIR_W)), full((1, RWKV_W)), full((1, RWKV_W)), full((D_MODEL, D_MODEL)),
            full((1, D_MODEL)), mod_spec(cur),
            full((1, D_MODEL)), mod_spec(prev), full((D_MODEL, n_next)),
            pl.BlockSpec((TILE, RET_HEAD), lambda b, j: (prev(j), 0)),
            pl.BlockSpec((TILE, RET_HEAD), lambda b, j: (prev(j), 0)),
        ],
        out_specs=(cur_spec(D_MODEL), pl.BlockSpec((1, TILE, n_next), lambda b, j: (b, prev(j), 0))),
        scratch_shapes=[pltpu.VMEM((TILE, D_MODEL), F32)],
        compiler_params=_params("arbitrary", "arbitrary"),
        name="even_finish",
    )(yf, yb, bonus, gate_r, gate_f, four_lat, four_ctx, x_lat, x_ctx, ones_bd, lnx_g.reshape(1, RWKV_W),
      lnx_b.reshape(1, RWKV_W), w_out_bf16, npost.reshape(1, D_MODEL), mod.reshape(8, 1, 3 * D_MODEL),
      next_g.reshape(1, D_MODEL), next_mod.reshape(8, 1, 3 * D_MODEL), next_w_bf16, cosf, sinf)


def _ret_chunk_index(d, s):
    lat = jnp.where(d == 0, s - 1, RET_NCHUNK - 1 - s)
    return jnp.where(s == 0, RET_NCHUNK - 1, lat)


def _ret_scan_kernel(qf_ref, kf_ref, vf_ref, qb_ref, kb_ref, vb_ref, dl_ref, of_ref, ob_ref, st_ref):
    s = pl.program_id(1)
    c = RET_SUB

    @pl.when(s == 0)
    def _():
        st_ref[...] = jnp.zeros_like(st_ref)

    ri = lax.broadcasted_iota(jnp.int32, (c, c), 0)
    ci = lax.broadcasted_iota(jnp.int32, (c, c), 1)
    ti = lax.broadcasted_iota(jnp.int32, (c, RET_HEAD), 0)
    masks = (ri >= ci, ci > ri)
    dists = (jnp.where(masks[0], ri - ci, 0).astype(F32), jnp.where(masks[1], ci - ri, 0).astype(F32))
    poss = (ti.astype(F32), (c - 1 - ti).astype(F32))
    dir_refs = ((qf_ref, kf_ref, vf_ref, of_ref), (qb_ref, kb_ref, vb_ref, ob_ref))
    streams = range(2 * RET_HEADS)
    drn = [n // RET_HEADS for n in streams]
    lg = []
    for n in streams:
        x = dl_ref[drn[n], n % RET_HEADS]
        lg.append((jnp.minimum(x, 0.0) - jnp.log(1.0 + jnp.exp(-jnp.abs(x)))) * np.float32(np.log2(np.e)))
    lax.fori_loop(0, RET_CHUNK // RET_SUB, functools.partial(
        _ret_sub_chunk, dir_refs=dir_refs, st_ref=st_ref, streams=streams, drn=drn, lg=lg, masks=masks, dists=dists,
        poss=poss), 0)


def _ret_sub_chunk(i, carry, *, dir_refs, st_ref, streams, drn, lg, masks, dists, poss):
    c = RET_SUB
    last = RET_CHUNK // RET_SUB - 1
    rows_d = (pl.ds(pl.multiple_of(i * c, c), c), pl.ds(pl.multiple_of((last - i) * c, c), c))
    rows = [rows_d[drn[n]] for n in streams]
    col = [slice((n % RET_HEADS) * RET_HEAD, (n % RET_HEADS + 1) * RET_HEAD) for n in streams]
    q = [dir_refs[drn[n]][0][0, rows[n], col[n]] for n in streams]
    k = [dir_refs[drn[n]][1][0, rows[n], col[n]] for n in streams]
    v = [dir_refs[drn[n]][2][0, rows[n], col[n]] for n in streams]
    qk = [_dot_nt(q[n], k[n], None) for n in streams]
    st = [st_ref[n] for n in streams]
    cross = [_dot(q[n], st[n].astype(BF16), None) for n in streams]
    sc = [(qk[n] * jnp.where(masks[drn[n]], jnp.exp2(dists[drn[n]] * jnp.broadcast_to(lg[n][:, :1], (c, c))),
                             0.0)).astype(BF16) for n in streams]
    inner = [_dot(sc[n], v[n], None) for n in streams]
    upd = [_dot_tn((k[n].astype(F32) * jnp.exp2((c - 1.0 - poss[drn[n]]) * lg[n])).astype(BF16), v[n], None)
           for n in streams]
    for n in streams:
        o_ref = dir_refs[drn[n]][3]
        o_ref[0, rows[n], col[n]] = (inner[n] + cross[n] * jnp.exp2((poss[drn[n]] + 1.0) * lg[n])).astype(o_ref.dtype)
        st_ref[n] = st[n] * jnp.exp2(np.float32(c) * lg[n]) + upd[n]
    return carry


def _ret_scan(u2, dl):
    def tok_spec(d, blk):
        return pl.BlockSpec((1, RET_CHUNK, RET_W), lambda b, s: (b, _ret_chunk_index(d, s), blk))
    o_shape = jax.ShapeDtypeStruct((BATCH, TOK, RET_W), BF16)
    return pl.pallas_call(
        _ret_scan_kernel,
        out_shape=(o_shape, o_shape),
        grid=(BATCH, RET_NCHUNK),
        in_specs=[tok_spec(0, 0), tok_spec(0, 1), tok_spec(0, 2), tok_spec(1, 0), tok_spec(1, 1), tok_spec(1, 2),
                  pl.BlockSpec((2, RET_HEADS, 1, RET_HEAD), lambda b, s: (0, 0, 0, 0))],
        out_specs=(tok_spec(0, 0), tok_spec(1, 0)),
        scratch_shapes=[pltpu.VMEM((2 * RET_HEADS, RET_HEAD, RET_HEAD), F32)],
        compiler_params=_params("arbitrary", "arbitrary"),
        name="ret_scan",
    )(u2, u2, u2, u2, u2, u2, dl)


def _rope_tables():
    pos = np.arange(SEQ)
    half = RET_HEAD // 2
    inv = ROPE_BASE ** (-np.arange(0, half, 2, dtype=np.float32) / half)
    ang = np.concatenate([(pos // GRID_W)[:, None] * inv[None, :], (pos % GRID_W)[:, None] * inv[None, :]], axis=1)
    cos = np.concatenate([np.cos(ang), np.ones((CTX_LEN, half))], axis=0).astype(np.float32)
    sin = np.concatenate([np.sin(ang), np.zeros((CTX_LEN, half))], axis=0).astype(np.float32)
    return np.concatenate([cos, cos], axis=1), np.concatenate([-sin, sin], axis=1)


def _odd_finish_kernel(of_ref, ob_ref, g_ref, x_ref, w_ref, np_ref, mod_ref, out_ref):
    o = of_ref[0].astype(F32) + ob_ref[0].astype(F32)
    parts = []
    ones = jnp.ones((RET_HEAD, RET_HEAD), BF16)
    for h in range(RET_HEADS):
        oh = o[:, h * RET_HEAD:(h + 1) * RET_HEAD]
        ms = _dot((oh * oh).astype(BF16), ones, None) * np.float32(1.0 / RET_HEAD)
        parts.append(oh * lax.rsqrt(ms + EPS))
    z = (jnp.concatenate(parts, axis=-1) * _silu(g_ref[0].astype(F32))).astype(BF16)
    out = jnp.dot(z, w_ref[...], preferred_element_type=F32)
    out_ref[0] = _post_residual(x_ref[0], out, np_ref[...], mod_ref[0, :, 2 * D_MODEL:])


ODD_TILE = 1024


def _odd_finish(o_f, o_b, u2, xcomb, w_out_bf16, npost, mod):
    full = lambda shape: pl.BlockSpec(shape, lambda b, j: (0,) * len(shape))
    tile = lambda w, blk: pl.BlockSpec((1, ODD_TILE, w), lambda b, j: (b, j, blk))
    return pl.pallas_call(
        _odd_finish_kernel,
        out_shape=jax.ShapeDtypeStruct((BATCH, SEQ, D_MODEL), F32),
        grid=(BATCH, SEQ // ODD_TILE),
        in_specs=[
            tile(RET_W, 0), tile(RET_W, 0), tile(RET_W, 3), tile(D_MODEL, 0),
            full((RET_W, D_MODEL)), full((1, D_MODEL)),
            pl.BlockSpec((1, 1, 3 * D_MODEL), lambda b, j: (b, 0, 0)),
        ],
        out_specs=tile(D_MODEL, 0),
        compiler_params=_params("arbitrary", "arbitrary"),
        name="odd_finish",
    )(o_f, o_b, u2, xcomb, w_out_bf16, npost.reshape(1, D_MODEL), mod.reshape(8, 1, 3 * D_MODEL))


def _head_ones():
    return np.kron(np.eye(2, dtype=np.float32), np.ones((RWKV_HEAD, RWKV_HEAD), np.float32))


def kernel(x, c, ctx, c_ctx, ada_w, ada_b, norm_pre, norm_post, ev_w_in, ev_mu, ev_w0, ev_w2, ev_a0, ev_a2, ev_k_k, ev_k_a, ev_r_k, ev_lnx_g, ev_lnx_b, ev_w_out, od_w_in, od_decay_logit, od_w_out):
    cvec = jnp.concatenate([c, c_ctx[None, :], jnp.zeros((8 - BATCH - 1, D_MODEL), F32)], axis=0)
    mod = _ada_mod(cvec, ada_w, ada_b)
    ones_bd = jnp.asarray(_head_ones(), dtype=BF16)

    gate_r, four_in, gate_f, r, v, kk, k2, b2, lw2, bonus = _proj_terms(
        x, ctx, norm_pre[0], mod[0], ev_w_in[0].astype(BF16), ev_mu[0], ev_w0[0], ev_w2[0], ev_a0[0], ev_a2[0],
        ev_k_k[0], ev_k_a[0], ev_r_k[0].reshape(RWKV_W), ones_bd)
    yf, yb = _rwkv_scan(r, v, kk, k2, b2, lw2)
    cosf, sinf = _rope_tables()
    xcomb, u2 = _even_finish(yf, yb, bonus, gate_r, gate_f, _four_lat(four_in), _four_ctx(four_in), x, ctx, ones_bd,
                             ev_lnx_g[0], ev_lnx_b[0], ev_w_out[0].astype(BF16), norm_post[0], mod[0],
                             norm_pre[1], mod[1], od_w_in[0].astype(BF16), jnp.asarray(cosf), jnp.asarray(sinf))

    dl = jnp.broadcast_to(od_decay_logit[0][:, :, None, None], (2, RET_HEADS, 1, RET_HEAD))
    o_f, o_b = _ret_scan(u2, dl)
    return _odd_finish(o_f, o_b, u2, xcomb, od_w_out[0].astype(BF16), norm_post[1], mod[1])
```

```python
import functools

import numpy as np
import jax
import jax.numpy as jnp
from jax import lax
from jax.experimental import pallas as pl
from jax.experimental.pallas import tpu as pltpu

F32 = jnp.float32
BF16 = jnp.bfloat16

D_MODEL = 1024
BATCH = 2
SEQ = 8192
GRID_W = 64
GRID_ROWS = SEQ // GRID_W
CTX_LEN = 256
TOK = SEQ + CTX_LEN
EPS = 1e-6

RWKV_HEAD = 64
RWKV_HEADS = 12
RWKV_W = RWKV_HEADS * RWKV_HEAD
LORA = 64
FOUR_GROUPS = 4
FOUR_W = 256
FOUR_GROUP_W = 64
N_SHIFT = 3 * RWKV_W + 4 * LORA
EVEN_IN = N_SHIFT + RWKV_W + 2 * FOUR_W
RWKV_GN_EPS = 64e-5

RET_HEADS = 8
RET_HEAD = 128
RET_W = 1024
ODD_IN = 4 * RET_W
ROPE_BASE = 10000.0

TILE = 256
N_TILES = TOK // TILE
CTX_TILE = N_TILES - 1
RW_CHUNK = 64
RW_CHUNKS_PER_STEP = 4
RW_NBLOCK = TOK // (RW_CHUNK * RW_CHUNKS_PER_STEP)
RW_CTX_BLOCKS = CTX_LEN // (RW_CHUNK * RW_CHUNKS_PER_STEP)
PAIR_W = 2 * RWKV_HEAD
N_PAIRS = RWKV_W // PAIR_W
RET_CHUNK = 256
RET_SUB = 128
RET_NCHUNK = TOK // RET_CHUNK
VMEM_LIMIT = 56 * 1024 * 1024


def _dot(a, b, precision=None):
    return jnp.dot(a, b, precision=precision, preferred_element_type=F32)


def _dot_nt(a, b, precision=None):
    return lax.dot_general(a, b, (((1,), (1,)), ((), ())), precision=precision, preferred_element_type=F32)


def _dot_tn(a, b, precision=None):
    return lax.dot_general(a, b, (((0,), (0,)), ((), ())), precision=precision, preferred_element_type=F32)


def _split_bf16(x):
    hi = x.astype(BF16)
    return hi, (x - hi.astype(F32)).astype(BF16)


def _head_sum(x, ones_bd):
    xb = x.astype(BF16)
    out = []
    for c in range(x.shape[1] // PAIR_W):
        out.append(_dot(xb[:, c * PAIR_W:(c + 1) * PAIR_W], ones_bd, None))
    return jnp.concatenate(out, axis=1)


def _silu(x):
    return x * (1.0 / (1.0 + jnp.exp(-x)))


def _sigmoid(x):
    return 1.0 / (1.0 + jnp.exp(-x))


def _params(*sem):
    return pltpu.CompilerParams(dimension_semantics=sem, vmem_limit_bytes=VMEM_LIMIT)


def _ada_kernel(c_ref, w_ref, b_ref, o_ref):
    o_ref[0] = _dot(_silu(c_ref[...]), w_ref[0], lax.Precision.HIGHEST) + b_ref[0]


def _ada_mod(cvec, ada_w, ada_b):
    depth = ada_w.shape[0]
    nt = 3 * D_MODEL // 1024
    return pl.pallas_call(
        _ada_kernel,
        out_shape=jax.ShapeDtypeStruct((depth, 8, 3 * D_MODEL), F32),
        grid=(depth, nt),
        in_specs=[
            pl.BlockSpec((8, D_MODEL), lambda l, n: (0, 0)),
            pl.BlockSpec((1, D_MODEL, 1024), lambda l, n: (l, 0, n)),
            pl.BlockSpec((1, 1, 1024), lambda l, n: (l, 0, n)),
        ],
        out_specs=pl.BlockSpec((1, 8, 1024), lambda l, n: (l, 0, n)),
        compiler_params=_params("arbitrary", "arbitrary"),
        name="ada_mod",
    )(cvec, ada_w, ada_b.reshape(depth, 1, 3 * D_MODEL))


def _mod_row(b, j):
    return jnp.where(j == CTX_TILE, BATCH, b)


def _tile_rows(lat_ref, ctx_ref, tile):
    return jnp.where(tile == CTX_TILE, ctx_ref[0], lat_ref[0])


def _lat_ctx_specs(width):
    lat = pl.BlockSpec((1, TILE, width), lambda b, j: (b, jnp.minimum(j, CTX_TILE - 1), 0))
    ctx = pl.BlockSpec((1, TILE, width), lambda b, j: (b, 0, 0))
    return lat, ctx


PT_LAG = 2
PROJ_BLOCK = 256


def _proj_terms_kernel(x_ref, xc_ref, g_ref, mod_ref, w_ref, mu_ref, w0_ref, w2_ref, a0_ref, a2_ref, kk_ref_, ka_ref,
                       rk_ref, ones_ref, gr_out, four_out, gf_out, r_out, v_out, kk_out, k_out, b_out, lw_out,
                       bonus_out, cur_ref, nxt_ref, halo_ref):
    step = pl.program_id(1)
    j = step - PT_LAG

    @pl.when(step == 0)
    def _():
        cur_ref[...] = jnp.zeros_like(cur_ref)
        nxt_ref[...] = jnp.zeros_like(nxt_ref)
        halo_ref[...] = jnp.zeros_like(halo_ref)

    y = _tile_rows(x_ref, xc_ref, jnp.minimum(step, N_TILES - 1))
    gain = g_ref[...] * (1.0 + mod_ref[0, :, D_MODEL:2 * D_MODEL])
    h = (y * lax.rsqrt(jnp.mean(y * y, axis=-1, keepdims=True) + EPS) * gain + mod_ref[0, :, 0:D_MODEL]).astype(BF16)

    is_ctx = j == CTX_TILE
    cur_u = cur_ref[...]
    nxt_u = nxt_ref[...]
    up_rows = halo_ref[...]
    dn_rows = nxt_u[0:GRID_W, :]
    halo_ref[...] = cur_u[TILE - GRID_W:, :]
    cur_ref[...] = nxt_u
    x = cur_u.astype(F32)

    blocks = iter(range(EVEN_IN // PROJ_BLOCK))
    targets = ((nxt_ref, 0, N_SHIFT), (gr_out, N_SHIFT, RWKV_W), (four_out, N_SHIFT + RWKV_W, FOUR_W),
               (gf_out, N_SHIFT + RWKV_W + FOUR_W, FOUR_W))

    def issue(count):
        for _ in range(count):
            lo = next(blocks) * PROJ_BLOCK
            acc = jnp.dot(h, w_ref[:, lo:lo + PROJ_BLOCK], preferred_element_type=F32)
            for ref, start, width in targets:
                if start <= lo < start + width:
                    if ref is nxt_ref:
                        ref[:, lo - start:lo - start + PROJ_BLOCK] = acc.astype(ref.dtype)
                    else:
                        ref[0, :, lo - start:lo - start + PROJ_BLOCK] = acc.astype(ref.dtype)

    left = pltpu.roll(x, 1, 0)
    right = pltpu.roll(x, TILE - 1, 0)
    row8 = lax.broadcasted_iota(jnp.int32, (8, N_SHIFT), 0)
    inner_first = jnp.where(is_ctx, 0, 1)
    inner_last = jnp.where(is_ctx, 8, 7)

    def zero_rows(a, first):
        parts, pos = [], 0
        for g in range(0, TILE, GRID_W):
            at = g if first else g + GRID_W - 8
            if first:
                keep = row8 >= (1 if g == 0 else inner_first)
            else:
                keep = row8 < (7 if g == TILE - GRID_W else inner_last)
            parts += [a[pos:at], jnp.where(keep, a[at:at + 8], 0.0)]
            pos = at + 8
        return jnp.concatenate([p for p in parts + [a[pos:]] if p.shape[0]], axis=0)

    left = zero_rows(left, True)
    right = zero_rows(right, False)
    up_ok = jnp.where(j == 0, 0.0, 1.0)
    dn_ok = jnp.where(j == CTX_TILE - 1, 0.0, 1.0)
    up = jnp.concatenate([up_rows.astype(F32) * up_ok, x[:TILE - GRID_W]], axis=0)
    down = jnp.concatenate([x[GRID_W:], dn_rows.astype(F32) * dn_ok], axis=0)
    lane = lax.broadcasted_iota(jnp.int32, (1, N_SHIFT), 1)
    sel = jnp.where(is_ctx, lane & 1, lane & 3)
    shifted = jnp.where(sel == 0, left, jnp.where(sel == 1, right, jnp.where(sel == 2, up, down)))
    s = x + (shifted - x) * mu_ref[...]
    issue(6)

    r = s[:, 0:RWKV_W]
    k0 = s[:, RWKV_W:2 * RWKV_W]
    v = s[:, 2 * RWKV_W:3 * RWKV_W]
    ones_bd = ones_ref[...]
    kk = k0 * kk_ref_[...]
    kk = kk * lax.rsqrt(jnp.maximum(_head_sum(kk * kk, ones_bd), 1e-24))
    issue(2)
    r_out[0] = r.astype(BF16)
    v_out[0] = v.astype(BF16)
    kk_out[0] = kk.astype(BF16)
    bonus = jnp.zeros_like(v)
    for d in range(2):
        wl = s[:, 3 * RWKV_W + d * LORA:3 * RWKV_W + (d + 1) * LORA]
        al = s[:, 3 * RWKV_W + (2 + d) * LORA:3 * RWKV_W + (3 + d) * LORA]
        z = w0_ref[d:d + 1, :] + _dot(jnp.tanh(wl).astype(BF16), w2_ref[d].astype(BF16), None)
        lw_out[d, 0] = -np.float32(np.exp(-0.5)) * _sigmoid(z)
        a = _sigmoid(a0_ref[d:d + 1, :] + _dot(al.astype(BF16), a2_ref[d].astype(BF16), None))
        issue(2)
        k = k0 * (1.0 + (a - 1.0) * ka_ref[...])
        k_out[d, 0] = k.astype(BF16)
        b_out[d, 0] = (kk * a).astype(BF16)
        bonus = bonus + _head_sum(r * k * rk_ref[...], ones_bd) * v
        issue(2 if d == 0 else 1)
    bonus_out[0] = bonus.astype(BF16)


def _proj_terms(x, ctx, g, mod, w_bf16, mu, w0, w2, a0, a2, k_k, k_a, r_k, ones_bd):
    tok = jax.ShapeDtypeStruct((BATCH, TOK, RWKV_W), BF16)
    tok2 = jax.ShapeDtypeStruct((2, BATCH, TOK, RWKV_W), BF16)
    tok2_f32 = jax.ShapeDtypeStruct((2, BATCH, TOK, RWKV_W), F32)
    full = lambda shape: pl.BlockSpec(shape, lambda b, j: (0,) * len(shape))
    cur = lambda j: jnp.minimum(j, N_TILES - 1)
    lag = lambda j: jnp.maximum(j - PT_LAG, 0)
    cur_spec = lambda w: pl.BlockSpec((1, TILE, w), lambda b, j: (b, cur(j), 0))
    tspec = pl.BlockSpec((1, TILE, RWKV_W), lambda b, j: (b, lag(j), 0))
    t2spec = pl.BlockSpec((2, 1, TILE, RWKV_W), lambda b, j: (0, b, lag(j), 0))
    return pl.pallas_call(
        _proj_terms_kernel,
        out_shape=(tok, jax.ShapeDtypeStruct((BATCH, TOK, FOUR_W), F32), jax.ShapeDtypeStruct((BATCH, TOK, FOUR_W), BF16),
                   tok, tok, tok, tok2, tok2, tok2_f32, tok),
        grid=(BATCH, N_TILES + PT_LAG),
        in_specs=[
            *_lat_ctx_specs(D_MODEL),
            full((1, D_MODEL)),
            pl.BlockSpec((1, 1, 3 * D_MODEL), lambda b, j: (_mod_row(b, cur(j)), 0, 0)),
            full((D_MODEL, EVEN_IN)),
            full((1, N_SHIFT)), full((2, RWKV_W)), full((2, LORA, RWKV_W)), full((2, RWKV_W)),
            full((2, LORA, RWKV_W)), full((1, RWKV_W)), full((1, RWKV_W)), full((1, RWKV_W)),
            full((PAIR_W, PAIR_W)),
        ],
        out_specs=(cur_spec(RWKV_W), cur_spec(FOUR_W), cur_spec(FOUR_W),
                   tspec, tspec, tspec, t2spec, t2spec, t2spec, tspec),
        scratch_shapes=[pltpu.VMEM((TILE, N_SHIFT), BF16), pltpu.VMEM((TILE, N_SHIFT), BF16),
                        pltpu.VMEM((GRID_W, N_SHIFT), BF16)],
        compiler_params=_params("arbitrary", "arbitrary"),
        name="proj_terms",
    )(x, ctx, g.reshape(1, D_MODEL), mod.reshape(8, 1, 3 * D_MODEL), w_bf16, mu.reshape(1, N_SHIFT), w0, w2, a0, a2,
      k_k.reshape(1, RWKV_W), k_a.reshape(1, RWKV_W), r_k.reshape(1, RWKV_W), ones_bd)


def _rw_block_index(d, s):
    lat = jnp.where(d == 0, s - RW_CTX_BLOCKS, RW_NBLOCK - 1 - s)
    ctx = RW_NBLOCK - RW_CTX_BLOCKS + jnp.where(d == 0, s, RW_CTX_BLOCKS - 1 - s)
    return jnp.where(s < RW_CTX_BLOCKS, ctx, lat)


def _rwkv_scan_kernel(rf_ref, vf_ref, kkf_ref, rb_ref, vb_ref, kkb_ref, kf_ref, bf_ref, lwf_ref, kb_ref, bb_ref,
                      lwb_ref, yf_ref, yb_ref, h_ref):
    s = pl.program_id(0)
    n = PAIR_W

    @pl.when(s == 0)
    def _():
        h_ref[...] = jnp.zeros_like(h_ref)

    c = RW_CHUNK
    ti = lax.broadcasted_iota(jnp.int32, (c, n), 0)
    li = lax.broadcasted_iota(jnp.int32, (c, n), 1)
    si = li & (c - 1)
    strict_d = (si < ti, si > ti)
    incl_d = (si <= ti, si >= ti)
    ident = jnp.where(si == ti, 1.0, 0.0)
    blk8 = (ti >> 3) == (si >> 3)
    merges = [((ti >> (sh + 1)) == (si >> (sh + 1))) & ((ti >> sh) != (si >> sh)) for sh in (3, 4, 5)]
    head0 = li < RWKV_HEAD
    cumb_d = [jnp.where(m[:, :c], 1.0, 0.0).astype(BF16) for m in incl_d]
    ri = lax.broadcasted_iota(jnp.int32, (n, n), 0)
    ci = lax.broadcasted_iota(jnp.int32, (n, n), 1)
    eye = ri == ci

    def stack(xb):
        zero = jnp.zeros_like(xb)
        return jnp.concatenate([jnp.where(head0, xb, zero), jnp.where(head0, zero, xb)], axis=0)

    def prod(x, y):
        return _dot(x.astype(BF16), stack(y.astype(BF16)), None)

    dir_refs = ((rf_ref, vf_ref, kkf_ref, kf_ref, bf_ref, lwf_ref, yf_ref),
                (rb_ref, vb_ref, kkb_ref, kb_ref, bb_ref, lwb_ref, yb_ref))
    pairs = range(2 * BATCH * N_PAIRS)
    drn = [p // (BATCH * N_PAIRS) for p in pairs]
    bat = [(p // N_PAIRS) % BATCH for p in pairs]
    col = [slice((p % N_PAIRS) * PAIR_W, (p % N_PAIRS + 1) * PAIR_W) for p in pairs]
    strict = [strict_d[drn[p]] for p in pairs]
    incl = [incl_d[drn[p]] for p in pairs]
    lax.fori_loop(0, RW_CHUNKS_PER_STEP, functools.partial(
        _rwkv_chunk, dir_refs=dir_refs, h_ref=h_ref, pairs=pairs, drn=drn, bat=bat, col=col, strict=strict, incl=incl,
        cumb_d=cumb_d, ident=ident, blk8=blk8, merges=merges, eye=eye, stack=stack, prod=prod), 0)


def _rwkv_chunk(i, carry, *, dir_refs, h_ref, pairs, drn, bat, col, strict, incl, cumb_d, ident, blk8, merges, eye,
                stack, prod):
    c, n = RW_CHUNK, PAIR_W
    rows_d = (pl.ds(pl.multiple_of(i * c, c), c), pl.ds(pl.multiple_of((RW_CHUNKS_PER_STEP - 1 - i) * c, c), c))
    rows = [rows_d[drn[p]] for p in pairs]
    lw_all, lc_all = {}, {}
    for dd in range(2):
        for bb in range(BATCH):
            lw_b = dir_refs[dd][5][0, bb, rows_d[dd], :]
            lw_hi = lw_b.astype(BF16)
            lw_mid, lw_lo = _split_bf16(lw_b - lw_hi.astype(F32))
            lw_all[dd, bb] = lw_b
            lc_all[dd, bb] = _dot(jnp.concatenate([cumb_d[dd]] * 3, axis=1),
                                  jnp.concatenate([lw_hi, lw_mid, lw_lo], axis=0), None)
    ar, kbt, vb, g_col = [], [], [], []
    for p in pairs:
        r_ref, v_ref, kk_ref, k_ref, b_ref = dir_refs[drn[p]][:5]
        lw, lc = lw_all[drn[p], bat[p]][:, col[p]], lc_all[drn[p], bat[p]][:, col[p]]
        lc_end = jnp.sum(lw, axis=0, keepdims=True)
        g_inv = jnp.exp(-lc)
        kc, bc = k_ref[0, bat[p], rows[p], col[p]], b_ref[0, bat[p], rows[p], col[p]]
        at = (-kk_ref[bat[p], rows[p], col[p]] * jnp.exp(lc - lw)).astype(BF16)
        rt = (r_ref[bat[p], rows[p], col[p]] * jnp.exp(lc)).astype(BF16)
        ar.append(jnp.concatenate([at, rt], axis=0))
        kb = jnp.concatenate([stack((kc * g_inv).astype(BF16)), stack((bc * g_inv).astype(BF16))], axis=0)
        kbt.append(kb.T)
        vb.append(v_ref[bat[p], rows[p], col[p]].astype(BF16))
        g_col.append(jnp.sum(jnp.where(eye, jnp.broadcast_to(jnp.exp(lc_end), (n, n)), 0.0), axis=1, keepdims=True))

    quad = [_dot(ar[p], kbt[p], None) for p in pairs]
    a_b = [jnp.where(strict[p], quad[p][:c, n:], 0.0) for p in pairs]
    akbk = [jnp.concatenate([jnp.where(strict[p], quad[p][:c, :n], 0.0), jnp.where(incl[p], quad[p][c:, :n], 0.0)],
                            axis=0).astype(BF16) for p in pairs]
    b_b = [jnp.where(incl[p], quad[p][c:, n:], 0.0).astype(BF16) for p in pairs]
    h0 = [h_ref[p] for p in pairs]
    arh = [_dot(ar[p], h0[p].astype(BF16), None) for p in pairs]
    abv = [_dot(jnp.concatenate([akbk[p], kbt[p][:, :n]], axis=0), stack(vb[p]), None) for p in pairs]

    n1 = [jnp.where(blk8, a_b[p], 0.0) for p in pairs]
    n2 = [prod(n1[p], n1[p]) for p in pairs]
    both = [prod(jnp.concatenate([n2[p], ident + n1[p]], axis=0), n2[p]) for p in pairs]
    n4 = [both[p][:c] for p in pairs]
    t12 = [ident + n1[p] + both[p][c:] for p in pairs]
    tinv = [t12[p] + prod(t12[p], n4[p]) for p in pairs]
    for m in merges:
        off = [jnp.where(m, a_b[p], 0.0) for p in pairs]
        half = [prod(tinv[p], off[p]) for p in pairs]
        tinv = [tinv[p] + prod(half[p], tinv[p]) for p in pairs]

    ub = [_dot(tinv[p].astype(BF16), stack((arh[p][:c] + abv[p][:c]).astype(BF16)), None).astype(BF16) for p in pairs]
    bbu = [_dot(jnp.concatenate([b_b[p], kbt[p][:, n:]], axis=0), stack(ub[p]), None) for p in pairs]
    for p in pairs:
        dir_refs[drn[p]][6][bat[p], rows[p], col[p]] = (arh[p][c:] + abv[p][c:2 * c] + bbu[p][:c]).astype(BF16)
        h_ref[p] = (h0[p] + abv[p][2 * c:] + bbu[p][c:]) * g_col[p]
    return carry


def _rwkv_scan(r, v, kk, k2, b2, lw2):
    rows = RW_CHUNK * RW_CHUNKS_PER_STEP

    def tok_spec(d):
        return pl.BlockSpec((BATCH, rows, RWKV_W), lambda s: (0, _rw_block_index(d, s), 0))

    def dir_spec(d):
        return pl.BlockSpec((1, BATCH, rows, RWKV_W), lambda s: (d, 0, _rw_block_index(d, s), 0))

    y_shape = jax.ShapeDtypeStruct((BATCH, TOK, RWKV_W), BF16)
    return pl.pallas_call(
        _rwkv_scan_kernel,
        out_shape=(y_shape, y_shape),
        grid=(RW_NBLOCK,),
        in_specs=[tok_spec(0), tok_spec(0), tok_spec(0), tok_spec(1), tok_spec(1), tok_spec(1),
                  dir_spec(0), dir_spec(0), dir_spec(0), dir_spec(1), dir_spec(1), dir_spec(1)],
        out_specs=(tok_spec(0), tok_spec(1)),
        scratch_shapes=[pltpu.VMEM((2 * BATCH * N_PAIRS, PAIR_W, PAIR_W), F32)],
        compiler_params=_params("arbitrary"),
        name="rwkv_scan",
    )(r, v, kk, r, v, kk, k2, b2, lw2, k2, b2, lw2)


def _dft_mats(n):
    i = np.arange(n)
    ang = 2.0 * np.pi * ((i[:, None] * i[None, :]) % n) / n
    return np.cos(ang).astype(np.float32), np.sin(ang).astype(np.float32)


def _chan_dft_mat():
    c, s = _dft_mats(FOUR_GROUP_W)
    eye = np.eye(FOUR_GROUPS, dtype=np.float32)
    return np.concatenate([np.kron(eye, c), np.kron(eye, s)], axis=1)


def _bdot(a, b):
    return jnp.dot(a.astype(BF16), b.astype(BF16), preferred_element_type=F32)


def _four_ctx_kernel(u_ref, cs_ref, c_ref, s_ref, o_ref):
    pq = _bdot(u_ref[0], cs_ref[...])
    scale = np.float32(1.0 / np.sqrt(CTX_LEN * FOUR_GROUP_W))
    o_ref[0] = (_bdot(c_ref[...], pq[:, :FOUR_W]) - _bdot(s_ref[...], pq[:, FOUR_W:])) * scale


def _four_ctx(u):
    c, s = _dft_mats(CTX_LEN)
    full = lambda shape: pl.BlockSpec(shape, lambda b: (0,) * len(shape))
    return pl.pallas_call(
        _four_ctx_kernel,
        out_shape=jax.ShapeDtypeStruct((BATCH, CTX_LEN, FOUR_W), F32),
        grid=(BATCH,),
        in_specs=[pl.BlockSpec((1, CTX_LEN, FOUR_W), lambda b: (b, CTX_TILE, 0)),
                  full((FOUR_W, 2 * FOUR_W)), full((CTX_LEN, CTX_LEN)), full((CTX_LEN, CTX_LEN))],
        out_specs=pl.BlockSpec((1, CTX_LEN, FOUR_W), lambda b: (b, 0, 0)),
        compiler_params=_params("arbitrary"),
        name="fourier_ctx",
    )(u, jnp.asarray(_chan_dft_mat()), jnp.asarray(c), jnp.asarray(s))


L2_PER_STEP = 16


def _four_lat_a_kernel(u_ref, cs_ref, c_ref, s_ref, tc_ref, ts_ref, zr_ref, zi_ref):
    cs = cs_ref[...]
    cms = jnp.concatenate([c_ref[...], s_ref[...]], axis=0)
    for i in range(L2_PER_STEP):
        x = u_ref[0, :, i, :]
        pq = _bdot(x, cs)
        both = _bdot(cms, pq)
        cpq, spq = both[:GRID_ROWS], both[GRID_ROWS:]
        zr = cpq[:, :FOUR_W] - spq[:, FOUR_W:]
        zi = -cpq[:, FOUR_W:] - spq[:, :FOUR_W]
        tc, ts = tc_ref[i], ts_ref[i]
        zr_ref[0, i] = zr * tc + zi * ts
        zi_ref[0, i] = zi * tc - zr * ts


K1_PER_STEP = 16


def _four_lat_b_kernel(zr_ref, zi_ref, cs_ref, o_ref):
    scale = np.float32(1.0 / np.sqrt(SEQ * FOUR_GROUP_W))
    cs = cs_ref[...]
    for i in range(K1_PER_STEP):
        z = jnp.concatenate([zr_ref[0, :, i, :], zi_ref[0, :, i, :]], axis=0)
        o_ref[0, :, i, :] = _bdot(cs, z) * scale


def _four_lat(u):
    c1, s1 = _dft_mats(GRID_ROWS)
    c2, s2 = _dft_mats(GRID_W)
    k1 = np.arange(GRID_ROWS)[None, :, None]
    l2 = np.arange(GRID_W)[:, None, None]
    ang = 2.0 * np.pi * (k1 * l2) / SEQ
    tc = np.broadcast_to(np.cos(ang), (GRID_W, GRID_ROWS, FOUR_W)).astype(np.float32)
    ts = np.broadcast_to(np.sin(ang), (GRID_W, GRID_ROWS, FOUR_W)).astype(np.float32)
    u4 = u.reshape(BATCH, TOK // GRID_W, GRID_W, FOUR_W)
    full = lambda shape: pl.BlockSpec(shape, lambda b, g: (0,) * len(shape))
    zshape = jax.ShapeDtypeStruct((BATCH, GRID_W, GRID_ROWS, FOUR_W), F32)
    zspec = pl.BlockSpec((1, L2_PER_STEP, GRID_ROWS, FOUR_W), lambda b, g: (b, g, 0, 0))
    tspec = pl.BlockSpec((L2_PER_STEP, GRID_ROWS, FOUR_W), lambda b, g: (g, 0, 0))
    zr, zi = pl.pallas_call(
        _four_lat_a_kernel,
        out_shape=(zshape, zshape),
        grid=(BATCH, GRID_W // L2_PER_STEP),
        in_specs=[pl.BlockSpec((1, GRID_ROWS, L2_PER_STEP, FOUR_W), lambda b, g: (b, 0, g, 0)),
                  full((FOUR_W, 2 * FOUR_W)), full((GRID_ROWS, GRID_ROWS)), full((GRID_ROWS, GRID_ROWS)),
                  tspec, tspec],
        out_specs=(zspec, zspec),
        compiler_params=_params("arbitrary", "arbitrary"),
        name="fourier_lat_a",
    )(u4, jnp.asarray(_chan_dft_mat()), jnp.asarray(c1), jnp.asarray(s1), jnp.asarray(tc), jnp.asarray(ts))
    zb = pl.BlockSpec((1, GRID_W, K1_PER_STEP, FOUR_W), lambda b, g: (b, 0, g, 0))
    out = pl.pallas_call(
        _four_lat_b_kernel,
        out_shape=jax.ShapeDtypeStruct((BATCH, GRID_W, GRID_ROWS, FOUR_W), F32),
        grid=(BATCH, GRID_ROWS // K1_PER_STEP),
        in_specs=[zb, zb, full((GRID_W, 2 * GRID_W))],
        out_specs=zb,
        compiler_params=_params("arbitrary", "arbitrary"),
        name="fourier_lat_b",
    )(zr, zi, jnp.asarray(np.concatenate([c2, s2], axis=1)))
    return out.reshape(BATCH, SEQ, FOUR_W)


def _post_residual(x, out, npost, gate):
    y = out * lax.rsqrt(jnp.mean(out * out, axis=-1, keepdims=True) + EPS) * npost
    return x + gate * y


def _retention_project(x, g_ref, mod_ref, w_ref, cos_ref, sin_ref, u_ref):
    gain = g_ref[...] * (1.0 + mod_ref[0, :, D_MODEL:2 * D_MODEL])
    h = (x * lax.rsqrt(jnp.mean(x * x, axis=-1, keepdims=True) + EPS) * gain + mod_ref[0, :, 0:D_MODEL]).astype(BF16)
    cosf, sinf = cos_ref[...], sin_ref[...]
    blocks = iter(range(ODD_IN // PROJ_BLOCK))

    def issue(count):
        for _ in range(count):
            lo = next(blocks) * PROJ_BLOCK
            acc = jnp.dot(h, w_ref[:, lo:lo + PROJ_BLOCK], preferred_element_type=F32)
            if lo >= 2 * RET_W:
                u_ref[0, :, lo:lo + PROJ_BLOCK] = acc.astype(u_ref.dtype)
                continue
            for i in range(PROJ_BLOCK // RET_HEAD):
                t = acc[:, i * RET_HEAD:(i + 1) * RET_HEAD]
                t = t * cosf + pltpu.roll(t, RET_HEAD // 2, 1) * sinf
                if lo >= RET_W:
                    t = t * np.float32(RET_HEAD ** -0.5)
                u_ref[0, :, lo + i * RET_HEAD:lo + (i + 1) * RET_HEAD] = t.astype(u_ref.dtype)

    return issue


def _even_finish_kernel(yf_ref, yb_ref, bonus_ref, gr_ref, gf_ref, four_ref, fourc_ref, x_ref, xc_ref, ones_ref, g_ref,
                        bta_ref, w_ref, np_ref, mod_ref, g1_ref, mod1_ref, w1_ref, cos_ref, sin_ref, o_ref, u1_ref,
                        x1_ref):
    j = pl.program_id(1)
    tile = jnp.minimum(j, N_TILES - 1)

    @pl.when(j == 0)
    def _():
        x1_ref[...] = jnp.zeros_like(x1_ref)

    issue = _retention_project(x1_ref[...], g1_ref, mod1_ref, w1_ref, cos_ref, sin_ref, u1_ref)
    issue(3)
    o = yf_ref[0].astype(F32) + yb_ref[0].astype(F32) + bonus_ref[0].astype(F32)
    ones_bd = ones_ref[...]
    inv = np.float32(1.0 / RWKV_HEAD)
    mean = _head_sum(o, ones_bd) * inv
    issue(3)
    cen = o - mean
    var = _head_sum(cen * cen, ones_bd) * inv
    issue(3)
    on = cen * lax.rsqrt(var + RWKV_GN_EPS) * g_ref[...] + bta_ref[...]
    yr = on * _silu(gr_ref[0].astype(F32))
    yf = _tile_rows(four_ref, fourc_ref, tile) * _silu(gf_ref[0].astype(F32))
    z = jnp.concatenate([yr, yf], axis=-1).astype(BF16)
    out = jnp.dot(z, w_ref[...], preferred_element_type=F32)
    issue(ODD_IN // PROJ_BLOCK - 9)
    x1 = _post_residual(_tile_rows(x_ref, xc_ref, tile), out, np_ref[...], mod_ref[0, :, 2 * D_MODEL:])
    o_ref[0] = x1
    x1_ref[...] = x1


def _even_finish(yf, yb, bonus, gate_r, gate_f, four_lat, four_ctx, x_lat, x_ctx, ones_bd, lnx_g, lnx_b, w_out_bf16,
                 npost, mod, next_g, next_mod, next_w_bf16, cosf, sinf):
    full = lambda shape: pl.BlockSpec(shape, lambda b, j: (0,) * len(shape))
    n_next = next_w_bf16.shape[1]
    cur = lambda j: jnp.minimum(j, N_TILES - 1)
    prev = lambda j: jnp.maximum(j - 1, 0)
    cur_spec = lambda w: pl.BlockSpec((1, TILE, w), lambda b, j: (b, cur(j), 0))
    mod_spec = lambda t: pl.BlockSpec((1, 1, 3 * D_MODEL), lambda b, j: (_mod_row(b, t(j)), 0, 0))
    return pl.pallas_call(
        _even_finish_kernel,
        out_shape=(jax.ShapeDtypeStruct((BATCH, TOK, D_MODEL), F32), jax.ShapeDtypeStruct((BATCH, TOK, n_next), BF16)),
        grid=(BATCH, N_TILES + 1),
        in_specs=[
            cur_spec(RWKV_W), cur_spec(RWKV_W), cur_spec(RWKV_W), cur_spec(RWKV_W), cur_spec(FOUR_W),
            *_lat_ctx_specs(FOUR_W),
            *_lat_ctx_specs(D_MODEL),
            full((PAIR_W, PAIR_W)), full((1, RWKV_W)), full((1, RWKV_W)), full((D_MODEL, D_MODEL)),
            full((1, D_MODEL)), mod_spec(cur),
            full((1, D_MODEL)), mod_spec(prev), full((D_MODEL, n_next)),
            pl.BlockSpec((TILE, RET_HEAD), lambda b, j: (prev(j), 0)),
            pl.BlockSpec((TILE, RET_HEAD), lambda b, j: (prev(j), 0)),
        ],
        out_specs=(cur_spec(D_MODEL), pl.BlockSpec((1, TILE, n_next), lambda b, j: (b, prev(j), 0))),
        scratch_shapes=[pltpu.VMEM((TILE, D_MODEL), F32)],
        compiler_params=_params("arbitrary", "arbitrary"),
        name="even_finish",
    )(yf, yb, bonus, gate_r, gate_f, four_lat, four_ctx, x_lat, x_ctx, ones_bd, lnx_g.reshape(1, RWKV_W),
      lnx_b.reshape(1, RWKV_W), w_out_bf16, npost.reshape(1, D_MODEL), mod.reshape(8, 1, 3 * D_MODEL),
      next_g.reshape(1, D_MODEL), next_mod.reshape(8, 1, 3 * D_MODEL), next_w_bf16, cosf, sinf)


def _ret_chunk_index(d, s):
    lat = jnp.where(d == 0, s - 1, RET_NCHUNK - 1 - s)
    return jnp.where(s == 0, RET_NCHUNK - 1, lat)


def _ret_scan_kernel(qf_ref, kf_ref, vf_ref, qb_ref, kb_ref, vb_ref, dl_ref, of_ref, ob_ref, st_ref):
    s = pl.program_id(1)
    c = RET_SUB

    @pl.when(s == 0)
    def _():
        st_ref[...] = jnp.zeros_like(st_ref)

    ri = lax.broadcasted_iota(jnp.int32, (c, c), 0)
    ci = lax.broadcasted_iota(jnp.int32, (c, c), 1)
    ti = lax.broadcasted_iota(jnp.int32, (c, RET_HEAD), 0)
    masks = (ri >= ci, ci > ri)
    dists = (jnp.where(masks[0], ri - ci, 0).astype(F32), jnp.where(masks[1], ci - ri, 0).astype(F32))
    poss = (ti.astype(F32), (c - 1 - ti).astype(F32))
    dir_refs = ((qf_ref, kf_ref, vf_ref, of_ref), (qb_ref, kb_ref, vb_ref, ob_ref))
    streams = range(2 * RET_HEADS)
    drn = [n // RET_HEADS for n in streams]
    lg = []
    for n in streams:
        x = dl_ref[drn[n], n % RET_HEADS]
        lg.append((jnp.minimum(x, 0.0) - jnp.log(1.0 + jnp.exp(-jnp.abs(x)))) * np.float32(np.log2(np.e)))
    lax.fori_loop(0, RET_CHUNK // RET_SUB, functools.partial(
        _ret_sub_chunk, dir_refs=dir_refs, st_ref=st_ref, streams=streams, drn=drn, lg=lg, masks=masks, dists=dists,
        poss=poss), 0)


def _ret_sub_chunk(i, carry, *, dir_refs, st_ref, streams, drn, lg, masks, dists, poss):
    c = RET_SUB
    last = RET_CHUNK // RET_SUB - 1
    rows_d = (pl.ds(pl.multiple_of(i * c, c), c), pl.ds(pl.multiple_of((last - i) * c, c), c))
    rows = [rows_d[drn[n]] for n in streams]
    col = [slice((n % RET_HEADS) * RET_HEAD, (n % RET_HEADS + 1) * RET_HEAD) for n in streams]
    q = [dir_refs[drn[n]][0][0, rows[n], col[n]] for n in streams]
    k = [dir_refs[drn[n]][1][0, rows[n], col[n]] for n in streams]
    v = [dir_refs[drn[n]][2][0, rows[n], col[n]] for n in streams]
    qk = [_dot_nt(q[n], k[n], None) for n in streams]
    st = [st_ref[n] for n in streams]
    cross = [_dot(q[n], st[n].astype(BF16), None) for n in streams]
    sc = [(qk[n] * jnp.where(masks[drn[n]], jnp.exp2(dists[drn[n]] * jnp.broadcast_to(lg[n][:, :1], (c, c))),
                             0.0)).astype(BF16) for n in streams]
    inner = [_dot(sc[n], v[n], None) for n in streams]
    upd = [_dot_tn((k[n].astype(F32) * jnp.exp2((c - 1.0 - poss[drn[n]]) * lg[n])).astype(BF16), v[n], None)
           for n in streams]
    for n in streams:
        o_ref = dir_refs[drn[n]][3]
        o_ref[0, rows[n], col[n]] = (inner[n] + cross[n] * jnp.exp2((poss[drn[n]] + 1.0) * lg[n])).astype(o_ref.dtype)
        st_ref[n] = st[n] * jnp.exp2(np.float32(c) * lg[n]) + upd[n]
    return carry


def _ret_scan(u2, dl):
    def tok_spec(d, blk):
        return pl.BlockSpec((1, RET_CHUNK, RET_W), lambda b, s: (b, _ret_chunk_index(d, s), blk))
    o_shape = jax.ShapeDtypeStruct((BATCH, TOK, RET_W), BF16)
    return pl.pallas_call(
        _ret_scan_kernel,
        out_shape=(o_shape, o_shape),
        grid=(BATCH, RET_NCHUNK),
        in_specs=[tok_spec(0, 0), tok_spec(0, 1), tok_spec(0, 2), tok_spec(1, 0), tok_spec(1, 1), tok_spec(1, 2),
                  pl.BlockSpec((2, RET_HEADS, 1, RET_HEAD), lambda b, s: (0, 0, 0, 0))],
        out_specs=(tok_spec(0, 0), tok_spec(1, 0)),
        scratch_shapes=[pltpu.VMEM((2 * RET_HEADS, RET_HEAD, RET_HEAD), F32)],
        compiler_params=_params("arbitrary", "arbitrary"),
        name="ret_scan",
    )(u2, u2, u2, u2, u2, u2, dl)


def _rope_tables():
    pos = np.arange(SEQ)
    half = RET_HEAD // 2
    inv = ROPE_BASE ** (-np.arange(0, half, 2, dtype=np.float32) / half)
    ang = np.concatenate([(pos // GRID_W)[:, None] * inv[None, :], (pos % GRID_W)[:, None] * inv[None, :]], axis=1)
    cos = np.concatenate([np.cos(ang), np.ones((CTX_LEN, half))], axis=0).astype(np.float32)
    sin = np.concatenate([np.sin(ang), np.zeros((CTX_LEN, half))], axis=0).astype(np.float32)
    return np.concatenate([cos, cos], axis=1), np.concatenate([-sin, sin], axis=1)


def _odd_finish_kernel(of_ref, ob_ref, g_ref, x_ref, w_ref, np_ref, mod_ref, out_ref):
    o = of_ref[0].astype(F32) + ob_ref[0].astype(F32)
    parts = []
    ones = jnp.ones((RET_HEAD, RET_HEAD), BF16)
    for h in range(RET_HEADS):
        oh = o[:, h * RET_HEAD:(h + 1) * RET_HEAD]
        ms = _dot((oh * oh).astype(BF16), ones, None) * np.float32(1.0 / RET_HEAD)
        parts.append(oh * lax.rsqrt(ms + EPS))
    z = (jnp.concatenate(parts, axis=-1) * _silu(g_ref[0].astype(F32))).astype(BF16)
    out = jnp.dot(z, w_ref[...], preferred_element_type=F32)
    out_ref[0] = _post_residual(x_ref[0], out, np_ref[...], mod_ref[0, :, 2 * D_MODEL:])


ODD_TILE = 1024


def _odd_finish(o_f, o_b, u2, xcomb, w_out_bf16, npost, mod):
    full = lambda shape: pl.BlockSpec(shape, lambda b, j: (0,) * len(shape))
    tile = lambda w, blk: pl.BlockSpec((1, ODD_TILE, w), lambda b, j: (b, j, blk))
    return pl.pallas_call(
        _odd_finish_kernel,
        out_shape=jax.ShapeDtypeStruct((BATCH, SEQ, D_MODEL), F32),
        grid=(BATCH, SEQ // ODD_TILE),
        in_specs=[
            tile(RET_W, 0), tile(RET_W, 0), tile(RET_W, 3), tile(D_MODEL, 0),
            full((RET_W, D_MODEL)), full((1, D_MODEL)),
            pl.BlockSpec((1, 1, 3 * D_MODEL), lambda b, j: (b, 0, 0)),
        ],
        out_specs=tile(D_MODEL, 0),
        compiler_params=_params("arbitrary", "arbitrary"),
        name="odd_finish",
    )(o_f, o_b, u2, xcomb, w_out_bf16, npost.reshape(1, D_MODEL), mod.reshape(8, 1, 3 * D_MODEL))


def _head_ones():
    return np.kron(np.eye(2, dtype=np.float32), np.ones((RWKV_HEAD, RWKV_HEAD), np.float32))


def kernel(x, c, ctx, c_ctx, ada_w, ada_b, norm_pre, norm_post, ev_w_in, ev_mu, ev_w0, ev_w2, ev_a0, ev_a2, ev_k_k, ev_k_a, ev_r_k, ev_lnx_g, ev_lnx_b, ev_w_out, od_w_in, od_decay_logit, od_w_out):
    cvec = jnp.concatenate([c, c_ctx[None, :], jnp.zeros((8 - BATCH - 1, D_MODEL), F32)], axis=0)
    mod = _ada_mod(cvec, ada_w, ada_b)
    ones_bd = jnp.asarray(_head_ones(), dtype=BF16)

    gate_r, four_in, gate_f, r, v, kk, k2, b2, lw2, bonus = _proj_terms(
        x, ctx, norm_pre[0], mod[0], ev_w_in[0].astype(BF16), ev_mu[0], ev_w0[0], ev_w2[0], ev_a0[0], ev_a2[0],
        ev_k_k[0], ev_k_a[0], ev_r_k[0].reshape(RWKV_W), ones_bd)
    yf, yb = _rwkv_scan(r, v, kk, k2, b2, lw2)
    cosf, sinf = _rope_tables()
    xcomb, u2 = _even_finish(yf, yb, bonus, gate_r, gate_f, _four_lat(four_in), _four_ctx(four_in), x, ctx, ones_bd,
                             ev_lnx_g[0], ev_lnx_b[0], ev_w_out[0].astype(BF16), norm_post[0], mod[0],
                             norm_pre[1], mod[1], od_w_in[0].astype(BF16), jnp.asarray(cosf), jnp.asarray(sinf))

    dl = jnp.broadcast_to(od_decay_logit[0][:, :, None, None], (2, RET_HEADS, 1, RET_HEAD))
    o_f, o_b = _ret_scan(u2, dl)
    return _odd_finish(o_f, o_b, u2, xcomb, od_w_out[0].astype(BF16), norm_post[1], mod[1])
```

```python
import functools

import numpy as np
import jax
import jax.numpy as jnp
from jax import lax
from jax.experimental import pallas as pl
from jax.experimental.pallas import tpu as pltpu

F32 = jnp.float32
BF16 = jnp.bfloat16

D_MODEL = 1024
BATCH = 2
SEQ = 8192
GRID_W = 64
GRID_ROWS = SEQ // GRID_W
CTX_LEN = 256
TOK = SEQ + CTX_LEN
EPS = 1e-6

RWKV_HEAD = 64
RWKV_HEADS = 12
RWKV_W = RWKV_HEADS * RWKV_HEAD
LORA = 64
FOUR_GROUPS = 4
FOUR_W = 256
FOUR_GROUP_W = 64
N_SHIFT = 3 * RWKV_W + 4 * LORA
EVEN_IN = N_SHIFT + RWKV_W + 2 * FOUR_W
RWKV_GN_EPS = 64e-5

RET_HEADS = 8
RET_HEAD = 128
RET_W = 1024
ODD_IN = 4 * RET_W
ROPE_BASE = 10000.0

TILE = 256
N_TILES = TOK // TILE
CTX_TILE = N_TILES - 1
RW_CHUNK = 64
RW_CHUNKS_PER_STEP = 4
RW_NBLOCK = TOK // (RW_CHUNK * RW_CHUNKS_PER_STEP)
RW_CTX_BLOCKS = CTX_LEN // (RW_CHUNK * RW_CHUNKS_PER_STEP)
PAIR_W = 2 * RWKV_HEAD
N_PAIRS = RWKV_W // PAIR_W
RET_CHUNK = 256
RET_SUB = 128
RET_NCHUNK = TOK // RET_CHUNK
VMEM_LIMIT = 56 * 1024 * 1024


def _dot(a, b, precision=None):
    return jnp.dot(a, b, precision=precision, preferred_element_type=F32)


def _dot_nt(a, b, precision=None):
    return lax.dot_general(a, b, (((1,), (1,)), ((), ())), precision=precision, preferred_element_type=F32)


def _dot_tn(a, b, precision=None):
    return lax.dot_general(a, b, (((0,), (0,)), ((), ())), precision=precision, preferred_element_type=F32)


def _split_bf16(x):
    hi = x.astype(BF16)
    return hi, (x - hi.astype(F32)).astype(BF16)


def _head_sum(x, ones_bd):
    xb = x.astype(BF16)
    out = []
    for c in range(x.shape[1] // PAIR_W):
        out.append(_dot(xb[:, c * PAIR_W:(c + 1) * PAIR_W], ones_bd, None))
    return jnp.concatenate(out, axis=1)


def _silu(x):
    return x * (1.0 / (1.0 + jnp.exp(-x)))


def _sigmoid(x):
    return 1.0 / (1.0 + jnp.exp(-x))


def _params(*sem):
    return pltpu.CompilerParams(dimension_semantics=sem, vmem_limit_bytes=VMEM_LIMIT)


def _ada_kernel(c_ref, w_ref, b_ref, o_ref):
    o_ref[0] = _dot(_silu(c_ref[...]), w_ref[0], lax.Precision.HIGHEST) + b_ref[0]


def _ada_mod(cvec, ada_w, ada_b):
    depth = ada_w.shape[0]
    nt = 3 * D_MODEL // 1024
    return pl.pallas_call(
        _ada_kernel,
        out_shape=jax.ShapeDtypeStruct((depth, 8, 3 * D_MODEL), F32),
        grid=(depth, nt),
        in_specs=[
            pl.BlockSpec((8, D_MODEL), lambda l, n: (0, 0)),
            pl.BlockSpec((1, D_MODEL, 1024), lambda l, n: (l, 0, n)),
            pl.BlockSpec((1, 1, 1024), lambda l, n: (l, 0, n)),
        ],
        out_specs=pl.BlockSpec((1, 8, 1024), lambda l, n: (l, 0, n)),
        compiler_params=_params("arbitrary", "arbitrary"),
        name="ada_mod",
    )(cvec, ada_w, ada_b.reshape(depth, 1, 3 * D_MODEL))


def _mod_row(b, j):
    return jnp.where(j == CTX_TILE, BATCH, b)


def _tile_rows(lat_ref, ctx_ref, tile):
    return jnp.where(tile == CTX_TILE, ctx_ref[0], lat_ref[0])


def _lat_ctx_specs(width):
    lat = pl.BlockSpec((1, TILE, width), lambda b, j: (b, jnp.minimum(j, CTX_TILE - 1), 0))
    ctx = pl.BlockSpec((1, TILE, width), lambda b, j: (b, 0, 0))
    return lat, ctx


PT_LAG = 2
PROJ_BLOCK = 256


def _proj_terms_kernel(x_ref, xc_ref, g_ref, mod_ref, w_ref, mu_ref, w0_ref, w2_ref, a0_ref, a2_ref, kk_ref_, ka_ref,
                       rk_ref, ones_ref, gr_out, four_out, gf_out, r_out, v_out, kk_out, k_out, b_out, lw_out,
                       bonus_out, cur_ref, nxt_ref, halo_ref):
    step = pl.program_id(1)
    j = step - PT_LAG

    @pl.when(step == 0)
    def _():
        cur_ref[...] = jnp.zeros_like(cur_ref)
        nxt_ref[...] = jnp.zeros_like(nxt_ref)
        halo_ref[...] = jnp.zeros_like(halo_ref)

    y = _tile_rows(x_ref, xc_ref, jnp.minimum(step, N_TILES - 1))
    gain = g_ref[...] * (1.0 + mod_ref[0, :, D_MODEL:2 * D_MODEL])
    h = (y * lax.rsqrt(jnp.mean(y * y, axis=-1, keepdims=True) + EPS) * gain + mod_ref[0, :, 0:D_MODEL]).astype(BF16)

    is_ctx = j == CTX_TILE
    cur_u = cur_ref[...]
    nxt_u = nxt_ref[...]
    up_rows = halo_ref[...]
    dn_rows = nxt_u[0:GRID_W, :]
    halo_ref[...] = cur_u[TILE - GRID_W:, :]
    cur_ref[...] = nxt_u
    x = cur_u.astype(F32)

    blocks = iter(range(EVEN_IN // PROJ_BLOCK))
    targets = ((nxt_ref, 0, N_SHIFT), (gr_out, N_SHIFT, RWKV_W), (four_out, N_SHIFT + RWKV_W, FOUR_W),
               (gf_out, N_SHIFT + RWKV_W + FOUR_W, FOUR_W))

    def issue(count):
        for _ in range(count):
            lo = next(blocks) * PROJ_BLOCK
            acc = jnp.dot(h, w_ref[:, lo:lo + PROJ_BLOCK], preferred_element_type=F32)
            for ref, start, width in targets:
                if start <= lo < start + width:
                    if ref is nxt_ref:
                        ref[:, lo - start:lo - start + PROJ_BLOCK] = acc.astype(ref.dtype)
                    else:
                        ref[0, :, lo - start:lo - start + PROJ_BLOCK] = acc.astype(ref.dtype)

    left = pltpu.roll(x, 1, 0)
    right = pltpu.roll(x, TILE - 1, 0)
    row8 = lax.broadcasted_iota(jnp.int32, (8, N_SHIFT), 0)
    inner_first = jnp.where(is_ctx, 0, 1)
    inner_last = jnp.where(is_ctx, 8, 7)

    def zero_rows(a, first):
        parts, pos = [], 0
        for g in range(0, TILE, GRID_W):
            at = g if first else g + GRID_W - 8
            if first:
                keep = row8 >= (1 if g == 0 else inner_first)
            else:
                keep = row8 < (7 if g == TILE - GRID_W else inner_last)
            parts += [a[pos:at], jnp.where(keep, a[at:at + 8], 0.0)]
            pos = at + 8
        return jnp.concatenate([p for p in parts + [a[pos:]] if p.shape[0]], axis=0)

    left = zero_rows(left, True)
    right = zero_rows(right, False)
    up_ok = jnp.where(j == 0, 0.0, 1.0)
    dn_ok = jnp.where(j == CTX_TILE - 1, 0.0, 1.0)
    up = jnp.concatenate([up_rows.astype(F32) * up_ok, x[:TILE - GRID_W]], axis=0)
    down = jnp.concatenate([x[GRID_W:], dn_rows.astype(F32) * dn_ok], axis=0)
    lane = lax.broadcasted_iota(jnp.int32, (1, N_SHIFT), 1)
    sel = jnp.where(is_ctx, lane & 1, lane & 3)
    shifted = jnp.where(sel == 0, left, jnp.where(sel == 1, right, jnp.where(sel == 2, up, down)))
    s = x + (shifted - x) * mu_ref[...]
    issue(6)

    r = s[:, 0:RWKV_W]
    k0 = s[:, RWKV_W:2 * RWKV_W]
    v = s[:, 2 * RWKV_W:3 * RWKV_W]
    ones_bd = ones_ref[...]
    kk = k0 * kk_ref_[...]
    kk = kk * lax.rsqrt(jnp.maximum(_head_sum(kk * kk, ones_bd), 1e-24))
    issue(2)
    r_out[0] = r.astype(BF16)
    v_out[0] = v.astype(BF16)
    kk_out[0] = kk.astype(BF16)
    bonus = jnp.zeros_like(v)
    for d in range(2):
        wl = s[:, 3 * RWKV_W + d * LORA:3 * RWKV_W + (d + 1) * LORA]
        al = s[:, 3 * RWKV_W + (2 + d) * LORA:3 * RWKV_W + (3 + d) * LORA]
        z = w0_ref[d:d + 1, :] + _dot(jnp.tanh(wl).astype(BF16), w2_ref[d].astype(BF16), None)
        lw_out[d, 0] = -np.float32(np.exp(-0.5)) * _sigmoid(z)
        a = _sigmoid(a0_ref[d:d + 1, :] + _dot(al.astype(BF16), a2_ref[d].astype(BF16), None))
        issue(2)
        k = k0 * (1.0 + (a - 1.0) * ka_ref[...])
        k_out[d, 0] = k.astype(BF16)
        b_out[d, 0] = (kk * a).astype(BF16)
        bonus = bonus + _head_sum(r * k * rk_ref[...], ones_bd) * v
        issue(2 if d == 0 else 1)
    bonus_out[0] = bonus.astype(BF16)


def _proj_terms(x, ctx, g, mod, w_bf16, mu, w0, w2, a0, a2, k_k, k_a, r_k, ones_bd):
    tok = jax.ShapeDtypeStruct((BATCH, TOK, RWKV_W), BF16)
    tok2 = jax.ShapeDtypeStruct((2, BATCH, TOK, RWKV_W), BF16)
    tok2_f32 = jax.ShapeDtypeStruct((2, BATCH, TOK, RWKV_W), F32)
    full = lambda shape: pl.BlockSpec(shape, lambda b, j: (0,) * len(shape))
    cur = lambda j: jnp.minimum(j, N_TILES - 1)
    lag = lambda j: jnp.maximum(j - PT_LAG, 0)
    cur_spec = lambda w: pl.BlockSpec((1, TILE, w), lambda b, j: (b, cur(j), 0))
    tspec = pl.BlockSpec((1, TILE, RWKV_W), lambda b, j: (b, lag(j), 0))
    t2spec = pl.BlockSpec((2, 1, TILE, RWKV_W), lambda b, j: (0, b, lag(j), 0))
    return pl.pallas_call(
        _proj_terms_kernel,
        out_shape=(tok, jax.ShapeDtypeStruct((BATCH, TOK, FOUR_W), F32), jax.ShapeDtypeStruct((BATCH, TOK, FOUR_W), BF16),
                   tok, tok, tok, tok2, tok2, tok2_f32, tok),
        grid=(BATCH, N_TILES + PT_LAG),
        in_specs=[
            *_lat_ctx_specs(D_MODEL),
            full((1, D_MODEL)),
            pl.BlockSpec((1, 1, 3 * D_MODEL), lambda b, j: (_mod_row(b, cur(j)), 0, 0)),
            full((D_MODEL, EVEN_IN)),
            full((1, N_SHIFT)), full((2, RWKV_W)), full((2, LORA, RWKV_W)), full((2, RWKV_W)),
            full((2, LORA, RWKV_W)), full((1, RWKV_W)), full((1, RWKV_W)), full((1, RWKV_W)),
            full((PAIR_W, PAIR_W)),
        ],
        out_specs=(cur_spec(RWKV_W), cur_spec(FOUR_W), cur_spec(FOUR_W),
                   tspec, tspec, tspec, t2spec, t2spec, t2spec, tspec),
        scratch_shapes=[pltpu.VMEM((TILE, N_SHIFT), BF16), pltpu.VMEM((TILE, N_SHIFT), BF16),
                        pltpu.VMEM((GRID_W, N_SHIFT), BF16)],
        compiler_params=_params("arbitrary", "arbitrary"),
        name="proj_terms",
    )(x, ctx, g.reshape(1, D_MODEL), mod.reshape(8, 1, 3 * D_MODEL), w_bf16, mu.reshape(1, N_SHIFT), w0, w2, a0, a2,
      k_k.reshape(1, RWKV_W), k_a.reshape(1, RWKV_W), r_k.reshape(1, RWKV_W), ones_bd)


def _rw_block_index(d, s):
    lat = jnp.where(d == 0, s - RW_CTX_BLOCKS, RW_NBLOCK - 1 - s)
    ctx = RW_NBLOCK - RW_CTX_BLOCKS + jnp.where(d == 0, s, RW_CTX_BLOCKS - 1 - s)
    return jnp.where(s < RW_CTX_BLOCKS, ctx, lat)


def _rwkv_scan_kernel(rf_ref, vf_ref, kkf_ref, rb_ref, vb_ref, kkb_ref, kf_ref, bf_ref, lwf_ref, kb_ref, bb_ref,
                      lwb_ref, yf_ref, yb_ref, h_ref):
    s = pl.program_id(0)
    n = PAIR_W

    @pl.when(s == 0)
    def _():
        h_ref[...] = jnp.zeros_like(h_ref)

    c = RW_CHUNK
    ti = lax.broadcasted_iota(jnp.int32, (c, n), 0)
    li = lax.broadcasted_iota(jnp.int32, (c, n), 1)
    si = li & (c - 1)
    strict_d = (si < ti, si > ti)
    incl_d = (si <= ti, si >= ti)
    ident = jnp.where(si == ti, 1.0, 0.0)
    blk8 = (ti >> 3) == (si >> 3)
    merges = [((ti >> (sh + 1)) == (si >> (sh + 1))) & ((ti >> sh) != (si >> sh)) for sh in (3, 4, 5)]
    head0 = li < RWKV_HEAD
    cumb_d = [jnp.where(m[:, :c], 1.0, 0.0).astype(BF16) for m in incl_d]
    ri = lax.broadcasted_iota(jnp.int32, (n, n), 0)
    ci = lax.broadcasted_iota(jnp.int32, (n, n), 1)
    eye = ri == ci

    def stack(xb):
        zero = jnp.zeros_like(xb)
        return jnp.concatenate([jnp.where(head0, xb, zero), jnp.where(head0, zero, xb)], axis=0)

    def prod(x, y):
        return _dot(x.astype(BF16), stack(y.astype(BF16)), None)

    dir_refs = ((rf_ref, vf_ref, kkf_ref, kf_ref, bf_ref, lwf_ref, yf_ref),
                (rb_ref, vb_ref, kkb_ref, kb_ref, bb_ref, lwb_ref, yb_ref))
    pairs = range(2 * BATCH * N_PAIRS)
    drn = [p // (BATCH * N_PAIRS) for p in pairs]
    bat = [(p // N_PAIRS) % BATCH for p in pairs]
    col = [slice((p % N_PAIRS) * PAIR_W, (p % N_PAIRS + 1) * PAIR_W) for p in pairs]
    strict = [strict_d[drn[p]] for p in pairs]
    incl = [incl_d[drn[p]] for p in pairs]
    lax.fori_loop(0, RW_CHUNKS_PER_STEP, functools.partial(
        _rwkv_chunk, dir_refs=dir_refs, h_ref=h_ref, pairs=pairs, drn=drn, bat=bat, col=col, strict=strict, incl=incl,
        cumb_d=cumb_d, ident=ident, blk8=blk8, merges=merges, eye=eye, stack=stack, prod=prod), 0)


def _rwkv_chunk(i, carry, *, dir_refs, h_ref, pairs, drn, bat, col, strict, incl, cumb_d, ident, blk8, merges, eye,
                stack, prod):
    c, n = RW_CHUNK, PAIR_W
    rows_d = (pl.ds(pl.multiple_of(i * c, c), c), pl.ds(pl.multiple_of((RW_CHUNKS_PER_STEP - 1 - i) * c, c), c))
    rows = [rows_d[drn[p]] for p in pairs]
    lw_all, lc_all = {}, {}
    for dd in range(2):
        for bb in range(BATCH):
            lw_b = dir_refs[dd][5][0, bb, rows_d[dd], :]
            lw_hi = lw_b.astype(BF16)
            lw_mid, lw_lo = _split_bf16(lw_b - lw_hi.astype(F32))
            lw_all[dd, bb] = lw_b
            lc_all[dd, bb] = _dot(jnp.concatenate([cumb_d[dd]] * 3, axis=1),
                                  jnp.concatenate([lw_hi, lw_mid, lw_lo], axis=0), None)
    ar, kbt, vb, g_col = [], [], [], []
    for p in pairs:
        r_ref, v_ref, kk_ref, k_ref, b_ref = dir_refs[drn[p]][:5]
        lw, lc = lw_all[drn[p], bat[p]][:, col[p]], lc_all[drn[p], bat[p]][:, col[p]]
        lc_end = jnp.sum(lw, axis=0, keepdims=True)
        g_inv = jnp.exp(-lc)
        kc, bc = k_ref[0, bat[p], rows[p], col[p]], b_ref[0, bat[p], rows[p], col[p]]
        at = (-kk_ref[bat[p], rows[p], col[p]] * jnp.exp(lc - lw)).astype(BF16)
        rt = (r_ref[bat[p], rows[p], col[p]] * jnp.exp(lc)).astype(BF16)
        ar.append(jnp.concatenate([at, rt], axis=0))
        kb = jnp.concatenate([stack((kc * g_inv).astype(BF16)), stack((bc * g_inv).astype(BF16))], axis=0)
        kbt.append(kb.T)
        vb.append(v_ref[bat[p], rows[p], col[p]].astype(BF16))
        g_col.append(jnp.sum(jnp.where(eye, jnp.broadcast_to(jnp.exp(lc_end), (n, n)), 0.0), axis=1, keepdims=True))

    quad = [_dot(ar[p], kbt[p], None) for p in pairs]
    a_b = [jnp.where(strict[p], quad[p][:c, n:], 0.0) for p in pairs]
    akbk = [jnp.concatenate([jnp.where(strict[p], quad[p][:c, :n], 0.0), jnp.where(incl[p], quad[p][c:, :n], 0.0)],
                            axis=0).astype(BF16) for p in pairs]
    b_b = [jnp.where(incl[p], quad[p][c:, n:], 0.0).astype(BF16) for p in pairs]
    h0 = [h_ref[p] for p in pairs]
    arh = [_dot(ar[p], h0[p].astype(BF16), None) for p in pairs]
    abv = [_dot(jnp.concatenate([akbk[p], kbt[p][:, :n]], axis=0), stack(vb[p]), None) for p in pairs]

    n1 = [jnp.where(blk8, a_b[p], 0.0) for p in pairs]
    n2 = [prod(n1[p], n1[p]) for p in pairs]
    both = [prod(jnp.concatenate([n2[p], ident + n1[p]], axis=0), n2[p]) for p in pairs]
    n4 = [both[p][:c] for p in pairs]
    t12 = [ident + n1[p] + both[p][c:] for p in pairs]
    tinv = [t12[p] + prod(t12[p], n4[p]) for p in pairs]
    for m in merges:
        off = [jnp.where(m, a_b[p], 0.0) for p in pairs]
        half = [prod(tinv[p], off[p]) for p in pairs]
        tinv = [tinv[p] + prod(half[p], tinv[p]) for p in pairs]

    ub = [_dot(tinv[p].astype(BF16), stack((arh[p][:c] + abv[p][:c]).astype(BF16)), None).astype(BF16) for p in pairs]
    bbu = [_dot(jnp.concatenate([b_b[p], kbt[p][:, n:]], axis=0), stack(ub[p]), None) for p in pairs]
    for p in pairs:
        dir_refs[drn[p]][6][bat[p], rows[p], col[p]] = (arh[p][c:] + abv[p][c:2 * c] + bbu[p][:c]).astype(BF16)
        h_ref[p] = (h0[p] + abv[p][2 * c:] + bbu[p][c:]) * g_col[p]
    return carry


def _rwkv_scan(r, v, kk, k2, b2, lw2):
    rows = RW_CHUNK * RW_CHUNKS_PER_STEP

    def tok_spec(d):
        return pl.BlockSpec((BATCH, rows, RWKV_W), lambda s: (0, _rw_block_index(d, s), 0))

    def dir_spec(d):
        return pl.BlockSpec((1, BATCH, rows, RWKV_W), lambda s: (d, 0, _rw_block_index(d, s), 0))

    y_shape = jax.ShapeDtypeStruct((BATCH, TOK, RWKV_W), BF16)
    return pl.pallas_call(
        _rwkv_scan_kernel,
        out_shape=(y_shape, y_shape),
        grid=(RW_NBLOCK,),
        in_specs=[tok_spec(0), tok_spec(0), tok_spec(0), tok_spec(1), tok_spec(1), tok_spec(1),
                  dir_spec(0), dir_spec(0), dir_spec(0), dir_spec(1), dir_spec(1), dir_spec(1)],
        out_specs=(tok_spec(0), tok_spec(1)),
        scratch_shapes=[pltpu.VMEM((2 * BATCH * N_PAIRS, PAIR_W, PAIR_W), F32)],
        compiler_params=_params("arbitrary"),
        name="rwkv_scan",
    )(r, v, kk, r, v, kk, k2, b2, lw2, k2, b2, lw2)


def _dft_mats(n):
    i = np.arange(n)
    ang = 2.0 * np.pi * ((i[:, None] * i[None, :]) % n) / n
    return np.cos(ang).astype(np.float32), np.sin(ang).astype(np.float32)


def _chan_dft_mat():
    c, s = _dft_mats(FOUR_GROUP_W)
    eye = np.eye(FOUR_GROUPS, dtype=np.float32)
    return np.concatenate([np.kron(eye, c), np.kron(eye, s)], axis=1)


def _bdot(a, b):
    return jnp.dot(a.astype(BF16), b.astype(BF16), preferred_element_type=F32)


def _four_ctx_kernel(u_ref, cs_ref, c_ref, s_ref, o_ref):
    pq = _bdot(u_ref[0], cs_ref[...])
    scale = np.float32(1.0 / np.sqrt(CTX_LEN * FOUR_GROUP_W))
    o_ref[0] = (_bdot(c_ref[...], pq[:, :FOUR_W]) - _bdot(s_ref[...], pq[:, FOUR_W:])) * scale


def _four_ctx(u):
    c, s = _dft_mats(CTX_LEN)
    full = lambda shape: pl.BlockSpec(shape, lambda b: (0,) * len(shape))
    return pl.pallas_call(
        _four_ctx_kernel,
        out_shape=jax.ShapeDtypeStruct((BATCH, CTX_LEN, FOUR_W), F32),
        grid=(BATCH,),
        in_specs=[pl.BlockSpec((1, CTX_LEN, FOUR_W), lambda b: (b, CTX_TILE, 0)),
                  full((FOUR_W, 2 * FOUR_W)), full((CTX_LEN, CTX_LEN)), full((CTX_LEN, CTX_LEN))],
        out_specs=pl.BlockSpec((1, CTX_LEN, FOUR_W), lambda b: (b, 0, 0)),
        compiler_params=_params("arbitrary"),
        name="fourier_ctx",
    )(u, jnp.asarray(_chan_dft_mat()), jnp.asarray(c), jnp.asarray(s))


L2_PER_STEP = 16


def _four_lat_a_kernel(u_ref, cs_ref, c_ref, s_ref, tc_ref, ts_ref, zr_ref, zi_ref):
    cs = cs_ref[...]
    cms = jnp.concatenate([c_ref[...], s_ref[...]], axis=0)
    for i in range(L2_PER_STEP):
        x = u_ref[0, :, i, :]
        pq = _bdot(x, cs)
        both = _bdot(cms, pq)
        cpq, spq = both[:GRID_ROWS], both[GRID_ROWS:]
        zr = cpq[:, :FOUR_W] - spq[:, FOUR_W:]
        zi = -cpq[:, FOUR_W:] - spq[:, :FOUR_W]
        tc, ts = tc_ref[i], ts_ref[i]
        zr_ref[0, i] = zr * tc + zi * ts
        zi_ref[0, i] = zi * tc - zr * ts


K1_PER_STEP = 16


def _four_lat_b_kernel(zr_ref, zi_ref, cs_ref, o_ref):
    scale = np.float32(1.0 / np.sqrt(SEQ * FOUR_GROUP_W))
    cs = cs_ref[...]
    for i in range(K1_PER_STEP):
        z = jnp.concatenate([zr_ref[0, :, i, :], zi_ref[0, :, i, :]], axis=0)
        o_ref[0, :, i, :] = _bdot(cs, z) * scale


def _four_lat(u):
    c1, s1 = _dft_mats(GRID_ROWS)
    c2, s2 = _dft_mats(GRID_W)
    k1 = np.arange(GRID_ROWS)[None, :, None]
    l2 = np.arange(GRID_W)[:, None, None]
    ang = 2.0 * np.pi * (k1 * l2) / SEQ
    tc = np.broadcast_to(np.cos(ang), (GRID_W, GRID_ROWS, FOUR_W)).astype(np.float32)
    ts = np.broadcast_to(np.sin(ang), (GRID_W, GRID_ROWS, FOUR_W)).astype(np.float32)
    u4 = u.reshape(BATCH, TOK // GRID_W, GRID_W, FOUR_W)
    full = lambda shape: pl.BlockSpec(shape, lambda b, g: (0,) * len(shape))
    zshape = jax.ShapeDtypeStruct((BATCH, GRID_W, GRID_ROWS, FOUR_W), F32)
    zspec = pl.BlockSpec((1, L2_PER_STEP, GRID_ROWS, FOUR_W), lambda b, g: (b, g, 0, 0))
    tspec = pl.BlockSpec((L2_PER_STEP, GRID_ROWS, FOUR_W), lambda b, g: (g, 0, 0))
    zr, zi = pl.pallas_call(
        _four_lat_a_kernel,
        out_shape=(zshape, zshape),
        grid=(BATCH, GRID_W // L2_PER_STEP),
        in_specs=[pl.BlockSpec((1, GRID_ROWS, L2_PER_STEP, FOUR_W), lambda b, g: (b, 0, g, 0)),
                  full((FOUR_W, 2 * FOUR_W)), full((GRID_ROWS, GRID_ROWS)), full((GRID_ROWS, GRID_ROWS)),
                  tspec, tspec],
        out_specs=(zspec, zspec),
        compiler_params=_params("arbitrary", "arbitrary"),
        name="fourier_lat_a",
    )(u4, jnp.asarray(_chan_dft_mat()), jnp.asarray(c1), jnp.asarray(s1), jnp.asarray(tc), jnp.asarray(ts))
    zb = pl.BlockSpec((1, GRID_W, K1_PER_STEP, FOUR_W), lambda b, g: (b, 0, g, 0))
    out = pl.pallas_call(
        _four_lat_b_kernel,
        out_shape=jax.ShapeDtypeStruct((BATCH, GRID_W, GRID_ROWS, FOUR_W), F32),
        grid=(BATCH, GRID_ROWS // K1_PER_STEP),
        in_specs=[zb, zb, full((GRID_W, 2 * GRID_W))],
        out_specs=zb,
        compiler_params=_params("arbitrary", "arbitrary"),
        name="fourier_lat_b",
    )(zr, zi, jnp.asarray(np.concatenate([c2, s2], axis=1)))
    return out.reshape(BATCH, SEQ, FOUR_W)


def _post_residual(x, out, npost, gate):
    y = out * lax.rsqrt(jnp.mean(out * out, axis=-1, keepdims=True) + EPS) * npost
    return x + gate * y


def _retention_project(x, g_ref, mod_ref, w_ref, cos_ref, sin_ref, u_ref):
    gain = g_ref[...] * (1.0 + mod_ref[0, :, D_MODEL:2 * D_MODEL])
    h = (x * lax.rsqrt(jnp.mean(x * x, axis=-1, keepdims=True) + EPS) * gain + mod_ref[0, :, 0:D_MODEL]).astype(BF16)
    cosf, sinf = cos_ref[...], sin_ref[...]
    blocks = iter(range(ODD_IN // PROJ_BLOCK))

    def issue(count):
        for _ in range(count):
            lo = next(blocks) * PROJ_BLOCK
            acc = jnp.dot(h, w_ref[:, lo:lo + PROJ_BLOCK], preferred_element_type=F32)
            if lo >= 2 * RET_W:
                u_ref[0, :, lo:lo + PROJ_BLOCK] = acc.astype(u_ref.dtype)
                continue
            for i in range(PROJ_BLOCK // RET_HEAD):
                t = acc[:, i * RET_HEAD:(i + 1) * RET_HEAD]
                t = t * cosf + pltpu.roll(t, RET_HEAD // 2, 1) * sinf
                if lo >= RET_W:
                    t = t * np.float32(RET_HEAD ** -0.5)
                u_ref[0, :, lo + i * RET_HEAD:lo + (i + 1) * RET_HEAD] = t.astype(u_ref.dtype)

    return issue


def _even_finish_kernel(yf_ref, yb_ref, bonus_ref, gr_ref, gf_ref, four_ref, fourc_ref, x_ref, xc_ref, ones_ref, g_ref,
                        bta_ref, w_ref, np_ref, mod_ref, g1_ref, mod1_ref, w1_ref, cos_ref, sin_ref, o_ref, u1_ref,
                        x1_ref):
    j = pl.program_id(1)
    tile = jnp.minimum(j, N_TILES - 1)

    @pl.when(j == 0)
    def _():
        x1_ref[...] = jnp.zeros_like(x1_ref)

    issue = _retention_project(x1_ref[...], g1_ref, mod1_ref, w1_ref, cos_ref, sin_ref, u1_ref)
    issue(3)
    o = yf_ref[0].astype(F32) + yb_ref[0].astype(F32) + bonus_ref[0].astype(F32)
    ones_bd = ones_ref[...]
    inv = np.float32(1.0 / RWKV_HEAD)
    mean = _head_sum(o, ones_bd) * inv
    issue(3)
    cen = o - mean
    var = _head_sum(cen * cen, ones_bd) * inv
    issue(3)
    on = cen * lax.rsqrt(var + RWKV_GN_EPS) * g_ref[...] + bta_ref[...]
    yr = on * _silu(gr_ref[0].astype(F32))
    yf = _tile_rows(four_ref, fourc_ref, tile) * _silu(gf_ref[0].astype(F32))
    z = jnp.concatenate([yr, yf], axis=-1).astype(BF16)
    out = jnp.dot(z, w_ref[...], preferred_element_type=F32)
    issue(ODD_IN // PROJ_BLOCK - 9)
    x1 = _post_residual(_tile_rows(x_ref, xc_ref, tile), out, np_ref[...], mod_ref[0, :, 2 * D_MODEL:])
    o_ref[0] = x1
    x1_ref[...] = x1


def _even_finish(yf, yb, bonus, gate_r, gate_f, four_lat, four_ctx, x_lat, x_ctx, ones_bd, lnx_g, lnx_b, w_out_bf16,
                 npost, mod, next_g, next_mod, next_w_bf16, cosf, sinf):
    full = lambda shape: pl.BlockSpec(shape, lambda b, j: (0,) * len(shape))
    n_next = next_w_bf16.shape[1]
    cur = lambda j: jnp.minimum(j, N_TILES - 1)
    prev = lambda j: jnp.maximum(j - 1, 0)
    cur_spec = lambda w: pl.BlockSpec((1, TILE, w), lambda b, j: (b, cur(j), 0))
    mod_spec = lambda t: pl.BlockSpec((1, 1, 3 * D_MODEL), lambda b, j: (_mod_row(b, t(j)), 0, 0))
    return pl.pallas_call(
        _even_finish_kernel,
        out_shape=(jax.ShapeDtypeStruct((BATCH, TOK, D_MODEL), F32), jax.ShapeDtypeStruct((BATCH, TOK, n_next), BF16)),
        grid=(BATCH, N_TILES + 1),
        in_specs=[
            cur_spec(RWKV_W), cur_spec(RWKV_W), cur_spec(RWKV_W), cur_spec(RWKV_W), cur_spec(FOUR_W),
            *_lat_ctx_specs(FOUR_W),
            *_lat_ctx_specs(D_MODEL),
            full((PAIR_W, PAIR_W)), full((1, RWKV_W)), full((1, RWKV_W)), full((D_MODEL, D_MODEL)),
            full((1, D_MODEL)), mod_spec(cur),
            full((1, D_MODEL)), mod_spec(prev), full((D_MODEL, n_next)),
            pl.BlockSpec((TILE, RET_HEAD), lambda b, j: (prev(j), 0)),
            pl.BlockSpec((TILE, RET_HEAD), lambda b, j: (prev(j), 0)),
        ],
        out_specs=(cur_spec(D_MODEL), pl.BlockSpec((1, TILE, n_next), lambda b, j: (b, prev(j), 0))),
        scratch_shapes=[pltpu.VMEM((TILE, D_MODEL), F32)],
        compiler_params=_params("arbitrary", "arbitrary"),
        name="even_finish",
    )(yf, yb, bonus, gate_r, gate_f, four_lat, four_ctx, x_lat, x_ctx, ones_bd, lnx_g.reshape(1, RWKV_W),
      lnx_b.reshape(1, RWKV_W), w_out_bf16, npost.reshape(1, D_MODEL), mod.reshape(8, 1, 3 * D_MODEL),
      next_g.reshape(1, D_MODEL), next_mod.reshape(8, 1, 3 * D_MODEL), next_w_bf16, cosf, sinf)


def _ret_chunk_index(d, s):
    lat = jnp.where(d == 0, s - 1, RET_NCHUNK - 1 - s)
    return jnp.where(s == 0, RET_NCHUNK - 1, lat)


def _ret_scan_kernel(qf_ref, kf_ref, vf_ref, qb_ref, kb_ref, vb_ref, dl_ref, of_ref, ob_ref, st_ref):
    s = pl.program_id(0)
    c = RET_SUB

    @pl.when(s == 0)
    def _():
        st_ref[...] = jnp.zeros_like(st_ref)

    ri = lax.broadcasted_iota(jnp.int32, (c, c), 0)
    ci = lax.broadcasted_iota(jnp.int32, (c, c), 1)
    ti = lax.broadcasted_iota(jnp.int32, (c, RET_HEAD), 0)
    masks = (ri >= ci, ci > ri)
    dists = (jnp.where(masks[0], ri - ci, 0).astype(F32), jnp.where(masks[1], ci - ri, 0).astype(F32))
    poss = (ti.astype(F32), (c - 1 - ti).astype(F32))
    dir_refs = ((qf_ref, kf_ref, vf_ref, of_ref), (qb_ref, kb_ref, vb_ref, ob_ref))
    streams = range(2 * BATCH * RET_HEADS)
    drn = [n // (BATCH * RET_HEADS) for n in streams]
    lg = []
    for n in streams:
        x = dl_ref[drn[n], n % RET_HEADS]
        lg.append((jnp.minimum(x, 0.0) - jnp.log(1.0 + jnp.exp(-jnp.abs(x)))) * np.float32(np.log2(np.e)))
    lax.fori_loop(0, RET_CHUNK // RET_SUB, functools.partial(
        _ret_sub_chunk, dir_refs=dir_refs, st_ref=st_ref, streams=streams, drn=drn, lg=lg, masks=masks, dists=dists,
        poss=poss), 0)


def _ret_sub_chunk(i, carry, *, dir_refs, st_ref, streams, drn, lg, masks, dists, poss):
    c = RET_SUB
    last = RET_CHUNK // RET_SUB - 1
    rows_d = (pl.ds(pl.multiple_of(i * c, c), c), pl.ds(pl.multiple_of((last - i) * c, c), c))
    rows = [rows_d[drn[n]] for n in streams]
    col = [slice((n % RET_HEADS) * RET_HEAD, (n % RET_HEADS + 1) * RET_HEAD) for n in streams]
    bat = [(n // RET_HEADS) % BATCH for n in streams]
    q = [dir_refs[drn[n]][0][bat[n], rows[n], col[n]] for n in streams]
    k = [dir_refs[drn[n]][1][bat[n], rows[n], col[n]] for n in streams]
    v = [dir_refs[drn[n]][2][bat[n], rows[n], col[n]] for n in streams]
    qk = [_dot_nt(q[n], k[n], None) for n in streams]
    st = [st_ref[n] for n in streams]
    cross = [_dot(q[n], st[n].astype(BF16), None) for n in streams]
    sc = [(qk[n] * jnp.where(masks[drn[n]], jnp.exp2(dists[drn[n]] * jnp.broadcast_to(lg[n][:, :1], (c, c))),
                             0.0)).astype(BF16) for n in streams]
    inner = [_dot(sc[n], v[n], None) for n in streams]
    upd = [_dot_tn((k[n].astype(F32) * jnp.exp2((c - 1.0 - poss[drn[n]]) * lg[n])).astype(BF16), v[n], None)
           for n in streams]
    for n in streams:
        o_ref = dir_refs[drn[n]][3]
        o_ref[bat[n], rows[n], col[n]] = (inner[n] + cross[n] * jnp.exp2((poss[drn[n]] + 1.0) * lg[n])).astype(o_ref.dtype)
        st_ref[n] = st[n] * jnp.exp2(np.float32(c) * lg[n]) + upd[n]
    return carry


def _ret_scan(u2, dl):
    def tok_spec(d, blk):
        return pl.BlockSpec((BATCH, RET_CHUNK, RET_W), lambda s: (0, _ret_chunk_index(d, s), blk))
    o_shape = jax.ShapeDtypeStruct((BATCH, TOK, RET_W), BF16)
    return pl.pallas_call(
        _ret_scan_kernel,
        out_shape=(o_shape, o_shape),
        grid=(RET_NCHUNK,),
        in_specs=[tok_spec(0, 0), tok_spec(0, 1), tok_spec(0, 2), tok_spec(1, 0), tok_spec(1, 1), tok_spec(1, 2),
                  pl.BlockSpec((2, RET_HEADS, 1, RET_HEAD), lambda s: (0, 0, 0, 0))],
        out_specs=(tok_spec(0, 0), tok_spec(1, 0)),
        scratch_shapes=[pltpu.VMEM((2 * BATCH * RET_HEADS, RET_HEAD, RET_HEAD), F32)],
        compiler_params=_params("arbitrary"),
        name="ret_scan",
    )(u2, u2, u2, u2, u2, u2, dl)


def _rope_tables():
    pos = np.arange(SEQ)
    half = RET_HEAD // 2
    inv = ROPE_BASE ** (-np.arange(0, half, 2, dtype=np.float32) / half)
    ang = np.concatenate([(pos // GRID_W)[:, None] * inv[None, :], (pos % GRID_W)[:, None] * inv[None, :]], axis=1)
    cos = np.concatenate([np.cos(ang), np.ones((CTX_LEN, half))], axis=0).astype(np.float32)
    sin = np.concatenate([np.sin(ang), np.zeros((CTX_LEN, half))], axis=0).astype(np.float32)
    return np.concatenate([cos, cos], axis=1), np.concatenate([-sin, sin], axis=1)


def _odd_finish_kernel(of_ref, ob_ref, g_ref, x_ref, w_ref, np_ref, mod_ref, out_ref):
    o = of_ref[0].astype(F32) + ob_ref[0].astype(F32)
    parts = []
    ones = jnp.ones((RET_HEAD, RET_HEAD), BF16)
    for h in range(RET_HEADS):
        oh = o[:, h * RET_HEAD:(h + 1) * RET_HEAD]
        ms = _dot((oh * oh).astype(BF16), ones, None) * np.float32(1.0 / RET_HEAD)
        parts.append(oh * lax.rsqrt(ms + EPS))
    z = (jnp.concatenate(parts, axis=-1) * _silu(g_ref[0].astype(F32))).astype(BF16)
    out = jnp.dot(z, w_ref[...], preferred_element_type=F32)
    out_ref[0] = _post_residual(x_ref[0], out, np_ref[...], mod_ref[0, :, 2 * D_MODEL:])


ODD_TILE = 1024


def _odd_finish(o_f, o_b, u2, xcomb, w_out_bf16, npost, mod):
    full = lambda shape: pl.BlockSpec(shape, lambda b, j: (0,) * len(shape))
    tile = lambda w, blk: pl.BlockSpec((1, ODD_TILE, w), lambda b, j: (b, j, blk))
    return pl.pallas_call(
        _odd_finish_kernel,
        out_shape=jax.ShapeDtypeStruct((BATCH, SEQ, D_MODEL), F32),
        grid=(BATCH, SEQ // ODD_TILE),
        in_specs=[
            tile(RET_W, 0), tile(RET_W, 0), tile(RET_W, 3), tile(D_MODEL, 0),
            full((RET_W, D_MODEL)), full((1, D_MODEL)),
            pl.BlockSpec((1, 1, 3 * D_MODEL), lambda b, j: (b, 0, 0)),
        ],
        out_specs=tile(D_MODEL, 0),
        compiler_params=_params("arbitrary", "arbitrary"),
        name="odd_finish",
    )(o_f, o_b, u2, xcomb, w_out_bf16, npost.reshape(1, D_MODEL), mod.reshape(8, 1, 3 * D_MODEL))


def _head_ones():
    return np.kron(np.eye(2, dtype=np.float32), np.ones((RWKV_HEAD, RWKV_HEAD), np.float32))


def kernel(x, c, ctx, c_ctx, ada_w, ada_b, norm_pre, norm_post, ev_w_in, ev_mu, ev_w0, ev_w2, ev_a0, ev_a2, ev_k_k, ev_k_a, ev_r_k, ev_lnx_g, ev_lnx_b, ev_w_out, od_w_in, od_decay_logit, od_w_out):
    cvec = jnp.concatenate([c, c_ctx[None, :], jnp.zeros((8 - BATCH - 1, D_MODEL), F32)], axis=0)
    mod = _ada_mod(cvec, ada_w, ada_b)
    ones_bd = jnp.asarray(_head_ones(), dtype=BF16)

    gate_r, four_in, gate_f, r, v, kk, k2, b2, lw2, bonus = _proj_terms(
        x, ctx, norm_pre[0], mod[0], ev_w_in[0].astype(BF16), ev_mu[0], ev_w0[0], ev_w2[0], ev_a0[0], ev_a2[0],
        ev_k_k[0], ev_k_a[0], ev_r_k[0].reshape(RWKV_W), ones_bd)
    yf, yb = _rwkv_scan(r, v, kk, k2, b2, lw2)
    cosf, sinf = _rope_tables()
    xcomb, u2 = _even_finish(yf, yb, bonus, gate_r, gate_f, _four_lat(four_in), _four_ctx(four_in), x, ctx, ones_bd,
                             ev_lnx_g[0], ev_lnx_b[0], ev_w_out[0].astype(BF16), norm_post[0], mod[0],
                             norm_pre[1], mod[1], od_w_in[0].astype(BF16), jnp.asarray(cosf), jnp.asarray(sinf))

    dl = jnp.broadcast_to(od_decay_logit[0][:, :, None, None], (2, RET_HEADS, 1, RET_HEAD))
    o_f, o_b = _ret_scan(u2, dl)
    return _odd_finish(o_f, o_b, u2, xcomb, od_w_out[0].astype(BF16), norm_post[1], mod[1])
```

```python
import functools

import numpy as np
import jax
import jax.numpy as jnp
from jax import lax
from jax.experimental import pallas as pl
from jax.experimental.pallas import tpu as pltpu

F32 = jnp.float32
BF16 = jnp.bfloat16

D_MODEL = 1024
BATCH = 2
SEQ = 8192
GRID_W = 64
GRID_ROWS = SEQ // GRID_W
CTX_LEN = 256
TOK = SEQ + CTX_LEN
EPS = 1e-6

RWKV_HEAD = 64
RWKV_HEADS = 12
RWKV_W = RWKV_HEADS * RWKV_HEAD
LORA = 64
FOUR_GROUPS = 4
FOUR_W = 256
FOUR_GROUP_W = 64
N_SHIFT = 3 * RWKV_W + 4 * LORA
EVEN_IN = N_SHIFT + RWKV_W + 2 * FOUR_W
RWKV_GN_EPS = 64e-5

RET_HEADS = 8
RET_HEAD = 128
RET_W = 1024
ODD_IN = 4 * RET_W
ROPE_BASE = 10000.0

TILE = 256
N_TILES = TOK // TILE
CTX_TILE = N_TILES - 1
RW_CHUNK = 64
RW_CHUNKS_PER_STEP = 4
RW_NBLOCK = TOK // (RW_CHUNK * RW_CHUNKS_PER_STEP)
RW_CTX_BLOCKS = CTX_LEN // (RW_CHUNK * RW_CHUNKS_PER_STEP)
PAIR_W = 2 * RWKV_HEAD
N_PAIRS = RWKV_W // PAIR_W
RET_CHUNK = 256
RET_SUB = 128
RET_NCHUNK = TOK // RET_CHUNK
RET_GROUP = 4
VMEM_LIMIT = 56 * 1024 * 1024


def _dot(a, b, precision=None):
    return jnp.dot(a, b, precision=precision, preferred_element_type=F32)


def _dot_nt(a, b, precision=None):
    return lax.dot_general(a, b, (((1,), (1,)), ((), ())), precision=precision, preferred_element_type=F32)


def _dot_tn(a, b, precision=None):
    return lax.dot_general(a, b, (((0,), (0,)), ((), ())), precision=precision, preferred_element_type=F32)


def _split_bf16(x):
    hi = x.astype(BF16)
    return hi, (x - hi.astype(F32)).astype(BF16)


def _head_sum(x, ones_bd):
    xb = x.astype(BF16)
    out = []
    for c in range(x.shape[1] // PAIR_W):
        out.append(_dot(xb[:, c * PAIR_W:(c + 1) * PAIR_W], ones_bd, None))
    return jnp.concatenate(out, axis=1)


def _silu(x):
    return x * (1.0 / (1.0 + jnp.exp(-x)))


def _sigmoid(x):
    return 1.0 / (1.0 + jnp.exp(-x))


def _params(*sem):
    return pltpu.CompilerParams(dimension_semantics=sem, vmem_limit_bytes=VMEM_LIMIT)


def _ada_kernel(c_ref, w_ref, b_ref, o_ref):
    o_ref[0] = _dot(_silu(c_ref[...]), w_ref[0], lax.Precision.HIGHEST) + b_ref[0]


def _ada_mod(cvec, ada_w, ada_b):
    depth = ada_w.shape[0]
    nt = 3 * D_MODEL // 1024
    return pl.pallas_call(
        _ada_kernel,
        out_shape=jax.ShapeDtypeStruct((depth, 8, 3 * D_MODEL), F32),
        grid=(depth, nt),
        in_specs=[
            pl.BlockSpec((8, D_MODEL), lambda l, n: (0, 0)),
            pl.BlockSpec((1, D_MODEL, 1024), lambda l, n: (l, 0, n)),
            pl.BlockSpec((1, 1, 1024), lambda l, n: (l, 0, n)),
        ],
        out_specs=pl.BlockSpec((1, 8, 1024), lambda l, n: (l, 0, n)),
        compiler_params=_params("arbitrary", "arbitrary"),
        name="ada_mod",
    )(cvec, ada_w, ada_b.reshape(depth, 1, 3 * D_MODEL))


def _mod_row(b, j):
    return jnp.where(j == CTX_TILE, BATCH, b)


def _tile_rows(lat_ref, ctx_ref, tile):
    return jnp.where(tile == CTX_TILE, ctx_ref[0], lat_ref[0])


def _lat_ctx_specs(width):
    lat = pl.BlockSpec((1, TILE, width), lambda b, j: (b, jnp.minimum(j, CTX_TILE - 1), 0))
    ctx = pl.BlockSpec((1, TILE, width), lambda b, j: (b, 0, 0))
    return lat, ctx


PT_LAG = 2
PROJ_BLOCK = 256


def _proj_terms_kernel(x_ref, xc_ref, g_ref, mod_ref, w_ref, mu_ref, w0_ref, w2_ref, a0_ref, a2_ref, kk_ref_, ka_ref,
                       rk_ref, ones_ref, gr_out, four_out, gf_out, r_out, v_out, kk_out, k_out, b_out, lw_out,
                       bonus_out, cur_ref, nxt_ref, halo_ref):
    step = pl.program_id(1)
    j = step - PT_LAG

    @pl.when(step == 0)
    def _():
        cur_ref[...] = jnp.zeros_like(cur_ref)
        nxt_ref[...] = jnp.zeros_like(nxt_ref)
        halo_ref[...] = jnp.zeros_like(halo_ref)

    y = _tile_rows(x_ref, xc_ref, jnp.minimum(step, N_TILES - 1))
    gain = g_ref[...] * (1.0 + mod_ref[0, :, D_MODEL:2 * D_MODEL])
    h = (y * lax.rsqrt(jnp.mean(y * y, axis=-1, keepdims=True) + EPS) * gain + mod_ref[0, :, 0:D_MODEL]).astype(BF16)

    is_ctx = j == CTX_TILE
    cur_u = cur_ref[...]
    nxt_u = nxt_ref[...]
    up_rows = halo_ref[...]
    dn_rows = nxt_u[0:GRID_W, :]
    halo_ref[...] = cur_u[TILE - GRID_W:, :]
    cur_ref[...] = nxt_u
    x = cur_u.astype(F32)

    blocks = iter(range(EVEN_IN // PROJ_BLOCK))
    targets = ((nxt_ref, 0, N_SHIFT), (gr_out, N_SHIFT, RWKV_W), (four_out, N_SHIFT + RWKV_W, FOUR_W),
               (gf_out, N_SHIFT + RWKV_W + FOUR_W, FOUR_W))

    def issue(count):
        for _ in range(count):
            lo = next(blocks) * PROJ_BLOCK
            acc = jnp.dot(h, w_ref[:, lo:lo + PROJ_BLOCK], preferred_element_type=F32)
            for ref, start, width in targets:
                if start <= lo < start + width:
                    if ref is nxt_ref:
                        ref[:, lo - start:lo - start + PROJ_BLOCK] = acc.astype(ref.dtype)
                    else:
                        ref[0, :, lo - start:lo - start + PROJ_BLOCK] = acc.astype(ref.dtype)

    left = pltpu.roll(x, 1, 0)
    right = pltpu.roll(x, TILE - 1, 0)
    row8 = lax.broadcasted_iota(jnp.int32, (8, N_SHIFT), 0)
    inner_first = jnp.where(is_ctx, 0, 1)
    inner_last = jnp.where(is_ctx, 8, 7)

    def zero_rows(a, first):
        parts, pos = [], 0
        for g in range(0, TILE, GRID_W):
            at = g if first else g + GRID_W - 8
            if first:
                keep = row8 >= (1 if g == 0 else inner_first)
            else:
                keep = row8 < (7 if g == TILE - GRID_W else inner_last)
            parts += [a[pos:at], jnp.where(keep, a[at:at + 8], 0.0)]
            pos = at + 8
        return jnp.concatenate([p for p in parts + [a[pos:]] if p.shape[0]], axis=0)

    left = zero_rows(left, True)
    right = zero_rows(right, False)
    up_ok = jnp.where(j == 0, 0.0, 1.0)
    dn_ok = jnp.where(j == CTX_TILE - 1, 0.0, 1.0)
    up = jnp.concatenate([up_rows.astype(F32) * up_ok, x[:TILE - GRID_W]], axis=0)
    down = jnp.concatenate([x[GRID_W:], dn_rows.astype(F32) * dn_ok], axis=0)
    lane = lax.broadcasted_iota(jnp.int32, (1, N_SHIFT), 1)
    sel = jnp.where(is_ctx, lane & 1, lane & 3)
    shifted = jnp.where(sel == 0, left, jnp.where(sel == 1, right, jnp.where(sel == 2, up, down)))
    s = x + (shifted - x) * mu_ref[...]
    issue(6)

    r = s[:, 0:RWKV_W]
    k0 = s[:, RWKV_W:2 * RWKV_W]
    v = s[:, 2 * RWKV_W:3 * RWKV_W]
    ones_bd = ones_ref[...]
    kk = k0 * kk_ref_[...]
    kk = kk * lax.rsqrt(jnp.maximum(_head_sum(kk * kk, ones_bd), 1e-24))
    issue(2)
    r_out[0] = r.astype(BF16)
    v_out[0] = v.astype(BF16)
    kk_out[0] = kk.astype(BF16)
    bonus = jnp.zeros_like(v)
    for d in range(2):
        wl = s[:, 3 * RWKV_W + d * LORA:3 * RWKV_W + (d + 1) * LORA]
        al = s[:, 3 * RWKV_W + (2 + d) * LORA:3 * RWKV_W + (3 + d) * LORA]
        z = w0_ref[d:d + 1, :] + _dot(jnp.tanh(wl).astype(BF16), w2_ref[d].astype(BF16), None)
        lw_out[d, 0] = -np.float32(np.exp(-0.5)) * _sigmoid(z)
        a = _sigmoid(a0_ref[d:d + 1, :] + _dot(al.astype(BF16), a2_ref[d].astype(BF16), None))
        issue(2)
        k = k0 * (1.0 + (a - 1.0) * ka_ref[...])
        k_out[d, 0] = k.astype(BF16)
        b_out[d, 0] = (kk * a).astype(BF16)
        bonus = bonus + _head_sum(r * k * rk_ref[...], ones_bd) * v
        issue(2 if d == 0 else 1)
    bonus_out[0] = bonus.astype(BF16)


def _proj_terms(x, ctx, g, mod, w_bf16, mu, w0, w2, a0, a2, k_k, k_a, r_k, ones_bd):
    tok = jax.ShapeDtypeStruct((BATCH, TOK, RWKV_W), BF16)
    tok2 = jax.ShapeDtypeStruct((2, BATCH, TOK, RWKV_W), BF16)
    tok2_f32 = jax.ShapeDtypeStruct((2, BATCH, TOK, RWKV_W), F32)
    full = lambda shape: pl.BlockSpec(shape, lambda b, j: (0,) * len(shape))
    cur = lambda j: jnp.minimum(j, N_TILES - 1)
    lag = lambda j: jnp.maximum(j - PT_LAG, 0)
    cur_spec = lambda w: pl.BlockSpec((1, TILE, w), lambda b, j: (b, cur(j), 0))
    tspec = pl.BlockSpec((1, TILE, RWKV_W), lambda b, j: (b, lag(j), 0))
    t2spec = pl.BlockSpec((2, 1, TILE, RWKV_W), lambda b, j: (0, b, lag(j), 0))
    return pl.pallas_call(
        _proj_terms_kernel,
        out_shape=(tok, jax.ShapeDtypeStruct((BATCH, TOK, FOUR_W), F32), jax.ShapeDtypeStruct((BATCH, TOK, FOUR_W), BF16),
                   tok, tok, tok, tok2, tok2, tok2_f32, tok),
        grid=(BATCH, N_TILES + PT_LAG),
        in_specs=[
            *_lat_ctx_specs(D_MODEL),
            full((1, D_MODEL)),
            pl.BlockSpec((1, 1, 3 * D_MODEL), lambda b, j: (_mod_row(b, cur(j)), 0, 0)),
            full((D_MODEL, EVEN_IN)),
            full((1, N_SHIFT)), full((2, RWKV_W)), full((2, LORA, RWKV_W)), full((2, RWKV_W)),
            full((2, LORA, RWKV_W)), full((1, RWKV_W)), full((1, RWKV_W)), full((1, RWKV_W)),
            full((PAIR_W, PAIR_W)),
        ],
        out_specs=(cur_spec(RWKV_W), cur_spec(FOUR_W), cur_spec(FOUR_W),
                   tspec, tspec, tspec, t2spec, t2spec, t2spec, tspec),
        scratch_shapes=[pltpu.VMEM((TILE, N_SHIFT), BF16), pltpu.VMEM((TILE, N_SHIFT), BF16),
                        pltpu.VMEM((GRID_W, N_SHIFT), BF16)],
        compiler_params=_params("arbitrary", "arbitrary"),
        name="proj_terms",
    )(x, ctx, g.reshape(1, D_MODEL), mod.reshape(8, 1, 3 * D_MODEL), w_bf16, mu.reshape(1, N_SHIFT), w0, w2, a0, a2,
      k_k.reshape(1, RWKV_W), k_a.reshape(1, RWKV_W), r_k.reshape(1, RWKV_W), ones_bd)


def _rw_block_index(d, s):
    lat = jnp.where(d == 0, s - RW_CTX_BLOCKS, RW_NBLOCK - 1 - s)
    ctx = RW_NBLOCK - RW_CTX_BLOCKS + jnp.where(d == 0, s, RW_CTX_BLOCKS - 1 - s)
    return jnp.where(s < RW_CTX_BLOCKS, ctx, lat)


def _rwkv_scan_kernel(rf_ref, vf_ref, kkf_ref, rb_ref, vb_ref, kkb_ref, kf_ref, bf_ref, lwf_ref, kb_ref, bb_ref,
                      lwb_ref, yf_ref, yb_ref, h_ref):
    s = pl.program_id(0)
    n = PAIR_W

    @pl.when(s == 0)
    def _():
        h_ref[...] = jnp.zeros_like(h_ref)

    c = RW_CHUNK
    ti = lax.broadcasted_iota(jnp.int32, (c, n), 0)
    li = lax.broadcasted_iota(jnp.int32, (c, n), 1)
    si = li & (c - 1)
    strict_d = (si < ti, si > ti)
    incl_d = (si <= ti, si >= ti)
    ident = jnp.where(si == ti, 1.0, 0.0)
    blk8 = (ti >> 3) == (si >> 3)
    merges = [((ti >> (sh + 1)) == (si >> (sh + 1))) & ((ti >> sh) != (si >> sh)) for sh in (3, 4, 5)]
    head0 = li < RWKV_HEAD
    cumb_d = [jnp.where(m[:, :c], 1.0, 0.0).astype(BF16) for m in incl_d]
    ri = lax.broadcasted_iota(jnp.int32, (n, n), 0)
    ci = lax.broadcasted_iota(jnp.int32, (n, n), 1)
    eye = ri == ci

    def stack(xb):
        zero = jnp.zeros_like(xb)
        return jnp.concatenate([jnp.where(head0, xb, zero), jnp.where(head0, zero, xb)], axis=0)

    def prod(x, y):
        return _dot(x.astype(BF16), stack(y.astype(BF16)), None)

    dir_refs = ((rf_ref, vf_ref, kkf_ref, kf_ref, bf_ref, lwf_ref, yf_ref),
                (rb_ref, vb_ref, kkb_ref, kb_ref, bb_ref, lwb_ref, yb_ref))
    pairs = range(2 * BATCH * N_PAIRS)
    drn = [p // (BATCH * N_PAIRS) for p in pairs]
    bat = [(p // N_PAIRS) % BATCH for p in pairs]
    col = [slice((p % N_PAIRS) * PAIR_W, (p % N_PAIRS + 1) * PAIR_W) for p in pairs]
    strict = [strict_d[drn[p]] for p in pairs]
    incl = [incl_d[drn[p]] for p in pairs]
    lax.fori_loop(0, RW_CHUNKS_PER_STEP, functools.partial(
        _rwkv_chunk, dir_refs=dir_refs, h_ref=h_ref, pairs=pairs, drn=drn, bat=bat, col=col, strict=strict, incl=incl,
        cumb_d=cumb_d, ident=ident, blk8=blk8, merges=merges, eye=eye, stack=stack, prod=prod), 0)


def _rwkv_chunk(i, carry, *, dir_refs, h_ref, pairs, drn, bat, col, strict, incl, cumb_d, ident, blk8, merges, eye,
                stack, prod):
    c, n = RW_CHUNK, PAIR_W
    rows_d = (pl.ds(pl.multiple_of(i * c, c), c), pl.ds(pl.multiple_of((RW_CHUNKS_PER_STEP - 1 - i) * c, c), c))
    rows = [rows_d[drn[p]] for p in pairs]
    lw_all, lc_all = {}, {}
    for dd in range(2):
        for bb in range(BATCH):
            lw_b = dir_refs[dd][5][0, bb, rows_d[dd], :]
            lw_hi = lw_b.astype(BF16)
            lw_mid, lw_lo = _split_bf16(lw_b - lw_hi.astype(F32))
            lw_all[dd, bb] = lw_b
            lc_all[dd, bb] = _dot(jnp.concatenate([cumb_d[dd]] * 3, axis=1),
                                  jnp.concatenate([lw_hi, lw_mid, lw_lo], axis=0), None)
    ar, kbt, vb, g_col = [], [], [], []
    for p in pairs:
        r_ref, v_ref, kk_ref, k_ref, b_ref = dir_refs[drn[p]][:5]
        lw, lc = lw_all[drn[p], bat[p]][:, col[p]], lc_all[drn[p], bat[p]][:, col[p]]
        lc_end = jnp.sum(lw, axis=0, keepdims=True)
        g_inv = jnp.exp(-lc)
        kc, bc = k_ref[0, bat[p], rows[p], col[p]], b_ref[0, bat[p], rows[p], col[p]]
        at = (-kk_ref[bat[p], rows[p], col[p]] * jnp.exp(lc - lw)).astype(BF16)
        rt = (r_ref[bat[p], rows[p], col[p]] * jnp.exp(lc)).astype(BF16)
        ar.append(jnp.concatenate([at, rt], axis=0))
        kb = jnp.concatenate([stack((kc * g_inv).astype(BF16)), stack((bc * g_inv).astype(BF16))], axis=0)
        kbt.append(kb.T)
        vb.append(v_ref[bat[p], rows[p], col[p]].astype(BF16))
        g_col.append(jnp.sum(jnp.where(eye, jnp.broadcast_to(jnp.exp(lc_end), (n, n)), 0.0), axis=1, keepdims=True))

    quad = [_dot(ar[p], kbt[p], None) for p in pairs]
    a_b = [jnp.where(strict[p], quad[p][:c, n:], 0.0) for p in pairs]
    akbk = [jnp.concatenate([jnp.where(strict[p], quad[p][:c, :n], 0.0), jnp.where(incl[p], quad[p][c:, :n], 0.0)],
                            axis=0).astype(BF16) for p in pairs]
    b_b = [jnp.where(incl[p], quad[p][c:, n:], 0.0).astype(BF16) for p in pairs]
    h0 = [h_ref[p] for p in pairs]
    arh = [_dot(ar[p], h0[p].astype(BF16), None) for p in pairs]
    abv = [_dot(jnp.concatenate([akbk[p], kbt[p][:, :n]], axis=0), stack(vb[p]), None) for p in pairs]

    n1 = [jnp.where(blk8, a_b[p], 0.0) for p in pairs]
    n2 = [prod(n1[p], n1[p]) for p in pairs]
    both = [prod(jnp.concatenate([n2[p], ident + n1[p]], axis=0), n2[p]) for p in pairs]
    n4 = [both[p][:c] for p in pairs]
    t12 = [ident + n1[p] + both[p][c:] for p in pairs]
    tinv = [t12[p] + prod(t12[p], n4[p]) for p in pairs]
    for m in merges:
        off = [jnp.where(m, a_b[p], 0.0) for p in pairs]
        half = [prod(tinv[p], off[p]) for p in pairs]
        tinv = [tinv[p] + prod(half[p], tinv[p]) for p in pairs]

    ub = [_dot(tinv[p].astype(BF16), stack((arh[p][:c] + abv[p][:c]).astype(BF16)), None).astype(BF16) for p in pairs]
    bbu = [_dot(jnp.concatenate([b_b[p], kbt[p][:, n:]], axis=0), stack(ub[p]), None) for p in pairs]
    for p in pairs:
        dir_refs[drn[p]][6][bat[p], rows[p], col[p]] = (arh[p][c:] + abv[p][c:2 * c] + bbu[p][:c]).astype(BF16)
        h_ref[p] = (h0[p] + abv[p][2 * c:] + bbu[p][c:]) * g_col[p]
    return carry


def _rwkv_scan(r, v, kk, k2, b2, lw2):
    rows = RW_CHUNK * RW_CHUNKS_PER_STEP

    def tok_spec(d):
        return pl.BlockSpec((BATCH, rows, RWKV_W), lambda s: (0, _rw_block_index(d, s), 0))

    def dir_spec(d):
        return pl.BlockSpec((1, BATCH, rows, RWKV_W), lambda s: (d, 0, _rw_block_index(d, s), 0))

    y_shape = jax.ShapeDtypeStruct((BATCH, TOK, RWKV_W), BF16)
    return pl.pallas_call(
        _rwkv_scan_kernel,
        out_shape=(y_shape, y_shape),
        grid=(RW_NBLOCK,),
        in_specs=[tok_spec(0), tok_spec(0), tok_spec(0), tok_spec(1), tok_spec(1), tok_spec(1),
                  dir_spec(0), dir_spec(0), dir_spec(0), dir_spec(1), dir_spec(1), dir_spec(1)],
        out_specs=(tok_spec(0), tok_spec(1)),
        scratch_shapes=[pltpu.VMEM((2 * BATCH * N_PAIRS, PAIR_W, PAIR_W), F32)],
        compiler_params=_params("arbitrary"),
        name="rwkv_scan",
    )(r, v, kk, r, v, kk, k2, b2, lw2, k2, b2, lw2)


def _dft_mats(n):
    i = np.arange(n)
    ang = 2.0 * np.pi * ((i[:, None] * i[None, :]) % n) / n
    return np.cos(ang).astype(np.float32), np.sin(ang).astype(np.float32)


def _chan_dft_mat():
    c, s = _dft_mats(FOUR_GROUP_W)
    eye = np.eye(FOUR_GROUPS, dtype=np.float32)
    return np.concatenate([np.kron(eye, c), np.kron(eye, s)], axis=1)


def _bdot(a, b):
    return jnp.dot(a.astype(BF16), b.astype(BF16), preferred_element_type=F32)


def _four_ctx_kernel(u_ref, cs_ref, c_ref, s_ref, o_ref):
    pq = _bdot(u_ref[0], cs_ref[...])
    scale = np.float32(1.0 / np.sqrt(CTX_LEN * FOUR_GROUP_W))
    o_ref[0] = (_bdot(c_ref[...], pq[:, :FOUR_W]) - _bdot(s_ref[...], pq[:, FOUR_W:])) * scale


def _four_ctx(u):
    c, s = _dft_mats(CTX_LEN)
    full = lambda shape: pl.BlockSpec(shape, lambda b: (0,) * len(shape))
    return pl.pallas_call(
        _four_ctx_kernel,
        out_shape=jax.ShapeDtypeStruct((BATCH, CTX_LEN, FOUR_W), F32),
        grid=(BATCH,),
        in_specs=[pl.BlockSpec((1, CTX_LEN, FOUR_W), lambda b: (b, CTX_TILE, 0)),
                  full((FOUR_W, 2 * FOUR_W)), full((CTX_LEN, CTX_LEN)), full((CTX_LEN, CTX_LEN))],
        out_specs=pl.BlockSpec((1, CTX_LEN, FOUR_W), lambda b: (b, 0, 0)),
        compiler_params=_params("arbitrary"),
        name="fourier_ctx",
    )(u, jnp.asarray(_chan_dft_mat()), jnp.asarray(c), jnp.asarray(s))


L2_PER_STEP = 16


def _four_lat_a_kernel(u_ref, cs_ref, c_ref, s_ref, tc_ref, ts_ref, zr_ref, zi_ref):
    cs = cs_ref[...]
    cms = jnp.concatenate([c_ref[...], s_ref[...]], axis=0)
    for i in range(L2_PER_STEP):
        x = u_ref[0, :, i, :]
        pq = _bdot(x, cs)
        both = _bdot(cms, pq)
        cpq, spq = both[:GRID_ROWS], both[GRID_ROWS:]
        zr = cpq[:, :FOUR_W] - spq[:, FOUR_W:]
        zi = -cpq[:, FOUR_W:] - spq[:, :FOUR_W]
        tc, ts = tc_ref[i], ts_ref[i]
        zr_ref[0, i] = zr * tc + zi * ts
        zi_ref[0, i] = zi * tc - zr * ts


K1_PER_STEP = 16


def _four_lat_b_kernel(zr_ref, zi_ref, cs_ref, o_ref):
    scale = np.float32(1.0 / np.sqrt(SEQ * FOUR_GROUP_W))
    cs = cs_ref[...]
    for i in range(K1_PER_STEP):
        z = jnp.concatenate([zr_ref[0, :, i, :], zi_ref[0, :, i, :]], axis=0)
        o_ref[0, :, i, :] = _bdot(cs, z) * scale


def _four_lat(u):
    c1, s1 = _dft_mats(GRID_ROWS)
    c2, s2 = _dft_mats(GRID_W)
    k1 = np.arange(GRID_ROWS)[None, :, None]
    l2 = np.arange(GRID_W)[:, None, None]
    ang = 2.0 * np.pi * (k1 * l2) / SEQ
    tc = np.broadcast_to(np.cos(ang), (GRID_W, GRID_ROWS, FOUR_W)).astype(np.float32)
    ts = np.broadcast_to(np.sin(ang), (GRID_W, GRID_ROWS, FOUR_W)).astype(np.float32)
    u4 = u.reshape(BATCH, TOK // GRID_W, GRID_W, FOUR_W)
    full = lambda shape: pl.BlockSpec(shape, lambda b, g: (0,) * len(shape))
    zshape = jax.ShapeDtypeStruct((BATCH, GRID_W, GRID_ROWS, FOUR_W), F32)
    zspec = pl.BlockSpec((1, L2_PER_STEP, GRID_ROWS, FOUR_W), lambda b, g: (b, g, 0, 0))
    tspec = pl.BlockSpec((L2_PER_STEP, GRID_ROWS, FOUR_W), lambda b, g: (g, 0, 0))
    zr, zi = pl.pallas_call(
        _four_lat_a_kernel,
        out_shape=(zshape, zshape),
        grid=(BATCH, GRID_W // L2_PER_STEP),
        in_specs=[pl.BlockSpec((1, GRID_ROWS, L2_PER_STEP, FOUR_W), lambda b, g: (b, 0, g, 0)),
                  full((FOUR_W, 2 * FOUR_W)), full((GRID_ROWS, GRID_ROWS)), full((GRID_ROWS, GRID_ROWS)),
                  tspec, tspec],
        out_specs=(zspec, zspec),
        compiler_params=_params("arbitrary", "arbitrary"),
        name="fourier_lat_a",
    )(u4, jnp.asarray(_chan_dft_mat()), jnp.asarray(c1), jnp.asarray(s1), jnp.asarray(tc), jnp.asarray(ts))
    zb = pl.BlockSpec((1, GRID_W, K1_PER_STEP, FOUR_W), lambda b, g: (b, 0, g, 0))
    out = pl.pallas_call(
        _four_lat_b_kernel,
        out_shape=jax.ShapeDtypeStruct((BATCH, GRID_W, GRID_ROWS, FOUR_W), F32),
        grid=(BATCH, GRID_ROWS // K1_PER_STEP),
        in_specs=[zb, zb, full((GRID_W, 2 * GRID_W))],
        out_specs=zb,
        compiler_params=_params("arbitrary", "arbitrary"),
        name="fourier_lat_b",
    )(zr, zi, jnp.asarray(np.concatenate([c2, s2], axis=1)))
    return out.reshape(BATCH, SEQ, FOUR_W)


def _post_residual(x, out, npost, gate):
    y = out * lax.rsqrt(jnp.mean(out * out, axis=-1, keepdims=True) + EPS) * npost
    return x + gate * y


def _retention_project(x, g_ref, mod_ref, w_ref, cos_ref, sin_ref, u_ref):
    gain = g_ref[...] * (1.0 + mod_ref[0, :, D_MODEL:2 * D_MODEL])
    h = (x * lax.rsqrt(jnp.mean(x * x, axis=-1, keepdims=True) + EPS) * gain + mod_ref[0, :, 0:D_MODEL]).astype(BF16)
    cosf, sinf = cos_ref[...], sin_ref[...]
    blocks = iter(range(ODD_IN // PROJ_BLOCK))

    def issue(count):
        for _ in range(count):
            lo = next(blocks) * PROJ_BLOCK
            acc = jnp.dot(h, w_ref[:, lo:lo + PROJ_BLOCK], preferred_element_type=F32)
            if lo >= 2 * RET_W:
                u_ref[0, :, lo:lo + PROJ_BLOCK] = acc.astype(u_ref.dtype)
                continue
            for i in range(PROJ_BLOCK // RET_HEAD):
                t = acc[:, i * RET_HEAD:(i + 1) * RET_HEAD]
                t = t * cosf + pltpu.roll(t, RET_HEAD // 2, 1) * sinf
                if lo >= RET_W:
                    t = t * np.float32(RET_HEAD ** -0.5)
                u_ref[0, :, lo + i * RET_HEAD:lo + (i + 1) * RET_HEAD] = t.astype(u_ref.dtype)

    return issue


def _even_finish_kernel(yf_ref, yb_ref, bonus_ref, gr_ref, gf_ref, four_ref, fourc_ref, x_ref, xc_ref, ones_ref, g_ref,
                        bta_ref, w_ref, np_ref, mod_ref, g1_ref, mod1_ref, w1_ref, cos_ref, sin_ref, o_ref, u1_ref,
                        x1_ref):
    j = pl.program_id(1)
    tile = jnp.minimum(j, N_TILES - 1)

    @pl.when(j == 0)
    def _():
        x1_ref[...] = jnp.zeros_like(x1_ref)

    issue = _retention_project(x1_ref[...], g1_ref, mod1_ref, w1_ref, cos_ref, sin_ref, u1_ref)
    issue(3)
    o = yf_ref[0].astype(F32) + yb_ref[0].astype(F32) + bonus_ref[0].astype(F32)
    ones_bd = ones_ref[...]
    inv = np.float32(1.0 / RWKV_HEAD)
    mean = _head_sum(o, ones_bd) * inv
    issue(3)
    cen = o - mean
    var = _head_sum(cen * cen, ones_bd) * inv
    issue(3)
    on = cen * lax.rsqrt(var + RWKV_GN_EPS) * g_ref[...] + bta_ref[...]
    yr = on * _silu(gr_ref[0].astype(F32))
    yf = _tile_rows(four_ref, fourc_ref, tile) * _silu(gf_ref[0].astype(F32))
    z = jnp.concatenate([yr, yf], axis=-1).astype(BF16)
    out = jnp.dot(z, w_ref[...], preferred_element_type=F32)
    issue(ODD_IN // PROJ_BLOCK - 9)
    x1 = _post_residual(_tile_rows(x_ref, xc_ref, tile), out, np_ref[...], mod_ref[0, :, 2 * D_MODEL:])
    o_ref[0] = x1
    x1_ref[...] = x1


def _even_finish(yf, yb, bonus, gate_r, gate_f, four_lat, four_ctx, x_lat, x_ctx, ones_bd, lnx_g, lnx_b, w_out_bf16,
                 npost, mod, next_g, next_mod, next_w_bf16, cosf, sinf):
    full = lambda shape: pl.BlockSpec(shape, lambda b, j: (0,) * len(shape))
    n_next = next_w_bf16.shape[1]
    cur = lambda j: jnp.minimum(j, N_TILES - 1)
    prev = lambda j: jnp.maximum(j - 1, 0)
    cur_spec = lambda w: pl.BlockSpec((1, TILE, w), lambda b, j: (b, cur(j), 0))
    mod_spec = lambda t: pl.BlockSpec((1, 1, 3 * D_MODEL), lambda b, j: (_mod_row(b, t(j)), 0, 0))
    return pl.pallas_call(
        _even_finish_kernel,
        out_shape=(jax.ShapeDtypeStruct((BATCH, TOK, D_MODEL), F32), jax.ShapeDtypeStruct((BATCH, TOK, n_next), BF16)),
        grid=(BATCH, N_TILES + 1),
        in_specs=[
            cur_spec(RWKV_W), cur_spec(RWKV_W), cur_spec(RWKV_W), cur_spec(RWKV_W), cur_spec(FOUR_W),
            *_lat_ctx_specs(FOUR_W),
            *_lat_ctx_specs(D_MODEL),
            full((PAIR_W, PAIR_W)), full((1, RWKV_W)), full((1, RWKV_W)), full((D_MODEL, D_MODEL)),
            full((1, D_MODEL)), mod_spec(cur),
            full((1, D_MODEL)), mod_spec(prev), full((D_MODEL, n_next)),
            pl.BlockSpec((TILE, RET_HEAD), lambda b, j: (prev(j), 0)),
            pl.BlockSpec((TILE, RET_HEAD), lambda b, j: (prev(j), 0)),
        ],
        out_specs=(cur_spec(D_MODEL), pl.BlockSpec((1, TILE, n_next), lambda b, j: (b, prev(j), 0))),
        scratch_shapes=[pltpu.VMEM((TILE, D_MODEL), F32)],
        compiler_params=_params("arbitrary", "arbitrary"),
        name="even_finish",
    )(yf, yb, bonus, gate_r, gate_f, four_lat, four_ctx, x_lat, x_ctx, ones_bd, lnx_g.reshape(1, RWKV_W),
      lnx_b.reshape(1, RWKV_W), w_out_bf16, npost.reshape(1, D_MODEL), mod.reshape(8, 1, 3 * D_MODEL),
      next_g.reshape(1, D_MODEL), next_mod.reshape(8, 1, 3 * D_MODEL), next_w_bf16, cosf, sinf)


def _ret_chunk_index(d, s):
    lat = jnp.where(d == 0, s - 1, RET_NCHUNK - 1 - s)
    return jnp.where(s == 0, RET_NCHUNK - 1, lat)


def _ret_scan_kernel(qf_ref, kf_ref, vf_ref, qb_ref, kb_ref, vb_ref, dl_ref, of_ref, ob_ref, st_ref):
    s = pl.program_id(0)
    c = RET_SUB

    @pl.when(s == 0)
    def _():
        st_ref[...] = jnp.zeros_like(st_ref)

    ri = lax.broadcasted_iota(jnp.int32, (c, c), 0)
    ci = lax.broadcasted_iota(jnp.int32, (c, c), 1)
    ti = lax.broadcasted_iota(jnp.int32, (c, RET_HEAD), 0)
    masks = (ri >= ci, ci > ri)
    dists = (jnp.where(masks[0], ri - ci, 0).astype(F32), jnp.where(masks[1], ci - ri, 0).astype(F32))
    poss = (ti.astype(F32), (c - 1 - ti).astype(F32))
    dir_refs = ((qf_ref, kf_ref, vf_ref, of_ref), (qb_ref, kb_ref, vb_ref, ob_ref))
    streams = range(2 * BATCH * RET_HEADS)
    drn = [n // (BATCH * RET_HEADS) for n in streams]
    lg = []
    for n in streams:
        x = dl_ref[drn[n], n % RET_HEADS]
        lg.append((jnp.minimum(x, 0.0) - jnp.log(1.0 + jnp.exp(-jnp.abs(x)))) * np.float32(np.log2(np.e)))
    lax.fori_loop(0, RET_CHUNK // RET_SUB, functools.partial(
        _ret_sub_chunk, dir_refs=dir_refs, st_ref=st_ref, streams=streams, drn=drn, lg=lg, masks=masks, dists=dists,
        poss=poss), 0)


def _ret_sub_chunk(i, carry, *, dir_refs, st_ref, streams, drn, lg, masks, dists, poss):
    c = RET_SUB
    last = RET_CHUNK // RET_SUB - 1
    rows_d = (pl.ds(pl.multiple_of(i * c, c), c), pl.ds(pl.multiple_of((last - i) * c, c), c))
    rows = [rows_d[drn[n]] for n in streams]
    col = [slice((n % RET_HEADS) * RET_HEAD, (n % RET_HEADS + 1) * RET_HEAD) for n in streams]
    bat = [(n // RET_HEADS) % BATCH for n in streams]
    for g0 in range(0, len(streams), RET_GROUP):
        grp = streams[g0:g0 + RET_GROUP]
        q = {n: dir_refs[drn[n]][0][bat[n], rows[n], col[n]] for n in grp}
        k = {n: dir_refs[drn[n]][1][bat[n], rows[n], col[n]] for n in grp}
        v = {n: dir_refs[drn[n]][2][bat[n], rows[n], col[n]] for n in grp}
        qk = {n: _dot_nt(q[n], k[n], None) for n in grp}
        st = {n: st_ref[n] for n in grp}
        cross = {n: _dot(q[n], st[n].astype(BF16), None) for n in grp}
        sc = {n: (qk[n] * jnp.where(masks[drn[n]], jnp.exp2(dists[drn[n]] * jnp.broadcast_to(lg[n][:, :1], (c, c))),
                                    0.0)).astype(BF16) for n in grp}
        inner = {n: _dot(sc[n], v[n], None) for n in grp}
        upd = {n: _dot_tn((k[n].astype(F32) * jnp.exp2((c - 1.0 - poss[drn[n]]) * lg[n])).astype(BF16), v[n], None)
               for n in grp}
        for n in grp:
            o_ref = dir_refs[drn[n]][3]
            o_ref[bat[n], rows[n], col[n]] = (inner[n] + cross[n] * jnp.exp2((poss[drn[n]] + 1.0) * lg[n])).astype(o_ref.dtype)
            st_ref[n] = st[n] * jnp.exp2(np.float32(c) * lg[n]) + upd[n]
    return carry


def _ret_scan(u2, dl):
    def tok_spec(d, blk):
        return pl.BlockSpec((BATCH, RET_CHUNK, RET_W), lambda s: (0, _ret_chunk_index(d, s), blk))
    o_shape = jax.ShapeDtypeStruct((BATCH, TOK, RET_W), BF16)
    return pl.pallas_call(
        _ret_scan_kernel,
        out_shape=(o_shape, o_shape),
        grid=(RET_NCHUNK,),
        in_specs=[tok_spec(0, 0), tok_spec(0, 1), tok_spec(0, 2), tok_spec(1, 0), tok_spec(1, 1), tok_spec(1, 2),
                  pl.BlockSpec((2, RET_HEADS, 1, RET_HEAD), lambda s: (0, 0, 0, 0))],
        out_specs=(tok_spec(0, 0), tok_spec(1, 0)),
        scratch_shapes=[pltpu.VMEM((2 * BATCH * RET_HEADS, RET_HEAD, RET_HEAD), F32)],
        compiler_params=_params("arbitrary"),
        name="ret_scan",
    )(u2, u2, u2, u2, u2, u2, dl)


def _rope_tables():
    pos = np.arange(SEQ)
    half = RET_HEAD // 2
    inv = ROPE_BASE ** (-np.arange(0, half, 2, dtype=np.float32) / half)
    ang = np.concatenate([(pos // GRID_W)[:, None] * inv[None, :], (pos % GRID_W)[:, None] * inv[None, :]], axis=1)
    cos = np.concatenate([np.cos(ang), np.ones((CTX_LEN, half))], axis=0).astype(np.float32)
    sin = np.concatenate([np.sin(ang), np.zeros((CTX_LEN, half))], axis=0).astype(np.float32)
    return np.concatenate([cos, cos], axis=1), np.concatenate([-sin, sin], axis=1)


def _odd_finish_kernel(of_ref, ob_ref, g_ref, x_ref, w_ref, np_ref, mod_ref, out_ref):
    o = of_ref[0].astype(F32) + ob_ref[0].astype(F32)
    parts = []
    ones = jnp.ones((RET_HEAD, RET_HEAD), BF16)
    for h in range(RET_HEADS):
        oh = o[:, h * RET_HEAD:(h + 1) * RET_HEAD]
        ms = _dot((oh * oh).astype(BF16), ones, None) * np.float32(1.0 / RET_HEAD)
        parts.append(oh * lax.rsqrt(ms + EPS))
    z = (jnp.concatenate(parts, axis=-1) * _silu(g_ref[0].astype(F32))).astype(BF16)
    out = jnp.dot(z, w_ref[...], preferred_element_type=F32)
    out_ref[0] = _post_residual(x_ref[0], out, np_ref[...], mod_ref[0, :, 2 * D_MODEL:])


ODD_TILE = 1024


def _odd_finish(o_f, o_b, u2, xcomb, w_out_bf16, npost, mod):
    full = lambda shape: pl.BlockSpec(shape, lambda b, j: (0,) * len(shape))
    tile = lambda w, blk: pl.BlockSpec((1, ODD_TILE, w), lambda b, j: (b, j, blk))
    return pl.pallas_call(
        _odd_finish_kernel,
        out_shape=jax.ShapeDtypeStruct((BATCH, SEQ, D_MODEL), F32),
        grid=(BATCH, SEQ // ODD_TILE),
        in_specs=[
            tile(RET_W, 0), tile(RET_W, 0), tile(RET_W, 3), tile(D_MODEL, 0),
            full((RET_W, D_MODEL)), full((1, D_MODEL)),
            pl.BlockSpec((1, 1, 3 * D_MODEL), lambda b, j: (b, 0, 0)),
        ],
        out_specs=tile(D_MODEL, 0),
        compiler_params=_params("arbitrary", "arbitrary"),
        name="odd_finish",
    )(o_f, o_b, u2, xcomb, w_out_bf16, npost.reshape(1, D_MODEL), mod.reshape(8, 1, 3 * D_MODEL))


def _head_ones():
    return np.kron(np.eye(2, dtype=np.float32), np.ones((RWKV_HEAD, RWKV_HEAD), np.float32))


def kernel(x, c, ctx, c_ctx, ada_w, ada_b, norm_pre, norm_post, ev_w_in, ev_mu, ev_w0, ev_w2, ev_a0, ev_a2, ev_k_k, ev_k_a, ev_r_k, ev_lnx_g, ev_lnx_b, ev_w_out, od_w_in, od_decay_logit, od_w_out):
    cvec = jnp.concatenate([c, c_ctx[None, :], jnp.zeros((8 - BATCH - 1, D_MODEL), F32)], axis=0)
    mod = _ada_mod(cvec, ada_w, ada_b)
    ones_bd = jnp.asarray(_head_ones(), dtype=BF16)

    gate_r, four_in, gate_f, r, v, kk, k2, b2, lw2, bonus = _proj_terms(
        x, ctx, norm_pre[0], mod[0], ev_w_in[0].astype(BF16), ev_mu[0], ev_w0[0], ev_w2[0], ev_a0[0], ev_a2[0],
        ev_k_k[0], ev_k_a[0], ev_r_k[0].reshape(RWKV_W), ones_bd)
    yf, yb = _rwkv_scan(r, v, kk, k2, b2, lw2)
    cosf, sinf = _rope_tables()
    xcomb, u2 = _even_finish(yf, yb, bonus, gate_r, gate_f, _four_lat(four_in), _four_ctx(four_in), x, ctx, ones_bd,
                             ev_lnx_g[0], ev_lnx_b[0], ev_w_out[0].astype(BF16), norm_post[0], mod[0],
                             norm_pre[1], mod[1], od_w_in[0].astype(BF16), jnp.asarray(cosf), jnp.asarray(sinf))

    dl = jnp.broadcast_to(od_decay_logit[0][:, :, None, None], (2, RET_HEADS, 1, RET_HEAD))
    o_f, o_b = _ret_scan(u2, dl)
    return _odd_finish(o_f, o_b, u2, xcomb, od_w_out[0].astype(BF16), norm_post[1], mod[1])
```
